```python
import jax, jax.numpy as jnp
from jax import lax
import numpy as np

D_MODEL = 1024
BATCH = 8
SEQ = 8192
DEPTH = 4

CHUNK = 64
Q_BLOCK = 128
PLE_DIM = 256
GDN_HEADS = 4
GDN_DK = 128
GDN_DV = 128
CONV_WIDTH = 4
MLA_HEADS = 4
MLA_NOPE = 128
MLA_ROPE = 64
MLA_V = 128
Q_LORA = 384
KV_LORA = 256
ROPE_THETA = 10000.0
D_FF = -(-(8 * D_MODEL) // (3 * 256)) * 256

GDN_QK = GDN_HEADS * GDN_DK
GDN_VW = GDN_HEADS * GDN_DV
MIX_WIDTH = GDN_VW + MLA_HEADS * MLA_V
IN_SIZES = (GDN_QK, GDN_QK, GDN_VW, GDN_VW, GDN_HEADS, GDN_HEADS, Q_LORA, KV_LORA, MLA_ROPE)
IN_SPLITS = tuple(int(v) for v in np.cumsum(IN_SIZES)[:-1])
IN_WIDTH = int(sum(IN_SIZES))
ALPHA = (2.0 * DEPTH) ** 0.25
BETA = (8.0 * DEPTH) ** -0.25
LN_EPS = 1e-5
RMS_EPS = 1e-6

kernel_name = 'hymba_gdn_mla_deepnorm_ple_trunk'


def layer_norm(x, g, b):
    xf = x.astype(jnp.float32)
    mu = jnp.mean(xf, -1, keepdims=True)
    var = jnp.mean(jnp.square(xf - mu), -1, keepdims=True)
    return ((xf - mu) * lax.rsqrt(var + LN_EPS) * g + b).astype(x.dtype)


def rms_norm(x, g):
    xf = x.astype(jnp.float32)
    return (xf * lax.rsqrt(jnp.mean(xf * xf, -1, keepdims=True) + RMS_EPS) * g).astype(x.dtype)


def l2_norm(x):
    xf = x.astype(jnp.float32)
    return xf * lax.rsqrt(jnp.sum(xf * xf, -1, keepdims=True) + RMS_EPS)


def rope_tables(positions):
    inv_freq = ROPE_THETA ** (-jnp.arange(0, MLA_ROPE, 2, dtype=jnp.float32) / MLA_ROPE)
    ang = positions.astype(jnp.float32)[..., None] * inv_freq
    return jnp.cos(ang), jnp.sin(ang)


def apply_rope(x, cos, sin):
    x1, x2 = jnp.split(x.astype(jnp.float32), 2, axis=-1)
    return jnp.concatenate([x1 * cos - x2 * sin, x2 * cos + x1 * sin], -1).astype(x.dtype)


def causal_dwconv(x, w):
    return lax.conv_general_dilated(
        x, w[:, None, :].astype(x.dtype), window_strides=(1,),
        padding=[(CONV_WIDTH - 1, 0)], dimension_numbers=('NWC', 'WIO', 'NWC'),
        feature_group_count=x.shape[-1])


def gated_delta_rule(q, k, v, g, beta):
    B, S, H, DK = q.shape
    DV = v.shape[-1]
    N = S // CHUNK
    f32 = jnp.float32

    def chunks(t):
        t = t.astype(f32).reshape((B, N, CHUNK, H) + t.shape[3:])
        return jnp.moveaxis(t, (1, 3), (0, 2))

    q = chunks(q) * DK ** -0.5
    k = chunks(k)
    v = chunks(v)
    beta = chunks(beta)
    g = jnp.cumsum(chunks(g), axis=-1)
    tri_incl = jnp.tril(jnp.ones((CHUNK, CHUNK), bool))
    tri_strict = jnp.tril(jnp.ones((CHUNK, CHUNK), bool), -1)
    decay = jnp.exp(jnp.where(tri_incl, g[..., :, None] - g[..., None, :], -jnp.inf))
    k_beta = k * beta[..., None]
    lower = jnp.where(tri_strict, jnp.einsum('nbhid,nbhjd->nbhij', k_beta, k) * decay, 0.0)
    rhs = jnp.concatenate([v * beta[..., None], k_beta * jnp.exp(g)[..., None]], -1)
    sol = lax.linalg.triangular_solve(lower + jnp.eye(CHUNK, dtype=f32), rhs,
                                      left_side=True, lower=True, unit_diagonal=True)
    u, w = sol[..., :DV], sol[..., DV:]
    attn = jnp.where(tri_incl, jnp.einsum('nbhid,nbhjd->nbhij', q, k) * decay, 0.0)
    g_last = g[..., -1]
    q_dec = q * jnp.exp(g)[..., None]
    k_dec = k * jnp.exp(g_last[..., None] - g)[..., None]

    def step(state, xs):
        q_c, k_c, u_c, w_c, attn_c, gl = xs
        v_new = u_c - jnp.einsum('bhck,bhkv->bhcv', w_c, state)
        o = jnp.einsum('bhck,bhkv->bhcv', q_c, state) + jnp.einsum('bhij,bhjv->bhiv', attn_c, v_new)
        state = state * jnp.exp(gl)[..., None, None] + jnp.einsum('bhck,bhcv->bhkv', k_c, v_new)
        return state, o

    s0 = jnp.zeros((B, H, DK, DV), f32)
    _, o = lax.scan(step, s0, (q_dec, k_dec, u, w, attn, g_last))
    return jnp.moveaxis(o, (0, 2), (1, 3)).reshape(B, S, H, DV)


def mla_attention(q_nope, q_rope, k_nope, k_rope, v):
    S = q_nope.shape[1]
    scale = (MLA_NOPE + MLA_ROPE) ** -0.5
    frame_chunk = jnp.arange(S) // CHUNK
    outs = []
    for blk in range(S // Q_BLOCK):
        q0, q1 = blk * Q_BLOCK, (blk + 1) * Q_BLOCK
        s = (jnp.einsum('bqhd,bkhd->bhqk', q_nope[:, q0:q1], k_nope[:, :q1])
             + jnp.einsum('bqhd,bkd->bhqk', q_rope[:, q0:q1], k_rope[:, :q1])).astype(jnp.float32) * scale
        mask = frame_chunk[None, :q1] <= frame_chunk[q0:q1, None]
        prob = jax.nn.softmax(jnp.where(mask, s, -jnp.inf), axis=-1).astype(v.dtype)
        outs.append(jnp.einsum('bhqk,bkhd->bqhd', prob, v[:, :q1]))
    return jnp.concatenate(outs, axis=1)


def hybrid_layer(x, p_i, cos, sin, w_in, conv_w, a_log, dt_bias, gdn_norm_g, q_norm_g,
                 w_uq, kv_norm_g, w_ukv, w_out, ln1_g, ln1_b, w_gate_up, w_down,
                 ln2_g, ln2_b, w_ple, w_ple_gate):
    B, S, _ = x.shape
    h = x @ w_in
    q, k, v, z, b, a, c_q, c_kv, k_r = jnp.split(h, IN_SPLITS, axis=-1)

    qkv = jax.nn.silu(causal_dwconv(jnp.concatenate([q, k, v], -1), conv_w))
    q, k, v = jnp.split(qkv, [GDN_QK, 2 * GDN_QK], axis=-1)
    q = l2_norm(q.reshape(B, S, GDN_HEADS, GDN_DK))
    k = l2_norm(k.reshape(B, S, GDN_HEADS, GDN_DK))
    v = v.reshape(B, S, GDN_HEADS, GDN_DV)
    beta = jax.nn.sigmoid(b.astype(jnp.float32))
    g = -jnp.exp(a_log.astype(jnp.float32)) * jax.nn.softplus(a.astype(jnp.float32) + dt_bias)
    o_gdn = gated_delta_rule(q, k, v, g, beta)
    o_gdn = rms_norm(o_gdn, gdn_norm_g) * jax.nn.silu(z.reshape(B, S, GDN_HEADS, GDN_DV).astype(jnp.float32))

    qm = (rms_norm(c_q, q_norm_g) @ w_uq).reshape(B, S, MLA_HEADS, MLA_NOPE + MLA_ROPE)
    q_nope, q_rope = jnp.split(qm, [MLA_NOPE], axis=-1)
    q_rope = apply_rope(q_rope, cos[:, :, None, :], sin[:, :, None, :])
    kv = (rms_norm(c_kv, kv_norm_g) @ w_ukv).reshape(B, S, MLA_HEADS, MLA_NOPE + MLA_V)
    k_nope, v_m = jnp.split(kv, [MLA_NOPE], axis=-1)
    k_rope = apply_rope(k_r, cos, sin)
    o_mla = mla_attention(q_nope, q_rope, k_nope, k_rope, v_m)

    mix = jnp.concatenate([o_gdn.reshape(B, S, GDN_VW).astype(x.dtype),
                           o_mla.reshape(B, S, MLA_HEADS * MLA_V)], axis=-1) @ w_out
    x = layer_norm(ALPHA * x + mix, ln1_g, ln1_b)

    gate, up = jnp.split(x @ w_gate_up, 2, axis=-1)
    x = layer_norm(ALPHA * x + (jax.nn.silu(gate) * up) @ w_down, ln2_g, ln2_b)

    return x + jax.nn.sigmoid(x @ w_ple_gate) * (p_i @ w_ple)


def _fwd_setup_inputs(seed: int = 0) -> dict:
    key = jax.random.key(seed)
    ks = jax.random.split(key, 24)
    f32 = jnp.float32
    nrm = lambda k, shape, scale: jax.random.normal(k, shape, f32) * scale
    x = jax.random.normal(ks[0], (BATCH, SEQ, D_MODEL), f32)
    p = jax.random.normal(ks[1], (DEPTH, BATCH, SEQ, PLE_DIM), f32)
    offset = jax.random.randint(ks[2], (BATCH, 1), 0, 64) * CHUNK
    positions = (offset + jnp.arange(SEQ, dtype=jnp.int32)[None, :]).astype(jnp.int32)
    dt = jnp.exp(jax.random.uniform(ks[3], (DEPTH, GDN_HEADS), f32) * (np.log(0.1) - np.log(1e-3)) + np.log(1e-3))
    return {
        'x': x,
        'p': p,
        'positions': positions,
        'w_in': nrm(ks[4], (DEPTH, D_MODEL, IN_WIDTH), D_MODEL ** -0.5),
        'conv_w': nrm(ks[5], (DEPTH, CONV_WIDTH, 2 * GDN_QK + GDN_VW), CONV_WIDTH ** -0.5),
        'a_log': jnp.log(jax.random.uniform(ks[6], (DEPTH, GDN_HEADS), f32, 1.0, 16.0)),
        'dt_bias': dt + jnp.log(-jnp.expm1(-dt)),
        'gdn_norm_g': 1.0 + nrm(ks[7], (DEPTH, GDN_DV), 0.1),
        'q_norm_g': 1.0 + nrm(ks[8], (DEPTH, Q_LORA), 0.1),
        'w_uq': nrm(ks[9], (DEPTH, Q_LORA, MLA_HEADS * (MLA_NOPE + MLA_ROPE)), Q_LORA ** -0.5),
        'kv_norm_g': 1.0 + nrm(ks[10], (DEPTH, KV_LORA), 0.1),
        'w_ukv': nrm(ks[11], (DEPTH, KV_LORA, MLA_HEADS * (MLA_NOPE + MLA_V)), KV_LORA ** -0.5),
        'w_out': nrm(ks[12], (DEPTH, MIX_WIDTH, D_MODEL), MIX_WIDTH ** -0.5 * BETA),
        'ln1_g': 1.0 + nrm(ks[13], (DEPTH, D_MODEL), 0.1),
        'ln1_b': nrm(ks[14], (DEPTH, D_MODEL), 0.02),
        'w_gate_up': nrm(ks[15], (DEPTH, D_MODEL, 2 * D_FF), D_MODEL ** -0.5),
        'w_down': nrm(ks[16], (DEPTH, D_FF, D_MODEL), D_FF ** -0.5 * BETA),
        'ln2_g': 1.0 + nrm(ks[17], (DEPTH, D_MODEL), 0.1),
        'ln2_b': nrm(ks[18], (DEPTH, D_MODEL), 0.02),
        'w_ple': nrm(ks[19], (DEPTH, PLE_DIM, D_MODEL), PLE_DIM ** -0.5),
        'w_ple_gate': nrm(ks[20], (DEPTH, D_MODEL, D_MODEL), D_MODEL ** -0.5),
    }


def _fwd_reference(x, p, positions, w_in, conv_w, a_log, dt_bias, gdn_norm_g, q_norm_g, w_uq,
              kv_norm_g, w_ukv, w_out, ln1_g, ln1_b, w_gate_up, w_down, ln2_g, ln2_b,
              w_ple, w_ple_gate):
    cos, sin = rope_tables(positions)
    for i in range(DEPTH):
        x = hybrid_layer(x, p[i], cos, sin, w_in[i], conv_w[i], a_log[i], dt_bias[i],
                         gdn_norm_g[i], q_norm_g[i], w_uq[i], kv_norm_g[i], w_ukv[i],
                         w_out[i], ln1_g[i], ln1_b[i], w_gate_up[i], w_down[i],
                         ln2_g[i], ln2_b[i], w_ple[i], w_ple_gate[i])
    return x


import jax as _jax
import jax.numpy as _jnp

TWIN_FORMAT = 'train_step'
FWD_PARAMS = ['x', 'p', 'positions', 'w_in', 'conv_w', 'a_log', 'dt_bias', 'gdn_norm_g', 'q_norm_g', 'w_uq', 'kv_norm_g', 'w_ukv', 'w_out', 'ln1_g', 'ln1_b', 'w_gate_up', 'w_down', 'ln2_g', 'ln2_b', 'w_ple', 'w_ple_gate']
TWIN_WEIGHTS = ['w_in', 'conv_w', 'a_log', 'dt_bias', 'gdn_norm_g', 'q_norm_g', 'w_uq', 'kv_norm_g', 'w_ukv', 'w_out', 'ln1_g', 'ln1_b', 'w_gate_up', 'w_down', 'ln2_g', 'ln2_b', 'w_ple', 'w_ple_gate']
TWIN_DIFF_INPUT = 'x'
TWIN_INPUTS = ['x', 'p', 'positions', 'w_in', 'conv_w', 'a_log', 'dt_bias', 'gdn_norm_g', 'q_norm_g', 'w_uq', 'kv_norm_g', 'w_ukv', 'w_out', 'ln1_g', 'ln1_b', 'w_gate_up', 'w_down', 'ln2_g', 'ln2_b', 'w_ple', 'w_ple_gate', 'loss_target', 'm_w_in', 'm_conv_w', 'm_a_log', 'm_dt_bias', 'm_gdn_norm_g', 'm_q_norm_g', 'm_w_uq', 'm_kv_norm_g', 'm_w_ukv', 'm_w_out', 'm_ln1_g', 'm_ln1_b', 'm_w_gate_up', 'm_w_down', 'm_ln2_g', 'm_ln2_b', 'm_w_ple', 'm_w_ple_gate', 'v_w_in', 'v_conv_w', 'v_a_log', 'v_dt_bias', 'v_gdn_norm_g', 'v_q_norm_g', 'v_w_uq', 'v_kv_norm_g', 'v_w_ukv', 'v_w_out', 'v_ln1_g', 'v_ln1_b', 'v_w_gate_up', 'v_w_down', 'v_ln2_g', 'v_ln2_b', 'v_w_ple', 'v_w_ple_gate']
TWIN_OUTPUTS = ['loss', 'grad_x', 'grad_w_in', 'grad_conv_w', 'grad_a_log', 'grad_dt_bias', 'grad_gdn_norm_g', 'grad_q_norm_g', 'grad_w_uq', 'grad_kv_norm_g', 'grad_w_ukv', 'grad_w_out', 'grad_ln1_g', 'grad_ln1_b', 'grad_w_gate_up', 'grad_w_down', 'grad_ln2_g', 'grad_ln2_b', 'grad_w_ple', 'grad_w_ple_gate', 'delta_w_in', 'delta_conv_w', 'delta_a_log', 'delta_dt_bias', 'delta_gdn_norm_g', 'delta_q_norm_g', 'delta_w_uq', 'delta_kv_norm_g', 'delta_w_ukv', 'delta_w_out', 'delta_ln1_g', 'delta_ln1_b', 'delta_w_gate_up', 'delta_w_down', 'delta_ln2_g', 'delta_ln2_b', 'delta_w_ple', 'delta_w_ple_gate', 'new_m_w_in', 'new_m_conv_w', 'new_m_a_log', 'new_m_dt_bias', 'new_m_gdn_norm_g', 'new_m_q_norm_g', 'new_m_w_uq', 'new_m_kv_norm_g', 'new_m_w_ukv', 'new_m_w_out', 'new_m_ln1_g', 'new_m_ln1_b', 'new_m_w_gate_up', 'new_m_w_down', 'new_m_ln2_g', 'new_m_ln2_b', 'new_m_w_ple', 'new_m_w_ple_gate', 'new_v_w_in', 'new_v_conv_w', 'new_v_a_log', 'new_v_dt_bias', 'new_v_gdn_norm_g', 'new_v_q_norm_g', 'new_v_w_uq', 'new_v_kv_norm_g', 'new_v_w_ukv', 'new_v_w_out', 'new_v_ln1_g', 'new_v_ln1_b', 'new_v_w_gate_up', 'new_v_w_down', 'new_v_ln2_g', 'new_v_ln2_b', 'new_v_w_ple', 'new_v_w_ple_gate']
TWIN_LEAF_KINDS = {'loss': 'loss', 'grad_x': 'grad_x', 'grad_w_in': 'grad_w', 'grad_conv_w': 'grad_w', 'grad_a_log': 'grad_w', 'grad_dt_bias': 'grad_w', 'grad_gdn_norm_g': 'grad_w', 'grad_q_norm_g': 'grad_w', 'grad_w_uq': 'grad_w', 'grad_kv_norm_g': 'grad_w', 'grad_w_ukv': 'grad_w', 'grad_w_out': 'grad_w', 'grad_ln1_g': 'grad_w', 'grad_ln1_b': 'grad_w', 'grad_w_gate_up': 'grad_w', 'grad_w_down': 'grad_w', 'grad_ln2_g': 'grad_w', 'grad_ln2_b': 'grad_w', 'grad_w_ple': 'grad_w', 'grad_w_ple_gate': 'grad_w', 'delta_w_in': 'delta_w', 'delta_conv_w': 'delta_w', 'delta_a_log': 'delta_w', 'delta_dt_bias': 'delta_w', 'delta_gdn_norm_g': 'delta_w', 'delta_q_norm_g': 'delta_w', 'delta_w_uq': 'delta_w', 'delta_kv_norm_g': 'delta_w', 'delta_w_ukv': 'delta_w', 'delta_w_out': 'delta_w', 'delta_ln1_g': 'delta_w', 'delta_ln1_b': 'delta_w', 'delta_w_gate_up': 'delta_w', 'delta_w_down': 'delta_w', 'delta_ln2_g': 'delta_w', 'delta_ln2_b': 'delta_w', 'delta_w_ple': 'delta_w', 'delta_w_ple_gate': 'delta_w', 'new_m_w_in': 'new_m', 'new_m_conv_w': 'new_m', 'new_m_a_log': 'new_m', 'new_m_dt_bias': 'new_m', 'new_m_gdn_norm_g': 'new_m', 'new_m_q_norm_g': 'new_m', 'new_m_w_uq': 'new_m', 'new_m_kv_norm_g': 'new_m', 'new_m_w_ukv': 'new_m', 'new_m_w_out': 'new_m', 'new_m_ln1_g': 'new_m', 'new_m_ln1_b': 'new_m', 'new_m_w_gate_up': 'new_m', 'new_m_w_down': 'new_m', 'new_m_ln2_g': 'new_m', 'new_m_ln2_b': 'new_m', 'new_m_w_ple': 'new_m', 'new_m_w_ple_gate': 'new_m', 'new_v_w_in': 'new_v', 'new_v_conv_w': 'new_v', 'new_v_a_log': 'new_v', 'new_v_dt_bias': 'new_v', 'new_v_gdn_norm_g': 'new_v', 'new_v_q_norm_g': 'new_v', 'new_v_w_uq': 'new_v', 'new_v_kv_norm_g': 'new_v', 'new_v_w_ukv': 'new_v', 'new_v_w_out': 'new_v', 'new_v_ln1_g': 'new_v', 'new_v_ln1_b': 'new_v', 'new_v_w_gate_up': 'new_v', 'new_v_w_down': 'new_v', 'new_v_ln2_g': 'new_v', 'new_v_ln2_b': 'new_v', 'new_v_w_ple': 'new_v', 'new_v_w_ple_gate': 'new_v'}


def _forward(args):
    return _fwd_reference(*[args[k] for k in FWD_PARAMS])


def _output_shape():
    def fwd():
        inp = _fwd_setup_inputs(0)
        return _fwd_reference(*[inp[k] for k in FWD_PARAMS])
    out = _jax.eval_shape(fwd)
    return out.shape, out.dtype

N_MICROBATCH = 1
ADAM_LR = 0.001
ADAM_B1 = 0.9
ADAM_B2 = 0.999
ADAM_EPS = 1e-08
ADAM_WD = 0.01
ADAM_STEP = 10
PER_EXAMPLE_BATCH_AXIS = {'x': 0, 'p': 1, 'positions': 0, 'loss_target': 0}
SHARED_INPUTS = []
_WEIGHT_DTYPES = {'w_in': _jnp.float32, 'conv_w': _jnp.float32, 'a_log': _jnp.float32, 'dt_bias': _jnp.float32, 'gdn_norm_g': _jnp.float32, 'q_norm_g': _jnp.float32, 'w_uq': _jnp.float32, 'kv_norm_g': _jnp.float32, 'w_ukv': _jnp.float32, 'w_out': _jnp.float32, 'ln1_g': _jnp.float32, 'ln1_b': _jnp.float32, 'w_gate_up': _jnp.float32, 'w_down': _jnp.float32, 'ln2_g': _jnp.float32, 'ln2_b': _jnp.float32, 'w_ple': _jnp.float32, 'w_ple_gate': _jnp.float32}
MOMENT_SCALE = {'w_in': 3.180182e-02, 'conv_w': 5.903697e-02, 'a_log': 3.722725e-01, 'dt_bias': 3.289558e-01, 'gdn_norm_g': 3.014811e-01, 'q_norm_g': 1.169752e-02, 'w_uq': 8.131599e-03, 'kv_norm_g': 4.738449e-02, 'w_ukv': 2.138862e-02, 'w_out': 2.627154e-01, 'ln1_g': 9.355370e+00, 'ln1_b': 5.385541e+00, 'w_gate_up': 2.627488e-02, 'w_down': 1.045161e-01, 'ln2_g': 3.572688e+01, 'ln2_b': 5.706339e+00, 'w_ple': 4.319802e-01, 'w_ple_gate': 1.518366e-01}


def _to_microbatches(a, axis):
    t = _jnp.moveaxis(a, axis, 0)
    t = t.reshape((N_MICROBATCH, t.shape[0] // N_MICROBATCH) + t.shape[1:])
    return _jnp.moveaxis(t, 1, axis + 1)


def setup_inputs(seed: int = 0) -> dict:
    inp = _fwd_setup_inputs(seed)
    key = _jax.random.fold_in(_jax.random.key(seed), 7919)
    shape, _ = _output_shape()
    out = dict(inp)
    out["loss_target"] = _jax.random.normal(_jax.random.fold_in(key, 0), shape, _jnp.float32)
    for i, name in enumerate(TWIN_WEIGHTS):
        w = inp[name].astype(_jnp.float32)
        if MOMENT_SCALE is None:
            s = _jnp.sqrt(_jnp.mean(_jnp.square(w)) + 1e-30)
        else:
            s = MOMENT_SCALE[name]
        km, kv = _jax.random.split(_jax.random.fold_in(key, i + 1))
        out[name] = w
        out["m_" + name] = s * _jax.random.normal(km, w.shape, _jnp.float32)
        out["v_" + name] = (s * s) * _jax.random.uniform(kv, w.shape, _jnp.float32, 0.5, 1.5)
    if N_MICROBATCH > 1:
        for name, axis in PER_EXAMPLE_BATCH_AXIS.items():
            out[name] = _to_microbatches(out[name], axis)
    return {'x': out['x'], 'p': out['p'], 'positions': out['positions'], 'w_in': out['w_in'], 'conv_w': out['conv_w'], 'a_log': out['a_log'], 'dt_bias': out['dt_bias'], 'gdn_norm_g': out['gdn_norm_g'], 'q_norm_g': out['q_norm_g'], 'w_uq': out['w_uq'], 'kv_norm_g': out['kv_norm_g'], 'w_ukv': out['w_ukv'], 'w_out': out['w_out'], 'ln1_g': out['ln1_g'], 'ln1_b': out['ln1_b'], 'w_gate_up': out['w_gate_up'], 'w_down': out['w_down'], 'ln2_g': out['ln2_g'], 'ln2_b': out['ln2_b'], 'w_ple': out['w_ple'], 'w_ple_gate': out['w_ple_gate'], 'loss_target': out['loss_target'], 'm_w_in': out['m_w_in'], 'm_conv_w': out['m_conv_w'], 'm_a_log': out['m_a_log'], 'm_dt_bias': out['m_dt_bias'], 'm_gdn_norm_g': out['m_gdn_norm_g'], 'm_q_norm_g': out['m_q_norm_g'], 'm_w_uq': out['m_w_uq'], 'm_kv_norm_g': out['m_kv_norm_g'], 'm_w_ukv': out['m_w_ukv'], 'm_w_out': out['m_w_out'], 'm_ln1_g': out['m_ln1_g'], 'm_ln1_b': out['m_ln1_b'], 'm_w_gate_up': out['m_w_gate_up'], 'm_w_down': out['m_w_down'], 'm_ln2_g': out['m_ln2_g'], 'm_ln2_b': out['m_ln2_b'], 'm_w_ple': out['m_w_ple'], 'm_w_ple_gate': out['m_w_ple_gate'], 'v_w_in': out['v_w_in'], 'v_conv_w': out['v_conv_w'], 'v_a_log': out['v_a_log'], 'v_dt_bias': out['v_dt_bias'], 'v_gdn_norm_g': out['v_gdn_norm_g'], 'v_q_norm_g': out['v_q_norm_g'], 'v_w_uq': out['v_w_uq'], 'v_kv_norm_g': out['v_kv_norm_g'], 'v_w_ukv': out['v_w_ukv'], 'v_w_out': out['v_w_out'], 'v_ln1_g': out['v_ln1_g'], 'v_ln1_b': out['v_ln1_b'], 'v_w_gate_up': out['v_w_gate_up'], 'v_w_down': out['v_w_down'], 'v_ln2_g': out['v_ln2_g'], 'v_ln2_b': out['v_ln2_b'], 'v_w_ple': out['v_w_ple'], 'v_w_ple_gate': out['v_w_ple_gate']}


def _loss(weights, diff, rest, loss_target):
    with _jax.named_scope("forward"):
        args = {**rest, TWIN_DIFF_INPUT: diff, **{k: w.astype(_WEIGHT_DTYPES[k]) for k, w in weights.items()}}
        y = _forward(args)
    with _jax.named_scope("loss_head"):
        err = _jnp.square(y.astype(_jnp.float32) - loss_target)
        return 0.5 * _jnp.sum(_jnp.mean(err, axis=-1)) if err.ndim else 0.5 * err


def _adamw(w, g, m, v):
    m = ADAM_B1 * m + (1.0 - ADAM_B1) * g
    v = ADAM_B2 * v + (1.0 - ADAM_B2) * _jnp.square(g)
    m_hat = m / (1.0 - ADAM_B1 ** ADAM_STEP)
    v_hat = v / (1.0 - ADAM_B2 ** ADAM_STEP)
    delta = -ADAM_LR * (m_hat / (_jnp.sqrt(v_hat) + ADAM_EPS) + ADAM_WD * w)
    return delta, m, v


def reference(x, p, positions, w_in, conv_w, a_log, dt_bias, gdn_norm_g, q_norm_g, w_uq, kv_norm_g, w_ukv, w_out, ln1_g, ln1_b, w_gate_up, w_down, ln2_g, ln2_b, w_ple, w_ple_gate, loss_target, m_w_in, m_conv_w, m_a_log, m_dt_bias, m_gdn_norm_g, m_q_norm_g, m_w_uq, m_kv_norm_g, m_w_ukv, m_w_out, m_ln1_g, m_ln1_b, m_w_gate_up, m_w_down, m_ln2_g, m_ln2_b, m_w_ple, m_w_ple_gate, v_w_in, v_conv_w, v_a_log, v_dt_bias, v_gdn_norm_g, v_q_norm_g, v_w_uq, v_kv_norm_g, v_w_ukv, v_w_out, v_ln1_g, v_ln1_b, v_w_gate_up, v_w_down, v_ln2_g, v_ln2_b, v_w_ple, v_w_ple_gate):
    given = dict(x=x, p=p, positions=positions, w_in=w_in, conv_w=conv_w, a_log=a_log, dt_bias=dt_bias, gdn_norm_g=gdn_norm_g, q_norm_g=q_norm_g, w_uq=w_uq, kv_norm_g=kv_norm_g, w_ukv=w_ukv, w_out=w_out, ln1_g=ln1_g, ln1_b=ln1_b, w_gate_up=w_gate_up, w_down=w_down, ln2_g=ln2_g, ln2_b=ln2_b, w_ple=w_ple, w_ple_gate=w_ple_gate, loss_target=loss_target, m_w_in=m_w_in, m_conv_w=m_conv_w, m_a_log=m_a_log, m_dt_bias=m_dt_bias, m_gdn_norm_g=m_gdn_norm_g, m_q_norm_g=m_q_norm_g, m_w_uq=m_w_uq, m_kv_norm_g=m_kv_norm_g, m_w_ukv=m_w_ukv, m_w_out=m_w_out, m_ln1_g=m_ln1_g, m_ln1_b=m_ln1_b, m_w_gate_up=m_w_gate_up, m_w_down=m_w_down, m_ln2_g=m_ln2_g, m_ln2_b=m_ln2_b, m_w_ple=m_w_ple, m_w_ple_gate=m_w_ple_gate, v_w_in=v_w_in, v_conv_w=v_conv_w, v_a_log=v_a_log, v_dt_bias=v_dt_bias, v_gdn_norm_g=v_gdn_norm_g, v_q_norm_g=v_q_norm_g, v_w_uq=v_w_uq, v_kv_norm_g=v_kv_norm_g, v_w_ukv=v_w_ukv, v_w_out=v_w_out, v_ln1_g=v_ln1_g, v_ln1_b=v_ln1_b, v_w_gate_up=v_w_gate_up, v_w_down=v_w_down, v_ln2_g=v_ln2_g, v_ln2_b=v_ln2_b, v_w_ple=v_w_ple, v_w_ple_gate=v_w_ple_gate)
    weights = {n: given[n] for n in TWIN_WEIGHTS}
    shared = {n: given[n] for n in SHARED_INPUTS}
    per_example = {n: given[n] for n in ['x', 'p', 'positions']}
    grad_fn = _jax.value_and_grad(_loss, argnums=(0, 1))

    def one_microbatch(ex, loss_target):
        ex = dict(ex)
        diff = ex.pop(TWIN_DIFF_INPUT)
        return grad_fn(weights, diff, {**shared, **ex}, loss_target)

    if N_MICROBATCH == 1:
        loss, (grad_w, grad_x) = one_microbatch(per_example, given["loss_target"])
    else:
        def body(carry, xs):
            loss_sum, grad_sum = carry
            l_k, (gw_k, gx_k) = one_microbatch(xs[0], xs[1])
            with _jax.named_scope("update"):
                return (loss_sum + l_k, _jax.tree.map(_jnp.add, grad_sum, gw_k)), gx_k

        init = (_jnp.zeros((), _jnp.float32), _jax.tree.map(_jnp.zeros_like, weights))
        (loss, grad_w), grad_x = _jax.lax.scan(body, init, (per_example, given["loss_target"]))
    with _jax.named_scope("update"):
        delta_w, new_m, new_v = {}, {}, {}
        for n in TWIN_WEIGHTS:
            delta_w[n], new_m[n], new_v[n] = _adamw(weights[n], grad_w[n], given["m_" + n], given["v_" + n])
    return (loss, grad_x, *[grad_w[n] for n in TWIN_WEIGHTS], *[delta_w[n] for n in TWIN_WEIGHTS],
            *[new_m[n] for n in TWIN_WEIGHTS], *[new_v[n] for n in TWIN_WEIGHTS])
```

```python
import functools

import jax
import jax.numpy as jnp
import numpy as np
from jax import lax
from jax.experimental import pallas as pl
from jax.experimental.pallas import tpu as pltpu

f32 = jnp.float32
bf16 = jnp.bfloat16
MX = jnp.bfloat16

D = 1024
DEPTH = 4
CHUNK = 64
H = 4
DK = 128
PLE = 256
QL = 384
KVL = 256
ROPE = 64
DFF = 2816
IN_W = 2760
ROPE_THETA = 10000.0
ALPHA = (2.0 * DEPTH) ** 0.25
LN_EPS = 1e-5
RMS_EPS = 1e-6
ATT_SCALE = (128 + 64) ** -0.5
N_DEV = 8

ADAM_LR, ADAM_B1, ADAM_B2, ADAM_EPS, ADAM_WD, ADAM_STEP = 0.001, 0.9, 0.999, 1e-08, 0.01, 10

OFF_QKV, OFF_Z, OFF_CKV, OFF_KR, OFF_KRS, OFF_CQ, HP = 0, 1536, 2048, 2304, 2432, 2560, 3072
CQP = 512
LANE = 128
VMEM_LIMIT = 48 * 1024 * 1024
ROW_TB = 256


def _dot(a, b, ca, cb, prec=None):
    return lax.dot_general(a, b, (((ca,), (cb,)), ((), ())), precision=prec, preferred_element_type=f32)


@jax.custom_vjp
def bdot(a, w):
    return _dot(a.astype(MX), w.astype(MX), 1, 0)


def _bdot_fwd(a, w):
    return bdot(a, w), (a, w)


def _bdot_bwd(res, g):
    a, w = res
    gb = g.astype(MX)
    return _dot(gb, w.astype(MX), 1, 1).astype(a.dtype), _dot(a.astype(MX), gb, 0, 0).astype(w.dtype)


bdot.defvjp(_bdot_fwd, _bdot_bwd)


@jax.custom_vjp
def bdot_nt(a, b):
    return _dot(a.astype(MX), b.astype(MX), 1, 1)


def _bdot_nt_fwd(a, b):
    return bdot_nt(a, b), (a, b)


def _bdot_nt_bwd(res, g):
    a, b = res
    gb = g.astype(MX)
    return _dot(gb, b.astype(MX), 1, 0).astype(a.dtype), _dot(gb, a.astype(MX), 0, 0).astype(b.dtype)


bdot_nt.defvjp(_bdot_nt_fwd, _bdot_nt_bwd)


@jax.custom_vjp
def bdot_tn(a, b):
    return _dot(a.astype(MX), b.astype(MX), 0, 0)


def _bdot_tn_fwd(a, b):
    return bdot_tn(a, b), (a, b)


def _bdot_tn_bwd(res, g):
    a, b = res
    gb = g.astype(MX)
    return _dot(b.astype(MX), gb, 1, 1).astype(a.dtype), _dot(a.astype(MX), gb, 1, 0).astype(b.dtype)


bdot_tn.defvjp(_bdot_tn_fwd, _bdot_tn_bwd)


def hdot(a, b):
    return _dot(a, b, 1, 0, lax.Precision.HIGHEST)


def _pick(n, cap):
    best = None
    for t in range(LANE, min(n, cap) + 1, LANE):
        if n % t == 0:
            best = t
    assert best is not None, (n, cap)
    return best


def _cparams(sem):
    return pltpu.CompilerParams(dimension_semantics=sem, vmem_limit_bytes=VMEM_LIMIT)


def mm(name, a, b, *, mode, out_dtype, add=None):
    if mode == "tn":
        kdim, m = a.shape
        n = b.shape[1]
        tm, tn, tk = _pick(m, 1536), _pick(n, 1536), min(512, kdim)
        nk = kdim // tk

        def body(a_ref, b_ref, o_ref, acc):
            k = pl.program_id(2)

            @pl.when(k == 0)
            def _():
                acc[...] = jnp.zeros_like(acc)

            acc[...] += _dot(a_ref[...].astype(MX), b_ref[...].astype(MX), 0, 0)

            @pl.when(k == nk - 1)
            def _():
                o_ref[...] = acc[...].astype(o_ref.dtype)

        return pl.pallas_call(
            body, name=name, grid=(m // tm, n // tn, nk),
            in_specs=[pl.BlockSpec((tk, tm), lambda i, j, k: (k, i)), pl.BlockSpec((tk, tn), lambda i, j, k: (k, j))],
            out_specs=pl.BlockSpec((tm, tn), lambda i, j, k: (i, j)),
            out_shape=jax.ShapeDtypeStruct((m, n), out_dtype),
            scratch_shapes=[pltpu.VMEM((tm, tn), f32)],
            compiler_params=_cparams(("parallel", "parallel", "arbitrary")),
        )(a, b)

    m, kdim = a.shape
    n = b.shape[1] if mode == "nn" else b.shape[0]
    tm, tn = min(512, m), _pick(n, 1536)
    has_add = add is not None

    def body(*refs):
        a_ref, b_ref = refs[0], refs[1]
        o_ref = refs[-1]
        r = _dot(a_ref[...].astype(MX), b_ref[...].astype(MX), 1, 0 if mode == "nn" else 1)
        if has_add:
            r = r + refs[2][...].astype(f32)
        o_ref[...] = r.astype(o_ref.dtype)

    b_spec = (pl.BlockSpec((kdim, tn), lambda i, j: (0, j)) if mode == "nn"
              else pl.BlockSpec((tn, kdim), lambda i, j: (j, 0)))
    in_specs = [pl.BlockSpec((tm, kdim), lambda i, j: (i, 0)), b_spec]
    args = [a, b]
    if has_add:
        in_specs.append(pl.BlockSpec((tm, tn), lambda i, j: (i, j)))
        args.append(add)
    return pl.pallas_call(
        body, name=name, grid=(m // tm, n // tn), in_specs=in_specs,
        out_specs=pl.BlockSpec((tm, tn), lambda i, j: (i, j)),
        out_shape=jax.ShapeDtypeStruct((m, n), out_dtype),
        compiler_params=_cparams(("parallel", "parallel")),
    )(*args)


def _row_spec(tb, w, cb):
    return pl.BlockSpec((tb, w), lambda i: (i, cb))


def _full_spec(shape):
    return pl.BlockSpec(shape, lambda i: (0,) * len(shape))


def row_fwd(name, f, rows, consts, outs):
    t = rows[0][0].shape[0]
    tb = min(ROW_TB, t)
    nr, nc = len(rows), len(consts)

    def body(*refs):
        vals = [r[...] for r in refs[:nr + nc]]
        res = f(*vals)
        for o_ref, val in zip(refs[nr + nc:], res):
            o_ref[...] = val.astype(o_ref.dtype)

    return pl.pallas_call(
        body, name=name, grid=(t // tb,),
        in_specs=[_row_spec(tb, w, cb) for _, w, cb in rows] + [_full_spec(c.shape) for c in consts],
        out_specs=[_row_spec(tb, w, 0) for w, _ in outs],
        out_shape=[jax.ShapeDtypeStruct((t, w), dt) for w, dt in outs],
        compiler_params=_cparams(("parallel",)),
    )(*[r[0] for r in rows], *consts)


def row_bwd(name, f, rows, consts, cots, row_diff, const_diff, drow_dtypes, row_add=None):
    t = rows[0][0].shape[0]
    tb = min(ROW_TB, t)
    nr, nc, nct = len(rows), len(consts), len(cots)
    row_add = row_add or {}
    add_keys = sorted(row_add)
    n_in = nr + nc + nct + len(add_keys)

    def body(*refs):
        i = pl.program_id(0)
        rv = [r[...] for r in refs[:nr]]
        cv = [refs[nr + k][...].astype(f32) if k in const_diff else refs[nr + k][...] for k in range(nc)]
        cot_refs = refs[nr + nc:nr + nc + nct]
        add_refs = refs[nr + nc + nct:n_in]
        drow_refs = refs[n_in:n_in + len(row_diff)]
        dconst_refs = refs[n_in + len(row_diff):]

        def g(*dv):
            r2, c2 = list(rv), list(cv)
            for p, k in enumerate(row_diff):
                r2[k] = dv[p]
            for p, k in enumerate(const_diff):
                c2[k] = dv[len(row_diff) + p]
            return tuple(f(*r2, *c2))

        prim = [rv[k].astype(f32) for k in row_diff] + [cv[k] for k in const_diff]
        outs, vf = jax.vjp(g, *prim)
        grads = vf(tuple(c[...].astype(o.dtype) for c, o in zip(cot_refs, outs)))
        for p, ref in enumerate(drow_refs):
            val = grads[p]
            if p in row_add:
                val = val + add_refs[add_keys.index(p)][...].astype(f32)
            ref[...] = val.astype(ref.dtype)

        @pl.when(i == 0)
        def _():
            for ref in dconst_refs:
                ref[...] = jnp.zeros_like(ref)

        for p, ref in enumerate(dconst_refs):
            ref[...] += grads[len(row_diff) + p]

    widths = [rows[k][1] for k in row_diff]
    return pl.pallas_call(
        body, name=name, grid=(t // tb,),
        in_specs=([_row_spec(tb, w, cb) for _, w, cb in rows] + [_full_spec(c.shape) for c in consts]
                  + [_row_spec(tb, c.shape[1], 0) for c in cots] + [_row_spec(tb, row_add[k].shape[1], 0) for k in add_keys]),
        out_specs=([_row_spec(tb, w, 0) for w in widths] + [_full_spec(consts[k].shape) for k in const_diff]),
        out_shape=([jax.ShapeDtypeStruct((t, w), dt) for w, dt in zip(widths, drow_dtypes)]
                   + [jax.ShapeDtypeStruct(consts[k].shape, f32) for k in const_diff]),
        compiler_params=_cparams(("arbitrary",)),
    )(*[r[0] for r in rows], *consts, *cots, *[row_add[k] for k in add_keys])


def _heads(x, w=DK):
    return [x[:, w * h:w * (h + 1)] for h in range(H)]


def _layer_norm(r, g, b):
    mu = jnp.mean(r, -1, keepdims=True)
    var = jnp.mean(jnp.square(r - mu), -1, keepdims=True)
    return (r - mu) * lax.rsqrt(var + LN_EPS) * g + b


def gdn_point_f(c, ba, alog_b, dtb_b):
    s = c * jax.nn.sigmoid(c)
    q, k, v = s[:, :512], s[:, 512:1024], s[:, 1024:]

    def l2(x):
        return jnp.concatenate([xh * lax.rsqrt(jnp.sum(xh * xh, -1, keepdims=True) + RMS_EPS) for xh in _heads(x)], axis=1)

    tb = c.shape[0]
    b_b = jnp.concatenate([jnp.broadcast_to(ba[:, 64 + h:65 + h], (tb, DK)) for h in range(H)], axis=1)
    a_b = jnp.concatenate([jnp.broadcast_to(ba[:, 68 + h:69 + h], (tb, DK)) for h in range(H)], axis=1)
    beta = jax.nn.sigmoid(b_b)
    g = -jnp.exp(alog_b) * jax.nn.softplus(a_b + dtb_b)
    return l2(q), l2(k), v, beta, g


def mla_pre_f(cq, ckv, krg, krs, cf, sf, qg, wq, kvg, wkv):
    cqn = cq * lax.rsqrt(jnp.sum(cq * cq, -1, keepdims=True) * (1.0 / QL) + RMS_EPS) * qg
    qa = bdot(cqn, wq)
    ckvn = ckv * lax.rsqrt(jnp.mean(ckv * ckv, -1, keepdims=True) + RMS_EPS) * kvg
    kv = bdot(ckvn, wkv)
    kro = krg * cf + krs * sf
    qs, ks, vs = [], [], []
    for h in range(H):
        qs += [qa[:, DK * h:DK * (h + 1)], qa[:, 512 + DK * h:512 + DK * (h + 1)] * cf + qa[:, 1024 + DK * h:1024 + DK * (h + 1)] * sf]
        ks += [kv[:, 256 * h:256 * h + DK], kro]
        vs += [kv[:, 256 * h + DK:256 * (h + 1)]]
    return jnp.concatenate(qs, axis=1), jnp.concatenate(ks, axis=1), jnp.concatenate(vs, axis=1)


def post1_f(o, z, omla, x, gg_b, wout, g1, b1):
    on = jnp.concatenate([oh * lax.rsqrt(jnp.mean(oh * oh, -1, keepdims=True) + RMS_EPS) for oh in _heads(o)], axis=1) * gg_b
    ogdn = on * (z * jax.nn.sigmoid(z))
    mix = bdot(jnp.concatenate([ogdn, omla], axis=1), wout)
    return (_layer_norm(ALPHA * x + mix, g1, b1),)


def post2_f(x1, down, p, g2, b2, wpg, wple):
    x2 = _layer_norm(ALPHA * x1 + down, g2, b2)
    return (x2 + jax.nn.sigmoid(bdot(x2, wpg)) * bdot(p, wple),)


HALO = 8
CW = 4


def _conv_from_scratch(xs, cw_ref, tb):
    c = xs[pl.ds(HALO - 3, tb), :] * cw_ref[0:1, :]
    for j in range(1, CW):
        c = c + xs[pl.ds(HALO - 3 + j, tb), :] * cw_ref[j:j + 1, :]
    return c


def gdn_pre_fwd(hbuf, conv_w8, alog_b, dtb_b):
    t = hbuf.shape[0]
    tb = min(ROW_TB, t)

    def body(x_ref, halo_ref, ba_ref, cw_ref, al_ref, dt_ref, q_ref, k_ref, v_ref, be_ref, g_ref, xs):
        i = pl.program_id(0)
        xs[pl.ds(0, HALO), :] = jnp.where(i == 0, 0.0, halo_ref[...])
        xs[pl.ds(HALO, tb), :] = x_ref[...]
        c = _conv_from_scratch(xs, cw_ref, tb)
        q, k, v, be, g = gdn_point_f(c, ba_ref[...], al_ref[...], dt_ref[...])
        q_ref[...], k_ref[...], v_ref[...], be_ref[...], g_ref[...] = q, k, v, be, g

    return pl.pallas_call(
        body, name="gdn_pre_fwd", grid=(t // tb,),
        in_specs=[_row_spec(tb, 1536, 0),
                  pl.BlockSpec((HALO, 1536), lambda i: (jnp.maximum(i * (tb // HALO) - 1, 0), 0)),
                  _row_spec(tb, LANE, OFF_KR // LANE),
                  _full_spec(conv_w8.shape), _full_spec(alog_b.shape), _full_spec(dtb_b.shape)],
        out_specs=[_row_spec(tb, 512, 0)] * 5,
        out_shape=[jax.ShapeDtypeStruct((t, 512), f32)] * 5,
        scratch_shapes=[pltpu.VMEM((tb + HALO, 1536), f32)],
        compiler_params=_cparams(("arbitrary",)),
    )(hbuf, hbuf, hbuf, conv_w8, alog_b, dtb_b)


def gdn_pre_bwd(hbuf, conv_w8, alog_b, dtb_b, dq, dk, dv, dbe, dg):
    t = hbuf.shape[0]
    tb = min(ROW_TB, t)
    n = t // tb

    def body(x_ref, halo_ref, ba_ref, cw_ref, al_ref, dt_ref, dq_ref, dk_ref, dv_ref, dbe_ref, dg_ref,
             dx_ref, dba_ref, dcw_ref, dal_ref, ddt_ref, xs, dcs):
        s = pl.program_id(0)
        i = n - 1 - s
        xs[pl.ds(0, HALO), :] = jnp.where(i == 0, 0.0, halo_ref[...])
        xs[pl.ds(HALO, tb), :] = x_ref[...]
        c = _conv_from_scratch(xs, cw_ref, tb)
        _, vf = jax.vjp(gdn_point_f, c, ba_ref[...], al_ref[...], dt_ref[...])
        dc, dba, dal, ddt = vf((dq_ref[...], dk_ref[...], dv_ref[...], dbe_ref[...], dg_ref[...]))

        @pl.when(s == 0)
        def _():
            dcs[pl.ds(tb, HALO), :] = jnp.zeros((HALO, 1536), f32)
            dcw_ref[...] = jnp.zeros_like(dcw_ref)
            dal_ref[...] = jnp.zeros_like(dal_ref)
            ddt_ref[...] = jnp.zeros_like(ddt_ref)

        @pl.when(s > 0)
        def _():
            dcs[pl.ds(tb, HALO), :] = dcs[pl.ds(0, HALO), :]

        dcs[pl.ds(0, tb), :] = dc
        dx = dcs[pl.ds(3, tb), :] * cw_ref[0:1, :]
        for j in range(1, CW):
            dx = dx + dcs[pl.ds(3 - j, tb), :] * cw_ref[j:j + 1, :]
        dx_ref[...] = dx.astype(dx_ref.dtype)
        dba_ref[...] = dba
        for j in range(CW):
            dcw_ref[j:j + 1, :] += jnp.sum(dc * xs[pl.ds(HALO - 3 + j, tb), :], axis=0, keepdims=True)
        dal_ref[...] += dal
        ddt_ref[...] += ddt

    rev = lambda cb: (lambda s: (n - 1 - s, cb))
    return pl.pallas_call(
        body, name="gdn_pre_bwd", grid=(n,),
        in_specs=[pl.BlockSpec((tb, 1536), rev(0)),
                  pl.BlockSpec((HALO, 1536), lambda s: (jnp.maximum((n - 1 - s) * (tb // HALO) - 1, 0), 0)),
                  pl.BlockSpec((tb, LANE), rev(OFF_KR // LANE)),
                  _full_spec(conv_w8.shape), _full_spec(alog_b.shape), _full_spec(dtb_b.shape)]
        + [pl.BlockSpec((tb, 512), rev(0))] * 5,
        out_specs=[pl.BlockSpec((tb, 1536), rev(0)), pl.BlockSpec((tb, LANE), rev(0)),
                   _full_spec(conv_w8.shape), _full_spec(alog_b.shape), _full_spec(dtb_b.shape)],
        out_shape=[jax.ShapeDtypeStruct((t, 1536), bf16), jax.ShapeDtypeStruct((t, LANE), f32),
                   jax.ShapeDtypeStruct(conv_w8.shape, f32), jax.ShapeDtypeStruct(alog_b.shape, f32),
                   jax.ShapeDtypeStruct(dtb_b.shape, f32)],
        scratch_shapes=[pltpu.VMEM((tb + HALO, 1536), f32), pltpu.VMEM((tb + HALO, 1536), f32)],
        compiler_params=_cparams(("arbitrary",)),
    )(hbuf, hbuf, hbuf, conv_w8, alog_b, dtb_b, dq, dk, dv, dbe, dg)


def gdn_chunk_f(q, k, v, beta, g, state):
    c = CHUNK
    row = lax.broadcasted_iota(jnp.int32, (c, c), 0)
    col = lax.broadcasted_iota(jnp.int32, (c, c), 1)
    incl, strict, eye = row >= col, row > col, row == col
    gc_all = hdot(jnp.where(incl, 1.0, 0.0), g)
    ones = jnp.ones((c, c), f32)
    outs, states = [], []
    for h in range(H):
        sl = slice(DK * h, DK * (h + 1))
        qh, kh, vh, bh, gc = q[:, sl] * DK ** -0.5, k[:, sl], v[:, sl], beta[:, sl], gc_all[:, sl]
        sh = state[sl, :]
        gcol = gc[:, :c]
        grow = hdot(ones, jnp.where(eye, gcol, 0.0))
        decay = jnp.where(incl, jnp.exp(jnp.where(incl, gcol - grow, 0.0)), 0.0)
        kb = kh * bh
        a = jnp.where(strict, bdot_nt(kb, kh) * decay, 0.0)
        nn = -a
        bk = bdot(a, a)
        for step in range(5):
            nn = nn + bk + bdot(nn, bk)
            if step < 4:
                bk = bdot(bk, bk)
        eg = jnp.exp(gc)
        rhs_v, rhs_k = vh * bh, kb * eg
        u = rhs_v + bdot(nn, rhs_v)
        w = rhs_k + bdot(nn, rhs_k)
        attn = jnp.where(incl, bdot_nt(qh, kh) * decay, 0.0)
        g_last = gc[c - 1:c, :]
        v_new = u - bdot(w, sh)
        outs.append(bdot(qh * eg, sh) + bdot(attn, v_new))
        states.append(sh * jnp.exp(g_last) + bdot_tn(kh * jnp.exp(g_last - gc), v_new))
    return jnp.concatenate(outs, axis=1), jnp.concatenate(states, axis=0)


def gdn_core_fwd(q, k, v, beta, g):
    t = q.shape[0]
    nchunk = t // CHUNK

    def body(q_ref, k_ref, v_ref, be_ref, g_ref, o_ref, s_ref, state):
        @pl.when(pl.program_id(0) == 0)
        def _():
            state[...] = jnp.zeros_like(state)

        s_ref[0] = state[...]
        o, s_new = gdn_chunk_f(q_ref[...], k_ref[...], v_ref[...], be_ref[...], g_ref[...], state[...])
        o_ref[...] = o
        state[...] = s_new

    return pl.pallas_call(
        body, name="gdn_core_fwd", grid=(nchunk,),
        in_specs=[_row_spec(CHUNK, 512, 0)] * 5,
        out_specs=[_row_spec(CHUNK, 512, 0), pl.BlockSpec((1, 512, DK), lambda i: (i, 0, 0))],
        out_shape=[jax.ShapeDtypeStruct((t, 512), f32), jax.ShapeDtypeStruct((nchunk, 512, DK), f32)],
        scratch_shapes=[pltpu.VMEM((512, DK), f32)],
        compiler_params=_cparams(("arbitrary",)),
    )(q, k, v, beta, g)


def gdn_core_bwd(q, k, v, beta, g, states, do):
    t = q.shape[0]
    nchunk = t // CHUNK

    def body(q_ref, k_ref, v_ref, be_ref, g_ref, s_ref, do_ref, dq_ref, dk_ref, dv_ref, dbe_ref, dg_ref, dstate):
        @pl.when(pl.program_id(0) == 0)
        def _():
            dstate[...] = jnp.zeros_like(dstate)

        _, vf = jax.vjp(gdn_chunk_f, q_ref[...], k_ref[...], v_ref[...], be_ref[...], g_ref[...], s_ref[0])
        dq, dk, dv, dbe, dg, ds = vf((do_ref[...], dstate[...]))
        dq_ref[...], dk_ref[...], dv_ref[...], dbe_ref[...], dg_ref[...] = dq, dk, dv, dbe, dg
        dstate[...] = ds

    rev = pl.BlockSpec((CHUNK, 512), lambda s: (nchunk - 1 - s, 0))
    return pl.pallas_call(
        body, name="gdn_core_bwd", grid=(nchunk,),
        in_specs=[rev] * 5 + [pl.BlockSpec((1, 512, DK), lambda s: (nchunk - 1 - s, 0, 0)), rev],
        out_specs=[rev] * 5,
        out_shape=[jax.ShapeDtypeStruct((t, 512), f32)] * 5,
        scratch_shapes=[pltpu.VMEM((512, DK), f32)],
        compiler_params=_cparams(("arbitrary",)),
    )(q, k, v, beta, g, states, do)


NEG = -1e30


def _chunk_mask(i, j, tq, tk):
    qrow = lax.broadcasted_iota(jnp.int32, (tq, tk), 0) + i * tq
    kcol = lax.broadcasted_iota(jnp.int32, (tq, tk), 1) + j * tk
    return jnp.right_shift(kcol, 6) <= jnp.right_shift(qrow, 6)


def attn_fwd(q, k, v):
    t = q.shape[0]
    tq = tk = min(1024, t)
    nq, nk = t // tq, t // tk
    last = lambda i: ((i + 1) * tq - 1) // tk

    def body(q_ref, k_ref, v_ref, o_ref, lse_ref, m_s, l_s, acc):
        i, j = pl.program_id(1), pl.program_id(2)

        @pl.when(j == 0)
        def _():
            m_s[...] = jnp.full_like(m_s, NEG)
            l_s[...] = jnp.zeros_like(l_s)
            acc[...] = jnp.zeros_like(acc)

        @pl.when(j <= last(i))
        def _():
            s = _dot(q_ref[...], k_ref[...], 1, 1) * ATT_SCALE
            s = jnp.where(_chunk_mask(i, j, tq, tk), s, NEG)
            m_new = jnp.maximum(m_s[...], jnp.max(s, axis=-1, keepdims=True))
            p = jnp.exp(s - m_new)
            alpha = jnp.exp(m_s[...] - m_new)
            l_s[...] = alpha * l_s[...] + jnp.sum(p, axis=-1, keepdims=True)
            acc[...] = alpha * acc[...] + _dot(p.astype(MX), v_ref[...], 1, 0)
            m_s[...] = m_new

        @pl.when(j == nk - 1)
        def _():
            o_ref[...] = acc[...] / l_s[...]
            lse_ref[0] = m_s[...] + jnp.log(l_s[...])

    return pl.pallas_call(
        body, name="attn_fwd", grid=(H, nq, nk),
        in_specs=[pl.BlockSpec((tq, 256), lambda h, i, j: (i, h)),
                  pl.BlockSpec((tk, 256), lambda h, i, j: (jnp.minimum(j, last(i)), h)),
                  pl.BlockSpec((tk, DK), lambda h, i, j: (jnp.minimum(j, last(i)), h))],
        out_specs=[pl.BlockSpec((tq, DK), lambda h, i, j: (i, h)), pl.BlockSpec((1, tq, 1), lambda h, i, j: (h, i, 0))],
        out_shape=[jax.ShapeDtypeStruct((t, 512), f32), jax.ShapeDtypeStruct((H, t, 1), f32)],
        scratch_shapes=[pltpu.VMEM((tq, 1), f32), pltpu.VMEM((tq, 1), f32), pltpu.VMEM((tq, DK), f32)],
        compiler_params=_cparams(("parallel", "parallel", "arbitrary")),
    )(q, k, v)


def attn_bwd(q, k, v, o, lse, do):
    t = q.shape[0]
    tq = tk = min(512, t)
    nq, nk = t // tq, t // tk
    first = lambda j: (j * tk) // tq

    def body(q_ref, k_ref, v_ref, o_ref, lse_ref, do_ref, dq_ref, dk_ref, dv_ref, dk_acc, dv_acc):
        j, i = pl.program_id(1), pl.program_id(2)

        @pl.when((j == 0) & (i == 0))
        def _():
            dq_ref[...] = jnp.zeros_like(dq_ref)

        @pl.when(i == 0)
        def _():
            dk_acc[...] = jnp.zeros_like(dk_acc)
            dv_acc[...] = jnp.zeros_like(dv_acc)

        @pl.when(i >= first(j))
        def _():
            qb, kb, vb = q_ref[...], k_ref[...], v_ref[...]
            dob = do_ref[...]
            s = _dot(qb, kb, 1, 1) * ATT_SCALE
            s = jnp.where(_chunk_mask(i, j, tq, tk), s, NEG)
            p = jnp.exp(s - lse_ref[0])
            dv_acc[...] += _dot(p.astype(MX), dob.astype(MX), 0, 0)
            dp = _dot(dob.astype(MX), vb, 1, 1)
            delta = jnp.sum(dob * o_ref[...], axis=-1, keepdims=True)
            ds = (p * (dp - delta) * ATT_SCALE).astype(MX)
            dk_acc[...] += _dot(ds, qb, 0, 0)
            rows = pl.ds(pl.multiple_of(i * tq, tq), tq)
            dq_ref[rows, :] += _dot(ds, kb, 1, 0)

        @pl.when(i == nq - 1)
        def _():
            dk_ref[...] = dk_acc[...]
            dv_ref[...] = dv_acc[...]

    qi = lambda h, j, i: (jnp.maximum(i, first(j)), h)
    return pl.pallas_call(
        body, name="attn_bwd", grid=(H, nk, nq),
        in_specs=[pl.BlockSpec((tq, 256), qi),
                  pl.BlockSpec((tk, 256), lambda h, j, i: (j, h)),
                  pl.BlockSpec((tk, DK), lambda h, j, i: (j, h)),
                  pl.BlockSpec((tq, DK), qi),
                  pl.BlockSpec((1, tq, 1), lambda h, j, i: (h, jnp.maximum(i, first(j)), 0)),
                  pl.BlockSpec((tq, DK), qi)],
        out_specs=[pl.BlockSpec((t, 256), lambda h, j, i: (0, h)),
                   pl.BlockSpec((tk, 256), lambda h, j, i: (j, h)),
                   pl.BlockSpec((tk, DK), lambda h, j, i: (j, h))],
        out_shape=[jax.ShapeDtypeStruct((t, 1024), f32), jax.ShapeDtypeStruct((t, 1024), f32),
                   jax.ShapeDtypeStruct((t, 512), f32)],
        scratch_shapes=[pltpu.VMEM((tk, 256), f32), pltpu.VMEM((tk, DK), f32)],
        compiler_params=_cparams(("arbitrary", "arbitrary", "arbitrary")),
    )(q, k, v, o, lse, do)


def ffn_up(x1, wg, wu):
    t = x1.shape[0]
    tm, tn = min(512, t), _pick(DFF, 1536)

    def body(x_ref, wg_ref, wu_ref, g_ref, u_ref, a_ref):
        xb = x_ref[...].astype(MX)
        g = _dot(xb, wg_ref[...], 1, 0)
        u = _dot(xb, wu_ref[...], 1, 0)
        g_ref[...] = g.astype(g_ref.dtype)
        u_ref[...] = u.astype(u_ref.dtype)
        a_ref[...] = (g * jax.nn.sigmoid(g) * u).astype(a_ref.dtype)

    w_spec = pl.BlockSpec((D, tn), lambda i, j: (0, j))
    o_spec = pl.BlockSpec((tm, tn), lambda i, j: (i, j))
    return pl.pallas_call(
        body, name="ffn_up", grid=(t // tm, DFF // tn),
        in_specs=[pl.BlockSpec((tm, D), lambda i, j: (i, 0)), w_spec, w_spec],
        out_specs=[o_spec] * 3, out_shape=[jax.ShapeDtypeStruct((t, DFF), bf16)] * 3,
        compiler_params=_cparams(("parallel", "parallel")),
    )(x1, wg, wu)


def ffn_dact(ddown, wd, g, u):
    t = ddown.shape[0]
    tm, tn = min(512, t), _pick(DFF, 1536)

    def body(dd_ref, wd_ref, g_ref, u_ref, dg_ref, du_ref):
        dact = _dot(dd_ref[...].astype(MX), wd_ref[...], 1, 1)
        gv, uv = g_ref[...].astype(f32), u_ref[...].astype(f32)
        sig = jax.nn.sigmoid(gv)
        dg_ref[...] = (dact * uv * sig * (1.0 + gv * (1.0 - sig))).astype(dg_ref.dtype)
        du_ref[...] = (dact * gv * sig).astype(du_ref.dtype)

    o_spec = pl.BlockSpec((tm, tn), lambda i, j: (i, j))
    return pl.pallas_call(
        body, name="ffn_dact", grid=(t // tm, DFF // tn),
        in_specs=[pl.BlockSpec((tm, D), lambda i, j: (i, 0)), pl.BlockSpec((tn, D), lambda i, j: (j, 0)), o_spec, o_spec],
        out_specs=[o_spec] * 2, out_shape=[jax.ShapeDtypeStruct((t, DFF), bf16)] * 2,
        compiler_params=_cparams(("parallel", "parallel")),
    )(ddown, wd, g, u)


def loss_head(y, target):
    t = y.shape[0]
    tb = min(ROW_TB, t)
    n = t // tb

    def body(y_ref, t_ref, dy_ref, loss_ref, acc):
        i = pl.program_id(0)

        @pl.when(i == 0)
        def _():
            acc[...] = jnp.zeros_like(acc)

        e = y_ref[...] - t_ref[...]
        dy_ref[...] = e * (1.0 / D)
        acc[...] += jnp.sum(e * e, axis=0, keepdims=True)

        @pl.when(i == n - 1)
        def _():
            loss_ref[...] = jnp.sum(acc[...], axis=1, keepdims=True) * (0.5 / D)

    return pl.pallas_call(
        body, name="loss_head", grid=(n,),
        in_specs=[_row_spec(tb, D, 0)] * 2,
        out_specs=[_row_spec(tb, D, 0), _full_spec((1, 1))],
        out_shape=[jax.ShapeDtypeStruct((t, D), f32), jax.ShapeDtypeStruct((1, 1), f32)],
        scratch_shapes=[pltpu.VMEM((1, D), f32)],
        compiler_params=_cparams(("arbitrary",)),
    )(y, target)


def _me_and_peers():
    x, y, c = lax.axis_index("x"), lax.axis_index("y"), lax.axis_index("c")
    me = 4 * x + 2 * y + c
    peers = []
    for kk in range(1, N_DEV):
        px = 1 - x if kk & 4 else x
        py = 1 - y if kk & 2 else y
        pc = 1 - c if kk & 1 else c
        peers.append(((px, py, pc), 4 * px + 2 * py + pc))
    return me, peers


_ANY = pl.BlockSpec(memory_space=pl.ANY)


def all_gather_rows(name, x):
    def body(x_ref, out_ref, send_sems, recv_sems, local_sem):
        me, peers = _me_and_peers()
        mine = pltpu.make_async_copy(x_ref, out_ref.at[me], local_sem)
        mine.start()
        copies = [pltpu.make_async_remote_copy(src_ref=x_ref, dst_ref=out_ref.at[me], send_sem=send_sems.at[kk],
                                               recv_sem=recv_sems.at[kk], device_id=dev, device_id_type=pl.DeviceIdType.MESH)
                  for kk, (dev, _) in enumerate(peers)]
        for cp in copies:
            cp.start()
        for cp in copies:
            cp.wait()
        mine.wait()

    return pl.pallas_call(
        body, name=name, in_specs=[_ANY], out_specs=_ANY,
        out_shape=jax.ShapeDtypeStruct((N_DEV,) + x.shape, x.dtype),
        scratch_shapes=[pltpu.SemaphoreType.DMA((N_DEV - 1,)), pltpu.SemaphoreType.DMA((N_DEV - 1,)), pltpu.SemaphoreType.DMA],
    )(x)


def exchange_shards(send):
    def body(s_ref, out_ref, send_sems, recv_sems, local_sem):
        me, peers = _me_and_peers()
        mine = pltpu.make_async_copy(s_ref.at[me], out_ref.at[me], local_sem)
        mine.start()
        copies = [pltpu.make_async_remote_copy(src_ref=s_ref.at[pid], dst_ref=out_ref.at[me], send_sem=send_sems.at[kk],
                                               recv_sem=recv_sems.at[kk], device_id=dev, device_id_type=pl.DeviceIdType.MESH)
                  for kk, (dev, pid) in enumerate(peers)]
        for cp in copies:
            cp.start()
        for cp in copies:
            cp.wait()
        mine.wait()

    return pl.pallas_call(
        body, name="exchange_grad_shards", in_specs=[_ANY], out_specs=_ANY,
        out_shape=jax.ShapeDtypeStruct(send.shape, send.dtype),
        scratch_shapes=[pltpu.SemaphoreType.DMA((N_DEV - 1,)), pltpu.SemaphoreType.DMA((N_DEV - 1,)), pltpu.SemaphoreType.DMA],
    )(send)


def adamw_rows(parts, w, m, v):
    r = w.shape[0]
    tb = _pick_rows(r)

    def body(p_ref, w_ref, m_ref, v_ref, g_ref, d_ref, nm_ref, nv_ref):
        g = p_ref[0]
        for s in range(1, N_DEV):
            g = g + p_ref[s]
        nm = ADAM_B1 * m_ref[...] + (1.0 - ADAM_B1) * g
        nv = ADAM_B2 * v_ref[...] + (1.0 - ADAM_B2) * jnp.square(g)
        m_hat = nm / (1.0 - ADAM_B1 ** ADAM_STEP)
        v_hat = nv / (1.0 - ADAM_B2 ** ADAM_STEP)
        g_ref[...] = g
        d_ref[...] = -ADAM_LR * (m_hat / (jnp.sqrt(v_hat) + ADAM_EPS) + ADAM_WD * w_ref[...])
        nm_ref[...] = nm
        nv_ref[...] = nv

    spec = pl.BlockSpec((tb, D), lambda i: (i, 0))
    return pl.pallas_call(
        body, name="adamw", grid=(r // tb,),
        in_specs=[pl.BlockSpec((N_DEV, tb, D), lambda i: (0, i, 0)), spec, spec, spec],
        out_specs=[spec] * 4, out_shape=[jax.ShapeDtypeStruct((r, D), f32)] * 4,
        compiler_params=_cparams(("parallel",)),
    )(parts, w, m, v)


def _pick_rows(r):
    return max(tb for tb in range(8, ROW_TB + 1, 8) if r % tb == 0)


BIG = ("w_in", "w_uq", "w_ukv", "w_out", "w_gate_up", "w_down", "w_ple", "w_ple_gate")
COL_SHARDED = ("w_in", "w_uq", "w_ukv", "w_gate_up", "w_ple")
SMALL = ("conv_w", "a_log", "dt_bias", "gdn_norm_g", "q_norm_g", "kv_norm_g", "ln1_g", "ln1_b", "ln2_g", "ln2_b")
FULL_SHAPE = {"w_in": (D, IN_W), "w_uq": (QL, 768), "w_ukv": (KVL, 1024), "w_out": (D, D), "w_gate_up": (D, 2 * DFF),
              "w_down": (DFF, D), "w_ple": (PLE, D), "w_ple_gate": (D, D)}
ROW_ALIGN = 16


def _pad_rows(a, mult=ROW_ALIGN, axis=0):
    pad = (-a.shape[axis]) % mult
    widths = [(0, 0)] * a.ndim
    widths[axis] = (0, pad)
    return a if pad == 0 else jnp.pad(a, widths)


def _pack_big_local(tree, dtype):
    pieces = [_pad_rows(tree[n][l].reshape(-1, D).astype(dtype)) for l in range(DEPTH) for n in BIG]
    return jnp.concatenate(pieces, axis=0)


def _big_rows(shards):
    rows, off = {}, 0
    for l in range(DEPTH):
        for n in BIG:
            r = shards[n][1] * shards[n][2] // D
            rows[(l, n)] = (off, r)
            off += r + (-r) % ROW_ALIGN
    return rows, off


def _unpack_gathered(gathered, shards, l, n):
    off, r = _big_rows(shards)[0][(l, n)]
    _, sr, sc = shards[n]
    blk = gathered[:, off:off + r, :].reshape(N_DEV, sr, sc)
    if n in COL_SHARDED:
        return jnp.transpose(blk, (1, 0, 2)).reshape(sr, N_DEV * sc)
    return blk.reshape(N_DEV * sr, sc)


def _shard_grad(gfull, n, shards):
    _, sr, sc = shards[n]
    if n in COL_SHARDED:
        blk = jnp.transpose(gfull.reshape(sr, N_DEV, sc), (1, 0, 2))
    else:
        blk = gfull.reshape(N_DEV, sr, sc)
    return _pad_rows(blk.reshape(N_DEV, sr * sc // D, D), axis=1)


def _pack_small(tree, extra):
    flat = jnp.concatenate([tree[n].reshape(-1).astype(f32) for n in SMALL] + [extra.reshape(-1).astype(f32)])
    return jnp.pad(flat, (0, (-flat.shape[0]) % (ROW_ALIGN * D))).reshape(-1, D)


def _unpack_small(rows, like):
    flat = rows.reshape(-1)
    out, off = {}, 0
    for n in SMALL:
        sz = int(np.prod(like[n].shape))
        out[n] = flat[off:off + sz].reshape(like[n].shape)
        off += sz
    return out, flat[off]


def _zeros(r, c, dt):
    return jnp.zeros((r, c), dt)


def _prep_w_in(w):
    dt = w.dtype
    kr = w[:, 2696:2760]
    return jnp.concatenate([
        w[:, 0:2048],
        w[:, 2440:2696],
        kr, w[:, 2048:2056], _zeros(D, 56, dt),
        kr[:, 32:], kr[:, :32], _zeros(D, 64, dt),
        w[:, 2056:2440], _zeros(D, CQP - QL, dt)], axis=1)


def _unprep_w_in(g):
    krs = g[:, OFF_KRS:OFF_KRS + 64]
    kr = g[:, OFF_KR:OFF_KR + 64] + jnp.concatenate([krs[:, 32:], krs[:, :32]], axis=1)
    return jnp.concatenate([g[:, 0:2048], g[:, OFF_KR + 64:OFF_KR + 72], g[:, OFF_CQ:OFF_CQ + QL],
                            g[:, OFF_CKV:OFF_CKV + KVL], kr], axis=1)


def _prep_w_uq(w):
    dt = w.dtype
    z64 = _zeros(QL, 64, dt)
    nope, ra, rb = [], [], []
    for h in range(H):
        nope.append(w[:, 192 * h:192 * h + 128])
        x1, x2 = w[:, 192 * h + 128:192 * h + 160], w[:, 192 * h + 160:192 * h + 192]
        ra += [x1, x2, z64]
        rb += [x2, x1, z64]
    return jnp.concatenate([jnp.concatenate(nope + ra + rb, axis=1), _zeros(CQP - QL, 1536, dt)], axis=0)


def _unprep_w_uq(g):
    g = g[:QL]
    cols = []
    for h in range(H):
        a = g[:, 512 + 128 * h:512 + 128 * h + 64]
        b = g[:, 1024 + 128 * h:1024 + 128 * h + 64]
        cols += [g[:, 128 * h:128 * (h + 1)], a[:, :32] + b[:, 32:], a[:, 32:] + b[:, :32]]
    return jnp.concatenate(cols, axis=1)


def _rope_tables(positions):
    inv_freq = ROPE_THETA ** (-jnp.arange(0, ROPE, 2, dtype=f32) / ROPE)
    ang = positions.astype(f32)[:, None] * inv_freq
    c, s = jnp.cos(ang), jnp.sin(ang)
    z = jnp.zeros((positions.shape[0], 64), f32)
    return jnp.concatenate([c, c, z], axis=1), jnp.concatenate([-s, s, z], axis=1)


def _tile_heads(vec, n=H):
    return jnp.tile(vec.reshape(1, -1), (1, n))


def _bcast_heads(vec):
    return jnp.repeat(vec, DK).reshape(1, H * DK)


def kernel(x, p, positions, w_in, conv_w, a_log, dt_bias, gdn_norm_g, q_norm_g, w_uq, kv_norm_g, w_ukv, w_out, ln1_g, ln1_b, w_gate_up, w_down, ln2_g, ln2_b, w_ple, w_ple_gate, loss_target, m_w_in, m_conv_w, m_a_log, m_dt_bias, m_gdn_norm_g, m_q_norm_g, m_w_uq, m_kv_norm_g, m_w_ukv, m_w_out, m_ln1_g, m_ln1_b, m_w_gate_up, m_w_down, m_ln2_g, m_ln2_b, m_w_ple, m_w_ple_gate, v_w_in, v_conv_w, v_a_log, v_dt_bias, v_gdn_norm_g, v_q_norm_g, v_w_uq, v_kv_norm_g, v_w_ukv, v_w_out, v_ln1_g, v_ln1_b, v_w_gate_up, v_w_down, v_ln2_g, v_ln2_b, v_w_ple, v_w_ple_gate):
    W = dict(w_in=w_in, conv_w=conv_w, a_log=a_log, dt_bias=dt_bias, gdn_norm_g=gdn_norm_g, q_norm_g=q_norm_g, w_uq=w_uq,
             kv_norm_g=kv_norm_g, w_ukv=w_ukv, w_out=w_out, ln1_g=ln1_g, ln1_b=ln1_b, w_gate_up=w_gate_up, w_down=w_down,
             ln2_g=ln2_g, ln2_b=ln2_b, w_ple=w_ple, w_ple_gate=w_ple_gate)
    M = dict(w_in=m_w_in, conv_w=m_conv_w, a_log=m_a_log, dt_bias=m_dt_bias, gdn_norm_g=m_gdn_norm_g, q_norm_g=m_q_norm_g,
             w_uq=m_w_uq, kv_norm_g=m_kv_norm_g, w_ukv=m_w_ukv, w_out=m_w_out, ln1_g=m_ln1_g, ln1_b=m_ln1_b,
             w_gate_up=m_w_gate_up, w_down=m_w_down, ln2_g=m_ln2_g, ln2_b=m_ln2_b, w_ple=m_w_ple, w_ple_gate=m_w_ple_gate)
    V = dict(w_in=v_w_in, conv_w=v_conv_w, a_log=v_a_log, dt_bias=v_dt_bias, gdn_norm_g=v_gdn_norm_g, q_norm_g=v_q_norm_g,
             w_uq=v_w_uq, kv_norm_g=v_kv_norm_g, w_ukv=v_w_ukv, w_out=v_w_out, ln1_g=v_ln1_g, ln1_b=v_ln1_b,
             w_gate_up=v_w_gate_up, w_down=v_w_down, ln2_g=v_ln2_g, ln2_b=v_ln2_b, w_ple=v_w_ple, w_ple_gate=v_w_ple_gate)
    shards = {n: W[n].shape for n in BIG}
    t = x.shape[1]
    xin = x.reshape(t, D)
    target = loss_target.reshape(t, D)
    cf, sf = _rope_tables(positions.reshape(t))

    gathered = all_gather_rows("all_gather_weights", _pad_rows(_pack_big_local(W, bf16)))
    conv_g = all_gather_rows("all_gather_conv", _pad_rows(jnp.pad(conv_w.reshape(-1), (0, (-conv_w.size) % D)).reshape(-1, D), 8))
    csz = conv_w.shape[1] * conv_w.shape[2]
    conv_full = jnp.transpose(conv_g.reshape(N_DEV, -1)[:, :DEPTH * csz].reshape(N_DEV, DEPTH, CW, -1), (1, 2, 0, 3)).reshape(DEPTH, CW, 1536)

    full = lambda l, n: _unpack_gathered(gathered, shards, l, n)

    acts = []
    h_cur = xin
    for l in range(DEPTH):
        win = _prep_w_in(full(l, "w_in"))
        wq = _prep_w_uq(full(l, "w_uq"))
        wkv, wout = full(l, "w_ukv"), full(l, "w_out")
        wgu, wd = full(l, "w_gate_up"), full(l, "w_down")
        wg, wu = wgu[:, :DFF], wgu[:, DFF:]
        wple, wpg = full(l, "w_ple"), full(l, "w_ple_gate")
        cw8 = jnp.concatenate([conv_full[l], jnp.zeros((8 - CW, 1536), f32)], axis=0)
        alog_b, dtb_b = _bcast_heads(a_log[l]), _bcast_heads(dt_bias[l])
        gg_b = _tile_heads(gdn_norm_g[l])
        qg = jnp.concatenate([q_norm_g[l], jnp.zeros((CQP - QL,), f32)]).reshape(1, CQP)
        kvg = kv_norm_g[l].reshape(1, KVL)
        g1, b1, g2, b2 = (a[l].reshape(1, D) for a in (ln1_g, ln1_b, ln2_g, ln2_b))
        p_l = p[l].reshape(t, PLE)

        hb = mm("in_proj", h_cur, win, mode="nn", out_dtype=f32)
        qn, kn, vs, beta, glog = gdn_pre_fwd(hb, cw8, alog_b, dtb_b)
        o_gdn, states = gdn_core_fwd(qn, kn, vs, beta, glog)
        mla_rows = [(hb, CQP, OFF_CQ // CQP), (hb, KVL, OFF_CKV // KVL), (hb, LANE, OFF_KR // LANE), (hb, LANE, OFF_KRS // LANE),
                    (cf, LANE, 0), (sf, LANE, 0)]
        mla_consts = [qg, wq, kvg, wkv]
        qm, km, vm = row_fwd("mla_pre", mla_pre_f, mla_rows, mla_consts, [(1024, bf16), (1024, bf16), (512, bf16)])
        o_mla, lse = attn_fwd(qm, km, vm)
        p1_rows = [(o_gdn, 512, 0), (hb, 512, OFF_Z // 512), (o_mla, 512, 0), (h_cur, D, 0)]
        p1_consts = [gg_b, wout, g1, b1]
        (x1,) = row_fwd("post1", post1_f, p1_rows, p1_consts, [(D, f32)])
        gate, up, act = ffn_up(x1, wg, wu)
        down = mm("ffn_down", act, wd, mode="nn", out_dtype=f32)
        p2_rows = [(x1, D, 0), (down, D, 0), (p_l, PLE, 0)]
        p2_consts = [g2, b2, wpg, wple]
        (y,) = row_fwd("post2", post2_f, p2_rows, p2_consts, [(D, f32)])
        acts.append(dict(x=h_cur, hb=hb, qn=qn, kn=kn, vs=vs, beta=beta, glog=glog, states=states, o_gdn=o_gdn, qm=qm, km=km,
                         vm=vm, o_mla=o_mla, lse=lse, x1=x1, gate=gate, up=up, act=act, win=win, wd=wd, wg=wg, wu=wu,
                         cw8=cw8, alog_b=alog_b, dtb_b=dtb_b, mla_rows=mla_rows, mla_consts=mla_consts, p1_rows=p1_rows,
                         p1_consts=p1_consts, p2_rows=p2_rows, p2_consts=p2_consts))
        h_cur = y

    dy, loss_part = loss_head(h_cur, target)

    G = {n: [None] * DEPTH for n in list(BIG) + list(SMALL)}
    for l in reversed(range(DEPTH)):
        a = acts[l]
        dx1, ddown, dg2, db2, dwpg, dwple = row_bwd("post2_bwd", post2_f, a["p2_rows"], a["p2_consts"], [dy], [0, 1], [0, 1, 2, 3], [f32, f32])
        dgate, dup = ffn_dact(ddown, a["wd"], a["gate"], a["up"])
        dwd = mm("dw_down", a["act"], ddown, mode="tn", out_dtype=f32)
        dx1 = mm("dx_gate", dgate, a["wg"], mode="nt", out_dtype=f32, add=dx1)
        dx1 = mm("dx_up", dup, a["wu"], mode="nt", out_dtype=f32, add=dx1)
        dwg = mm("dw_gate", a["x1"], dgate, mode="tn", out_dtype=f32)
        dwu = mm("dw_up", a["x1"], dup, mode="tn", out_dtype=f32)
        do_gdn, dz, do_mla, dxr, dgg, dwout, dg1, db1 = row_bwd("post1_bwd", post1_f, a["p1_rows"], a["p1_consts"], [dx1], [0, 1, 2, 3],
                                                              [0, 1, 2, 3], [f32, bf16, f32, f32])
        dqm, dkm, dvm = attn_bwd(a["qm"], a["km"], a["vm"], a["o_mla"], a["lse"], do_mla)
        dqn, dkn, dvs, dbeta, dglog = gdn_core_bwd(a["qn"], a["kn"], a["vs"], a["beta"], a["glog"], a["states"], do_gdn)
        dqkv, dba, dcw, dal, ddt = gdn_pre_bwd(a["hb"], a["cw8"], a["alog_b"], a["dtb_b"], dqn, dkn, dvs, dbeta, dglog)
        dcq, dckv, dkrg, dkrs, dqg, dwq, dkvg, dwkv = row_bwd("mla_pre_bwd", mla_pre_f, a["mla_rows"], a["mla_consts"], [dqm, dkm, dvm],
                                                              [0, 1, 2, 3], [0, 1, 2, 3], [bf16, bf16, bf16, bf16], row_add={2: dba})
        dh = jnp.concatenate([dqkv, dz, dckv, dkrg, dkrs, dcq], axis=1)
        dwin = mm("dw_in", a["x"], dh, mode="tn", out_dtype=f32)
        dy = mm("dx_in", dh, a["win"], mode="nt", out_dtype=f32, add=dxr)

        G["w_in"][l] = _unprep_w_in(dwin)
        G["w_uq"][l] = _unprep_w_uq(dwq)
        G["w_ukv"][l], G["w_out"][l], G["w_down"][l] = dwkv, dwout, dwd
        G["w_gate_up"][l] = jnp.concatenate([dwg, dwu], axis=1)
        G["w_ple"][l], G["w_ple_gate"][l] = dwple, dwpg
        G["conv_w"][l] = dcw[:CW]
        G["a_log"][l] = jnp.sum(dal.reshape(H, DK), axis=1)
        G["dt_bias"][l] = jnp.sum(ddt.reshape(H, DK), axis=1)
        G["gdn_norm_g"][l] = jnp.sum(dgg.reshape(H, DK), axis=0)
        G["q_norm_g"][l] = dqg[0, :QL]
        G["kv_norm_g"][l] = dkvg[0]
        G["ln1_g"][l], G["ln1_b"][l], G["ln2_g"][l], G["ln2_b"][l] = dg1[0], db1[0], dg2[0], db2[0]
    grad_x = dy.reshape(x.shape)

    small_like = {n: W[n] for n in SMALL}
    conv_parts = jnp.stack(G["conv_w"]).reshape(DEPTH, CW, N_DEV, -1)
    smalls = []
    for d in range(N_DEV):
        tree = {n: jnp.stack(G[n]) for n in SMALL if n != "conv_w"}
        tree["conv_w"] = conv_parts[:, :, d, :]
        smalls.append(_pack_small(tree, loss_part))
    small_send = jnp.stack(smalls)
    big_send = jnp.concatenate([_shard_grad(G[n][l], n, shards) for l in range(DEPTH) for n in BIG], axis=1)
    n_big = big_send.shape[1]
    tail = (-(n_big + small_send.shape[1])) % ROW_TB
    send = jnp.concatenate([big_send, small_send, jnp.zeros((N_DEV, tail, D), f32)], axis=1)
    parts = exchange_shards(send)

    zero = jnp.zeros((), f32)
    pack3 = lambda tree: jnp.concatenate([_pack_big_local(tree, f32), _pack_small(tree, zero), jnp.zeros((tail, D), f32)], axis=0)
    g_rows, d_rows, m_rows, v_rows = adamw_rows(parts, pack3(W), pack3(M), pack3(V))

    def unpack(rows_arr):
        out = {}
        rows, _ = _big_rows(shards)
        for n in BIG:
            out[n] = jnp.stack([rows_arr[rows[(l, n)][0]:rows[(l, n)][0] + rows[(l, n)][1]].reshape(shards[n][1:]) for l in range(DEPTH)])
        sm, extra = _unpack_small(rows_arr[n_big:], small_like)
        out.update(sm)
        return out, extra

    g_out, loss = unpack(g_rows)
    d_out, _ = unpack(d_rows)
    m_out, _ = unpack(m_rows)
    v_out, _ = unpack(v_rows)
    order = ["w_in", "conv_w", "a_log", "dt_bias", "gdn_norm_g", "q_norm_g", "w_uq", "kv_norm_g", "w_ukv", "w_out", "ln1_g", "ln1_b",
             "w_gate_up", "w_down", "ln2_g", "ln2_b", "w_ple", "w_ple_gate"]
    return (loss, grad_x, *[g_out[n] for n in order], *[d_out[n] for n in order], *[m_out[n] for n in order],
            *[v_out[n] for n in order])
```

```python
import functools

import jax
import jax.numpy as jnp
import numpy as np
from jax import lax
from jax.experimental import pallas as pl
from jax.experimental.pallas import tpu as pltpu

f32 = jnp.float32
bf16 = jnp.bfloat16
MX = jnp.bfloat16

D = 1024
DEPTH = 4
CHUNK = 64
H = 4
DK = 128
PLE = 256
QL = 384
KVL = 256
ROPE = 64
DFF = 2816
IN_W = 2760
ROPE_THETA = 10000.0
ALPHA = (2.0 * DEPTH) ** 0.25
LN_EPS = 1e-5
RMS_EPS = 1e-6
ATT_SCALE = (128 + 64) ** -0.5
N_DEV = 8

ADAM_LR, ADAM_B1, ADAM_B2, ADAM_EPS, ADAM_WD, ADAM_STEP = 0.001, 0.9, 0.999, 1e-08, 0.01, 10

OFF_QKV, OFF_Z, OFF_CKV, OFF_KR, OFF_KRS, OFF_CQ, HP = 0, 1536, 2048, 2304, 2432, 2560, 3072
CQP = 512
LANE = 128
VMEM_LIMIT = 48 * 1024 * 1024
ROW_TB = 256


def _dot(a, b, ca, cb, prec=None):
    if a.ndim == 3:
        return lax.dot_general(a, b, (((ca + 1,), (cb + 1,)), ((0,), (0,))), precision=prec, preferred_element_type=f32)
    return lax.dot_general(a, b, (((ca,), (cb,)), ((), ())), precision=prec, preferred_element_type=f32)


@jax.custom_vjp
def bdot(a, w):
    return _dot(a.astype(MX), w.astype(MX), 1, 0)


def _bdot_fwd(a, w):
    return bdot(a, w), (a, w)


def _bdot_bwd(res, g):
    a, w = res
    gb = g.astype(MX)
    return _dot(gb, w.astype(MX), 1, 1).astype(a.dtype), _dot(a.astype(MX), gb, 0, 0).astype(w.dtype)


bdot.defvjp(_bdot_fwd, _bdot_bwd)


@jax.custom_vjp
def bdot_nt(a, b):
    return _dot(a.astype(MX), b.astype(MX), 1, 1)


def _bdot_nt_fwd(a, b):
    return bdot_nt(a, b), (a, b)


def _bdot_nt_bwd(res, g):
    a, b = res
    gb = g.astype(MX)
    return _dot(gb, b.astype(MX), 1, 0).astype(a.dtype), _dot(gb, a.astype(MX), 0, 0).astype(b.dtype)


bdot_nt.defvjp(_bdot_nt_fwd, _bdot_nt_bwd)


@jax.custom_vjp
def bdot_tn(a, b):
    return _dot(a.astype(MX), b.astype(MX), 0, 0)


def _bdot_tn_fwd(a, b):
    return bdot_tn(a, b), (a, b)


def _bdot_tn_bwd(res, g):
    a, b = res
    gb = g.astype(MX)
    return _dot(b.astype(MX), gb, 1, 1).astype(a.dtype), _dot(a.astype(MX), gb, 1, 0).astype(b.dtype)


bdot_tn.defvjp(_bdot_tn_fwd, _bdot_tn_bwd)


@jax.custom_vjp
def hdot(a, b):
    return _dot(a, b, 1, 0, lax.Precision.HIGHEST)


def _hdot_fwd(a, b):
    return hdot(a, b), (a, b)


def _hdot_bwd(res, g):
    a, b = res
    return _dot(g, b, 1, 1, lax.Precision.HIGHEST), _dot(a, g, 0, 0, lax.Precision.HIGHEST)


hdot.defvjp(_hdot_fwd, _hdot_bwd)


def _pick(n, cap):
    best = None
    for t in range(LANE, min(n, cap) + 1, LANE):
        if n % t == 0:
            best = t
    assert best is not None, (n, cap)
    return best


def _cparams(sem):
    return pltpu.CompilerParams(dimension_semantics=sem, vmem_limit_bytes=VMEM_LIMIT)


def mm(name, a, b, *, mode, out_dtype, add=None):
    if mode == "tn":
        kdim, m = a.shape
        n = b.shape[1]
        tm, tn, tk = _pick(m, 1536), _pick(n, 1536), min(512, kdim)
        nk = kdim // tk

        def body(a_ref, b_ref, o_ref, acc):
            k = pl.program_id(2)

            @pl.when(k == 0)
            def _():
                acc[...] = jnp.zeros_like(acc)

            acc[...] += _dot(a_ref[...].astype(MX), b_ref[...].astype(MX), 0, 0)

            @pl.when(k == nk - 1)
            def _():
                o_ref[...] = acc[...].astype(o_ref.dtype)

        return pl.pallas_call(
            body, name=name, grid=(m // tm, n // tn, nk),
            in_specs=[pl.BlockSpec((tk, tm), lambda i, j, k: (k, i)), pl.BlockSpec((tk, tn), lambda i, j, k: (k, j))],
            out_specs=pl.BlockSpec((tm, tn), lambda i, j, k: (i, j)),
            out_shape=jax.ShapeDtypeStruct((m, n), out_dtype),
            scratch_shapes=[pltpu.VMEM((tm, tn), f32)],
            compiler_params=_cparams(("parallel", "parallel", "arbitrary")),
        )(a, b)

    m, kdim = a.shape
    n = b.shape[1] if mode == "nn" else b.shape[0]
    tm, tn = min(512, m), _pick(n, 1536)
    has_add = add is not None

    def body(*refs):
        a_ref, b_ref = refs[0], refs[1]
        o_ref = refs[-1]
        r = _dot(a_ref[...].astype(MX), b_ref[...].astype(MX), 1, 0 if mode == "nn" else 1)
        if has_add:
            r = r + refs[2][...].astype(f32)
        o_ref[...] = r.astype(o_ref.dtype)

    b_spec = (pl.BlockSpec((kdim, tn), lambda i, j: (0, j)) if mode == "nn"
              else pl.BlockSpec((tn, kdim), lambda i, j: (j, 0)))
    in_specs = [pl.BlockSpec((tm, kdim), lambda i, j: (i, 0)), b_spec]
    args = [a, b]
    if has_add:
        in_specs.append(pl.BlockSpec((tm, tn), lambda i, j: (i, j)))
        args.append(add)
    return pl.pallas_call(
        body, name=name, grid=(m // tm, n // tn), in_specs=in_specs,
        out_specs=pl.BlockSpec((tm, tn), lambda i, j: (i, j)),
        out_shape=jax.ShapeDtypeStruct((m, n), out_dtype),
        compiler_params=_cparams(("parallel", "parallel")),
    )(*args)


def _row_spec(tb, w, cb):
    return pl.BlockSpec((tb, w), lambda i: (i, cb))


def _full_spec(shape):
    return pl.BlockSpec(shape, lambda i: (0,) * len(shape))


def row_fwd(name, f, rows, consts, outs):
    t = rows[0][0].shape[0]
    tb = min(ROW_TB, t)
    nr, nc = len(rows), len(consts)

    def body(*refs):
        vals = [r[...] for r in refs[:nr + nc]]
        res = f(*vals)
        for o_ref, val in zip(refs[nr + nc:], res):
            o_ref[...] = val.astype(o_ref.dtype)

    return pl.pallas_call(
        body, name=name, grid=(t // tb,),
        in_specs=[_row_spec(tb, w, cb) for _, w, cb in rows] + [_full_spec(c.shape) for c in consts],
        out_specs=[_row_spec(tb, w, 0) for w, _ in outs],
        out_shape=[jax.ShapeDtypeStruct((t, w), dt) for w, dt in outs],
        compiler_params=_cparams(("parallel",)),
    )(*[r[0] for r in rows], *consts)


def row_bwd(name, f, rows, consts, cots, row_diff, const_diff, drow_dtypes, row_add=None):
    t = rows[0][0].shape[0]
    tb = min(ROW_TB, t)
    nr, nc, nct = len(rows), len(consts), len(cots)
    row_add = row_add or {}
    add_keys = sorted(row_add)
    n_in = nr + nc + nct + len(add_keys)

    def body(*refs):
        i = pl.program_id(0)
        rv = [r[...] for r in refs[:nr]]
        cv = [refs[nr + k][...].astype(f32) if k in const_diff else refs[nr + k][...] for k in range(nc)]
        cot_refs = refs[nr + nc:nr + nc + nct]
        add_refs = refs[nr + nc + nct:n_in]
        drow_refs = refs[n_in:n_in + len(row_diff)]
        dconst_refs = refs[n_in + len(row_diff):]

        def g(*dv):
            r2, c2 = list(rv), list(cv)
            for p, k in enumerate(row_diff):
                r2[k] = dv[p]
            for p, k in enumerate(const_diff):
                c2[k] = dv[len(row_diff) + p]
            return tuple(f(*r2, *c2))

        prim = [rv[k].astype(f32) for k in row_diff] + [cv[k] for k in const_diff]
        outs, vf = jax.vjp(g, *prim)
        grads = vf(tuple(c[...].astype(o.dtype) for c, o in zip(cot_refs, outs)))
        for p, ref in enumerate(drow_refs):
            val = grads[p]
            if p in row_add:
                val = val + add_refs[add_keys.index(p)][...].astype(f32)
            ref[...] = val.astype(ref.dtype)

        @pl.when(i == 0)
        def _():
            for ref in dconst_refs:
                ref[...] = jnp.zeros_like(ref)

        for p, ref in enumerate(dconst_refs):
            ref[...] += grads[len(row_diff) + p]

    widths = [rows[k][1] for k in row_diff]
    return pl.pallas_call(
        body, name=name, grid=(t // tb,),
        in_specs=([_row_spec(tb, w, cb) for _, w, cb in rows] + [_full_spec(c.shape) for c in consts]
                  + [_row_spec(tb, c.shape[1], 0) for c in cots] + [_row_spec(tb, row_add[k].shape[1], 0) for k in add_keys]),
        out_specs=([_row_spec(tb, w, 0) for w in widths] + [_full_spec(consts[k].shape) for k in const_diff]),
        out_shape=([jax.ShapeDtypeStruct((t, w), dt) for w, dt in zip(widths, drow_dtypes)]
                   + [jax.ShapeDtypeStruct(consts[k].shape, f32) for k in const_diff]),
        compiler_params=_cparams(("arbitrary",)),
    )(*[r[0] for r in rows], *consts, *cots, *[row_add[k] for k in add_keys])


def _heads(x, w=DK):
    return [x[:, w * h:w * (h + 1)] for h in range(H)]


def _layer_norm(r, g, b):
    mu = jnp.mean(r, -1, keepdims=True)
    var = jnp.mean(jnp.square(r - mu), -1, keepdims=True)
    return (r - mu) * lax.rsqrt(var + LN_EPS) * g + b


def gdn_point_f(c, ba, alog_b, dtb_b):
    s = c * jax.nn.sigmoid(c)
    q, k, v = s[:, :512], s[:, 512:1024], s[:, 1024:]

    def l2(x):
        return jnp.concatenate([xh * lax.rsqrt(jnp.sum(xh * xh, -1, keepdims=True) + RMS_EPS) for xh in _heads(x)], axis=1)

    tb = c.shape[0]
    b_b = jnp.concatenate([jnp.broadcast_to(ba[:, 64 + h:65 + h], (tb, DK)) for h in range(H)], axis=1)
    a_b = jnp.concatenate([jnp.broadcast_to(ba[:, 68 + h:69 + h], (tb, DK)) for h in range(H)], axis=1)
    beta = jax.nn.sigmoid(b_b)
    g = -jnp.exp(alog_b) * jax.nn.softplus(a_b + dtb_b)
    return l2(q), l2(k), v, beta, g


def mla_pre_f(cq, ckv, krg, krs, cf, sf, qg, wq, kvg, wkv):
    cqn = cq * lax.rsqrt(jnp.sum(cq * cq, -1, keepdims=True) * (1.0 / QL) + RMS_EPS) * qg
    qa = bdot(cqn, wq)
    ckvn = ckv * lax.rsqrt(jnp.mean(ckv * ckv, -1, keepdims=True) + RMS_EPS) * kvg
    kv = bdot(ckvn, wkv)
    kro = krg * cf + krs * sf
    qs, ks, vs = [], [], []
    for h in range(H):
        qs += [qa[:, DK * h:DK * (h + 1)], qa[:, 512 + DK * h:512 + DK * (h + 1)] * cf + qa[:, 1024 + DK * h:1024 + DK * (h + 1)] * sf]
        ks += [kv[:, 256 * h:256 * h + DK], kro]
        vs += [kv[:, 256 * h + DK:256 * (h + 1)]]
    return jnp.concatenate(qs, axis=1), jnp.concatenate(ks, axis=1), jnp.concatenate(vs, axis=1)


def post1_f(o, z, omla, x, gg_b, wout, g1, b1):
    on = jnp.concatenate([oh * lax.rsqrt(jnp.mean(oh * oh, -1, keepdims=True) + RMS_EPS) for oh in _heads(o)], axis=1) * gg_b
    ogdn = on * (z * jax.nn.sigmoid(z))
    mix = bdot(jnp.concatenate([ogdn, omla], axis=1), wout)
    return (_layer_norm(ALPHA * x + mix, g1, b1),)


def post2_f(x1, down, p, g2, b2, wpg, wple):
    x2 = _layer_norm(ALPHA * x1 + down, g2, b2)
    return (x2 + jax.nn.sigmoid(bdot(x2, wpg)) * bdot(p, wple),)


HALO = 8
CW = 4


def _conv_from_scratch(xs, cw_ref, tb):
    c = xs[pl.ds(HALO - 3, tb), :] * cw_ref[0:1, :]
    for j in range(1, CW):
        c = c + xs[pl.ds(HALO - 3 + j, tb), :] * cw_ref[j:j + 1, :]
    return c


def gdn_pre_fwd(hbuf, conv_w8, alog_b, dtb_b):
    t = hbuf.shape[0]
    tb = min(ROW_TB, t)

    def body(x_ref, halo_ref, ba_ref, cw_ref, al_ref, dt_ref, q_ref, k_ref, v_ref, be_ref, g_ref, xs):
        i = pl.program_id(0)
        xs[pl.ds(0, HALO), :] = jnp.where(i == 0, 0.0, halo_ref[...])
        xs[pl.ds(HALO, tb), :] = x_ref[...]
        c = _conv_from_scratch(xs, cw_ref, tb)
        q, k, v, be, g = gdn_point_f(c, ba_ref[...], al_ref[...], dt_ref[...])
        q_ref[...], k_ref[...], v_ref[...], be_ref[...], g_ref[...] = q, k, v, be, g

    return pl.pallas_call(
        body, name="gdn_pre_fwd", grid=(t // tb,),
        in_specs=[_row_spec(tb, 1536, 0),
                  pl.BlockSpec((HALO, 1536), lambda i: (jnp.maximum(i * (tb // HALO) - 1, 0), 0)),
                  _row_spec(tb, LANE, OFF_KR // LANE),
                  _full_spec(conv_w8.shape), _full_spec(alog_b.shape), _full_spec(dtb_b.shape)],
        out_specs=[_row_spec(tb, 512, 0)] * 5,
        out_shape=[jax.ShapeDtypeStruct((t, 512), f32)] * 5,
        scratch_shapes=[pltpu.VMEM((tb + HALO, 1536), f32)],
        compiler_params=_cparams(("arbitrary",)),
    )(hbuf, hbuf, hbuf, conv_w8, alog_b, dtb_b)


def gdn_pre_bwd(hbuf, conv_w8, alog_b, dtb_b, dq, dk, dv, dbe, dg):
    t = hbuf.shape[0]
    tb = min(ROW_TB, t)
    n = t // tb

    def body(x_ref, halo_ref, ba_ref, cw_ref, al_ref, dt_ref, dq_ref, dk_ref, dv_ref, dbe_ref, dg_ref,
             dx_ref, dba_ref, dcw_ref, dal_ref, ddt_ref, xs, dcs):
        s = pl.program_id(0)
        i = n - 1 - s
        xs[pl.ds(0, HALO), :] = jnp.where(i == 0, 0.0, halo_ref[...])
        xs[pl.ds(HALO, tb), :] = x_ref[...]
        c = _conv_from_scratch(xs, cw_ref, tb)
        _, vf = jax.vjp(gdn_point_f, c, ba_ref[...], al_ref[...], dt_ref[...])
        dc, dba, dal, ddt = vf((dq_ref[...], dk_ref[...], dv_ref[...], dbe_ref[...], dg_ref[...]))

        @pl.when(s == 0)
        def _():
            dcs[pl.ds(tb, HALO), :] = jnp.zeros((HALO, 1536), f32)
            dcw_ref[...] = jnp.zeros_like(dcw_ref)
            dal_ref[...] = jnp.zeros_like(dal_ref)
            ddt_ref[...] = jnp.zeros_like(ddt_ref)

        @pl.when(s > 0)
        def _():
            dcs[pl.ds(tb, HALO), :] = dcs[pl.ds(0, HALO), :]

        dcs[pl.ds(0, tb), :] = dc
        dx = dcs[pl.ds(3, tb), :] * cw_ref[0:1, :]
        for j in range(1, CW):
            dx = dx + dcs[pl.ds(3 - j, tb), :] * cw_ref[j:j + 1, :]
        dx_ref[...] = dx.astype(dx_ref.dtype)
        dba_ref[...] = dba
        for j in range(CW):
            dcw_ref[j:j + 1, :] += jnp.sum(dc * xs[pl.ds(HALO - 3 + j, tb), :], axis=0, keepdims=True)
        dal_ref[...] += dal
        ddt_ref[...] += ddt

    rev = lambda cb: (lambda s: (n - 1 - s, cb))
    return pl.pallas_call(
        body, name="gdn_pre_bwd", grid=(n,),
        in_specs=[pl.BlockSpec((tb, 1536), rev(0)),
                  pl.BlockSpec((HALO, 1536), lambda s: (jnp.maximum((n - 1 - s) * (tb // HALO) - 1, 0), 0)),
                  pl.BlockSpec((tb, LANE), rev(OFF_KR // LANE)),
                  _full_spec(conv_w8.shape), _full_spec(alog_b.shape), _full_spec(dtb_b.shape)]
        + [pl.BlockSpec((tb, 512), rev(0))] * 5,
        out_specs=[pl.BlockSpec((tb, 1536), rev(0)), pl.BlockSpec((tb, LANE), rev(0)),
                   _full_spec(conv_w8.shape), _full_spec(alog_b.shape), _full_spec(dtb_b.shape)],
        out_shape=[jax.ShapeDtypeStruct((t, 1536), bf16), jax.ShapeDtypeStruct((t, LANE), f32),
                   jax.ShapeDtypeStruct(conv_w8.shape, f32), jax.ShapeDtypeStruct(alog_b.shape, f32),
                   jax.ShapeDtypeStruct(dtb_b.shape, f32)],
        scratch_shapes=[pltpu.VMEM((tb + HALO, 1536), f32), pltpu.VMEM((tb + HALO, 1536), f32)],
        compiler_params=_cparams(("arbitrary",)),
    )(hbuf, hbuf, hbuf, conv_w8, alog_b, dtb_b, dq, dk, dv, dbe, dg)


GDN_NB = 4


def _to_batch(x, rows=CHUNK):
    n = x.shape[0] // rows
    return jnp.stack([x[rows * ci:rows * (ci + 1), DK * h:DK * (h + 1)] for ci in range(n) for h in range(H)], axis=0)


def _from_batch(y):
    n = y.shape[0] // H
    return jnp.concatenate([jnp.concatenate([y[ci * H + h] for h in range(H)], axis=1) for ci in range(n)], axis=0)


def gdn_intra_b(q, k, v, beta, g):
    c = CHUNK
    n = q.shape[0] // c
    tri = jnp.where(lax.broadcasted_iota(jnp.int32, (c, c), 0) >= lax.broadcasted_iota(jnp.int32, (c, c), 1), 1.0, 0.0)
    gc = _to_batch(jnp.concatenate([hdot(tri, g[c * ci:c * (ci + 1)]) for ci in range(n)], axis=0))
    qb, kb, vb, bb = _to_batch(q) * DK ** -0.5, _to_batch(k), _to_batch(v), _to_batch(beta)
    nbat = qb.shape[0]
    row = lax.broadcasted_iota(jnp.int32, (1, c, DK), 1)
    col = lax.broadcasted_iota(jnp.int32, (1, c, DK), 2)
    incl, strict, eye = row >= col, row > col, row == col
    grow = hdot(jnp.ones((nbat, c, c), f32), jnp.where(eye, gc, 0.0))
    decay = jnp.where(incl, jnp.exp(jnp.where(incl, gc - grow, 0.0)), 0.0)
    kbeta = kb * bb
    kpad = jnp.concatenate([kb, jnp.zeros((nbat, DK - c, DK), f32)], axis=1)
    a = jnp.where(strict, bdot_nt(kbeta, kpad) * decay, 0.0)[:, :, :c]
    nn = -a
    bk = bdot(a, a)
    for step in range(5):
        nn = nn + bk + bdot(nn, bk)
        if step < 4:
            bk = bdot(bk, bk)
    eg = jnp.exp(gc)
    rhs_v, rhs_k = vb * bb, kbeta * eg
    g_last = gc[:, c - 1:c, :]
    u = rhs_v + bdot(nn, rhs_v)
    w = rhs_k + bdot(nn, rhs_k)
    attn = jnp.where(incl, bdot_nt(qb, kpad) * decay, 0.0)
    return u, w, qb * eg, kb * jnp.exp(g_last - gc), attn, jnp.exp(g_last)


def gdn_scan_b(u, w, qd, kd, attn, el, state):
    v_new = u - bdot(w, state)
    return bdot(qd, state) + bdot(attn[:, :, :CHUNK], v_new), state * el + bdot_tn(kd, v_new)


def gdn_scan_bwd_b(do, w, qd, kd, attn, el, dstate):
    dvn = bdot_tn(attn[:, :, :CHUNK], do) + bdot(kd, dstate)
    return bdot_tn(qd, do) + dstate * el - bdot_tn(w, dvn)


def gdn_chunks_f(q, k, v, beta, g, states):
    n = states.shape[0]
    o, s_new = gdn_scan_b(*gdn_intra_b(q, k, v, beta, g), states.reshape(n * H, DK, DK))
    return _from_batch(o), s_new.reshape(n, H * DK, DK)


def _chunk_rows(cidx):
    return pl.ds(cidx * CHUNK, CHUNK)


def gdn_intra(q, k, v, beta, g):
    t = q.shape[0]
    nb = min(GDN_NB, t // CHUNK)
    tb = nb * CHUNK

    def body(q_ref, k_ref, v_ref, be_ref, g_ref, u_ref, w_ref, qd_ref, kd_ref, at_ref, el_ref):
        u, w, qd, kd, at, el = gdn_intra_b(q_ref[...], k_ref[...], v_ref[...], be_ref[...], g_ref[...])
        u_ref[...], w_ref[...], qd_ref[...], kd_ref[...], at_ref[...] = (_from_batch(a) for a in (u, w, qd, kd, at))
        el_ref[...] = _from_batch(jnp.broadcast_to(el, (nb * H, 8, DK))).reshape(nb, 8, H * DK)

    return pl.pallas_call(
        body, name="gdn_intra", grid=(t // tb,),
        in_specs=[_row_spec(tb, 512, 0)] * 5,
        out_specs=[_row_spec(tb, 512, 0)] * 5 + [pl.BlockSpec((nb, 8, 512), lambda i: (i, 0, 0))],
        out_shape=[jax.ShapeDtypeStruct((t, 512), f32)] * 5 + [jax.ShapeDtypeStruct((t // CHUNK, 8, 512), f32)],
        compiler_params=_cparams(("parallel",)),
    )(q, k, v, beta, g)


def gdn_scan_fwd(u, w, qd, kd, attn, el):
    t = u.shape[0]
    nb = min(GDN_NB, t // CHUNK)
    tb = nb * CHUNK

    def body(u_ref, w_ref, qd_ref, kd_ref, at_ref, el_ref, o_ref, s_ref, state):
        @pl.when(pl.program_id(0) == 0)
        def _():
            state[...] = jnp.zeros_like(state)

        for cidx in range(nb):
            r = _chunk_rows(cidx)
            s_ref[cidx] = state[...]
            ins = [_to_batch(ref[r, :]) for ref in (u_ref, w_ref, qd_ref, kd_ref, at_ref)]
            el = _to_batch(el_ref[cidx], 8)[:, 0:1, :]
            o, s_new = gdn_scan_b(*ins, el, state[...].reshape(H, DK, DK))
            o_ref[r, :] = _from_batch(o)
            state[...] = s_new.reshape(H * DK, DK)

    return pl.pallas_call(
        body, name="gdn_scan_fwd", grid=(t // tb,),
        in_specs=[_row_spec(tb, 512, 0)] * 5 + [pl.BlockSpec((nb, 8, 512), lambda i: (i, 0, 0))],
        out_specs=[_row_spec(tb, 512, 0), pl.BlockSpec((nb, 512, DK), lambda i: (i, 0, 0))],
        out_shape=[jax.ShapeDtypeStruct((t, 512), f32), jax.ShapeDtypeStruct((t // CHUNK, 512, DK), f32)],
        scratch_shapes=[pltpu.VMEM((512, DK), f32)],
        compiler_params=_cparams(("arbitrary",)),
    )(u, w, qd, kd, attn, el)


def gdn_scan_bwd(do, w, qd, kd, attn, el):
    t = do.shape[0]
    nb = min(GDN_NB, t // CHUNK)
    tb = nb * CHUNK
    n = t // tb

    def body(do_ref, w_ref, qd_ref, kd_ref, at_ref, el_ref, ds_ref, dstate):
        @pl.when(pl.program_id(0) == 0)
        def _():
            dstate[...] = jnp.zeros_like(dstate)

        for cidx in reversed(range(nb)):
            r = _chunk_rows(cidx)
            ds_ref[cidx] = dstate[...]
            ins = [_to_batch(ref[r, :]) for ref in (do_ref, w_ref, qd_ref, kd_ref, at_ref)]
            el = _to_batch(el_ref[cidx], 8)[:, 0:1, :]
            dstate[...] = gdn_scan_bwd_b(*ins, el, dstate[...].reshape(H, DK, DK)).reshape(H * DK, DK)

    rev = pl.BlockSpec((tb, 512), lambda s: (n - 1 - s, 0))
    return pl.pallas_call(
        body, name="gdn_scan_bwd", grid=(n,),
        in_specs=[rev] * 5 + [pl.BlockSpec((nb, 8, 512), lambda s: (n - 1 - s, 0, 0))],
        out_specs=pl.BlockSpec((nb, 512, DK), lambda s: (n - 1 - s, 0, 0)),
        out_shape=jax.ShapeDtypeStruct((t // CHUNK, 512, DK), f32),
        scratch_shapes=[pltpu.VMEM((512, DK), f32)],
        compiler_params=_cparams(("arbitrary",)),
    )(do, w, qd, kd, attn, el)


def gdn_local_bwd(q, k, v, beta, g, states, dstates, do):
    t = q.shape[0]
    nb = min(GDN_NB, t // CHUNK)
    tb = nb * CHUNK

    def body(q_ref, k_ref, v_ref, be_ref, g_ref, s_ref, ds_ref, do_ref, dq_ref, dk_ref, dv_ref, dbe_ref, dg_ref):
        states_v = s_ref[...]
        _, vf = jax.vjp(lambda *a: gdn_chunks_f(*a, states_v), q_ref[...], k_ref[...], v_ref[...], be_ref[...], g_ref[...])
        dq_ref[...], dk_ref[...], dv_ref[...], dbe_ref[...], dg_ref[...] = vf((do_ref[...], ds_ref[...]))

    st = pl.BlockSpec((nb, 512, DK), lambda i: (i, 0, 0))
    return pl.pallas_call(
        body, name="gdn_local_bwd", grid=(t // tb,),
        in_specs=[_row_spec(tb, 512, 0)] * 5 + [st, st, _row_spec(tb, 512, 0)],
        out_specs=[_row_spec(tb, 512, 0)] * 5,
        out_shape=[jax.ShapeDtypeStruct((t, 512), f32)] * 5,
        compiler_params=_cparams(("parallel",)),
    )(q, k, v, beta, g, states, dstates, do)


NEG = -1e30


def _chunk_mask(i, j, tq, tk):
    qrow = lax.broadcasted_iota(jnp.int32, (tq, tk), 0) + i * tq
    kcol = lax.broadcasted_iota(jnp.int32, (tq, tk), 1) + j * tk
    return jnp.right_shift(kcol, 6) <= jnp.right_shift(qrow, 6)


def _block_unmasked(i, j, tq, tk):
    return (i * tq) // CHUNK >= ((j + 1) * tk - 1) // CHUNK


def attn_fwd(q, k, v):
    t = q.shape[0]
    tq = tk = min(1024, t)
    nq, nk = t // tq, t // tk
    last = lambda i: ((i + 1) * tq - 1) // tk

    def body(q_ref, k_ref, v_ref, o_ref, lse_ref, m_s, l_s, acc):
        i, j = pl.program_id(1), pl.program_id(2)

        @pl.when(j == 0)
        def _():
            m_s[...] = jnp.full_like(m_s, NEG)
            l_s[...] = jnp.zeros_like(l_s)
            acc[...] = jnp.zeros_like(acc)

        def step(masked):
            s = _dot(q_ref[...], k_ref[...], 1, 1) * ATT_SCALE
            if masked:
                s = jnp.where(_chunk_mask(i, j, tq, tk), s, NEG)
            m_new = jnp.maximum(m_s[...], jnp.max(s, axis=-1, keepdims=True))
            p = jnp.exp(s - m_new)
            alpha = jnp.exp(m_s[...] - m_new)
            l_s[...] = alpha * l_s[...] + jnp.sum(p, axis=-1, keepdims=True)
            acc[...] = alpha * acc[...] + _dot(p.astype(MX), v_ref[...], 1, 0)
            m_s[...] = m_new

        whole = _block_unmasked(i, j, tq, tk)
        pl.when(whole)(lambda: step(False))
        pl.when((j <= last(i)) & jnp.logical_not(whole))(lambda: step(True))

        @pl.when(j == nk - 1)
        def _():
            o_ref[...] = acc[...] / l_s[...]
            lse_ref[0] = m_s[...] + jnp.log(l_s[...])

    return pl.pallas_call(
        body, name="attn_fwd", grid=(H, nq, nk),
        in_specs=[pl.BlockSpec((tq, 256), lambda h, i, j: (i, h)),
                  pl.BlockSpec((tk, 256), lambda h, i, j: (jnp.minimum(j, last(i)), h)),
                  pl.BlockSpec((tk, DK), lambda h, i, j: (jnp.minimum(j, last(i)), h))],
        out_specs=[pl.BlockSpec((tq, DK), lambda h, i, j: (i, h)), pl.BlockSpec((1, tq, 1), lambda h, i, j: (h, i, 0))],
        out_shape=[jax.ShapeDtypeStruct((t, 512), f32), jax.ShapeDtypeStruct((H, t, 1), f32)],
        scratch_shapes=[pltpu.VMEM((tq, 1), f32), pltpu.VMEM((tq, 1), f32), pltpu.VMEM((tq, DK), f32)],
        compiler_params=_cparams(("parallel", "parallel", "arbitrary")),
    )(q, k, v)


def attn_bwd(q, k, v, o, lse, do):
    t = q.shape[0]
    tq = tk = min(512, t)
    nq, nk = t // tq, t // tk
    first = lambda j: (j * tk) // tq

    def body(q_ref, k_ref, v_ref, o_ref, lse_ref, do_ref, dq_ref, dk_ref, dv_ref, dk_acc, dv_acc):
        j, i = pl.program_id(1), pl.program_id(2)

        @pl.when((j == 0) & (i == 0))
        def _():
            dq_ref[...] = jnp.zeros_like(dq_ref)

        @pl.when(i == 0)
        def _():
            dk_acc[...] = jnp.zeros_like(dk_acc)
            dv_acc[...] = jnp.zeros_like(dv_acc)

        def step(masked):
            qb, kb, vb = q_ref[...], k_ref[...], v_ref[...]
            dob = do_ref[...]
            s = _dot(qb, kb, 1, 1) * ATT_SCALE
            if masked:
                s = jnp.where(_chunk_mask(i, j, tq, tk), s, NEG)
            p = jnp.exp(s - lse_ref[0])
            dv_acc[...] += _dot(p.astype(MX), dob.astype(MX), 0, 0)
            dp = _dot(dob.astype(MX), vb, 1, 1)
            delta = jnp.sum(dob * o_ref[...], axis=-1, keepdims=True)
            ds = (p * (dp - delta) * ATT_SCALE).astype(MX)
            dk_acc[...] += _dot(ds, qb, 0, 0)
            rows = pl.ds(pl.multiple_of(i * tq, tq), tq)
            dq_ref[rows, :] += _dot(ds, kb, 1, 0)

        whole = _block_unmasked(i, j, tq, tk)
        pl.when(whole)(lambda: step(False))
        pl.when((i >= first(j)) & jnp.logical_not(whole))(lambda: step(True))

        @pl.when(i == nq - 1)
        def _():
            dk_ref[...] = dk_acc[...]
            dv_ref[...] = dv_acc[...]

    qi = lambda h, j, i: (jnp.maximum(i, first(j)), h)
    return pl.pallas_call(
        body, name="attn_bwd", grid=(H, nk, nq),
        in_specs=[pl.BlockSpec((tq, 256), qi),
                  pl.BlockSpec((tk, 256), lambda h, j, i: (j, h)),
                  pl.BlockSpec((tk, DK), lambda h, j, i: (j, h)),
                  pl.BlockSpec((tq, DK), qi),
                  pl.BlockSpec((1, tq, 1), lambda h, j, i: (h, jnp.maximum(i, first(j)), 0)),
                  pl.BlockSpec((tq, DK), qi)],
        out_specs=[pl.BlockSpec((t, 256), lambda h, j, i: (0, h)),
                   pl.BlockSpec((tk, 256), lambda h, j, i: (j, h)),
                   pl.BlockSpec((tk, DK), lambda h, j, i: (j, h))],
        out_shape=[jax.ShapeDtypeStruct((t, 1024), f32), jax.ShapeDtypeStruct((t, 1024), f32),
                   jax.ShapeDtypeStruct((t, 512), f32)],
        scratch_shapes=[pltpu.VMEM((tk, 256), f32), pltpu.VMEM((tk, DK), f32)],
        compiler_params=_cparams(("arbitrary", "arbitrary", "arbitrary")),
    )(q, k, v, o, lse, do)


def ffn_up(x1, wg, wu):
    t = x1.shape[0]
    tm, tn = min(512, t), _pick(DFF, 1536)

    def body(x_ref, wg_ref, wu_ref, g_ref, u_ref, a_ref):
        xb = x_ref[...].astype(MX)
        g = _dot(xb, wg_ref[...], 1, 0)
        u = _dot(xb, wu_ref[...], 1, 0)
        g_ref[...] = g.astype(g_ref.dtype)
        u_ref[...] = u.astype(u_ref.dtype)
        a_ref[...] = (g * jax.nn.sigmoid(g) * u).astype(a_ref.dtype)

    w_spec = pl.BlockSpec((D, tn), lambda i, j: (0, j))
    o_spec = pl.BlockSpec((tm, tn), lambda i, j: (i, j))
    return pl.pallas_call(
        body, name="ffn_up", grid=(t // tm, DFF // tn),
        in_specs=[pl.BlockSpec((tm, D), lambda i, j: (i, 0)), w_spec, w_spec],
        out_specs=[o_spec] * 3, out_shape=[jax.ShapeDtypeStruct((t, DFF), bf16)] * 3,
        compiler_params=_cparams(("parallel", "parallel")),
    )(x1, wg, wu)


def ffn_dact(ddown, wd, g, u):
    t = ddown.shape[0]
    tm, tn = min(512, t), _pick(DFF, 1536)

    def body(dd_ref, wd_ref, g_ref, u_ref, dg_ref, du_ref):
        dact = _dot(dd_ref[...].astype(MX), wd_ref[...], 1, 1)
        gv, uv = g_ref[...].astype(f32), u_ref[...].astype(f32)
        sig = jax.nn.sigmoid(gv)
        dg_ref[...] = (dact * uv * sig * (1.0 + gv * (1.0 - sig))).astype(dg_ref.dtype)
        du_ref[...] = (dact * gv * sig).astype(du_ref.dtype)

    o_spec = pl.BlockSpec((tm, tn), lambda i, j: (i, j))
    return pl.pallas_call(
        body, name="ffn_dact", grid=(t // tm, DFF // tn),
        in_specs=[pl.BlockSpec((tm, D), lambda i, j: (i, 0)), pl.BlockSpec((tn, D), lambda i, j: (j, 0)), o_spec, o_spec],
        out_specs=[o_spec] * 2, out_shape=[jax.ShapeDtypeStruct((t, DFF), bf16)] * 2,
        compiler_params=_cparams(("parallel", "parallel")),
    )(ddown, wd, g, u)


def loss_head(y, target):
    t = y.shape[0]
    tb = min(ROW_TB, t)
    n = t // tb

    def body(y_ref, t_ref, dy_ref, loss_ref, acc):
        i = pl.program_id(0)

        @pl.when(i == 0)
        def _():
            acc[...] = jnp.zeros_like(acc)

        e = y_ref[...] - t_ref[...]
        dy_ref[...] = e * (1.0 / D)
        acc[...] += jnp.sum(e * e, axis=0, keepdims=True)

        @pl.when(i == n - 1)
        def _():
            loss_ref[...] = jnp.sum(acc[...], axis=1, keepdims=True) * (0.5 / D)

    return pl.pallas_call(
        body, name="loss_head", grid=(n,),
        in_specs=[_row_spec(tb, D, 0)] * 2,
        out_specs=[_row_spec(tb, D, 0), _full_spec((1, 1))],
        out_shape=[jax.ShapeDtypeStruct((t, D), f32), jax.ShapeDtypeStruct((1, 1), f32)],
        scratch_shapes=[pltpu.VMEM((1, D), f32)],
        compiler_params=_cparams(("arbitrary",)),
    )(y, target)


def _me_and_peers():
    x, y, c = lax.axis_index("x"), lax.axis_index("y"), lax.axis_index("c")
    me = 4 * x + 2 * y + c
    peers = []
    for kk in range(1, N_DEV):
        px = 1 - x if kk & 4 else x
        py = 1 - y if kk & 2 else y
        pc = 1 - c if kk & 1 else c
        peers.append(((px, py, pc), 4 * px + 2 * py + pc))
    return me, peers


_ANY = pl.BlockSpec(memory_space=pl.ANY)


def all_gather_rows(name, x):
    def body(x_ref, out_ref, send_sems, recv_sems, local_sem):
        me, peers = _me_and_peers()
        mine = pltpu.make_async_copy(x_ref, out_ref.at[me], local_sem)
        mine.start()
        copies = [pltpu.make_async_remote_copy(src_ref=x_ref, dst_ref=out_ref.at[me], send_sem=send_sems.at[kk],
                                               recv_sem=recv_sems.at[kk], device_id=dev, device_id_type=pl.DeviceIdType.MESH)
                  for kk, (dev, _) in enumerate(peers)]
        for cp in copies:
            cp.start()
        for cp in copies:
            cp.wait()
        mine.wait()

    return pl.pallas_call(
        body, name=name, in_specs=[_ANY], out_specs=_ANY,
        out_shape=jax.ShapeDtypeStruct((N_DEV,) + x.shape, x.dtype),
        scratch_shapes=[pltpu.SemaphoreType.DMA((N_DEV - 1,)), pltpu.SemaphoreType.DMA((N_DEV - 1,)), pltpu.SemaphoreType.DMA],
    )(x)


def exchange_shards(send):
    def body(s_ref, out_ref, send_sems, recv_sems, local_sem):
        me, peers = _me_and_peers()
        mine = pltpu.make_async_copy(s_ref.at[me], out_ref.at[me], local_sem)
        mine.start()
        copies = [pltpu.make_async_remote_copy(src_ref=s_ref.at[pid], dst_ref=out_ref.at[me], send_sem=send_sems.at[kk],
                                               recv_sem=recv_sems.at[kk], device_id=dev, device_id_type=pl.DeviceIdType.MESH)
                  for kk, (dev, pid) in enumerate(peers)]
        for cp in copies:
            cp.start()
        for cp in copies:
            cp.wait()
        mine.wait()

    return pl.pallas_call(
        body, name="exchange_grad_shards", in_specs=[_ANY], out_specs=_ANY,
        out_shape=jax.ShapeDtypeStruct(send.shape, send.dtype),
        scratch_shapes=[pltpu.SemaphoreType.DMA((N_DEV - 1,)), pltpu.SemaphoreType.DMA((N_DEV - 1,)), pltpu.SemaphoreType.DMA],
    )(send)


def adamw_rows(parts, w, m, v):
    r = w.shape[0]
    tb = _pick_rows(r)

    def body(p_ref, w_ref, m_ref, v_ref, g_ref, d_ref, nm_ref, nv_ref):
        g = p_ref[0]
        for s in range(1, N_DEV):
            g = g + p_ref[s]
        nm = ADAM_B1 * m_ref[...] + (1.0 - ADAM_B1) * g
        nv = ADAM_B2 * v_ref[...] + (1.0 - ADAM_B2) * jnp.square(g)
        m_hat = nm / (1.0 - ADAM_B1 ** ADAM_STEP)
        v_hat = nv / (1.0 - ADAM_B2 ** ADAM_STEP)
        g_ref[...] = g
        d_ref[...] = -ADAM_LR * (m_hat / (jnp.sqrt(v_hat) + ADAM_EPS) + ADAM_WD * w_ref[...])
        nm_ref[...] = nm
        nv_ref[...] = nv

    spec = pl.BlockSpec((tb, D), lambda i: (i, 0))
    return pl.pallas_call(
        body, name="adamw", grid=(r // tb,),
        in_specs=[pl.BlockSpec((N_DEV, tb, D), lambda i: (0, i, 0)), spec, spec, spec],
        out_specs=[spec] * 4, out_shape=[jax.ShapeDtypeStruct((r, D), f32)] * 4,
        compiler_params=_cparams(("parallel",)),
    )(parts, w, m, v)


def _pick_rows(r):
    return max(tb for tb in range(8, ROW_TB + 1, 8) if r % tb == 0)


BIG = ("w_in", "w_uq", "w_ukv", "w_out", "w_gate_up", "w_down", "w_ple", "w_ple_gate")
COL_SHARDED = ("w_in", "w_uq", "w_ukv", "w_gate_up", "w_ple")
SMALL = ("conv_w", "a_log", "dt_bias", "gdn_norm_g", "q_norm_g", "kv_norm_g", "ln1_g", "ln1_b", "ln2_g", "ln2_b")
FULL_SHAPE = {"w_in": (D, IN_W), "w_uq": (QL, 768), "w_ukv": (KVL, 1024), "w_out": (D, D), "w_gate_up": (D, 2 * DFF),
              "w_down": (DFF, D), "w_ple": (PLE, D), "w_ple_gate": (D, D)}
ROW_ALIGN = 16


def _pad_rows(a, mult=ROW_ALIGN, axis=0):
    pad = (-a.shape[axis]) % mult
    widths = [(0, 0)] * a.ndim
    widths[axis] = (0, pad)
    return a if pad == 0 else jnp.pad(a, widths)


def _pack_big_local(tree, dtype):
    pieces = [_pad_rows(tree[n][l].reshape(-1, D).astype(dtype)) for l in range(DEPTH) for n in BIG]
    return jnp.concatenate(pieces, axis=0)


def _big_rows(shards):
    rows, off = {}, 0
    for l in range(DEPTH):
        for n in BIG:
            r = shards[n][1] * shards[n][2] // D
            rows[(l, n)] = (off, r)
            off += r + (-r) % ROW_ALIGN
    return rows, off


def _unpack_gathered(gathered, shards, l, n):
    off, r = _big_rows(shards)[0][(l, n)]
    _, sr, sc = shards[n]
    blk = gathered[:, off:off + r, :].reshape(N_DEV, sr, sc)
    if n in COL_SHARDED:
        return jnp.transpose(blk, (1, 0, 2)).reshape(sr, N_DEV * sc)
    return blk.reshape(N_DEV * sr, sc)


def _shard_grad(gfull, n, shards):
    _, sr, sc = shards[n]
    if n in COL_SHARDED:
        blk = jnp.transpose(gfull.reshape(sr, N_DEV, sc), (1, 0, 2))
    else:
        blk = gfull.reshape(N_DEV, sr, sc)
    return _pad_rows(blk.reshape(N_DEV, sr * sc // D, D), axis=1)


def _pack_small(tree, extra):
    flat = jnp.concatenate([tree[n].reshape(-1).astype(f32) for n in SMALL] + [extra.reshape(-1).astype(f32)])
    return jnp.pad(flat, (0, (-flat.shape[0]) % (ROW_ALIGN * D))).reshape(-1, D)


def _unpack_small(rows, like):
    flat = rows.reshape(-1)
    out, off = {}, 0
    for n in SMALL:
        sz = int(np.prod(like[n].shape))
        out[n] = flat[off:off + sz].reshape(like[n].shape)
        off += sz
    return out, flat[off]


def _zeros(r, c, dt):
    return jnp.zeros((r, c), dt)


def _prep_w_in(w):
    dt = w.dtype
    kr = w[:, 2696:2760]
    return jnp.concatenate([
        w[:, 0:2048],
        w[:, 2440:2696],
        kr, w[:, 2048:2056], _zeros(D, 56, dt),
        kr[:, 32:], kr[:, :32], _zeros(D, 64, dt),
        w[:, 2056:2440], _zeros(D, CQP - QL, dt)], axis=1)


def _unprep_w_in(g):
    krs = g[:, OFF_KRS:OFF_KRS + 64]
    kr = g[:, OFF_KR:OFF_KR + 64] + jnp.concatenate([krs[:, 32:], krs[:, :32]], axis=1)
    return jnp.concatenate([g[:, 0:2048], g[:, OFF_KR + 64:OFF_KR + 72], g[:, OFF_CQ:OFF_CQ + QL],
                            g[:, OFF_CKV:OFF_CKV + KVL], kr], axis=1)


def _prep_w_uq(w):
    dt = w.dtype
    z64 = _zeros(QL, 64, dt)
    nope, ra, rb = [], [], []
    for h in range(H):
        nope.append(w[:, 192 * h:192 * h + 128])
        x1, x2 = w[:, 192 * h + 128:192 * h + 160], w[:, 192 * h + 160:192 * h + 192]
        ra += [x1, x2, z64]
        rb += [x2, x1, z64]
    return jnp.concatenate([jnp.concatenate(nope + ra + rb, axis=1), _zeros(CQP - QL, 1536, dt)], axis=0)


def _unprep_w_uq(g):
    g = g[:QL]
    cols = []
    for h in range(H):
        a = g[:, 512 + 128 * h:512 + 128 * h + 64]
        b = g[:, 1024 + 128 * h:1024 + 128 * h + 64]
        cols += [g[:, 128 * h:128 * (h + 1)], a[:, :32] + b[:, 32:], a[:, 32:] + b[:, :32]]
    return jnp.concatenate(cols, axis=1)


def _rope_tables(positions):
    inv_freq = ROPE_THETA ** (-jnp.arange(0, ROPE, 2, dtype=f32) / ROPE)
    ang = positions.astype(f32)[:, None] * inv_freq
    c, s = jnp.cos(ang), jnp.sin(ang)
    z = jnp.zeros((positions.shape[0], 64), f32)
    return jnp.concatenate([c, c, z], axis=1), jnp.concatenate([-s, s, z], axis=1)


def _tile_heads(vec, n=H):
    return jnp.tile(vec.reshape(1, -1), (1, n))


def _bcast_heads(vec):
    return jnp.repeat(vec, DK).reshape(1, H * DK)


def kernel(x, p, positions, w_in, conv_w, a_log, dt_bias, gdn_norm_g, q_norm_g, w_uq, kv_norm_g, w_ukv, w_out, ln1_g, ln1_b, w_gate_up, w_down, ln2_g, ln2_b, w_ple, w_ple_gate, loss_target, m_w_in, m_conv_w, m_a_log, m_dt_bias, m_gdn_norm_g, m_q_norm_g, m_w_uq, m_kv_norm_g, m_w_ukv, m_w_out, m_ln1_g, m_ln1_b, m_w_gate_up, m_w_down, m_ln2_g, m_ln2_b, m_w_ple, m_w_ple_gate, v_w_in, v_conv_w, v_a_log, v_dt_bias, v_gdn_norm_g, v_q_norm_g, v_w_uq, v_kv_norm_g, v_w_ukv, v_w_out, v_ln1_g, v_ln1_b, v_w_gate_up, v_w_down, v_ln2_g, v_ln2_b, v_w_ple, v_w_ple_gate):
    W = dict(w_in=w_in, conv_w=conv_w, a_log=a_log, dt_bias=dt_bias, gdn_norm_g=gdn_norm_g, q_norm_g=q_norm_g, w_uq=w_uq,
             kv_norm_g=kv_norm_g, w_ukv=w_ukv, w_out=w_out, ln1_g=ln1_g, ln1_b=ln1_b, w_gate_up=w_gate_up, w_down=w_down,
             ln2_g=ln2_g, ln2_b=ln2_b, w_ple=w_ple, w_ple_gate=w_ple_gate)
    M = dict(w_in=m_w_in, conv_w=m_conv_w, a_log=m_a_log, dt_bias=m_dt_bias, gdn_norm_g=m_gdn_norm_g, q_norm_g=m_q_norm_g,
             w_uq=m_w_uq, kv_norm_g=m_kv_norm_g, w_ukv=m_w_ukv, w_out=m_w_out, ln1_g=m_ln1_g, ln1_b=m_ln1_b,
             w_gate_up=m_w_gate_up, w_down=m_w_down, ln2_g=m_ln2_g, ln2_b=m_ln2_b, w_ple=m_w_ple, w_ple_gate=m_w_ple_gate)
    V = dict(w_in=v_w_in, conv_w=v_conv_w, a_log=v_a_log, dt_bias=v_dt_bias, gdn_norm_g=v_gdn_norm_g, q_norm_g=v_q_norm_g,
             w_uq=v_w_uq, kv_norm_g=v_kv_norm_g, w_ukv=v_w_ukv, w_out=v_w_out, ln1_g=v_ln1_g, ln1_b=v_ln1_b,
             w_gate_up=v_w_gate_up, w_down=v_w_down, ln2_g=v_ln2_g, ln2_b=v_ln2_b, w_ple=v_w_ple, w_ple_gate=v_w_ple_gate)
    shards = {n: W[n].shape for n in BIG}
    t = x.shape[1]
    xin = x.reshape(t, D)
    target = loss_target.reshape(t, D)
    cf, sf = _rope_tables(positions.reshape(t))

    gathered = all_gather_rows("all_gather_weights", _pad_rows(_pack_big_local(W, bf16)))
    conv_g = all_gather_rows("all_gather_conv", _pad_rows(jnp.pad(conv_w.reshape(-1), (0, (-conv_w.size) % D)).reshape(-1, D), 8))
    csz = conv_w.shape[1] * conv_w.shape[2]
    conv_full = jnp.transpose(conv_g.reshape(N_DEV, -1)[:, :DEPTH * csz].reshape(N_DEV, DEPTH, CW, -1), (1, 2, 0, 3)).reshape(DEPTH, CW, 1536)

    full = lambda l, n: _unpack_gathered(gathered, shards, l, n)

    acts = []
    h_cur = xin
    for l in range(DEPTH):
        win = _prep_w_in(full(l, "w_in"))
        wq = _prep_w_uq(full(l, "w_uq"))
        wkv, wout = full(l, "w_ukv"), full(l, "w_out")
        wgu, wd = full(l, "w_gate_up"), full(l, "w_down")
        wg, wu = wgu[:, :DFF], wgu[:, DFF:]
        wple, wpg = full(l, "w_ple"), full(l, "w_ple_gate")
        cw8 = jnp.concatenate([conv_full[l], jnp.zeros((8 - CW, 1536), f32)], axis=0)
        alog_b, dtb_b = _bcast_heads(a_log[l]), _bcast_heads(dt_bias[l])
        gg_b = _tile_heads(gdn_norm_g[l])
        qg = jnp.concatenate([q_norm_g[l], jnp.zeros((CQP - QL,), f32)]).reshape(1, CQP)
        kvg = kv_norm_g[l].reshape(1, KVL)
        g1, b1, g2, b2 = (a[l].reshape(1, D) for a in (ln1_g, ln1_b, ln2_g, ln2_b))
        p_l = p[l].reshape(t, PLE)

        hb = mm("in_proj", h_cur, win, mode="nn", out_dtype=f32)
        qn, kn, vs, beta, glog = gdn_pre_fwd(hb, cw8, alog_b, dtb_b)
        g_u, g_w, g_qd, g_kd, g_at, g_el = gdn_intra(qn, kn, vs, beta, glog)
        o_gdn, states = gdn_scan_fwd(g_u, g_w, g_qd, g_kd, g_at, g_el)
        mla_rows = [(hb, CQP, OFF_CQ // CQP), (hb, KVL, OFF_CKV // KVL), (hb, LANE, OFF_KR // LANE), (hb, LANE, OFF_KRS // LANE),
                    (cf, LANE, 0), (sf, LANE, 0)]
        mla_consts = [qg, wq, kvg, wkv]
        qm, km, vm = row_fwd("mla_pre", mla_pre_f, mla_rows, mla_consts, [(1024, bf16), (1024, bf16), (512, bf16)])
        o_mla, lse = attn_fwd(qm, km, vm)
        p1_rows = [(o_gdn, 512, 0), (hb, 512, OFF_Z // 512), (o_mla, 512, 0), (h_cur, D, 0)]
        p1_consts = [gg_b, wout, g1, b1]
        (x1,) = row_fwd("post1", post1_f, p1_rows, p1_consts, [(D, f32)])
        gate, up, act = ffn_up(x1, wg, wu)
        down = mm("ffn_down", act, wd, mode="nn", out_dtype=f32)
        p2_rows = [(x1, D, 0), (down, D, 0), (p_l, PLE, 0)]
        p2_consts = [g2, b2, wpg, wple]
        (y,) = row_fwd("post2", post2_f, p2_rows, p2_consts, [(D, f32)])
        acts.append(dict(x=h_cur, hb=hb, qn=qn, kn=kn, vs=vs, beta=beta, glog=glog, states=states, o_gdn=o_gdn, qm=qm, km=km,
                         scan=(g_w, g_qd, g_kd, g_at, g_el),
                         vm=vm, o_mla=o_mla, lse=lse, x1=x1, gate=gate, up=up, act=act, win=win, wd=wd, wg=wg, wu=wu,
                         cw8=cw8, alog_b=alog_b, dtb_b=dtb_b, mla_rows=mla_rows, mla_consts=mla_consts, p1_rows=p1_rows,
                         p1_consts=p1_consts, p2_rows=p2_rows, p2_consts=p2_consts))
        h_cur = y

    dy, loss_part = loss_head(h_cur, target)

    G = {n: [None] * DEPTH for n in list(BIG) + list(SMALL)}
    for l in reversed(range(DEPTH)):
        a = acts[l]
        dx1, ddown, dg2, db2, dwpg, dwple = row_bwd("post2_bwd", post2_f, a["p2_rows"], a["p2_consts"], [dy], [0, 1], [0, 1, 2, 3], [f32, f32])
        dgate, dup = ffn_dact(ddown, a["wd"], a["gate"], a["up"])
        dwd = mm("dw_down", a["act"], ddown, mode="tn", out_dtype=f32)
        dx1 = mm("dx_gate", dgate, a["wg"], mode="nt", out_dtype=f32, add=dx1)
        dx1 = mm("dx_up", dup, a["wu"], mode="nt", out_dtype=f32, add=dx1)
        dwg = mm("dw_gate", a["x1"], dgate, mode="tn", out_dtype=f32)
        dwu = mm("dw_up", a["x1"], dup, mode="tn", out_dtype=f32)
        do_gdn, dz, do_mla, dxr, dgg, dwout, dg1, db1 = row_bwd("post1_bwd", post1_f, a["p1_rows"], a["p1_consts"], [dx1], [0, 1, 2, 3],
                                                              [0, 1, 2, 3], [f32, bf16, f32, f32])
        dqm, dkm, dvm = attn_bwd(a["qm"], a["km"], a["vm"], a["o_mla"], a["lse"], do_mla)
        dstates = gdn_scan_bwd(do_gdn, *a["scan"])
        dqn, dkn, dvs, dbeta, dglog = gdn_local_bwd(a["qn"], a["kn"], a["vs"], a["beta"], a["glog"], a["states"], dstates, do_gdn)
        dqkv, dba, dcw, dal, ddt = gdn_pre_bwd(a["hb"], a["cw8"], a["alog_b"], a["dtb_b"], dqn, dkn, dvs, dbeta, dglog)
        dcq, dckv, dkrg, dkrs, dqg, dwq, dkvg, dwkv = row_bwd("mla_pre_bwd", mla_pre_f, a["mla_rows"], a["mla_consts"], [dqm, dkm, dvm],
                                                              [0, 1, 2, 3], [0, 1, 2, 3], [bf16, bf16, bf16, bf16], row_add={2: dba})
        dh = jnp.concatenate([dqkv, dz, dckv, dkrg, dkrs, dcq], axis=1)
        dwin = mm("dw_in", a["x"], dh, mode="tn", out_dtype=f32)
        dy = mm("dx_in", dh, a["win"], mode="nt", out_dtype=f32, add=dxr)

        G["w_in"][l] = _unprep_w_in(dwin)
        G["w_uq"][l] = _unprep_w_uq(dwq)
        G["w_ukv"][l], G["w_out"][l], G["w_down"][l] = dwkv, dwout, dwd
        G["w_gate_up"][l] = jnp.concatenate([dwg, dwu], axis=1)
        G["w_ple"][l], G["w_ple_gate"][l] = dwple, dwpg
        G["conv_w"][l] = dcw[:CW]
        G["a_log"][l] = jnp.sum(dal.reshape(H, DK), axis=1)
        G["dt_bias"][l] = jnp.sum(ddt.reshape(H, DK), axis=1)
        G["gdn_norm_g"][l] = jnp.sum(dgg.reshape(H, DK), axis=0)
        G["q_norm_g"][l] = dqg[0, :QL]
        G["kv_norm_g"][l] = dkvg[0]
        G["ln1_g"][l], G["ln1_b"][l], G["ln2_g"][l], G["ln2_b"][l] = dg1[0], db1[0], dg2[0], db2[0]
    grad_x = dy.reshape(x.shape)

    small_like = {n: W[n] for n in SMALL}
    conv_parts = jnp.stack(G["conv_w"]).reshape(DEPTH, CW, N_DEV, -1)
    smalls = []
    for d in range(N_DEV):
        tree = {n: jnp.stack(G[n]) for n in SMALL if n != "conv_w"}
        tree["conv_w"] = conv_parts[:, :, d, :]
        smalls.append(_pack_small(tree, loss_part))
    small_send = jnp.stack(smalls)
    big_send = jnp.concatenate([_shard_grad(G[n][l], n, shards) for l in range(DEPTH) for n in BIG], axis=1)
    n_big = big_send.shape[1]
    tail = (-(n_big + small_send.shape[1])) % ROW_TB
    send = jnp.concatenate([big_send, small_send, jnp.zeros((N_DEV, tail, D), f32)], axis=1)
    parts = exchange_shards(send)

    zero = jnp.zeros((), f32)
    pack3 = lambda tree: jnp.concatenate([_pack_big_local(tree, f32), _pack_small(tree, zero), jnp.zeros((tail, D), f32)], axis=0)
    g_rows, d_rows, m_rows, v_rows = adamw_rows(parts, pack3(W), pack3(M), pack3(V))

    def unpack(rows_arr):
        out = {}
        rows, _ = _big_rows(shards)
        for n in BIG:
            out[n] = jnp.stack([rows_arr[rows[(l, n)][0]:rows[(l, n)][0] + rows[(l, n)][1]].reshape(shards[n][1:]) for l in range(DEPTH)])
        sm, extra = _unpack_small(rows_arr[n_big:], small_like)
        out.update(sm)
        return out, extra

    g_out, loss = unpack(g_rows)
    d_out, _ = unpack(d_rows)
    m_out, _ = unpack(m_rows)
    v_out, _ = unpack(v_rows)
    order = ["w_in", "conv_w", "a_log", "dt_bias", "gdn_norm_g", "q_norm_g", "w_uq", "kv_norm_g", "w_ukv", "w_out", "ln1_g", "ln1_b",
             "w_gate_up", "w_down", "ln2_g", "ln2_b", "w_ple", "w_ple_gate"]
    return (loss, grad_x, *[g_out[n] for n in order], *[d_out[n] for n in order], *[m_out[n] for n in order],
            *[v_out[n] for n in order])
```

```python
import functools
import math

import jax
import jax.numpy as jnp
import numpy as np
from jax import lax
from jax.experimental import pallas as pl
from jax.experimental.pallas import tpu as pltpu

f32 = jnp.float32
bf16 = jnp.bfloat16
MX = jnp.bfloat16

D = 1024
DEPTH = 4
CHUNK = 64
H = 4
DK = 128
PLE = 256
QL = 384
KVL = 256
ROPE = 64
DFF = 2816
IN_W = 2760
ROPE_THETA = 10000.0
ALPHA = (2.0 * DEPTH) ** 0.25
LN_EPS = 1e-5
RMS_EPS = 1e-6
ATT_SCALE = (128 + 64) ** -0.5
N_DEV = 8

ADAM_LR, ADAM_B1, ADAM_B2, ADAM_EPS, ADAM_WD, ADAM_STEP = 0.001, 0.9, 0.999, 1e-08, 0.01, 10

OFF_QKV, OFF_Z, OFF_CKV, OFF_KR, OFF_KRS, OFF_CQ, HP = 0, 1536, 2048, 2304, 2432, 2560, 3072
CQP = 512
LANE = 128
VMEM_LIMIT = 48 * 1024 * 1024
ROW_TB = 256


def _dot(a, b, ca, cb, prec=None):
    if a.ndim == 3:
        return lax.dot_general(a, b, (((ca + 1,), (cb + 1,)), ((0,), (0,))), precision=prec, preferred_element_type=f32)
    return lax.dot_general(a, b, (((ca,), (cb,)), ((), ())), precision=prec, preferred_element_type=f32)


@jax.custom_vjp
def bdot(a, w):
    return _dot(a.astype(MX), w.astype(MX), 1, 0)


def _bdot_fwd(a, w):
    return bdot(a, w), (a, w)


def _bdot_bwd(res, g):
    a, w = res
    gb = g.astype(MX)
    return _dot(gb, w.astype(MX), 1, 1).astype(a.dtype), _dot(a.astype(MX), gb, 0, 0).astype(w.dtype)


bdot.defvjp(_bdot_fwd, _bdot_bwd)


@jax.custom_vjp
def bdot_nt(a, b):
    return _dot(a.astype(MX), b.astype(MX), 1, 1)


def _bdot_nt_fwd(a, b):
    return bdot_nt(a, b), (a, b)


def _bdot_nt_bwd(res, g):
    a, b = res
    gb = g.astype(MX)
    return _dot(gb, b.astype(MX), 1, 0).astype(a.dtype), _dot(gb, a.astype(MX), 0, 0).astype(b.dtype)


bdot_nt.defvjp(_bdot_nt_fwd, _bdot_nt_bwd)


@jax.custom_vjp
def bdot_tn(a, b):
    return _dot(a.astype(MX), b.astype(MX), 0, 0)


def _bdot_tn_fwd(a, b):
    return bdot_tn(a, b), (a, b)


def _bdot_tn_bwd(res, g):
    a, b = res
    gb = g.astype(MX)
    return _dot(b.astype(MX), gb, 1, 1).astype(a.dtype), _dot(a.astype(MX), gb, 1, 0).astype(b.dtype)


bdot_tn.defvjp(_bdot_tn_fwd, _bdot_tn_bwd)


@jax.custom_vjp
def hdot(a, b):
    return _dot(a, b, 1, 0, lax.Precision.HIGHEST)


def _hdot_fwd(a, b):
    return hdot(a, b), (a, b)


def _hdot_bwd(res, g):
    a, b = res
    return _dot(g, b, 1, 1, lax.Precision.HIGHEST), _dot(a, g, 0, 0, lax.Precision.HIGHEST)


hdot.defvjp(_hdot_fwd, _hdot_bwd)


def _pick(n, cap):
    best = None
    for t in range(LANE, min(n, cap) + 1, LANE):
        if n % t == 0:
            best = t
    assert best is not None, (n, cap)
    return best


def _cparams(sem):
    return pltpu.CompilerParams(dimension_semantics=sem, vmem_limit_bytes=VMEM_LIMIT)


def mm(name, a, b, *, mode, out_dtype, add=None):
    if mode == "tn":
        kdim, m = a.shape
        n = b.shape[1]
        tm, tn, tk = _pick(m, 1536), _pick(n, 1536), min(512, kdim)
        nk = kdim // tk

        def body(a_ref, b_ref, o_ref, acc):
            k = pl.program_id(2)

            @pl.when(k == 0)
            def _():
                acc[...] = jnp.zeros_like(acc)

            acc[...] += _dot(a_ref[...].astype(MX), b_ref[...].astype(MX), 0, 0)

            @pl.when(k == nk - 1)
            def _():
                o_ref[...] = acc[...].astype(o_ref.dtype)

        return pl.pallas_call(
            body, name=name, grid=(m // tm, n // tn, nk),
            in_specs=[pl.BlockSpec((tk, tm), lambda i, j, k: (k, i)), pl.BlockSpec((tk, tn), lambda i, j, k: (k, j))],
            out_specs=pl.BlockSpec((tm, tn), lambda i, j, k: (i, j)),
            out_shape=jax.ShapeDtypeStruct((m, n), out_dtype),
            scratch_shapes=[pltpu.VMEM((tm, tn), f32)],
            compiler_params=_cparams(("parallel", "parallel", "arbitrary")),
        )(a, b)

    m, kdim = a.shape
    n = b.shape[1] if mode == "nn" else b.shape[0]
    tm, tn = min(512, m), _pick(n, 1536)
    has_add = add is not None

    def body(*refs):
        a_ref, b_ref = refs[0], refs[1]
        o_ref = refs[-1]
        r = _dot(a_ref[...].astype(MX), b_ref[...].astype(MX), 1, 0 if mode == "nn" else 1)
        if has_add:
            r = r + refs[2][...].astype(f32)
        o_ref[...] = r.astype(o_ref.dtype)

    b_spec = (pl.BlockSpec((kdim, tn), lambda i, j: (0, j)) if mode == "nn"
              else pl.BlockSpec((tn, kdim), lambda i, j: (j, 0)))
    in_specs = [pl.BlockSpec((tm, kdim), lambda i, j: (i, 0)), b_spec]
    args = [a, b]
    if has_add:
        in_specs.append(pl.BlockSpec((tm, tn), lambda i, j: (i, j)))
        args.append(add)
    return pl.pallas_call(
        body, name=name, grid=(m // tm, n // tn), in_specs=in_specs,
        out_specs=pl.BlockSpec((tm, tn), lambda i, j: (i, j)),
        out_shape=jax.ShapeDtypeStruct((m, n), out_dtype),
        compiler_params=_cparams(("parallel", "parallel")),
    )(*args)


def _row_spec(tb, w, cb):
    return pl.BlockSpec((tb, w), lambda i: (i, cb))


def _full_spec(shape):
    return pl.BlockSpec(shape, lambda i: (0,) * len(shape))


def row_fwd(name, f, rows, consts, outs):
    t = rows[0][0].shape[0]
    tb = min(ROW_TB, t)
    nr, nc = len(rows), len(consts)

    def body(*refs):
        vals = [r[...] for r in refs[:nr + nc]]
        res = f(*vals)
        for o_ref, val in zip(refs[nr + nc:], res):
            o_ref[...] = val.astype(o_ref.dtype)

    return pl.pallas_call(
        body, name=name, grid=(t // tb,),
        in_specs=[_row_spec(tb, w, cb) for _, w, cb in rows] + [_full_spec(c.shape) for c in consts],
        out_specs=[_row_spec(tb, w, 0) for w, _ in outs],
        out_shape=[jax.ShapeDtypeStruct((t, w), dt) for w, dt in outs],
        compiler_params=_cparams(("parallel",)),
    )(*[r[0] for r in rows], *consts)


def row_bwd(name, f, rows, consts, cots, row_diff, const_diff, drow_dtypes, row_add=None):
    t = rows[0][0].shape[0]
    tb = min(ROW_TB, t)
    nr, nc, nct = len(rows), len(consts), len(cots)
    row_add = row_add or {}
    add_keys = sorted(row_add)
    n_in = nr + nc + nct + len(add_keys)

    def body(*refs):
        i = pl.program_id(0)
        rv = [r[...] for r in refs[:nr]]
        cv = [refs[nr + k][...].astype(f32) if k in const_diff else refs[nr + k][...] for k in range(nc)]
        cot_refs = refs[nr + nc:nr + nc + nct]
        add_refs = refs[nr + nc + nct:n_in]
        drow_refs = refs[n_in:n_in + len(row_diff)]
        dconst_refs = refs[n_in + len(row_diff):]

        def g(*dv):
            r2, c2 = list(rv), list(cv)
            for p, k in enumerate(row_diff):
                r2[k] = dv[p]
            for p, k in enumerate(const_diff):
                c2[k] = dv[len(row_diff) + p]
            return tuple(f(*r2, *c2))

        prim = [rv[k].astype(f32) for k in row_diff] + [cv[k] for k in const_diff]
        outs, vf = jax.vjp(g, *prim)
        grads = vf(tuple(c[...].astype(o.dtype) for c, o in zip(cot_refs, outs)))
        for p, ref in enumerate(drow_refs):
            val = grads[p]
            if p in row_add:
                val = val + add_refs[add_keys.index(p)][...].astype(f32)
            ref[...] = val.astype(ref.dtype)

        @pl.when(i == 0)
        def _():
            for ref in dconst_refs:
                ref[...] = jnp.zeros_like(ref)

        for p, ref in enumerate(dconst_refs):
            ref[...] += grads[len(row_diff) + p]

    widths = [rows[k][1] for k in row_diff]
    return pl.pallas_call(
        body, name=name, grid=(t // tb,),
        in_specs=([_row_spec(tb, w, cb) for _, w, cb in rows] + [_full_spec(c.shape) for c in consts]
                  + [_row_spec(tb, c.shape[1], 0) for c in cots] + [_row_spec(tb, row_add[k].shape[1], 0) for k in add_keys]),
        out_specs=([_row_spec(tb, w, 0) for w in widths] + [_full_spec(consts[k].shape) for k in const_diff]),
        out_shape=([jax.ShapeDtypeStruct((t, w), dt) for w, dt in zip(widths, drow_dtypes)]
                   + [jax.ShapeDtypeStruct(consts[k].shape, f32) for k in const_diff]),
        compiler_params=_cparams(("arbitrary",)),
    )(*[r[0] for r in rows], *consts, *cots, *[row_add[k] for k in add_keys])


def _heads(x, w=DK):
    return [x[:, w * h:w * (h + 1)] for h in range(H)]


def _layer_norm(r, g, b):
    mu = jnp.mean(r, -1, keepdims=True)
    var = jnp.mean(jnp.square(r - mu), -1, keepdims=True)
    return (r - mu) * lax.rsqrt(var + LN_EPS) * g + b


def gdn_point_f(c, ba, alog_b, dtb_b):
    s = c * jax.nn.sigmoid(c)
    q, k, v = s[:, :512], s[:, 512:1024], s[:, 1024:]

    def l2(x):
        return jnp.concatenate([xh * lax.rsqrt(jnp.sum(xh * xh, -1, keepdims=True) + RMS_EPS) for xh in _heads(x)], axis=1)

    tb = c.shape[0]
    b_b = jnp.concatenate([jnp.broadcast_to(ba[:, 64 + h:65 + h], (tb, DK)) for h in range(H)], axis=1)
    a_b = jnp.concatenate([jnp.broadcast_to(ba[:, 68 + h:69 + h], (tb, DK)) for h in range(H)], axis=1)
    beta = jax.nn.sigmoid(b_b)
    g = -jnp.exp(alog_b) * jax.nn.softplus(a_b + dtb_b)
    return l2(q), l2(k), v, beta, g


def mla_pre_f(cq, ckv, krg, krs, cf, sf, qg, wq, kvg, wkv):
    cqn = cq * lax.rsqrt(jnp.sum(cq * cq, -1, keepdims=True) * (1.0 / QL) + RMS_EPS) * qg
    qa = bdot(cqn, wq)
    ckvn = ckv * lax.rsqrt(jnp.mean(ckv * ckv, -1, keepdims=True) + RMS_EPS) * kvg
    kv = bdot(ckvn, wkv)
    kro = krg * cf + krs * sf
    qs, ks, vs = [], [], []
    for h in range(H):
        qs += [qa[:, DK * h:DK * (h + 1)], qa[:, 512 + DK * h:512 + DK * (h + 1)] * cf + qa[:, 1024 + DK * h:1024 + DK * (h + 1)] * sf]
        ks += [kv[:, 256 * h:256 * h + DK], kro]
        vs += [kv[:, 256 * h + DK:256 * (h + 1)]]
    return jnp.concatenate(qs, axis=1), jnp.concatenate(ks, axis=1), jnp.concatenate(vs, axis=1)


def post1_f(o, z, omla, x, gg_b, wout, g1, b1):
    on = jnp.concatenate([oh * lax.rsqrt(jnp.mean(oh * oh, -1, keepdims=True) + RMS_EPS) for oh in _heads(o)], axis=1) * gg_b
    ogdn = on * (z * jax.nn.sigmoid(z))
    mix = bdot(jnp.concatenate([ogdn, omla], axis=1), wout)
    return (_layer_norm(ALPHA * x + mix, g1, b1),)


def post2_f(x1, down, p, g2, b2, wpg, wple):
    x2 = _layer_norm(ALPHA * x1 + down, g2, b2)
    return (x2 + jax.nn.sigmoid(bdot(x2, wpg)) * bdot(p, wple),)


HALO = 8
CW = 4


def _conv_from_scratch(xs, cw_ref, tb):
    c = xs[pl.ds(HALO - 3, tb), :] * cw_ref[0:1, :]
    for j in range(1, CW):
        c = c + xs[pl.ds(HALO - 3 + j, tb), :] * cw_ref[j:j + 1, :]
    return c


def gdn_pre_fwd(hbuf, conv_w8, alog_b, dtb_b):
    t = hbuf.shape[0]
    tb = min(ROW_TB, t)

    def body(x_ref, halo_ref, ba_ref, cw_ref, al_ref, dt_ref, q_ref, k_ref, v_ref, be_ref, g_ref, xs):
        i = pl.program_id(0)
        xs[pl.ds(0, HALO), :] = jnp.where(i == 0, 0.0, halo_ref[...])
        xs[pl.ds(HALO, tb), :] = x_ref[...]
        c = _conv_from_scratch(xs, cw_ref, tb)
        q, k, v, be, g = gdn_point_f(c, ba_ref[...], al_ref[...], dt_ref[...])
        q_ref[...], k_ref[...], v_ref[...], be_ref[...], g_ref[...] = q, k, v, be, g

    return pl.pallas_call(
        body, name="gdn_pre_fwd", grid=(t // tb,),
        in_specs=[_row_spec(tb, 1536, 0),
                  pl.BlockSpec((HALO, 1536), lambda i: (jnp.maximum(i * (tb // HALO) - 1, 0), 0)),
                  _row_spec(tb, LANE, OFF_KR // LANE),
                  _full_spec(conv_w8.shape), _full_spec(alog_b.shape), _full_spec(dtb_b.shape)],
        out_specs=[_row_spec(tb, 512, 0)] * 5,
        out_shape=[jax.ShapeDtypeStruct((t, 512), f32)] * 5,
        scratch_shapes=[pltpu.VMEM((tb + HALO, 1536), f32)],
        compiler_params=_cparams(("arbitrary",)),
    )(hbuf, hbuf, hbuf, conv_w8, alog_b, dtb_b)


def gdn_pre_bwd(hbuf, conv_w8, alog_b, dtb_b, dq, dk, dv, dbe, dg):
    t = hbuf.shape[0]
    tb = min(ROW_TB, t)
    n = t // tb

    def body(x_ref, halo_ref, ba_ref, cw_ref, al_ref, dt_ref, dq_ref, dk_ref, dv_ref, dbe_ref, dg_ref,
             dx_ref, dba_ref, dcw_ref, dal_ref, ddt_ref, xs, dcs):
        s = pl.program_id(0)
        i = n - 1 - s
        xs[pl.ds(0, HALO), :] = jnp.where(i == 0, 0.0, halo_ref[...])
        xs[pl.ds(HALO, tb), :] = x_ref[...]
        c = _conv_from_scratch(xs, cw_ref, tb)
        _, vf = jax.vjp(gdn_point_f, c, ba_ref[...], al_ref[...], dt_ref[...])
        dc, dba, dal, ddt = vf((dq_ref[...], dk_ref[...], dv_ref[...], dbe_ref[...], dg_ref[...]))

        @pl.when(s == 0)
        def _():
            dcs[pl.ds(tb, HALO), :] = jnp.zeros((HALO, 1536), f32)
            dcw_ref[...] = jnp.zeros_like(dcw_ref)
            dal_ref[...] = jnp.zeros_like(dal_ref)
            ddt_ref[...] = jnp.zeros_like(ddt_ref)

        @pl.when(s > 0)
        def _():
            dcs[pl.ds(tb, HALO), :] = dcs[pl.ds(0, HALO), :]

        dcs[pl.ds(0, tb), :] = dc
        dx = dcs[pl.ds(3, tb), :] * cw_ref[0:1, :]
        for j in range(1, CW):
            dx = dx + dcs[pl.ds(3 - j, tb), :] * cw_ref[j:j + 1, :]
        dx_ref[...] = dx.astype(dx_ref.dtype)
        dba_ref[...] = dba
        for j in range(CW):
            dcw_ref[j:j + 1, :] += jnp.sum(dc * xs[pl.ds(HALO - 3 + j, tb), :], axis=0, keepdims=True)
        dal_ref[...] += dal
        ddt_ref[...] += ddt

    rev = lambda cb: (lambda s: (n - 1 - s, cb))
    return pl.pallas_call(
        body, name="gdn_pre_bwd", grid=(n,),
        in_specs=[pl.BlockSpec((tb, 1536), rev(0)),
                  pl.BlockSpec((HALO, 1536), lambda s: (jnp.maximum((n - 1 - s) * (tb // HALO) - 1, 0), 0)),
                  pl.BlockSpec((tb, LANE), rev(OFF_KR // LANE)),
                  _full_spec(conv_w8.shape), _full_spec(alog_b.shape), _full_spec(dtb_b.shape)]
        + [pl.BlockSpec((tb, 512), rev(0))] * 5,
        out_specs=[pl.BlockSpec((tb, 1536), rev(0)), pl.BlockSpec((tb, LANE), rev(0)),
                   _full_spec(conv_w8.shape), _full_spec(alog_b.shape), _full_spec(dtb_b.shape)],
        out_shape=[jax.ShapeDtypeStruct((t, 1536), bf16), jax.ShapeDtypeStruct((t, LANE), f32),
                   jax.ShapeDtypeStruct(conv_w8.shape, f32), jax.ShapeDtypeStruct(alog_b.shape, f32),
                   jax.ShapeDtypeStruct(dtb_b.shape, f32)],
        scratch_shapes=[pltpu.VMEM((tb + HALO, 1536), f32), pltpu.VMEM((tb + HALO, 1536), f32)],
        compiler_params=_cparams(("arbitrary",)),
    )(hbuf, hbuf, hbuf, conv_w8, alog_b, dtb_b, dq, dk, dv, dbe, dg)


GDN_NB = 4


def _to_batch(x, rows=CHUNK):
    n = x.shape[0] // rows
    return jnp.concatenate([x[rows * ci:rows * (ci + 1), DK * h:DK * (h + 1)][None] for ci in range(n) for h in range(H)], axis=0)


def _from_batch(y):
    n = y.shape[0] // H
    return jnp.concatenate([jnp.concatenate([y[ci * H + h] for h in range(H)], axis=1) for ci in range(n)], axis=0)


def gdn_intra_b(q, k, v, beta, g):
    c = CHUNK
    n = q.shape[0] // c
    tri = jnp.where(lax.broadcasted_iota(jnp.int32, (c, c), 0) >= lax.broadcasted_iota(jnp.int32, (c, c), 1), 1.0, 0.0)
    gc = _to_batch(jnp.concatenate([hdot(tri, g[c * ci:c * (ci + 1)]) for ci in range(n)], axis=0))
    qb, kb, vb, bb = _to_batch(q) * DK ** -0.5, _to_batch(k), _to_batch(v), _to_batch(beta)
    nbat = qb.shape[0]
    row = lax.broadcasted_iota(jnp.int32, (1, c, DK), 1)
    col = lax.broadcasted_iota(jnp.int32, (1, c, DK), 2)
    incl, strict, eye = row >= col, row > col, row == col
    grow = hdot(jnp.ones((nbat, c, c), f32), jnp.where(eye, gc, 0.0))
    decay = jnp.where(incl, jnp.exp(jnp.where(incl, gc - grow, 0.0)), 0.0)
    kbeta = kb * bb
    kpad = jnp.concatenate([kb, jnp.zeros((nbat, DK - c, DK), f32)], axis=1)
    a = jnp.where(strict, bdot_nt(kbeta, kpad) * decay, 0.0)[:, :, :c]
    nn = -a
    bk = bdot(a, a)
    for step in range(5):
        nn = nn + bk + bdot(nn, bk)
        if step < 4:
            bk = bdot(bk, bk)
    eg = jnp.exp(gc)
    rhs_v, rhs_k = vb * bb, kbeta * eg
    g_last = gc[:, c - 1:c, :]
    u = rhs_v + bdot(nn, rhs_v)
    w = rhs_k + bdot(nn, rhs_k)
    attn = jnp.where(incl, bdot_nt(qb, kpad) * decay, 0.0)
    return u, w, qb * eg, kb * jnp.exp(g_last - gc), attn, jnp.exp(g_last)


def gdn_scan_b(u, w, qd, kd, attn, el, state):
    v_new = u - bdot(w, state)
    return bdot(qd, state) + bdot(attn[:, :, :CHUNK], v_new), state * el + bdot_tn(kd, v_new)


def gdn_scan_bwd_b(do, w, qd, kd, attn, el, dstate):
    dvn = bdot_tn(attn[:, :, :CHUNK], do) + bdot(kd, dstate)
    return bdot_tn(qd, do) + dstate * el - bdot_tn(w, dvn)


def gdn_chunks_f(q, k, v, beta, g, states):
    n = states.shape[0]
    o, s_new = gdn_scan_b(*gdn_intra_b(q, k, v, beta, g), states.reshape(n * H, DK, DK))
    return _from_batch(o), s_new.reshape(n, H * DK, DK)


def _chunk_rows(cidx):
    return pl.ds(cidx * CHUNK, CHUNK)


def gdn_intra(q, k, v, beta, g):
    t = q.shape[0]
    nb = min(GDN_NB, t // CHUNK)
    tb = nb * CHUNK

    def body(q_ref, k_ref, v_ref, be_ref, g_ref, u_ref, w_ref, qd_ref, kd_ref, at_ref, el_ref):
        u, w, qd, kd, at, el = gdn_intra_b(q_ref[...], k_ref[...], v_ref[...], be_ref[...], g_ref[...])
        u_ref[...], w_ref[...], qd_ref[...], kd_ref[...], at_ref[...] = (_from_batch(a) for a in (u, w, qd, kd, at))
        el_ref[...] = _from_batch(jnp.broadcast_to(el, (nb * H, 8, DK))).reshape(nb, 8, H * DK)

    return pl.pallas_call(
        body, name="gdn_intra", grid=(t // tb,),
        in_specs=[_row_spec(tb, 512, 0)] * 5,
        out_specs=[_row_spec(tb, 512, 0)] * 5 + [pl.BlockSpec((nb, 8, 512), lambda i: (i, 0, 0))],
        out_shape=[jax.ShapeDtypeStruct((t, 512), f32)] * 5 + [jax.ShapeDtypeStruct((t // CHUNK, 8, 512), f32)],
        compiler_params=_cparams(("parallel",)),
    )(q, k, v, beta, g)


def gdn_scan_fwd(u, w, qd, kd, attn, el):
    t = u.shape[0]
    nb = min(GDN_NB, t // CHUNK)
    tb = nb * CHUNK

    def body(u_ref, w_ref, qd_ref, kd_ref, at_ref, el_ref, o_ref, s_ref, state):
        @pl.when(pl.program_id(0) == 0)
        def _():
            state[...] = jnp.zeros_like(state)

        for cidx in range(nb):
            r = _chunk_rows(cidx)
            s_ref[cidx] = state[...]
            ins = [_to_batch(ref[r, :]) for ref in (u_ref, w_ref, qd_ref, kd_ref, at_ref)]
            el = _to_batch(el_ref[cidx], 8)[:, 0:1, :]
            o, s_new = gdn_scan_b(*ins, el, state[...].reshape(H, DK, DK))
            o_ref[r, :] = _from_batch(o)
            state[...] = s_new.reshape(H * DK, DK)

    return pl.pallas_call(
        body, name="gdn_scan_fwd", grid=(t // tb,),
        in_specs=[_row_spec(tb, 512, 0)] * 5 + [pl.BlockSpec((nb, 8, 512), lambda i: (i, 0, 0))],
        out_specs=[_row_spec(tb, 512, 0), pl.BlockSpec((nb, 512, DK), lambda i: (i, 0, 0))],
        out_shape=[jax.ShapeDtypeStruct((t, 512), f32), jax.ShapeDtypeStruct((t // CHUNK, 512, DK), f32)],
        scratch_shapes=[pltpu.VMEM((512, DK), f32)],
        compiler_params=_cparams(("arbitrary",)),
    )(u, w, qd, kd, attn, el)


def gdn_scan_bwd(do, w, qd, kd, attn, el):
    t = do.shape[0]
    nb = min(GDN_NB, t // CHUNK)
    tb = nb * CHUNK
    n = t // tb

    def body(do_ref, w_ref, qd_ref, kd_ref, at_ref, el_ref, ds_ref, dstate):
        @pl.when(pl.program_id(0) == 0)
        def _():
            dstate[...] = jnp.zeros_like(dstate)

        for cidx in reversed(range(nb)):
            r = _chunk_rows(cidx)
            ds_ref[cidx] = dstate[...]
            ins = [_to_batch(ref[r, :]) for ref in (do_ref, w_ref, qd_ref, kd_ref, at_ref)]
            el = _to_batch(el_ref[cidx], 8)[:, 0:1, :]
            dstate[...] = gdn_scan_bwd_b(*ins, el, dstate[...].reshape(H, DK, DK)).reshape(H * DK, DK)

    rev = pl.BlockSpec((tb, 512), lambda s: (n - 1 - s, 0))
    return pl.pallas_call(
        body, name="gdn_scan_bwd", grid=(n,),
        in_specs=[rev] * 5 + [pl.BlockSpec((nb, 8, 512), lambda s: (n - 1 - s, 0, 0))],
        out_specs=pl.BlockSpec((nb, 512, DK), lambda s: (n - 1 - s, 0, 0)),
        out_shape=jax.ShapeDtypeStruct((t // CHUNK, 512, DK), f32),
        scratch_shapes=[pltpu.VMEM((512, DK), f32)],
        compiler_params=_cparams(("arbitrary",)),
    )(do, w, qd, kd, attn, el)


def gdn_local_bwd(q, k, v, beta, g, states, dstates, do):
    t = q.shape[0]
    nb = min(GDN_NB, t // CHUNK)
    tb = nb * CHUNK

    def body(q_ref, k_ref, v_ref, be_ref, g_ref, s_ref, ds_ref, do_ref, dq_ref, dk_ref, dv_ref, dbe_ref, dg_ref):
        states_v = s_ref[...]
        _, vf = jax.vjp(lambda *a: gdn_chunks_f(*a, states_v), q_ref[...], k_ref[...], v_ref[...], be_ref[...], g_ref[...])
        dq_ref[...], dk_ref[...], dv_ref[...], dbe_ref[...], dg_ref[...] = vf((do_ref[...], ds_ref[...]))

    st = pl.BlockSpec((nb, 512, DK), lambda i: (i, 0, 0))
    return pl.pallas_call(
        body, name="gdn_local_bwd", grid=(t // tb,),
        in_specs=[_row_spec(tb, 512, 0)] * 5 + [st, st, _row_spec(tb, 512, 0)],
        out_specs=[_row_spec(tb, 512, 0)] * 5,
        out_shape=[jax.ShapeDtypeStruct((t, 512), f32)] * 5,
        compiler_params=_cparams(("parallel",)),
    )(q, k, v, beta, g, states, dstates, do)


NEG = -1e30


def _chunk_mask(i, j, tq, tk):
    qrow = lax.broadcasted_iota(jnp.int32, (tq, tk), 0) + i * tq
    kcol = lax.broadcasted_iota(jnp.int32, (tq, tk), 1) + j * tk
    return jnp.right_shift(kcol, 6) <= jnp.right_shift(qrow, 6)


def _block_unmasked(i, j, tq, tk):
    return (i * tq) // CHUNK >= ((j + 1) * tk - 1) // CHUNK


LOG2E = 1.4426950408889634
ATT_C2 = ATT_SCALE * LOG2E


def _hosted(refs, n_in, n_out, n_scratch):
    ins, src = refs[:n_in], refs[n_in]
    outs, dst = refs[n_in + 1:n_in + 1 + n_out], refs[n_in + 1 + n_out]
    rest = refs[n_in + n_out + 2:]
    return ins, outs, rest[:n_scratch], (src, dst) + tuple(rest[n_scratch:])


def _host_exchange(comm_refs, scatter, is_first, is_last):
    @pl.when(is_first)
    def _():
        for op in _exchange_copies(*comm_refs, scatter):
            op.start()

    @pl.when(is_last)
    def _():
        for op in _exchange_copies(*comm_refs, scatter):
            op.wait()


def attn_fwd(q, k, v, name="attn_fwd", comm=None):
    t = q.shape[0]
    tq = tk = min(1024, t)
    nq, nk = t // tq, t // tk
    last = lambda i: ((i + 1) * tq - 1) // tk

    def body(*refs):
        if comm is None:
            (q_ref, k_ref, v_ref), (o_ref, lse_ref), (m_s, l_s, acc) = refs[:3], refs[3:5], refs[5:]
        else:
            (q_ref, k_ref, v_ref), (o_ref, lse_ref), (m_s, l_s, acc), comm_refs = _hosted(refs, 3, 2, 3)
        h, i, j = pl.program_id(0), pl.program_id(1), pl.program_id(2)
        if comm is not None:
            _host_exchange(comm_refs, comm[1], (h == 0) & (i == 0) & (j == 0), (h == H - 1) & (i == nq - 1) & (j == nk - 1))

        @pl.when(j == 0)
        def _():
            m_s[...] = jnp.full_like(m_s, NEG)
            l_s[...] = jnp.zeros_like(l_s)
            acc[...] = jnp.zeros_like(acc)

        def step(masked):
            s = _dot(q_ref[...], k_ref[...], 1, 1)
            if masked:
                s = jnp.where(_chunk_mask(i, j, tq, tk), s, NEG)
            m_new = jnp.maximum(m_s[...], jnp.max(s, axis=-1, keepdims=True))
            p = jnp.exp2((s - m_new) * ATT_C2)
            alpha = jnp.exp2((m_s[...] - m_new) * ATT_C2)
            l_s[...] = alpha * l_s[...] + jnp.sum(p, axis=-1, keepdims=True)
            acc[...] = alpha * acc[...] + _dot(p.astype(MX), v_ref[...], 1, 0)
            m_s[...] = m_new

        whole = _block_unmasked(i, j, tq, tk)
        pl.when(whole)(lambda: step(False))
        pl.when((j <= last(i)) & jnp.logical_not(whole))(lambda: step(True))

        @pl.when(j == nk - 1)
        def _():
            o_ref[...] = acc[...] / l_s[...]
            lse_ref[0] = m_s[...] * ATT_SCALE + jnp.log(l_s[...])

    in_specs = [pl.BlockSpec((tq, 256), lambda h, i, j: (i, h)),
                pl.BlockSpec((tk, 256), lambda h, i, j: (jnp.minimum(j, last(i)), h)),
                pl.BlockSpec((tk, DK), lambda h, i, j: (jnp.minimum(j, last(i)), h))]
    out_specs = [pl.BlockSpec((tq, DK), lambda h, i, j: (i, h)), pl.BlockSpec((1, tq, 1), lambda h, i, j: (h, i, 0))]
    out_shape = [jax.ShapeDtypeStruct((t, 512), f32), jax.ShapeDtypeStruct((H, t, 1), f32)]
    scratch = [pltpu.VMEM((tq, 1), f32), pltpu.VMEM((tq, 1), f32), pltpu.VMEM((tq, DK), f32)]
    args = [q, k, v]
    if comm is not None:
        in_specs.append(_ANY)
        out_specs.append(_ANY)
        out_shape.append(_comm_out_shape(*comm))
        scratch += _comm_scratch()
        args.append(comm[0])
    sem = ("parallel", "parallel", "arbitrary") if comm is None else ("arbitrary",) * 3
    return pl.pallas_call(body, name=name, grid=(H, nq, nk), in_specs=in_specs, out_specs=out_specs, out_shape=out_shape,
                          scratch_shapes=scratch, compiler_params=_cparams(sem))(*args)


def attn_bwd(q, k, v, o, lse, do, name="attn_bwd", comm=None):
    t = q.shape[0]
    tq = tk = min(512, t)
    nq, nk = t // tq, t // tk
    first = lambda j: (j * tk) // tq

    def body(*refs):
        if comm is None:
            ins, outs, (dk_acc, dv_acc) = refs[:6], refs[6:9], refs[9:]
        else:
            ins, outs, (dk_acc, dv_acc), comm_refs = _hosted(refs, 6, 3, 2)
        q_ref, k_ref, v_ref, o_ref, lse_ref, do_ref = ins
        dq_ref, dk_ref, dv_ref = outs
        h, j, i = pl.program_id(0), pl.program_id(1), pl.program_id(2)
        if comm is not None:
            _host_exchange(comm_refs, comm[1], (h == 0) & (i == 0) & (j == 0), (h == H - 1) & (i == nq - 1) & (j == nk - 1))

        @pl.when((j == 0) & (i == 0))
        def _():
            dq_ref[...] = jnp.zeros_like(dq_ref)

        @pl.when(i == 0)
        def _():
            dk_acc[...] = jnp.zeros_like(dk_acc)
            dv_acc[...] = jnp.zeros_like(dv_acc)

        def step(masked):
            qb, kb, vb = q_ref[...], k_ref[...], v_ref[...]
            dob = do_ref[...]
            s = _dot(qb, kb, 1, 1)
            if masked:
                s = jnp.where(_chunk_mask(i, j, tq, tk), s, NEG)
            p = jnp.exp2(s * ATT_C2 - lse_ref[0] * LOG2E)
            dv_acc[...] += _dot(p.astype(MX), dob.astype(MX), 0, 0)
            dp = _dot(dob.astype(MX), vb, 1, 1)
            delta = jnp.sum(dob * o_ref[...], axis=-1, keepdims=True)
            ds = (p * (dp - delta)).astype(MX)
            dk_acc[...] += _dot(ds, qb, 0, 0) * ATT_SCALE
            rows = pl.ds(pl.multiple_of(i * tq, tq), tq)
            dq_ref[rows, :] += _dot(ds, kb, 1, 0) * ATT_SCALE

        whole = _block_unmasked(i, j, tq, tk)
        pl.when(whole)(lambda: step(False))
        pl.when((i >= first(j)) & jnp.logical_not(whole))(lambda: step(True))

        @pl.when(i == nq - 1)
        def _():
            dk_ref[...] = dk_acc[...]
            dv_ref[...] = dv_acc[...]

    qi = lambda h, j, i: (jnp.maximum(i, first(j)), h)
    in_specs = [pl.BlockSpec((tq, 256), qi),
                pl.BlockSpec((tk, 256), lambda h, j, i: (j, h)),
                pl.BlockSpec((tk, DK), lambda h, j, i: (j, h)),
                pl.BlockSpec((tq, DK), qi),
                pl.BlockSpec((1, tq, 1), lambda h, j, i: (h, jnp.maximum(i, first(j)), 0)),
                pl.BlockSpec((tq, DK), qi)]
    out_specs = [pl.BlockSpec((t, 256), lambda h, j, i: (0, h)),
                 pl.BlockSpec((tk, 256), lambda h, j, i: (j, h)),
                 pl.BlockSpec((tk, DK), lambda h, j, i: (j, h))]
    out_shape = [jax.ShapeDtypeStruct((t, 1024), f32), jax.ShapeDtypeStruct((t, 1024), f32), jax.ShapeDtypeStruct((t, 512), f32)]
    scratch = [pltpu.VMEM((tk, 256), f32), pltpu.VMEM((tk, DK), f32)]
    args = [q, k, v, o, lse, do]
    if comm is not None:
        in_specs.append(_ANY)
        out_specs.append(_ANY)
        out_shape.append(_comm_out_shape(*comm))
        scratch += _comm_scratch()
        args.append(comm[0])
    return pl.pallas_call(body, name=name, grid=(H, nk, nq), in_specs=in_specs, out_specs=out_specs, out_shape=out_shape,
                          scratch_shapes=scratch, compiler_params=_cparams(("arbitrary",) * 3))(*args)


def ffn_up(x1, wg, wu):
    t = x1.shape[0]
    tm, tn = min(512, t), _pick(DFF, 1536)

    def body(x_ref, wg_ref, wu_ref, g_ref, u_ref, a_ref):
        xb = x_ref[...].astype(MX)
        g = _dot(xb, wg_ref[...], 1, 0)
        u = _dot(xb, wu_ref[...], 1, 0)
        g_ref[...] = g.astype(g_ref.dtype)
        u_ref[...] = u.astype(u_ref.dtype)
        a_ref[...] = (g * jax.nn.sigmoid(g) * u).astype(a_ref.dtype)

    w_spec = pl.BlockSpec((D, tn), lambda i, j: (0, j))
    o_spec = pl.BlockSpec((tm, tn), lambda i, j: (i, j))
    return pl.pallas_call(
        body, name="ffn_up", grid=(t // tm, DFF // tn),
        in_specs=[pl.BlockSpec((tm, D), lambda i, j: (i, 0)), w_spec, w_spec],
        out_specs=[o_spec] * 3, out_shape=[jax.ShapeDtypeStruct((t, DFF), bf16)] * 3,
        compiler_params=_cparams(("parallel", "parallel")),
    )(x1, wg, wu)


def ffn_dact(ddown, wd, g, u):
    t = ddown.shape[0]
    tm, tn = min(512, t), _pick(DFF, 1536)

    def body(dd_ref, wd_ref, g_ref, u_ref, dg_ref, du_ref):
        dact = _dot(dd_ref[...].astype(MX), wd_ref[...], 1, 1)
        gv, uv = g_ref[...].astype(f32), u_ref[...].astype(f32)
        sig = jax.nn.sigmoid(gv)
        dg_ref[...] = (dact * uv * sig * (1.0 + gv * (1.0 - sig))).astype(dg_ref.dtype)
        du_ref[...] = (dact * gv * sig).astype(du_ref.dtype)

    o_spec = pl.BlockSpec((tm, tn), lambda i, j: (i, j))
    return pl.pallas_call(
        body, name="ffn_dact", grid=(t // tm, DFF // tn),
        in_specs=[pl.BlockSpec((tm, D), lambda i, j: (i, 0)), pl.BlockSpec((tn, D), lambda i, j: (j, 0)), o_spec, o_spec],
        out_specs=[o_spec] * 2, out_shape=[jax.ShapeDtypeStruct((t, DFF), bf16)] * 2,
        compiler_params=_cparams(("parallel", "parallel")),
    )(ddown, wd, g, u)


def loss_head(y, target):
    t = y.shape[0]
    tb = min(ROW_TB, t)
    n = t // tb

    def body(y_ref, t_ref, dy_ref, loss_ref, acc):
        i = pl.program_id(0)

        @pl.when(i == 0)
        def _():
            acc[...] = jnp.zeros_like(acc)

        e = y_ref[...] - t_ref[...]
        dy_ref[...] = e * (1.0 / D)
        acc[...] += jnp.sum(e * e, axis=0, keepdims=True)

        @pl.when(i == n - 1)
        def _():
            loss_ref[...] = jnp.sum(acc[...], axis=1, keepdims=True) * (0.5 / D)

    return pl.pallas_call(
        body, name="loss_head", grid=(n,),
        in_specs=[_row_spec(tb, D, 0)] * 2,
        out_specs=[_row_spec(tb, D, 0), _full_spec((1, 1))],
        out_shape=[jax.ShapeDtypeStruct((t, D), f32), jax.ShapeDtypeStruct((1, 1), f32)],
        scratch_shapes=[pltpu.VMEM((1, D), f32)],
        compiler_params=_cparams(("arbitrary",)),
    )(y, target)


def _me_and_peers():
    x, y, c = lax.axis_index("x"), lax.axis_index("y"), lax.axis_index("c")
    me = 4 * x + 2 * y + c
    peers = []
    for kk in range(1, N_DEV):
        px = 1 - x if kk & 4 else x
        py = 1 - y if kk & 2 else y
        pc = 1 - c if kk & 1 else c
        peers.append(((px, py, pc), 4 * px + 2 * py + pc))
    return me, peers


_ANY = pl.BlockSpec(memory_space=pl.ANY)


def _comm_scratch():
    return [pltpu.SemaphoreType.DMA((N_DEV - 1,)), pltpu.SemaphoreType.DMA((N_DEV - 1,)), pltpu.SemaphoreType.DMA]


def _exchange_copies(src_ref, out_ref, send_sems, recv_sems, local_sem, scatter):
    me, peers = _me_and_peers()
    pick = (lambda d: src_ref.at[d]) if scatter else (lambda d: src_ref)
    ops = [pltpu.make_async_copy(pick(me), out_ref.at[me], local_sem)]
    ops += [pltpu.make_async_remote_copy(src_ref=pick(pid), dst_ref=out_ref.at[me], send_sem=send_sems.at[kk],
                                         recv_sem=recv_sems.at[kk], device_id=dev, device_id_type=pl.DeviceIdType.MESH)
            for kk, (dev, pid) in enumerate(peers)]
    return ops


def _comm_out_shape(src, scatter):
    return jax.ShapeDtypeStruct(src.shape if scatter else (N_DEV,) + src.shape, src.dtype)


def exchange(name, src, scatter):
    def body(s_ref, out_ref, send_sems, recv_sems, local_sem):
        ops = _exchange_copies(s_ref, out_ref, send_sems, recv_sems, local_sem, scatter)
        for op in ops:
            op.start()
        for op in ops:
            op.wait()

    return pl.pallas_call(body, name=name, in_specs=[_ANY], out_specs=_ANY, out_shape=_comm_out_shape(src, scatter),
                          scratch_shapes=_comm_scratch())(src)


def adamw_rows(parts, w, m, v):
    r = w.shape[0]
    tb = _pick_rows(r)

    def body(p_ref, w_ref, m_ref, v_ref, g_ref, d_ref, nm_ref, nv_ref):
        g = p_ref[0]
        for s in range(1, N_DEV):
            g = g + p_ref[s]
        nm = ADAM_B1 * m_ref[...] + (1.0 - ADAM_B1) * g
        nv = ADAM_B2 * v_ref[...] + (1.0 - ADAM_B2) * jnp.square(g)
        m_hat = nm / (1.0 - ADAM_B1 ** ADAM_STEP)
        v_hat = nv / (1.0 - ADAM_B2 ** ADAM_STEP)
        g_ref[...] = g
        d_ref[...] = -ADAM_LR * (m_hat / (jnp.sqrt(v_hat) + ADAM_EPS) + ADAM_WD * w_ref[...])
        nm_ref[...] = nm
        nv_ref[...] = nv

    spec = pl.BlockSpec((tb, D), lambda i: (i, 0))
    return pl.pallas_call(
        body, name="adamw", grid=(r // tb,),
        in_specs=[pl.BlockSpec((N_DEV, tb, D), lambda i: (0, i, 0)), spec, spec, spec],
        out_specs=[spec] * 4, out_shape=[jax.ShapeDtypeStruct((r, D), f32)] * 4,
        compiler_params=_cparams(("parallel",)),
    )(parts, w, m, v)


def _pick_rows(r):
    return max(tb for tb in range(8, ROW_TB + 1, 8) if r % tb == 0)


BIG = ("w_in", "w_uq", "w_ukv", "w_out", "w_gate_up", "w_down", "w_ple", "w_ple_gate")
COL_SHARDED = ("w_in", "w_uq", "w_ukv", "w_gate_up", "w_ple")
SMALL = ("conv_w", "a_log", "dt_bias", "gdn_norm_g", "q_norm_g", "kv_norm_g", "ln1_g", "ln1_b", "ln2_g", "ln2_b")
FULL_SHAPE = {"w_in": (D, IN_W), "w_uq": (QL, 768), "w_ukv": (KVL, 1024), "w_out": (D, D), "w_gate_up": (D, 2 * DFF),
              "w_down": (DFF, D), "w_ple": (PLE, D), "w_ple_gate": (D, D)}
ROW_ALIGN = 16


def _pad_rows(a, mult=ROW_ALIGN, axis=0):
    pad = (-a.shape[axis]) % mult
    widths = [(0, 0)] * a.ndim
    widths[axis] = (0, pad)
    return a if pad == 0 else jnp.pad(a, widths)


def _pack_layer(tree, l, dtype):
    return jnp.concatenate([_pad_rows(tree[n][l].reshape(-1, D).astype(dtype)) for n in BIG], axis=0)


def _big_rows(shards):
    rows, off = {}, 0
    for n in BIG:
        r = shards[n][1] * shards[n][2] // D
        rows[n] = (off, r)
        off += r + (-r) % ROW_ALIGN
    return rows, off


def _unpack_gathered(gathered, shards, n):
    off, r = _big_rows(shards)[0][n]
    _, sr, sc = shards[n]
    blk = gathered[:, off:off + r, :].reshape(N_DEV, sr, sc)
    if n in COL_SHARDED:
        return jnp.transpose(blk, (1, 0, 2)).reshape(sr, N_DEV * sc)
    return blk.reshape(N_DEV * sr, sc)


def _shard_grad(gfull, n, shards):
    _, sr, sc = shards[n]
    if n in COL_SHARDED:
        blk = jnp.transpose(gfull.reshape(sr, N_DEV, sc), (1, 0, 2))
    else:
        blk = gfull.reshape(N_DEV, sr, sc)
    return _pad_rows(blk.reshape(N_DEV, sr * sc // D, D), axis=1)


def _pack_small(tree, extra):
    flat = jnp.concatenate([tree[n].reshape(-1).astype(f32) for n in SMALL] + [extra.reshape(-1).astype(f32)])
    return jnp.pad(flat, (0, (-flat.shape[0]) % (ROW_ALIGN * D))).reshape(-1, D)


def _unpack_small(rows, like):
    flat = rows.reshape(-1)
    out, off = {}, 0
    for n in SMALL:
        sz = int(np.prod(like[n].shape))
        out[n] = flat[off:off + sz].reshape(like[n].shape)
        off += sz
    return out, flat[off]


def _zeros(r, c, dt):
    return jnp.zeros((r, c), dt)


def _prep_w_in(w):
    dt = w.dtype
    kr = w[:, 2696:2760]
    return jnp.concatenate([
        w[:, 0:2048],
        w[:, 2440:2696],
        kr, w[:, 2048:2056], _zeros(D, 56, dt),
        kr[:, 32:], kr[:, :32], _zeros(D, 64, dt),
        w[:, 2056:2440], _zeros(D, CQP - QL, dt)], axis=1)


def _unprep_w_in(g):
    krs = g[:, OFF_KRS:OFF_KRS + 64]
    kr = g[:, OFF_KR:OFF_KR + 64] + jnp.concatenate([krs[:, 32:], krs[:, :32]], axis=1)
    return jnp.concatenate([g[:, 0:2048], g[:, OFF_KR + 64:OFF_KR + 72], g[:, OFF_CQ:OFF_CQ + QL],
                            g[:, OFF_CKV:OFF_CKV + KVL], kr], axis=1)


def _prep_w_uq(w):
    dt = w.dtype
    z64 = _zeros(QL, 64, dt)
    nope, ra, rb = [], [], []
    for h in range(H):
        nope.append(w[:, 192 * h:192 * h + 128])
        x1, x2 = w[:, 192 * h + 128:192 * h + 160], w[:, 192 * h + 160:192 * h + 192]
        ra += [x1, x2, z64]
        rb += [x2, x1, z64]
    return jnp.concatenate([jnp.concatenate(nope + ra + rb, axis=1), _zeros(CQP - QL, 1536, dt)], axis=0)


def _unprep_w_uq(g):
    g = g[:QL]
    cols = []
    for h in range(H):
        a = g[:, 512 + 128 * h:512 + 128 * h + 64]
        b = g[:, 1024 + 128 * h:1024 + 128 * h + 64]
        cols += [g[:, 128 * h:128 * (h + 1)], a[:, :32] + b[:, 32:], a[:, 32:] + b[:, :32]]
    return jnp.concatenate(cols, axis=1)


def _rope_tables(positions):
    inv_freq = ROPE_THETA ** (-jnp.arange(0, ROPE, 2, dtype=f32) / ROPE)
    ang = positions.astype(f32)[:, None] * inv_freq
    c, s = jnp.cos(ang), jnp.sin(ang)
    z = jnp.zeros((positions.shape[0], 64), f32)
    return jnp.concatenate([c, c, z], axis=1), jnp.concatenate([-s, s, z], axis=1)


def _tile_heads(vec, n=H):
    return jnp.tile(vec.reshape(1, -1), (1, n))


def _bcast_heads(vec):
    return jnp.repeat(vec, DK).reshape(1, H * DK)


def kernel(x, p, positions, w_in, conv_w, a_log, dt_bias, gdn_norm_g, q_norm_g, w_uq, kv_norm_g, w_ukv, w_out, ln1_g, ln1_b, w_gate_up, w_down, ln2_g, ln2_b, w_ple, w_ple_gate, loss_target, m_w_in, m_conv_w, m_a_log, m_dt_bias, m_gdn_norm_g, m_q_norm_g, m_w_uq, m_kv_norm_g, m_w_ukv, m_w_out, m_ln1_g, m_ln1_b, m_w_gate_up, m_w_down, m_ln2_g, m_ln2_b, m_w_ple, m_w_ple_gate, v_w_in, v_conv_w, v_a_log, v_dt_bias, v_gdn_norm_g, v_q_norm_g, v_w_uq, v_kv_norm_g, v_w_ukv, v_w_out, v_ln1_g, v_ln1_b, v_w_gate_up, v_w_down, v_ln2_g, v_ln2_b, v_w_ple, v_w_ple_gate):
    W = dict(w_in=w_in, conv_w=conv_w, a_log=a_log, dt_bias=dt_bias, gdn_norm_g=gdn_norm_g, q_norm_g=q_norm_g, w_uq=w_uq,
             kv_norm_g=kv_norm_g, w_ukv=w_ukv, w_out=w_out, ln1_g=ln1_g, ln1_b=ln1_b, w_gate_up=w_gate_up, w_down=w_down,
             ln2_g=ln2_g, ln2_b=ln2_b, w_ple=w_ple, w_ple_gate=w_ple_gate)
    M = dict(w_in=m_w_in, conv_w=m_conv_w, a_log=m_a_log, dt_bias=m_dt_bias, gdn_norm_g=m_gdn_norm_g, q_norm_g=m_q_norm_g,
             w_uq=m_w_uq, kv_norm_g=m_kv_norm_g, w_ukv=m_w_ukv, w_out=m_w_out, ln1_g=m_ln1_g, ln1_b=m_ln1_b,
             w_gate_up=m_w_gate_up, w_down=m_w_down, ln2_g=m_ln2_g, ln2_b=m_ln2_b, w_ple=m_w_ple, w_ple_gate=m_w_ple_gate)
    V = dict(w_in=v_w_in, conv_w=v_conv_w, a_log=v_a_log, dt_bias=v_dt_bias, gdn_norm_g=v_gdn_norm_g, q_norm_g=v_q_norm_g,
             w_uq=v_w_uq, kv_norm_g=v_kv_norm_g, w_ukv=v_w_ukv, w_out=v_w_out, ln1_g=v_ln1_g, ln1_b=v_ln1_b,
             w_gate_up=v_w_gate_up, w_down=v_w_down, ln2_g=v_ln2_g, ln2_b=v_ln2_b, w_ple=v_w_ple, w_ple_gate=v_w_ple_gate)
    shards = {n: W[n].shape for n in BIG}
    t = x.shape[1]
    xin = x.reshape(t, D)
    target = loss_target.reshape(t, D)
    cf, sf = _rope_tables(positions.reshape(t))

    packs = [_pack_layer(W, l, bf16) for l in range(DEPTH)]
    gathered = [exchange("all_gather_weights", packs[0], False)] + [None] * (DEPTH - 1)
    conv_g = exchange("all_gather_conv", _pad_rows(jnp.pad(conv_w.reshape(-1), (0, (-conv_w.size) % D)).reshape(-1, D), 8), False)
    csz = conv_w.shape[1] * conv_w.shape[2]
    conv_full = jnp.transpose(conv_g.reshape(N_DEV, -1)[:, :DEPTH * csz].reshape(N_DEV, DEPTH, CW, -1), (1, 2, 0, 3)).reshape(DEPTH, CW, 1536)

    acts = []
    h_cur = xin
    for l in range(DEPTH):
        full = functools.partial(_unpack_gathered, gathered[l], shards)
        win = _prep_w_in(full("w_in"))
        wq = _prep_w_uq(full("w_uq"))
        wkv, wout = full("w_ukv"), full("w_out")
        wgu, wd = full("w_gate_up"), full("w_down")
        wg, wu = wgu[:, :DFF], wgu[:, DFF:]
        wple, wpg = full("w_ple"), full("w_ple_gate")
        cw8 = jnp.concatenate([conv_full[l], jnp.zeros((8 - CW, 1536), f32)], axis=0)
        alog_b, dtb_b = _bcast_heads(a_log[l]), _bcast_heads(dt_bias[l])
        gg_b = _tile_heads(gdn_norm_g[l])
        qg = jnp.concatenate([q_norm_g[l], jnp.zeros((CQP - QL,), f32)]).reshape(1, CQP)
        kvg = kv_norm_g[l].reshape(1, KVL)
        g1, b1, g2, b2 = (a[l].reshape(1, D) for a in (ln1_g, ln1_b, ln2_g, ln2_b))
        p_l = p[l].reshape(t, PLE)

        hb = mm("in_proj", h_cur, win, mode="nn", out_dtype=f32)
        qn, kn, vs, beta, glog = gdn_pre_fwd(hb, cw8, alog_b, dtb_b)
        g_u, g_w, g_qd, g_kd, g_at, g_el = gdn_intra(qn, kn, vs, beta, glog)
        o_gdn, states = gdn_scan_fwd(g_u, g_w, g_qd, g_kd, g_at, g_el)
        mla_rows = [(hb, CQP, OFF_CQ // CQP), (hb, KVL, OFF_CKV // KVL), (hb, LANE, OFF_KR // LANE), (hb, LANE, OFF_KRS // LANE),
                    (cf, LANE, 0), (sf, LANE, 0)]
        mla_consts = [qg, wq, kvg, wkv]
        qm, km, vm = row_fwd("mla_pre", mla_pre_f, mla_rows, mla_consts, [(1024, bf16), (1024, bf16), (512, bf16)])
        if l + 1 < DEPTH:
            o_mla, lse, gathered[l + 1] = attn_fwd(qm, km, vm, name="attn_fwd_gather", comm=(packs[l + 1], False))
        else:
            o_mla, lse = attn_fwd(qm, km, vm)
        p1_rows = [(o_gdn, 512, 0), (hb, 512, OFF_Z // 512), (o_mla, 512, 0), (h_cur, D, 0)]
        p1_consts = [gg_b, wout, g1, b1]
        (x1,) = row_fwd("post1", post1_f, p1_rows, p1_consts, [(D, f32)])
        gate, up, act = ffn_up(x1, wg, wu)
        down = mm("ffn_down", act, wd, mode="nn", out_dtype=f32)
        p2_rows = [(x1, D, 0), (down, D, 0), (p_l, PLE, 0)]
        p2_consts = [g2, b2, wpg, wple]
        (y,) = row_fwd("post2", post2_f, p2_rows, p2_consts, [(D, f32)])
        acts.append(dict(x=h_cur, hb=hb, qn=qn, kn=kn, vs=vs, beta=beta, glog=glog, states=states, o_gdn=o_gdn, qm=qm, km=km,
                         scan=(g_w, g_qd, g_kd, g_at, g_el),
                         vm=vm, o_mla=o_mla, lse=lse, x1=x1, gate=gate, up=up, act=act, win=win, wd=wd, wg=wg, wu=wu,
                         cw8=cw8, alog_b=alog_b, dtb_b=dtb_b, mla_rows=mla_rows, mla_consts=mla_consts, p1_rows=p1_rows,
                         p1_consts=p1_consts, p2_rows=p2_rows, p2_consts=p2_consts))
        h_cur = y

    dy, loss_part = loss_head(h_cur, target)

    G = {n: [None] * DEPTH for n in list(BIG) + list(SMALL)}
    parts = [None] * DEPTH
    pending = None
    for l in reversed(range(DEPTH)):
        a = acts[l]
        dx1, ddown, dg2, db2, dwpg, dwple = row_bwd("post2_bwd", post2_f, a["p2_rows"], a["p2_consts"], [dy], [0, 1], [0, 1, 2, 3], [f32, f32])
        dgate, dup = ffn_dact(ddown, a["wd"], a["gate"], a["up"])
        dwd = mm("dw_down", a["act"], ddown, mode="tn", out_dtype=f32)
        dx1 = mm("dx_gate", dgate, a["wg"], mode="nt", out_dtype=f32, add=dx1)
        dx1 = mm("dx_up", dup, a["wu"], mode="nt", out_dtype=f32, add=dx1)
        dwg = mm("dw_gate", a["x1"], dgate, mode="tn", out_dtype=f32)
        dwu = mm("dw_up", a["x1"], dup, mode="tn", out_dtype=f32)
        do_gdn, dz, do_mla, dxr, dgg, dwout, dg1, db1 = row_bwd("post1_bwd", post1_f, a["p1_rows"], a["p1_consts"], [dx1], [0, 1, 2, 3],
                                                              [0, 1, 2, 3], [f32, bf16, f32, f32])
        if pending is not None:
            dqm, dkm, dvm, parts[pending[0]] = attn_bwd(a["qm"], a["km"], a["vm"], a["o_mla"], a["lse"], do_mla,
                                                        name="attn_bwd_exchange", comm=(pending[1], True))
        else:
            dqm, dkm, dvm = attn_bwd(a["qm"], a["km"], a["vm"], a["o_mla"], a["lse"], do_mla)
        dstates = gdn_scan_bwd(do_gdn, *a["scan"])
        dqn, dkn, dvs, dbeta, dglog = gdn_local_bwd(a["qn"], a["kn"], a["vs"], a["beta"], a["glog"], a["states"], dstates, do_gdn)
        dqkv, dba, dcw, dal, ddt = gdn_pre_bwd(a["hb"], a["cw8"], a["alog_b"], a["dtb_b"], dqn, dkn, dvs, dbeta, dglog)
        dcq, dckv, dkrg, dkrs, dqg, dwq, dkvg, dwkv = row_bwd("mla_pre_bwd", mla_pre_f, a["mla_rows"], a["mla_consts"], [dqm, dkm, dvm],
                                                              [0, 1, 2, 3], [0, 1, 2, 3], [bf16, bf16, bf16, bf16], row_add={2: dba})
        dh = jnp.concatenate([dqkv, dz, dckv, dkrg, dkrs, dcq], axis=1)
        dwin = mm("dw_in", a["x"], dh, mode="tn", out_dtype=f32)
        dy = mm("dx_in", dh, a["win"], mode="nt", out_dtype=f32, add=dxr)

        G["w_in"][l] = _unprep_w_in(dwin)
        G["w_uq"][l] = _unprep_w_uq(dwq)
        G["w_ukv"][l], G["w_out"][l], G["w_down"][l] = dwkv, dwout, dwd
        G["w_gate_up"][l] = jnp.concatenate([dwg, dwu], axis=1)
        G["w_ple"][l], G["w_ple_gate"][l] = dwple, dwpg
        G["conv_w"][l] = dcw[:CW]
        G["a_log"][l] = jnp.sum(dal.reshape(H, DK), axis=1)
        G["dt_bias"][l] = jnp.sum(ddt.reshape(H, DK), axis=1)
        G["gdn_norm_g"][l] = jnp.sum(dgg.reshape(H, DK), axis=0)
        G["q_norm_g"][l] = dqg[0, :QL]
        G["kv_norm_g"][l] = dkvg[0]
        G["ln1_g"][l], G["ln1_b"][l], G["ln2_g"][l], G["ln2_b"][l] = dg1[0], db1[0], dg2[0], db2[0]
        pending = (l, jnp.concatenate([_shard_grad(G[n][l], n, shards) for n in BIG], axis=1))
    grad_x = dy.reshape(x.shape)

    small_like = {n: W[n] for n in SMALL}
    conv_parts = jnp.stack(G["conv_w"]).reshape(DEPTH, CW, N_DEV, -1)
    smalls = []
    for d in range(N_DEV):
        tree = {n: jnp.stack(G[n]) for n in SMALL if n != "conv_w"}
        tree["conv_w"] = conv_parts[:, :, d, :]
        smalls.append(_pack_small(tree, loss_part))
    small_send = jnp.stack(smalls)
    rows_of, n_big = _big_rows(shards)
    tail = (-(n_big + small_send.shape[1])) % (ROW_ALIGN * _pick_rows(n_big) // math.gcd(ROW_ALIGN, _pick_rows(n_big)))
    parts[0] = exchange("exchange_last", jnp.concatenate([pending[1], small_send, jnp.zeros((N_DEV, tail, D), f32)], axis=1), True)

    zero = jnp.zeros((), f32)

    def pack3(tree, l):
        if l > 0:
            return _pack_layer(tree, l, f32)
        return jnp.concatenate([_pack_layer(tree, 0, f32), _pack_small(tree, zero), jnp.zeros((tail, D), f32)], axis=0)

    upd = [adamw_rows(parts[l], pack3(W, l), pack3(M, l), pack3(V, l)) for l in range(DEPTH)]

    def unpack(which):
        out = {n: jnp.stack([upd[l][which][rows_of[n][0]:rows_of[n][0] + rows_of[n][1]].reshape(shards[n][1:]) for l in range(DEPTH)])
               for n in BIG}
        sm, extra = _unpack_small(upd[0][which][n_big:], small_like)
        out.update(sm)
        return out, extra

    g_out, loss = unpack(0)
    d_out, _ = unpack(1)
    m_out, _ = unpack(2)
    v_out, _ = unpack(3)
    order =["w_in", "conv_w", "a_log", "dt_bias", "gdn_norm_g", "q_norm_g", "w_uq", "kv_norm_g", "w_ukv", "w_out", "ln1_g", "ln1_b",
             "w_gate_up", "w_down", "ln2_g", "ln2_b", "w_ple", "w_ple_gate"]
    return (loss, grad_x, *[g_out[n] for n in order], *[d_out[n] for n in order], *[m_out[n] for n in order],
            *[v_out[n] for n in order])
```

```python
import functools

import jax
import jax.numpy as jnp
import numpy as np
from jax import lax
from jax.experimental import pallas as pl
from jax.experimental.pallas import tpu as pltpu

f32 = jnp.float32
bf16 = jnp.bfloat16
MX = jnp.bfloat16

D = 1024
DEPTH = 4
CHUNK = 64
H = 4
DK = 128
PLE = 256
QL = 384
KVL = 256
ROPE = 64
DFF = 2816
IN_W = 2760
ROPE_THETA = 10000.0
ALPHA = (2.0 * DEPTH) ** 0.25
LN_EPS = 1e-5
RMS_EPS = 1e-6
ATT_SCALE = (128 + 64) ** -0.5
N_DEV = 8

ADAM_LR, ADAM_B1, ADAM_B2, ADAM_EPS, ADAM_WD, ADAM_STEP = 0.001, 0.9, 0.999, 1e-08, 0.01, 10

OFF_QKV, OFF_Z, OFF_CKV, OFF_KR, OFF_KRS, OFF_CQ, HP = 0, 1536, 2048, 2304, 2432, 2560, 3072
CQP = 512
LANE = 128
VMEM_LIMIT = 48 * 1024 * 1024
ROW_TB = 256


def _dot(a, b, ca, cb, prec=None):
    if a.ndim == 3:
        return lax.dot_general(a, b, (((ca + 1,), (cb + 1,)), ((0,), (0,))), precision=prec, preferred_element_type=f32)
    return lax.dot_general(a, b, (((ca,), (cb,)), ((), ())), precision=prec, preferred_element_type=f32)


@jax.custom_vjp
def bdot(a, w):
    return _dot(a.astype(MX), w.astype(MX), 1, 0)


def _bdot_fwd(a, w):
    return bdot(a, w), (a, w)


def _bdot_bwd(res, g):
    a, w = res
    gb = g.astype(MX)
    return _dot(gb, w.astype(MX), 1, 1).astype(a.dtype), _dot(a.astype(MX), gb, 0, 0).astype(w.dtype)


bdot.defvjp(_bdot_fwd, _bdot_bwd)


@jax.custom_vjp
def bdot_nt(a, b):
    return _dot(a.astype(MX), b.astype(MX), 1, 1)


def _bdot_nt_fwd(a, b):
    return bdot_nt(a, b), (a, b)


def _bdot_nt_bwd(res, g):
    a, b = res
    gb = g.astype(MX)
    return _dot(gb, b.astype(MX), 1, 0).astype(a.dtype), _dot(gb, a.astype(MX), 0, 0).astype(b.dtype)


bdot_nt.defvjp(_bdot_nt_fwd, _bdot_nt_bwd)


@jax.custom_vjp
def bdot_tn(a, b):
    return _dot(a.astype(MX), b.astype(MX), 0, 0)


def _bdot_tn_fwd(a, b):
    return bdot_tn(a, b), (a, b)


def _bdot_tn_bwd(res, g):
    a, b = res
    gb = g.astype(MX)
    return _dot(b.astype(MX), gb, 1, 1).astype(a.dtype), _dot(a.astype(MX), gb, 1, 0).astype(b.dtype)


bdot_tn.defvjp(_bdot_tn_fwd, _bdot_tn_bwd)


@jax.custom_vjp
def hdot(a, b):
    return _dot(a, b, 1, 0, lax.Precision.HIGHEST)


def _hdot_fwd(a, b):
    return hdot(a, b), (a, b)


def _hdot_bwd(res, g):
    a, b = res
    return _dot(g, b, 1, 1, lax.Precision.HIGHEST), _dot(a, g, 0, 0, lax.Precision.HIGHEST)


hdot.defvjp(_hdot_fwd, _hdot_bwd)


def _pick(n, cap):
    best = None
    for t in range(LANE, min(n, cap) + 1, LANE):
        if n % t == 0:
            best = t
    assert best is not None, (n, cap)
    return best


def _cparams(sem):
    return pltpu.CompilerParams(dimension_semantics=sem, vmem_limit_bytes=VMEM_LIMIT)


def mm(name, a, b, *, mode, out_dtype, add=None):
    if mode == "tn":
        kdim, m = a.shape
        n = b.shape[1]
        tm, tn, tk = _pick(m, 1536), _pick(n, 1536), min(512, kdim)
        nk = kdim // tk

        def body(a_ref, b_ref, o_ref, acc):
            k = pl.program_id(2)

            @pl.when(k == 0)
            def _():
                acc[...] = jnp.zeros_like(acc)

            acc[...] += _dot(a_ref[...].astype(MX), b_ref[...].astype(MX), 0, 0)

            @pl.when(k == nk - 1)
            def _():
                o_ref[...] = acc[...].astype(o_ref.dtype)

        return pl.pallas_call(
            body, name=name, grid=(m // tm, n // tn, nk),
            in_specs=[pl.BlockSpec((tk, tm), lambda i, j, k: (k, i)), pl.BlockSpec((tk, tn), lambda i, j, k: (k, j))],
            out_specs=pl.BlockSpec((tm, tn), lambda i, j, k: (i, j)),
            out_shape=jax.ShapeDtypeStruct((m, n), out_dtype),
            scratch_shapes=[pltpu.VMEM((tm, tn), f32)],
            compiler_params=_cparams(("parallel", "parallel", "arbitrary")),
        )(a, b)

    m, kdim = a.shape
    n = b.shape[1] if mode == "nn" else b.shape[0]
    tm, tn = min(512, m), _pick(n, 1536)
    has_add = add is not None

    def body(*refs):
        a_ref, b_ref = refs[0], refs[1]
        o_ref = refs[-1]
        r = _dot(a_ref[...].astype(MX), b_ref[...].astype(MX), 1, 0 if mode == "nn" else 1)
        if has_add:
            r = r + refs[2][...].astype(f32)
        o_ref[...] = r.astype(o_ref.dtype)

    b_spec = (pl.BlockSpec((kdim, tn), lambda i, j: (0, j)) if mode == "nn"
              else pl.BlockSpec((tn, kdim), lambda i, j: (j, 0)))
    in_specs = [pl.BlockSpec((tm, kdim), lambda i, j: (i, 0)), b_spec]
    args = [a, b]
    if has_add:
        in_specs.append(pl.BlockSpec((tm, tn), lambda i, j: (i, j)))
        args.append(add)
    return pl.pallas_call(
        body, name=name, grid=(m // tm, n // tn), in_specs=in_specs,
        out_specs=pl.BlockSpec((tm, tn), lambda i, j: (i, j)),
        out_shape=jax.ShapeDtypeStruct((m, n), out_dtype),
        compiler_params=_cparams(("parallel", "parallel")),
    )(*args)


def _row_spec(tb, w, cb):
    return pl.BlockSpec((tb, w), lambda i: (i, cb))


def _full_spec(shape):
    return pl.BlockSpec(shape, lambda i: (0,) * len(shape))


def row_fwd(name, f, rows, consts, outs):
    t = rows[0][0].shape[0]
    tb = min(ROW_TB, t)
    nr, nc = len(rows), len(consts)

    def body(*refs):
        vals = [r[...] for r in refs[:nr + nc]]
        res = f(*vals)
        for o_ref, val in zip(refs[nr + nc:], res):
            o_ref[...] = val.astype(o_ref.dtype)

    return pl.pallas_call(
        body, name=name, grid=(t // tb,),
        in_specs=[_row_spec(tb, w, cb) for _, w, cb in rows] + [_full_spec(c.shape) for c in consts],
        out_specs=[_row_spec(tb, w, 0) for w, _ in outs],
        out_shape=[jax.ShapeDtypeStruct((t, w), dt) for w, dt in outs],
        compiler_params=_cparams(("parallel",)),
    )(*[r[0] for r in rows], *consts)


def row_bwd(name, f, rows, consts, cots, row_diff, const_diff, drow_dtypes, row_add=None):
    t = rows[0][0].shape[0]
    tb = min(ROW_TB, t)
    nr, nc, nct = len(rows), len(consts), len(cots)
    row_add = row_add or {}
    add_keys = sorted(row_add)
    n_in = nr + nc + nct + len(add_keys)

    def body(*refs):
        i = pl.program_id(0)
        rv = [r[...] for r in refs[:nr]]
        cv = [refs[nr + k][...].astype(f32) if k in const_diff else refs[nr + k][...] for k in range(nc)]
        cot_refs = refs[nr + nc:nr + nc + nct]
        add_refs = refs[nr + nc + nct:n_in]
        drow_refs = refs[n_in:n_in + len(row_diff)]
        dconst_refs = refs[n_in + len(row_diff):]

        def g(*dv):
            r2, c2 = list(rv), list(cv)
            for p, k in enumerate(row_diff):
                r2[k] = dv[p]
            for p, k in enumerate(const_diff):
                c2[k] = dv[len(row_diff) + p]
            return tuple(f(*r2, *c2))

        prim = [rv[k].astype(f32) for k in row_diff] + [cv[k] for k in const_diff]
        outs, vf = jax.vjp(g, *prim)
        grads = vf(tuple(c[...].astype(o.dtype) for c, o in zip(cot_refs, outs)))
        for p, ref in enumerate(drow_refs):
            val = grads[p]
            if p in row_add:
                val = val + add_refs[add_keys.index(p)][...].astype(f32)
            ref[...] = val.astype(ref.dtype)

        @pl.when(i == 0)
        def _():
            for ref in dconst_refs:
                ref[...] = jnp.zeros_like(ref)

        for p, ref in enumerate(dconst_refs):
            ref[...] += grads[len(row_diff) + p]

    widths = [rows[k][1] for k in row_diff]
    return pl.pallas_call(
        body, name=name, grid=(t // tb,),
        in_specs=([_row_spec(tb, w, cb) for _, w, cb in rows] + [_full_spec(c.shape) for c in consts]
                  + [_row_spec(tb, c.shape[1], 0) for c in cots] + [_row_spec(tb, row_add[k].shape[1], 0) for k in add_keys]),
        out_specs=([_row_spec(tb, w, 0) for w in widths] + [_full_spec(consts[k].shape) for k in const_diff]),
        out_shape=([jax.ShapeDtypeStruct((t, w), dt) for w, dt in zip(widths, drow_dtypes)]
                   + [jax.ShapeDtypeStruct(consts[k].shape, f32) for k in const_diff]),
        compiler_params=_cparams(("arbitrary",)),
    )(*[r[0] for r in rows], *consts, *cots, *[row_add[k] for k in add_keys])


def _heads(x, w=DK):
    return [x[:, w * h:w * (h + 1)] for h in range(H)]


def _layer_norm(r, g, b):
    mu = jnp.mean(r, -1, keepdims=True)
    var = jnp.mean(jnp.square(r - mu), -1, keepdims=True)
    return (r - mu) * lax.rsqrt(var + LN_EPS) * g + b


def gdn_point_f(c, ba, alog_b, dtb_b):
    s = c * jax.nn.sigmoid(c)
    q, k, v = s[:, :512], s[:, 512:1024], s[:, 1024:]

    def l2(x):
        return jnp.concatenate([xh * lax.rsqrt(jnp.sum(xh * xh, -1, keepdims=True) + RMS_EPS) for xh in _heads(x)], axis=1)

    tb = c.shape[0]
    b_b = jnp.concatenate([jnp.broadcast_to(ba[:, 64 + h:65 + h], (tb, DK)) for h in range(H)], axis=1)
    a_b = jnp.concatenate([jnp.broadcast_to(ba[:, 68 + h:69 + h], (tb, DK)) for h in range(H)], axis=1)
    beta = jax.nn.sigmoid(b_b)
    g = -jnp.exp(alog_b) * jax.nn.softplus(a_b + dtb_b)
    return l2(q), l2(k), v, beta, g


def mla_pre_f(cq, ckv, krg, krs, cf, sf, qg, wq, kvg, wkv):
    cqn = cq * lax.rsqrt(jnp.sum(cq * cq, -1, keepdims=True) * (1.0 / QL) + RMS_EPS) * qg
    qa = bdot(cqn, wq)
    ckvn = ckv * lax.rsqrt(jnp.mean(ckv * ckv, -1, keepdims=True) + RMS_EPS) * kvg
    kv = bdot(ckvn, wkv)
    kro = krg * cf + krs * sf
    qs, ks, vs = [], [], []
    for h in range(H):
        qs += [qa[:, DK * h:DK * (h + 1)], qa[:, 512 + DK * h:512 + DK * (h + 1)] * cf + qa[:, 1024 + DK * h:1024 + DK * (h + 1)] * sf]
        ks += [kv[:, 256 * h:256 * h + DK], kro]
        vs += [kv[:, 256 * h + DK:256 * (h + 1)]]
    return jnp.concatenate(qs, axis=1) * ATT_C2, jnp.concatenate(ks, axis=1), jnp.concatenate(vs, axis=1)


def post1_f(o, z, omla, x, gg_b, wout, g1, b1):
    on = jnp.concatenate([oh * lax.rsqrt(jnp.mean(oh * oh, -1, keepdims=True) + RMS_EPS) for oh in _heads(o)], axis=1) * gg_b
    ogdn = on * (z * jax.nn.sigmoid(z))
    mix = bdot(jnp.concatenate([ogdn, omla], axis=1), wout)
    return (_layer_norm(ALPHA * x + mix, g1, b1),)


def post2_f(x1, down, p, g2, b2, wpg, wple):
    x2 = _layer_norm(ALPHA * x1 + down, g2, b2)
    return (x2 + jax.nn.sigmoid(bdot(x2, wpg)) * bdot(p, wple),)


HALO = 8
CW = 4


def _conv_from_scratch(xs, cw_ref, tb):
    c = xs[pl.ds(HALO - 3, tb), :] * cw_ref[0:1, :]
    for j in range(1, CW):
        c = c + xs[pl.ds(HALO - 3 + j, tb), :] * cw_ref[j:j + 1, :]
    return c


def gdn_pre_fwd(hbuf, conv_w8, alog_b, dtb_b):
    t = hbuf.shape[0]
    tb = min(ROW_TB, t)

    def body(x_ref, halo_ref, ba_ref, cw_ref, al_ref, dt_ref, q_ref, k_ref, v_ref, be_ref, g_ref, xs):
        i = pl.program_id(0)
        xs[pl.ds(0, HALO), :] = jnp.where(i == 0, 0.0, halo_ref[...])
        xs[pl.ds(HALO, tb), :] = x_ref[...]
        c = _conv_from_scratch(xs, cw_ref, tb)
        q, k, v, be, g = gdn_point_f(c, ba_ref[...], al_ref[...], dt_ref[...])
        q_ref[...], k_ref[...], v_ref[...], be_ref[...], g_ref[...] = q, k, v, be, g

    return pl.pallas_call(
        body, name="gdn_pre_fwd", grid=(t // tb,),
        in_specs=[_row_spec(tb, 1536, 0),
                  pl.BlockSpec((HALO, 1536), lambda i: (jnp.maximum(i * (tb // HALO) - 1, 0), 0)),
                  _row_spec(tb, LANE, OFF_KR // LANE),
                  _full_spec(conv_w8.shape), _full_spec(alog_b.shape), _full_spec(dtb_b.shape)],
        out_specs=[_row_spec(tb, 512, 0)] * 5,
        out_shape=[jax.ShapeDtypeStruct((t, 512), f32)] * 5,
        scratch_shapes=[pltpu.VMEM((tb + HALO, 1536), f32)],
        compiler_params=_cparams(("arbitrary",)),
    )(hbuf, hbuf, hbuf, conv_w8, alog_b, dtb_b)


def gdn_pre_bwd(hbuf, conv_w8, alog_b, dtb_b, dq, dk, dv, dbe, dg):
    t = hbuf.shape[0]
    tb = min(ROW_TB, t)
    n = t // tb

    def body(x_ref, halo_ref, ba_ref, cw_ref, al_ref, dt_ref, dq_ref, dk_ref, dv_ref, dbe_ref, dg_ref,
             dx_ref, dba_ref, dcw_ref, dal_ref, ddt_ref, xs, dcs):
        s = pl.program_id(0)
        i = n - 1 - s
        xs[pl.ds(0, HALO), :] = jnp.where(i == 0, 0.0, halo_ref[...])
        xs[pl.ds(HALO, tb), :] = x_ref[...]
        c = _conv_from_scratch(xs, cw_ref, tb)
        _, vf = jax.vjp(gdn_point_f, c, ba_ref[...], al_ref[...], dt_ref[...])
        dc, dba, dal, ddt = vf((dq_ref[...], dk_ref[...], dv_ref[...], dbe_ref[...], dg_ref[...]))

        @pl.when(s == 0)
        def _():
            dcs[pl.ds(tb, HALO), :] = jnp.zeros((HALO, 1536), f32)
            dcw_ref[...] = jnp.zeros_like(dcw_ref)
            dal_ref[...] = jnp.zeros_like(dal_ref)
            ddt_ref[...] = jnp.zeros_like(ddt_ref)

        @pl.when(s > 0)
        def _():
            dcs[pl.ds(tb, HALO), :] = dcs[pl.ds(0, HALO), :]

        dcs[pl.ds(0, tb), :] = dc
        dx = dcs[pl.ds(3, tb), :] * cw_ref[0:1, :]
        for j in range(1, CW):
            dx = dx + dcs[pl.ds(3 - j, tb), :] * cw_ref[j:j + 1, :]
        dx_ref[...] = dx.astype(dx_ref.dtype)
        dba_ref[...] = dba
        for j in range(CW):
            dcw_ref[j:j + 1, :] += jnp.sum(dc * xs[pl.ds(HALO - 3 + j, tb), :], axis=0, keepdims=True)
        dal_ref[...] += dal
        ddt_ref[...] += ddt

    rev = lambda cb: (lambda s: (n - 1 - s, cb))
    return pl.pallas_call(
        body, name="gdn_pre_bwd", grid=(n,),
        in_specs=[pl.BlockSpec((tb, 1536), rev(0)),
                  pl.BlockSpec((HALO, 1536), lambda s: (jnp.maximum((n - 1 - s) * (tb // HALO) - 1, 0), 0)),
                  pl.BlockSpec((tb, LANE), rev(OFF_KR // LANE)),
                  _full_spec(conv_w8.shape), _full_spec(alog_b.shape), _full_spec(dtb_b.shape)]
        + [pl.BlockSpec((tb, 512), rev(0))] * 5,
        out_specs=[pl.BlockSpec((tb, 1536), rev(0)), pl.BlockSpec((tb, LANE), rev(0)),
                   _full_spec(conv_w8.shape), _full_spec(alog_b.shape), _full_spec(dtb_b.shape)],
        out_shape=[jax.ShapeDtypeStruct((t, 1536), bf16), jax.ShapeDtypeStruct((t, LANE), f32),
                   jax.ShapeDtypeStruct(conv_w8.shape, f32), jax.ShapeDtypeStruct(alog_b.shape, f32),
                   jax.ShapeDtypeStruct(dtb_b.shape, f32)],
        scratch_shapes=[pltpu.VMEM((tb + HALO, 1536), f32), pltpu.VMEM((tb + HALO, 1536), f32)],
        compiler_params=_cparams(("arbitrary",)),
    )(hbuf, hbuf, hbuf, conv_w8, alog_b, dtb_b, dq, dk, dv, dbe, dg)


GDN_NB = 4


def _to_batch(x, rows=CHUNK):
    n = x.shape[0] // rows
    return jnp.concatenate([x[rows * ci:rows * (ci + 1), DK * h:DK * (h + 1)][None] for ci in range(n) for h in range(H)], axis=0)


def _from_batch(y):
    n = y.shape[0] // H
    return jnp.concatenate([jnp.concatenate([y[ci * H + h] for h in range(H)], axis=1) for ci in range(n)], axis=0)


def gdn_intra_b(q, k, v, beta, g):
    c = CHUNK
    n = q.shape[0] // c
    tri = jnp.where(lax.broadcasted_iota(jnp.int32, (c, c), 0) >= lax.broadcasted_iota(jnp.int32, (c, c), 1), 1.0, 0.0)
    gc = _to_batch(jnp.concatenate([hdot(tri, g[c * ci:c * (ci + 1)]) for ci in range(n)], axis=0))
    qb, kb, vb, bb = _to_batch(q) * DK ** -0.5, _to_batch(k), _to_batch(v), _to_batch(beta)
    nbat = qb.shape[0]
    row = lax.broadcasted_iota(jnp.int32, (1, c, DK), 1)
    col = lax.broadcasted_iota(jnp.int32, (1, c, DK), 2)
    incl, strict, eye = row >= col, row > col, row == col
    grow = hdot(jnp.ones((nbat, c, c), f32), jnp.where(eye, gc, 0.0))
    decay = jnp.where(incl, jnp.exp(jnp.where(incl, gc - grow, 0.0)), 0.0)
    kbeta = kb * bb
    kpad = jnp.concatenate([kb, jnp.zeros((nbat, DK - c, DK), f32)], axis=1)
    a = jnp.where(strict, bdot_nt(kbeta, kpad) * decay, 0.0)[:, :, :c]
    nn = -a
    bk = bdot(a, a)
    for step in range(5):
        nn = nn + bk + bdot(nn, bk)
        if step < 4:
            bk = bdot(bk, bk)
    eg = jnp.exp(gc)
    rhs_v, rhs_k = vb * bb, kbeta * eg
    g_last = gc[:, c - 1:c, :]
    u = rhs_v + bdot(nn, rhs_v)
    w = rhs_k + bdot(nn, rhs_k)
    attn = jnp.where(incl, bdot_nt(qb, kpad) * decay, 0.0)
    return u, w, qb * eg, kb * jnp.exp(g_last - gc), attn, jnp.exp(g_last)


def gdn_scan_b(u, w, qd, kd, attn, el, state):
    v_new = u - bdot(w, state)
    return bdot(qd, state) + bdot(attn[:, :, :CHUNK], v_new), state * el + bdot_tn(kd, v_new)


def gdn_scan_bwd_b(do, w, qd, kd, attn, el, dstate):
    dvn = bdot_tn(attn[:, :, :CHUNK], do) + bdot(kd, dstate)
    return bdot_tn(qd, do) + dstate * el - bdot_tn(w, dvn)


def gdn_chunks_f(q, k, v, beta, g, states):
    n = states.shape[0]
    o, s_new = gdn_scan_b(*gdn_intra_b(q, k, v, beta, g), states.reshape(n * H, DK, DK))
    return _from_batch(o), s_new.reshape(n, H * DK, DK)


def _chunk_rows(cidx):
    return pl.ds(cidx * CHUNK, CHUNK)


def gdn_intra(q, k, v, beta, g):
    t = q.shape[0]
    nb = min(GDN_NB, t // CHUNK)
    tb = nb * CHUNK

    def body(q_ref, k_ref, v_ref, be_ref, g_ref, u_ref, w_ref, qd_ref, kd_ref, at_ref, el_ref):
        u, w, qd, kd, at, el = gdn_intra_b(q_ref[...], k_ref[...], v_ref[...], be_ref[...], g_ref[...])
        u_ref[...], w_ref[...], qd_ref[...], kd_ref[...], at_ref[...] = (_from_batch(a) for a in (u, w, qd, kd, at))
        el_ref[...] = _from_batch(jnp.broadcast_to(el, (nb * H, 8, DK))).reshape(nb, 8, H * DK)

    return pl.pallas_call(
        body, name="gdn_intra", grid=(t // tb,),
        in_specs=[_row_spec(tb, 512, 0)] * 5,
        out_specs=[_row_spec(tb, 512, 0)] * 5 + [pl.BlockSpec((nb, 8, 512), lambda i: (i, 0, 0))],
        out_shape=[jax.ShapeDtypeStruct((t, 512), f32)] * 5 + [jax.ShapeDtypeStruct((t // CHUNK, 8, 512), f32)],
        compiler_params=_cparams(("parallel",)),
    )(q, k, v, beta, g)


def gdn_scan_fwd(u, w, qd, kd, attn, el):
    t = u.shape[0]
    nb = min(GDN_NB, t // CHUNK)
    tb = nb * CHUNK

    def body(u_ref, w_ref, qd_ref, kd_ref, at_ref, el_ref, o_ref, s_ref, state):
        @pl.when(pl.program_id(0) == 0)
        def _():
            state[...] = jnp.zeros_like(state)

        for cidx in range(nb):
            r = _chunk_rows(cidx)
            s_ref[cidx] = state[...]
            ins = [_to_batch(ref[r, :]) for ref in (u_ref, w_ref, qd_ref, kd_ref, at_ref)]
            el = _to_batch(el_ref[cidx], 8)[:, 0:1, :]
            o, s_new = gdn_scan_b(*ins, el, state[...].reshape(H, DK, DK))
            o_ref[r, :] = _from_batch(o)
            state[...] = s_new.reshape(H * DK, DK)

    return pl.pallas_call(
        body, name="gdn_scan_fwd", grid=(t // tb,),
        in_specs=[_row_spec(tb, 512, 0)] * 5 + [pl.BlockSpec((nb, 8, 512), lambda i: (i, 0, 0))],
        out_specs=[_row_spec(tb, 512, 0), pl.BlockSpec((nb, 512, DK), lambda i: (i, 0, 0))],
        out_shape=[jax.ShapeDtypeStruct((t, 512), f32), jax.ShapeDtypeStruct((t // CHUNK, 512, DK), f32)],
        scratch_shapes=[pltpu.VMEM((512, DK), f32)],
        compiler_params=_cparams(("arbitrary",)),
    )(u, w, qd, kd, attn, el)


def gdn_scan_bwd(do, w, qd, kd, attn, el):
    t = do.shape[0]
    nb = min(GDN_NB, t // CHUNK)
    tb = nb * CHUNK
    n = t // tb

    def body(do_ref, w_ref, qd_ref, kd_ref, at_ref, el_ref, ds_ref, dstate):
        @pl.when(pl.program_id(0) == 0)
        def _():
            dstate[...] = jnp.zeros_like(dstate)

        for cidx in reversed(range(nb)):
            r = _chunk_rows(cidx)
            ds_ref[cidx] = dstate[...]
            ins = [_to_batch(ref[r, :]) for ref in (do_ref, w_ref, qd_ref, kd_ref, at_ref)]
            el = _to_batch(el_ref[cidx], 8)[:, 0:1, :]
            dstate[...] = gdn_scan_bwd_b(*ins, el, dstate[...].reshape(H, DK, DK)).reshape(H * DK, DK)

    rev = pl.BlockSpec((tb, 512), lambda s: (n - 1 - s, 0))
    return pl.pallas_call(
        body, name="gdn_scan_bwd", grid=(n,),
        in_specs=[rev] * 5 + [pl.BlockSpec((nb, 8, 512), lambda s: (n - 1 - s, 0, 0))],
        out_specs=pl.BlockSpec((nb, 512, DK), lambda s: (n - 1 - s, 0, 0)),
        out_shape=jax.ShapeDtypeStruct((t // CHUNK, 512, DK), f32),
        scratch_shapes=[pltpu.VMEM((512, DK), f32)],
        compiler_params=_cparams(("arbitrary",)),
    )(do, w, qd, kd, attn, el)


def gdn_local_bwd(q, k, v, beta, g, states, dstates, do):
    t = q.shape[0]
    nb = min(GDN_NB, t // CHUNK)
    tb = nb * CHUNK

    def body(q_ref, k_ref, v_ref, be_ref, g_ref, s_ref, ds_ref, do_ref, dq_ref, dk_ref, dv_ref, dbe_ref, dg_ref):
        states_v = s_ref[...]
        _, vf = jax.vjp(lambda *a: gdn_chunks_f(*a, states_v), q_ref[...], k_ref[...], v_ref[...], be_ref[...], g_ref[...])
        dq_ref[...], dk_ref[...], dv_ref[...], dbe_ref[...], dg_ref[...] = vf((do_ref[...], ds_ref[...]))

    st = pl.BlockSpec((nb, 512, DK), lambda i: (i, 0, 0))
    return pl.pallas_call(
        body, name="gdn_local_bwd", grid=(t // tb,),
        in_specs=[_row_spec(tb, 512, 0)] * 5 + [st, st, _row_spec(tb, 512, 0)],
        out_specs=[_row_spec(tb, 512, 0)] * 5,
        out_shape=[jax.ShapeDtypeStruct((t, 512), f32)] * 5,
        compiler_params=_cparams(("parallel",)),
    )(q, k, v, beta, g, states, dstates, do)


NEG = -1e30


LOG2E = 1.4426950408889634
ATT_C2 = ATT_SCALE * LOG2E
ATT_RB = 256


def _hosted(refs, n_in, n_out, n_scratch):
    ins, src = refs[:n_in], refs[n_in]
    outs, dst = refs[n_in + 1:n_in + 1 + n_out], refs[n_in + 1 + n_out]
    rest = refs[n_in + n_out + 2:]
    return ins, outs, rest[:n_scratch], (src, dst) + tuple(rest[n_scratch:])


def _host_exchange(comm_refs, scatter, is_first, is_last):
    @pl.when(is_first)
    def _():
        for op in _exchange_copies(*comm_refs, scatter):
            op.start()

    @pl.when(is_last)
    def _():
        for op in _exchange_copies(*comm_refs, scatter):
            op.wait()


def attn_fwd(q, k, v, name="attn_fwd", comm=None):
    t = q.shape[0]
    tq = tk = min(1024, t)
    rb = min(ATT_RB, tq)
    nq, nk, nr = t // tq, t // tk, tq // rb
    last = lambda i: i

    def body(*refs):
        if comm is None:
            (q_ref, k_ref, v_ref), (o_ref, lse_ref), (m_s, acc) = refs[:3], refs[3:5], refs[5:]
        else:
            (q_ref, k_ref, v_ref), (o_ref, lse_ref), (m_s, acc), comm_refs = _hosted(refs, 3, 2, 2)
        h, i, j = pl.program_id(0), pl.program_id(1), pl.program_id(2)
        if comm is not None:
            _host_exchange(comm_refs, comm[1], (h == 0) & (i == 0) & (j == 0), (h == H - 1) & (i == nq - 1) & (j == nk - 1))

        @pl.when(j == 0)
        def _():
            m_s[...] = jnp.full_like(m_s, NEG)
            acc[...] = jnp.zeros_like(acc)

        one_col = jnp.where(lax.broadcasted_iota(jnp.int32, (tk, DK), 1) == 0, 1.0, 0.0).astype(MX)
        vx = jnp.concatenate([v_ref[...], one_col], axis=1)

        def rows_of(r):
            return pl.ds(r * rb, rb)

        def soft(r, s):
            m_old = m_s[rows_of(r), :]
            m_new = jnp.maximum(m_old, jnp.max(s, axis=-1, keepdims=True))
            m_s[rows_of(r), :] = m_new
            return jnp.exp2(s - m_new).astype(MX), jnp.exp2(m_old - m_new)

        def pv(r, p, alpha, vals):
            acc[rows_of(r), :] = alpha * acc[rows_of(r), :] + _dot(p, vals, 1, 0)

        def whole_block():
            kb = k_ref[...]
            ss = [_dot(q_ref[rows_of(r), :], kb, 1, 1) for r in range(min(2, nr))]
            pend = None
            for r in range(nr):
                if r + 2 < nr:
                    ss.append(_dot(q_ref[rows_of(r + 2), :], kb, 1, 1))
                p, alpha = soft(r, ss[r])
                if pend is not None:
                    pv(*pend, vx)
                pend = (r, p, alpha)
            pv(*pend, vx)

        def diagonal_block():
            for r in range(nr):
                cols = (r + 1) * rb
                s = _dot(q_ref[rows_of(r), :], k_ref[pl.ds(0, cols), :], 1, 1)
                qrow = lax.broadcasted_iota(jnp.int32, (rb, cols), 0) + r * rb
                kcol = lax.broadcasted_iota(jnp.int32, (rb, cols), 1)
                s = jnp.where(jnp.right_shift(kcol, 6) <= jnp.right_shift(qrow, 6), s, NEG)
                p, alpha = soft(r, s)
                pv(r, p, alpha, vx[:cols])

        pl.when(j < i)(whole_block)
        pl.when(j == i)(diagonal_block)

        @pl.when(j == nk - 1)
        def _():
            a = acc[...]
            l = a[:, DK:DK + 1]
            o_ref[...] = a[:, :DK] / l
            lse_ref[0] = m_s[...] + jnp.log2(l)

    in_specs = [pl.BlockSpec((tq, 256), lambda h, i, j: (i, h)),
                pl.BlockSpec((tk, 256), lambda h, i, j: (jnp.minimum(j, last(i)), h)),
                pl.BlockSpec((tk, DK), lambda h, i, j: (jnp.minimum(j, last(i)), h))]
    out_specs = [pl.BlockSpec((tq, DK), lambda h, i, j: (i, h)), pl.BlockSpec((1, tq, 1), lambda h, i, j: (h, i, 0))]
    out_shape = [jax.ShapeDtypeStruct((t, 512), f32), jax.ShapeDtypeStruct((H, t, 1), f32)]
    scratch = [pltpu.VMEM((tq, 1), f32), pltpu.VMEM((tq, 2 * DK), f32)]
    args = [q, k, v]
    if comm is not None:
        in_specs.append(_ANY)
        out_specs.append(_ANY)
        out_shape.append(_comm_out_shape(*comm))
        scratch += _comm_scratch()
        args.append(comm[0])
    sem = ("parallel", "parallel", "arbitrary") if comm is None else ("arbitrary",) * 3
    return pl.pallas_call(body, name=name, grid=(H, nq, nk), in_specs=in_specs, out_specs=out_specs, out_shape=out_shape,
                          scratch_shapes=scratch, compiler_params=_cparams(sem))(*args)


def attn_bwd(q, k, v, o, lse, do, name="attn_bwd", comm=None):
    t = q.shape[0]
    tq = tk = min(1024, t)
    rb = min(ATT_RB, tq)
    nq, nk, nr = t // tq, t // tk, tq // rb
    first = lambda j: j

    def body(*refs):
        if comm is None:
            ins, outs, (dk_acc, dv_acc) = refs[:6], refs[6:9], refs[9:]
        else:
            ins, outs, (dk_acc, dv_acc), comm_refs = _hosted(refs, 6, 3, 2)
        q_ref, k_ref, v_ref, o_ref, lse_ref, do_ref = ins
        dq_ref, dk_ref, dv_ref = outs
        h, j, i = pl.program_id(0), pl.program_id(1), pl.program_id(2)
        if comm is not None:
            _host_exchange(comm_refs, comm[1], (h == 0) & (i == 0) & (j == 0), (h == H - 1) & (i == nq - 1) & (j == nk - 1))

        @pl.when((j == 0) & (i == 0))
        def _():
            dq_ref[...] = jnp.zeros_like(dq_ref)

        @pl.when(i == 0)
        def _():
            dk_acc[...] = jnp.zeros_like(dk_acc)
            dv_acc[...] = jnp.zeros_like(dv_acc)

        def rows_of(r):
            return pl.ds(r * rb, rb)

        def front(r, cols):
            qb, dob = q_ref[rows_of(r), :], do_ref[rows_of(r), :]
            return qb, dob, _dot(qb, k_ref[pl.ds(0, cols), :], 1, 1), _dot(dob.astype(MX), v_ref[pl.ds(0, cols), :], 1, 1)

        def middle(r, dob, s, dp):
            p = jnp.exp2(s - lse_ref[0, rows_of(r), :])
            delta = jnp.sum(dob * o_ref[rows_of(r), :], axis=-1, keepdims=True)
            return p.astype(MX), (p * (dp - delta)).astype(MX)

        def back(r, cols, qb, dob, pb, ds):
            dv_acc[pl.ds(0, cols), :] += _dot(pb, dob.astype(MX), 0, 0)
            dk_acc[pl.ds(0, cols), :] += _dot(ds, qb, 0, 0) * (1.0 / LOG2E)
            grow = pl.ds(pl.multiple_of(i * tq + r * rb, rb), rb)
            dq_ref[grow, :] += _dot(ds, k_ref[pl.ds(0, cols), :], 1, 0) * (1.0 / LOG2E)

        def whole_block():
            nxt = front(0, tk)
            for r in range(nr):
                qb, dob, s, dp = nxt
                if r + 1 < nr:
                    nxt = front(r + 1, tk)
                back(r, tk, qb, dob, *middle(r, dob, s, dp))

        def diagonal_block():
            for r in range(nr):
                cols = (r + 1) * rb
                qb, dob, s, dp = front(r, cols)
                qrow = lax.broadcasted_iota(jnp.int32, (rb, cols), 0) + r * rb
                kcol = lax.broadcasted_iota(jnp.int32, (rb, cols), 1)
                s = jnp.where(jnp.right_shift(kcol, 6) <= jnp.right_shift(qrow, 6), s, NEG)
                back(r, cols, qb, dob, *middle(r, dob, s, dp))

        pl.when(i > j)(whole_block)
        pl.when(i == j)(diagonal_block)

        @pl.when(i == nq - 1)
        def _():
            dk_ref[...] = dk_acc[...]
            dv_ref[...] = dv_acc[...]

    qi = lambda h, j, i: (jnp.maximum(i, first(j)), h)
    in_specs = [pl.BlockSpec((tq, 256), qi),
                pl.BlockSpec((tk, 256), lambda h, j, i: (j, h)),
                pl.BlockSpec((tk, DK), lambda h, j, i: (j, h)),
                pl.BlockSpec((tq, DK), qi),
                pl.BlockSpec((1, tq, 1), lambda h, j, i: (h, jnp.maximum(i, first(j)), 0)),
                pl.BlockSpec((tq, DK), qi)]
    out_specs = [pl.BlockSpec((t, 256), lambda h, j, i: (0, h)),
                 pl.BlockSpec((tk, 256), lambda h, j, i: (j, h)),
                 pl.BlockSpec((tk, DK), lambda h, j, i: (j, h))]
    out_shape = [jax.ShapeDtypeStruct((t, 1024), f32), jax.ShapeDtypeStruct((t, 1024), f32), jax.ShapeDtypeStruct((t, 512), f32)]
    scratch = [pltpu.VMEM((tk, 256), f32), pltpu.VMEM((tk, DK), f32)]
    args = [q, k, v, o, lse, do]
    if comm is not None:
        in_specs.append(_ANY)
        out_specs.append(_ANY)
        out_shape.append(_comm_out_shape(*comm))
        scratch += _comm_scratch()
        args.append(comm[0])
    return pl.pallas_call(body, name=name, grid=(H, nk, nq), in_specs=in_specs, out_specs=out_specs, out_shape=out_shape,
                          scratch_shapes=scratch, compiler_params=_cparams(("arbitrary",) * 3))(*args)


def ffn_up(x1, wg, wu):
    t = x1.shape[0]
    tm, tn = min(512, t), _pick(DFF, 1536)

    def body(x_ref, wg_ref, wu_ref, g_ref, u_ref, a_ref):
        xb = x_ref[...].astype(MX)
        g = _dot(xb, wg_ref[...], 1, 0)
        u = _dot(xb, wu_ref[...], 1, 0)
        g_ref[...] = g.astype(g_ref.dtype)
        u_ref[...] = u.astype(u_ref.dtype)
        a_ref[...] = (g * jax.nn.sigmoid(g) * u).astype(a_ref.dtype)

    w_spec = pl.BlockSpec((D, tn), lambda i, j: (0, j))
    o_spec = pl.BlockSpec((tm, tn), lambda i, j: (i, j))
    return pl.pallas_call(
        body, name="ffn_up", grid=(t // tm, DFF // tn),
        in_specs=[pl.BlockSpec((tm, D), lambda i, j: (i, 0)), w_spec, w_spec],
        out_specs=[o_spec] * 3, out_shape=[jax.ShapeDtypeStruct((t, DFF), bf16)] * 3,
        compiler_params=_cparams(("parallel", "parallel")),
    )(x1, wg, wu)


def ffn_dact(ddown, wd, g, u):
    t = ddown.shape[0]
    tm, tn = min(512, t), _pick(DFF, 1536)

    def body(dd_ref, wd_ref, g_ref, u_ref, dg_ref, du_ref):
        dact = _dot(dd_ref[...].astype(MX), wd_ref[...], 1, 1)
        gv, uv = g_ref[...].astype(f32), u_ref[...].astype(f32)
        sig = jax.nn.sigmoid(gv)
        dg_ref[...] = (dact * uv * sig * (1.0 + gv * (1.0 - sig))).astype(dg_ref.dtype)
        du_ref[...] = (dact * gv * sig).astype(du_ref.dtype)

    o_spec = pl.BlockSpec((tm, tn), lambda i, j: (i, j))
    return pl.pallas_call(
        body, name="ffn_dact", grid=(t // tm, DFF // tn),
        in_specs=[pl.BlockSpec((tm, D), lambda i, j: (i, 0)), pl.BlockSpec((tn, D), lambda i, j: (j, 0)), o_spec, o_spec],
        out_specs=[o_spec] * 2, out_shape=[jax.ShapeDtypeStruct((t, DFF), bf16)] * 2,
        compiler_params=_cparams(("parallel", "parallel")),
    )(ddown, wd, g, u)


def loss_head(y, target):
    t = y.shape[0]
    tb = min(ROW_TB, t)
    n = t // tb

    def body(y_ref, t_ref, dy_ref, loss_ref, acc):
        i = pl.program_id(0)

        @pl.when(i == 0)
        def _():
            acc[...] = jnp.zeros_like(acc)

        e = y_ref[...] - t_ref[...]
        dy_ref[...] = e * (1.0 / D)
        acc[...] += jnp.sum(e * e, axis=0, keepdims=True)

        @pl.when(i == n - 1)
        def _():
            loss_ref[...] = jnp.sum(acc[...], axis=1, keepdims=True) * (0.5 / D)

    return pl.pallas_call(
        body, name="loss_head", grid=(n,),
        in_specs=[_row_spec(tb, D, 0)] * 2,
        out_specs=[_row_spec(tb, D, 0), _full_spec((1, 1))],
        out_shape=[jax.ShapeDtypeStruct((t, D), f32), jax.ShapeDtypeStruct((1, 1), f32)],
        scratch_shapes=[pltpu.VMEM((1, D), f32)],
        compiler_params=_cparams(("arbitrary",)),
    )(y, target)


def _me_and_peers():
    x, y, c = lax.axis_index("x"), lax.axis_index("y"), lax.axis_index("c")
    me = 4 * x + 2 * y + c
    peers = []
    for kk in range(1, N_DEV):
        px = 1 - x if kk & 4 else x
        py = 1 - y if kk & 2 else y
        pc = 1 - c if kk & 1 else c
        peers.append(((px, py, pc), 4 * px + 2 * py + pc))
    return me, peers


_ANY = pl.BlockSpec(memory_space=pl.ANY)


def _comm_scratch():
    return [pltpu.SemaphoreType.DMA((N_DEV - 1,)), pltpu.SemaphoreType.DMA((N_DEV - 1,)), pltpu.SemaphoreType.DMA]


def _exchange_copies(src_ref, out_ref, send_sems, recv_sems, local_sem, scatter):
    me, peers = _me_and_peers()
    pick = (lambda d: src_ref.at[d]) if scatter else (lambda d: src_ref)
    ops = [pltpu.make_async_copy(pick(me), out_ref.at[me], local_sem)]
    ops += [pltpu.make_async_remote_copy(src_ref=pick(pid), dst_ref=out_ref.at[me], send_sem=send_sems.at[kk],
                                         recv_sem=recv_sems.at[kk], device_id=dev, device_id_type=pl.DeviceIdType.MESH)
            for kk, (dev, pid) in enumerate(peers)]
    return ops


def _comm_out_shape(src, scatter):
    return jax.ShapeDtypeStruct(src.shape if scatter else (N_DEV,) + src.shape, src.dtype)


def exchange(name, src, scatter):
    def body(s_ref, out_ref, send_sems, recv_sems, local_sem):
        ops = _exchange_copies(s_ref, out_ref, send_sems, recv_sems, local_sem, scatter)
        for op in ops:
            op.start()
        for op in ops:
            op.wait()

    return pl.pallas_call(body, name=name, in_specs=[_ANY], out_specs=_ANY, out_shape=_comm_out_shape(src, scatter),
                          scratch_shapes=_comm_scratch())(src)


def sum_parts(parts):
    r = parts.shape[1]
    tb = _pick_rows(r)

    def body(p_ref, g_ref):
        g = p_ref[0]
        for s in range(1, N_DEV):
            g = g + p_ref[s]
        g_ref[...] = g

    return pl.pallas_call(
        body, name="sum_grad_parts", grid=(r // tb,),
        in_specs=[pl.BlockSpec((N_DEV, tb, D), lambda i: (0, i, 0))],
        out_specs=pl.BlockSpec((tb, D), lambda i: (i, 0)), out_shape=jax.ShapeDtypeStruct((r, D), f32),
        compiler_params=_cparams(("parallel",)),
    )(parts)


def adamw(w, m, v, g):
    r, c = w.shape
    tb = _pick_rows(r)

    def body(w_ref, m_ref, v_ref, g_ref, d_ref, nm_ref, nv_ref):
        g = g_ref[...]
        nm = ADAM_B1 * m_ref[...] + (1.0 - ADAM_B1) * g
        nv = ADAM_B2 * v_ref[...] + (1.0 - ADAM_B2) * jnp.square(g)
        m_hat = nm / (1.0 - ADAM_B1 ** ADAM_STEP)
        v_hat = nv / (1.0 - ADAM_B2 ** ADAM_STEP)
        d_ref[...] = -ADAM_LR * (m_hat / (jnp.sqrt(v_hat) + ADAM_EPS) + ADAM_WD * w_ref[...])
        nm_ref[...] = nm
        nv_ref[...] = nv

    spec = pl.BlockSpec((tb, c), lambda i: (i, 0))
    return pl.pallas_call(
        body, name="adamw", grid=(r // tb,), in_specs=[spec] * 4, out_specs=[spec] * 3,
        out_shape=[jax.ShapeDtypeStruct((r, c), f32)] * 3, compiler_params=_cparams(("parallel",)),
    )(w, m, v, g)


def _pick_rows(r):
    return max(tb for tb in range(8, ROW_TB + 1, 8) if r % tb == 0)


BIG = ("w_in", "w_uq", "w_ukv", "w_out", "w_gate_up", "w_down", "w_ple", "w_ple_gate")
COL_SHARDED = ("w_in", "w_uq", "w_ukv", "w_gate_up", "w_ple")
SMALL = ("conv_w", "a_log", "dt_bias", "gdn_norm_g", "q_norm_g", "kv_norm_g", "ln1_g", "ln1_b", "ln2_g", "ln2_b")
FULL_SHAPE = {"w_in": (D, IN_W), "w_uq": (QL, 768), "w_ukv": (KVL, 1024), "w_out": (D, D), "w_gate_up": (D, 2 * DFF),
              "w_down": (DFF, D), "w_ple": (PLE, D), "w_ple_gate": (D, D)}
ROW_ALIGN = 16


def _pad_rows(a, mult=ROW_ALIGN, axis=0):
    pad = (-a.shape[axis]) % mult
    widths = [(0, 0)] * a.ndim
    widths[axis] = (0, pad)
    return a if pad == 0 else jnp.pad(a, widths)


def _pack_layer(tree, l, dtype):
    return jnp.concatenate([_pad_rows(tree[n][l].reshape(-1, D).astype(dtype)) for n in BIG], axis=0)


def _big_rows(shards):
    rows, off = {}, 0
    for n in BIG:
        r = shards[n][1] * shards[n][2] // D
        rows[n] = (off, r)
        off += r + (-r) % ROW_ALIGN
    return rows, off


def _unpack_gathered(gathered, shards, n):
    off, r = _big_rows(shards)[0][n]
    _, sr, sc = shards[n]
    blk = gathered[:, off:off + r, :].reshape(N_DEV, sr, sc)
    if n in COL_SHARDED:
        return jnp.transpose(blk, (1, 0, 2)).reshape(sr, N_DEV * sc)
    return blk.reshape(N_DEV * sr, sc)


def _shard_grad(gfull, n, shards):
    _, sr, sc = shards[n]
    if n in COL_SHARDED:
        blk = jnp.transpose(gfull.reshape(sr, N_DEV, sc), (1, 0, 2))
    else:
        blk = gfull.reshape(N_DEV, sr, sc)
    return _pad_rows(blk.reshape(N_DEV, sr * sc // D, D), axis=1)


def _pack_small(tree, extra):
    flat = jnp.concatenate([tree[n].reshape(-1).astype(f32) for n in SMALL] + [extra.reshape(-1).astype(f32)])
    return jnp.pad(flat, (0, (-flat.shape[0]) % (ROW_ALIGN * D))).reshape(-1, D)


def _unpack_small(rows, like):
    flat = rows.reshape(-1)
    out, off = {}, 0
    for n in SMALL:
        sz = int(np.prod(like[n].shape))
        out[n] = flat[off:off + sz].reshape(like[n].shape)
        off += sz
    return out, flat[off]


def _zeros(r, c, dt):
    return jnp.zeros((r, c), dt)


def _prep_w_in(w):
    dt = w.dtype
    kr = w[:, 2696:2760]
    return jnp.concatenate([
        w[:, 0:2048],
        w[:, 2440:2696],
        kr, w[:, 2048:2056], _zeros(D, 56, dt),
        kr[:, 32:], kr[:, :32], _zeros(D, 64, dt),
        w[:, 2056:2440], _zeros(D, CQP - QL, dt)], axis=1)


def _unprep_w_in(g):
    krs = g[:, OFF_KRS:OFF_KRS + 64]
    kr = g[:, OFF_KR:OFF_KR + 64] + jnp.concatenate([krs[:, 32:], krs[:, :32]], axis=1)
    return jnp.concatenate([g[:, 0:2048], g[:, OFF_KR + 64:OFF_KR + 72], g[:, OFF_CQ:OFF_CQ + QL],
                            g[:, OFF_CKV:OFF_CKV + KVL], kr], axis=1)


def _prep_w_uq(w):
    dt = w.dtype
    z64 = _zeros(QL, 64, dt)
    nope, ra, rb = [], [], []
    for h in range(H):
        nope.append(w[:, 192 * h:192 * h + 128])
        x1, x2 = w[:, 192 * h + 128:192 * h + 160], w[:, 192 * h + 160:192 * h + 192]
        ra += [x1, x2, z64]
        rb += [x2, x1, z64]
    return jnp.concatenate([jnp.concatenate(nope + ra + rb, axis=1), _zeros(CQP - QL, 1536, dt)], axis=0)


def _unprep_w_uq(g):
    g = g[:QL]
    cols = []
    for h in range(H):
        a = g[:, 512 + 128 * h:512 + 128 * h + 64]
        b = g[:, 1024 + 128 * h:1024 + 128 * h + 64]
        cols += [g[:, 128 * h:128 * (h + 1)], a[:, :32] + b[:, 32:], a[:, 32:] + b[:, :32]]
    return jnp.concatenate(cols, axis=1)


def _rope_tables(positions):
    inv_freq = ROPE_THETA ** (-jnp.arange(0, ROPE, 2, dtype=f32) / ROPE)
    ang = positions.astype(f32)[:, None] * inv_freq
    c, s = jnp.cos(ang), jnp.sin(ang)
    z = jnp.zeros((positions.shape[0], 64), f32)
    return jnp.concatenate([c, c, z], axis=1), jnp.concatenate([-s, s, z], axis=1)


def _tile_heads(vec, n=H):
    return jnp.tile(vec.reshape(1, -1), (1, n))


def _bcast_heads(vec):
    return jnp.repeat(vec, DK).reshape(1, H * DK)


def kernel(x, p, positions, w_in, conv_w, a_log, dt_bias, gdn_norm_g, q_norm_g, w_uq, kv_norm_g, w_ukv, w_out, ln1_g, ln1_b, w_gate_up, w_down, ln2_g, ln2_b, w_ple, w_ple_gate, loss_target, m_w_in, m_conv_w, m_a_log, m_dt_bias, m_gdn_norm_g, m_q_norm_g, m_w_uq, m_kv_norm_g, m_w_ukv, m_w_out, m_ln1_g, m_ln1_b, m_w_gate_up, m_w_down, m_ln2_g, m_ln2_b, m_w_ple, m_w_ple_gate, v_w_in, v_conv_w, v_a_log, v_dt_bias, v_gdn_norm_g, v_q_norm_g, v_w_uq, v_kv_norm_g, v_w_ukv, v_w_out, v_ln1_g, v_ln1_b, v_w_gate_up, v_w_down, v_ln2_g, v_ln2_b, v_w_ple, v_w_ple_gate):
    W = dict(w_in=w_in, conv_w=conv_w, a_log=a_log, dt_bias=dt_bias, gdn_norm_g=gdn_norm_g, q_norm_g=q_norm_g, w_uq=w_uq,
             kv_norm_g=kv_norm_g, w_ukv=w_ukv, w_out=w_out, ln1_g=ln1_g, ln1_b=ln1_b, w_gate_up=w_gate_up, w_down=w_down,
             ln2_g=ln2_g, ln2_b=ln2_b, w_ple=w_ple, w_ple_gate=w_ple_gate)
    M = dict(w_in=m_w_in, conv_w=m_conv_w, a_log=m_a_log, dt_bias=m_dt_bias, gdn_norm_g=m_gdn_norm_g, q_norm_g=m_q_norm_g,
             w_uq=m_w_uq, kv_norm_g=m_kv_norm_g, w_ukv=m_w_ukv, w_out=m_w_out, ln1_g=m_ln1_g, ln1_b=m_ln1_b,
             w_gate_up=m_w_gate_up, w_down=m_w_down, ln2_g=m_ln2_g, ln2_b=m_ln2_b, w_ple=m_w_ple, w_ple_gate=m_w_ple_gate)
    V = dict(w_in=v_w_in, conv_w=v_conv_w, a_log=v_a_log, dt_bias=v_dt_bias, gdn_norm_g=v_gdn_norm_g, q_norm_g=v_q_norm_g,
             w_uq=v_w_uq, kv_norm_g=v_kv_norm_g, w_ukv=v_w_ukv, w_out=v_w_out, ln1_g=v_ln1_g, ln1_b=v_ln1_b,
             w_gate_up=v_w_gate_up, w_down=v_w_down, ln2_g=v_ln2_g, ln2_b=v_ln2_b, w_ple=v_w_ple, w_ple_gate=v_w_ple_gate)
    shards = {n: W[n].shape for n in BIG}
    t = x.shape[1]
    xin = x.reshape(t, D)
    target = loss_target.reshape(t, D)
    cf, sf = _rope_tables(positions.reshape(t))

    packs = [_pack_layer(W, l, bf16) for l in range(DEPTH)]
    gathered = [exchange("all_gather_weights", packs[0], False)] + [None] * (DEPTH - 1)
    conv_g = exchange("all_gather_conv", _pad_rows(jnp.pad(conv_w.reshape(-1), (0, (-conv_w.size) % D)).reshape(-1, D), 8), False)
    csz = conv_w.shape[1] * conv_w.shape[2]
    conv_full = jnp.transpose(conv_g.reshape(N_DEV, -1)[:, :DEPTH * csz].reshape(N_DEV, DEPTH, CW, -1), (1, 2, 0, 3)).reshape(DEPTH, CW, 1536)

    acts = []
    h_cur = xin
    for l in range(DEPTH):
        full = functools.partial(_unpack_gathered, gathered[l], shards)
        win = _prep_w_in(full("w_in"))
        wq = _prep_w_uq(full("w_uq"))
        wkv, wout = full("w_ukv"), full("w_out")
        wgu, wd = full("w_gate_up"), full("w_down")
        wg, wu = wgu[:, :DFF], wgu[:, DFF:]
        wple, wpg = full("w_ple"), full("w_ple_gate")
        cw8 = jnp.concatenate([conv_full[l], jnp.zeros((8 - CW, 1536), f32)], axis=0)
        alog_b, dtb_b = _bcast_heads(a_log[l]), _bcast_heads(dt_bias[l])
        gg_b = _tile_heads(gdn_norm_g[l])
        qg = jnp.concatenate([q_norm_g[l], jnp.zeros((CQP - QL,), f32)]).reshape(1, CQP)
        kvg = kv_norm_g[l].reshape(1, KVL)
        g1, b1, g2, b2 = (a[l].reshape(1, D) for a in (ln1_g, ln1_b, ln2_g, ln2_b))
        p_l = p[l].reshape(t, PLE)

        hb = mm("in_proj", h_cur, win, mode="nn", out_dtype=f32)
        qn, kn, vs, beta, glog = gdn_pre_fwd(hb, cw8, alog_b, dtb_b)
        g_u, g_w, g_qd, g_kd, g_at, g_el = gdn_intra(qn, kn, vs, beta, glog)
        o_gdn, states = gdn_scan_fwd(g_u, g_w, g_qd, g_kd, g_at, g_el)
        mla_rows = [(hb, CQP, OFF_CQ // CQP), (hb, KVL, OFF_CKV // KVL), (hb, LANE, OFF_KR // LANE), (hb, LANE, OFF_KRS // LANE),
                    (cf, LANE, 0), (sf, LANE, 0)]
        mla_consts = [qg, wq, kvg, wkv]
        qm, km, vm = row_fwd("mla_pre", mla_pre_f, mla_rows, mla_consts, [(1024, bf16), (1024, bf16), (512, bf16)])
        if l + 1 < DEPTH:
            o_mla, lse, gathered[l + 1] = attn_fwd(qm, km, vm, name="attn_fwd_gather", comm=(packs[l + 1], False))
        else:
            o_mla, lse = attn_fwd(qm, km, vm)
        p1_rows = [(o_gdn, 512, 0), (hb, 512, OFF_Z // 512), (o_mla, 512, 0), (h_cur, D, 0)]
        p1_consts = [gg_b, wout, g1, b1]
        (x1,) = row_fwd("post1", post1_f, p1_rows, p1_consts, [(D, f32)])
        gate, up, act = ffn_up(x1, wg, wu)
        down = mm("ffn_down", act, wd, mode="nn", out_dtype=f32)
        p2_rows = [(x1, D, 0), (down, D, 0), (p_l, PLE, 0)]
        p2_consts = [g2, b2, wpg, wple]
        (y,) = row_fwd("post2", post2_f, p2_rows, p2_consts, [(D, f32)])
        acts.append(dict(x=h_cur, hb=hb, qn=qn, kn=kn, vs=vs, beta=beta, glog=glog, states=states, o_gdn=o_gdn, qm=qm, km=km,
                         scan=(g_w, g_qd, g_kd, g_at, g_el),
                         vm=vm, o_mla=o_mla, lse=lse, x1=x1, gate=gate, up=up, act=act, win=win, wd=wd, wg=wg, wu=wu,
                         cw8=cw8, alog_b=alog_b, dtb_b=dtb_b, mla_rows=mla_rows, mla_consts=mla_consts, p1_rows=p1_rows,
                         p1_consts=p1_consts, p2_rows=p2_rows, p2_consts=p2_consts))
        h_cur = y

    dy, loss_part = loss_head(h_cur, target)

    G = {n: [None] * DEPTH for n in list(BIG) + list(SMALL)}
    parts = [None] * DEPTH
    pending = None
    for l in reversed(range(DEPTH)):
        a = acts[l]
        dx1, ddown, dg2, db2, dwpg, dwple = row_bwd("post2_bwd", post2_f, a["p2_rows"], a["p2_consts"], [dy], [0, 1], [0, 1, 2, 3], [f32, f32])
        dgate, dup = ffn_dact(ddown, a["wd"], a["gate"], a["up"])
        dwd = mm("dw_down", a["act"], ddown, mode="tn", out_dtype=f32)
        dx1 = mm("dx_gate", dgate, a["wg"], mode="nt", out_dtype=f32, add=dx1)
        dx1 = mm("dx_up", dup, a["wu"], mode="nt", out_dtype=f32, add=dx1)
        dwg = mm("dw_gate", a["x1"], dgate, mode="tn", out_dtype=f32)
        dwu = mm("dw_up", a["x1"], dup, mode="tn", out_dtype=f32)
        do_gdn, dz, do_mla, dxr, dgg, dwout, dg1, db1 = row_bwd("post1_bwd", post1_f, a["p1_rows"], a["p1_consts"], [dx1], [0, 1, 2, 3],
                                                              [0, 1, 2, 3], [f32, bf16, f32, f32])
        if pending is not None:
            dqm, dkm, dvm, parts[pending[0]] = attn_bwd(a["qm"], a["km"], a["vm"], a["o_mla"], a["lse"], do_mla,
                                                        name="attn_bwd_exchange", comm=(pending[1], True))
        else:
            dqm, dkm, dvm = attn_bwd(a["qm"], a["km"], a["vm"], a["o_mla"], a["lse"], do_mla)
        dstates = gdn_scan_bwd(do_gdn, *a["scan"])
        dqn, dkn, dvs, dbeta, dglog = gdn_local_bwd(a["qn"], a["kn"], a["vs"], a["beta"], a["glog"], a["states"], dstates, do_gdn)
        dqkv, dba, dcw, dal, ddt = gdn_pre_bwd(a["hb"], a["cw8"], a["alog_b"], a["dtb_b"], dqn, dkn, dvs, dbeta, dglog)
        dcq, dckv, dkrg, dkrs, dqg, dwq, dkvg, dwkv = row_bwd("mla_pre_bwd", mla_pre_f, a["mla_rows"], a["mla_consts"], [dqm, dkm, dvm],
                                                              [0, 1, 2, 3], [0, 1, 2, 3], [bf16, bf16, bf16, bf16], row_add={2: dba})
        dh = jnp.concatenate([dqkv, dz, dckv, dkrg, dkrs, dcq], axis=1)
        dwin = mm("dw_in", a["x"], dh, mode="tn", out_dtype=f32)
        dy = mm("dx_in", dh, a["win"], mode="nt", out_dtype=f32, add=dxr)

        G["w_in"][l] = _unprep_w_in(dwin)
        G["w_uq"][l] = _unprep_w_uq(dwq)
        G["w_ukv"][l], G["w_out"][l], G["w_down"][l] = dwkv, dwout, dwd
        G["w_gate_up"][l] = jnp.concatenate([dwg, dwu], axis=1)
        G["w_ple"][l], G["w_ple_gate"][l] = dwple, dwpg
        G["conv_w"][l] = dcw[:CW]
        G["a_log"][l] = jnp.sum(dal.reshape(H, DK), axis=1)
        G["dt_bias"][l] = jnp.sum(ddt.reshape(H, DK), axis=1)
        G["gdn_norm_g"][l] = jnp.sum(dgg.reshape(H, DK), axis=0)
        G["q_norm_g"][l] = dqg[0, :QL]
        G["kv_norm_g"][l] = dkvg[0]
        G["ln1_g"][l], G["ln1_b"][l], G["ln2_g"][l], G["ln2_b"][l] = dg1[0], db1[0], dg2[0], db2[0]
        pending = (l, jnp.concatenate([_shard_grad(G[n][l], n, shards) for n in BIG], axis=1))
    grad_x = dy.reshape(x.shape)

    small_like = {n: W[n] for n in SMALL}
    conv_parts = jnp.stack(G["conv_w"]).reshape(DEPTH, CW, N_DEV, -1)
    smalls = []
    for d in range(N_DEV):
        tree = {n: jnp.stack(G[n]) for n in SMALL if n != "conv_w"}
        tree["conv_w"] = conv_parts[:, :, d, :]
        smalls.append(_pack_small(tree, loss_part))
    small_send = jnp.stack(smalls)
    rows_of, n_big = _big_rows(shards)
    n_small = small_send.shape[1]
    tail = (-(n_big + n_small)) % (3 * ROW_ALIGN)
    parts[0] = exchange("exchange_last", jnp.concatenate([pending[1], small_send, jnp.zeros((N_DEV, tail, D), f32)], axis=1), True)

    g_rows = [sum_parts(parts[l]) for l in range(DEPTH)]
    g_out, d_out, m_out, v_out = {}, {}, {}, {}
    for n in BIG:
        off, r = rows_of[n]
        shape2 = (DEPTH * shards[n][1], shards[n][2])
        g_out[n] = jnp.stack([g_rows[l][off:off + r].reshape(shards[n][1:]) for l in range(DEPTH)])
        upd = adamw(W[n].reshape(shape2), M[n].reshape(shape2), V[n].reshape(shape2), g_out[n].reshape(shape2))
        d_out[n], m_out[n], v_out[n] = (u.reshape(shards[n]) for u in upd)
    zero = jnp.zeros((), f32)
    g_small = g_rows[0][n_big:n_big + n_small]
    upd = adamw(_pack_small(W, zero), _pack_small(M, zero), _pack_small(V, zero), g_small)
    for dst, rows_arr in zip((g_out, d_out, m_out, v_out), (g_small,) + tuple(upd)):
        dst.update(_unpack_small(rows_arr, small_like)[0])
    loss = _unpack_small(g_small, small_like)[1]
    order = ["w_in", "conv_w", "a_log", "dt_bias", "gdn_norm_g", "q_norm_g", "w_uq", "kv_norm_g", "w_ukv", "w_out", "ln1_g", "ln1_b",
             "w_gate_up", "w_down", "ln2_g", "ln2_b", "w_ple", "w_ple_gate"]
    return (loss, grad_x, *[g_out[n] for n in order], *[d_out[n] for n in order], *[m_out[n] for n in order],
            *[v_out[n] for n in order])
```

```python
import functools

import jax
import jax.numpy as jnp
import numpy as np
from jax import lax
from jax.experimental import pallas as pl
from jax.experimental.pallas import tpu as pltpu

f32 = jnp.float32
bf16 = jnp.bfloat16
MX = jnp.bfloat16

D = 1024
DEPTH = 4
CHUNK = 64
H = 4
DK = 128
PLE = 256
QL = 384
KVL = 256
ROPE = 64
DFF = 2816
IN_W = 2760
ROPE_THETA = 10000.0
ALPHA = (2.0 * DEPTH) ** 0.25
LN_EPS = 1e-5
RMS_EPS = 1e-6
ATT_SCALE = (128 + 64) ** -0.5
N_DEV = 8

ADAM_LR, ADAM_B1, ADAM_B2, ADAM_EPS, ADAM_WD, ADAM_STEP = 0.001, 0.9, 0.999, 1e-08, 0.01, 10

OFF_QKV, OFF_Z, OFF_CKV, OFF_KR, OFF_KRS, OFF_CQ, HP = 0, 1536, 2048, 2304, 2432, 2560, 3072
CQP = 512
LANE = 128
VMEM_LIMIT = 48 * 1024 * 1024
ROW_TB = 256


def _dot(a, b, ca, cb, prec=None):
    if a.ndim == 3:
        return lax.dot_general(a, b, (((ca + 1,), (cb + 1,)), ((0,), (0,))), precision=prec, preferred_element_type=f32)
    return lax.dot_general(a, b, (((ca,), (cb,)), ((), ())), precision=prec, preferred_element_type=f32)


@jax.custom_vjp
def bdot(a, w):
    return _dot(a.astype(MX), w.astype(MX), 1, 0)


def _bdot_fwd(a, w):
    return bdot(a, w), (a, w)


def _bdot_bwd(res, g):
    a, w = res
    gb = g.astype(MX)
    return _dot(gb, w.astype(MX), 1, 1).astype(a.dtype), _dot(a.astype(MX), gb, 0, 0).astype(w.dtype)


bdot.defvjp(_bdot_fwd, _bdot_bwd)


@jax.custom_vjp
def bdot_nt(a, b):
    return _dot(a.astype(MX), b.astype(MX), 1, 1)


def _bdot_nt_fwd(a, b):
    return bdot_nt(a, b), (a, b)


def _bdot_nt_bwd(res, g):
    a, b = res
    gb = g.astype(MX)
    return _dot(gb, b.astype(MX), 1, 0).astype(a.dtype), _dot(gb, a.astype(MX), 0, 0).astype(b.dtype)


bdot_nt.defvjp(_bdot_nt_fwd, _bdot_nt_bwd)


@jax.custom_vjp
def bdot_tn(a, b):
    return _dot(a.astype(MX), b.astype(MX), 0, 0)


def _bdot_tn_fwd(a, b):
    return bdot_tn(a, b), (a, b)


def _bdot_tn_bwd(res, g):
    a, b = res
    gb = g.astype(MX)
    return _dot(b.astype(MX), gb, 1, 1).astype(a.dtype), _dot(a.astype(MX), gb, 1, 0).astype(b.dtype)


bdot_tn.defvjp(_bdot_tn_fwd, _bdot_tn_bwd)


@jax.custom_vjp
def hdot(a, b):
    return _dot(a, b, 1, 0, lax.Precision.HIGHEST)


def _hdot_fwd(a, b):
    return hdot(a, b), (a, b)


def _hdot_bwd(res, g):
    a, b = res
    return _dot(g, b, 1, 1, lax.Precision.HIGHEST), _dot(a, g, 0, 0, lax.Precision.HIGHEST)


hdot.defvjp(_hdot_fwd, _hdot_bwd)


def _pick(n, cap):
    best = None
    for t in range(LANE, min(n, cap) + 1, LANE):
        if n % t == 0:
            best = t
    assert best is not None, (n, cap)
    return best


def _cparams(sem):
    return pltpu.CompilerParams(dimension_semantics=sem, vmem_limit_bytes=VMEM_LIMIT)


def mm(name, a, b, *, mode, out_dtype, add=None):
    if mode == "tn":
        kdim, m = a.shape
        n = b.shape[1]
        tm, tn, tk = _pick(m, 1536), _pick(n, 1536), min(512, kdim)
        nk = kdim // tk

        def body(a_ref, b_ref, o_ref, acc):
            k = pl.program_id(2)

            @pl.when(k == 0)
            def _():
                acc[...] = jnp.zeros_like(acc)

            acc[...] += _dot(a_ref[...].astype(MX), b_ref[...].astype(MX), 0, 0)

            @pl.when(k == nk - 1)
            def _():
                o_ref[...] = acc[...].astype(o_ref.dtype)

        return pl.pallas_call(
            body, name=name, grid=(m // tm, n // tn, nk),
            in_specs=[pl.BlockSpec((tk, tm), lambda i, j, k: (k, i)), pl.BlockSpec((tk, tn), lambda i, j, k: (k, j))],
            out_specs=pl.BlockSpec((tm, tn), lambda i, j, k: (i, j)),
            out_shape=jax.ShapeDtypeStruct((m, n), out_dtype),
            scratch_shapes=[pltpu.VMEM((tm, tn), f32)],
            compiler_params=_cparams(("parallel", "parallel", "arbitrary")),
        )(a, b)

    m, kdim = a.shape
    n = b.shape[1] if mode == "nn" else b.shape[0]
    tm, tn = min(512, m), _pick(n, 1536)
    has_add = add is not None

    def body(*refs):
        a_ref, b_ref = refs[0], refs[1]
        o_ref = refs[-1]
        r = _dot(a_ref[...].astype(MX), b_ref[...].astype(MX), 1, 0 if mode == "nn" else 1)
        if has_add:
            r = r + refs[2][...].astype(f32)
        o_ref[...] = r.astype(o_ref.dtype)

    b_spec = (pl.BlockSpec((kdim, tn), lambda i, j: (0, j)) if mode == "nn"
              else pl.BlockSpec((tn, kdim), lambda i, j: (j, 0)))
    in_specs = [pl.BlockSpec((tm, kdim), lambda i, j: (i, 0)), b_spec]
    args = [a, b]
    if has_add:
        in_specs.append(pl.BlockSpec((tm, tn), lambda i, j: (i, j)))
        args.append(add)
    return pl.pallas_call(
        body, name=name, grid=(m // tm, n // tn), in_specs=in_specs,
        out_specs=pl.BlockSpec((tm, tn), lambda i, j: (i, j)),
        out_shape=jax.ShapeDtypeStruct((m, n), out_dtype),
        compiler_params=_cparams(("parallel", "parallel")),
    )(*args)


def _row_spec(tb, w, cb):
    return pl.BlockSpec((tb, w), lambda i: (i, cb))


def _full_spec(shape):
    return pl.BlockSpec(shape, lambda i: (0,) * len(shape))


def row_fwd(name, f, rows, consts, outs):
    t = rows[0][0].shape[0]
    tb = min(ROW_TB, t)
    nr, nc = len(rows), len(consts)

    def body(*refs):
        vals = [r[...] for r in refs[:nr + nc]]
        res = f(*vals)
        for o_ref, val in zip(refs[nr + nc:], res):
            o_ref[...] = val.astype(o_ref.dtype)

    return pl.pallas_call(
        body, name=name, grid=(t // tb,),
        in_specs=[_row_spec(tb, w, cb) for _, w, cb in rows] + [_full_spec(c.shape) for c in consts],
        out_specs=[_row_spec(tb, w, 0) for w, _ in outs],
        out_shape=[jax.ShapeDtypeStruct((t, w), dt) for w, dt in outs],
        compiler_params=_cparams(("parallel",)),
    )(*[r[0] for r in rows], *consts)


def row_bwd(name, f, rows, consts, cots, row_diff, const_diff, drow_dtypes, row_add=None):
    t = rows[0][0].shape[0]
    tb = min(ROW_TB, t)
    nr, nc, nct = len(rows), len(consts), len(cots)
    row_add = row_add or {}
    add_keys = sorted(row_add)
    n_in = nr + nc + nct + len(add_keys)

    def body(*refs):
        i = pl.program_id(0)
        rv = [r[...] for r in refs[:nr]]
        cv = [refs[nr + k][...].astype(f32) if k in const_diff else refs[nr + k][...] for k in range(nc)]
        cot_refs = refs[nr + nc:nr + nc + nct]
        add_refs = refs[nr + nc + nct:n_in]
        drow_refs = refs[n_in:n_in + len(row_diff)]
        dconst_refs = refs[n_in + len(row_diff):]

        def g(*dv):
            r2, c2 = list(rv), list(cv)
            for p, k in enumerate(row_diff):
                r2[k] = dv[p]
            for p, k in enumerate(const_diff):
                c2[k] = dv[len(row_diff) + p]
            return tuple(f(*r2, *c2))

        prim = [rv[k].astype(f32) for k in row_diff] + [cv[k] for k in const_diff]
        outs, vf = jax.vjp(g, *prim)
        grads = vf(tuple(c[...].astype(o.dtype) for c, o in zip(cot_refs, outs)))
        for p, ref in enumerate(drow_refs):
            val = grads[p]
            if p in row_add:
                val = val + add_refs[add_keys.index(p)][...].astype(f32)
            ref[...] = val.astype(ref.dtype)

        @pl.when(i == 0)
        def _():
            for ref in dconst_refs:
                ref[...] = jnp.zeros_like(ref)

        for p, ref in enumerate(dconst_refs):
            ref[...] += grads[len(row_diff) + p]

    widths = [rows[k][1] for k in row_diff]
    return pl.pallas_call(
        body, name=name, grid=(t // tb,),
        in_specs=([_row_spec(tb, w, cb) for _, w, cb in rows] + [_full_spec(c.shape) for c in consts]
                  + [_row_spec(tb, c.shape[1], 0) for c in cots] + [_row_spec(tb, row_add[k].shape[1], 0) for k in add_keys]),
        out_specs=([_row_spec(tb, w, 0) for w in widths] + [_full_spec(consts[k].shape) for k in const_diff]),
        out_shape=([jax.ShapeDtypeStruct((t, w), dt) for w, dt in zip(widths, drow_dtypes)]
                   + [jax.ShapeDtypeStruct(consts[k].shape, f32) for k in const_diff]),
        compiler_params=_cparams(("arbitrary",)),
    )(*[r[0] for r in rows], *consts, *cots, *[row_add[k] for k in add_keys])


def _heads(x, w=DK):
    return [x[:, w * h:w * (h + 1)] for h in range(H)]


def _layer_norm(r, g, b):
    mu = jnp.mean(r, -1, keepdims=True)
    var = jnp.mean(jnp.square(r - mu), -1, keepdims=True)
    return (r - mu) * lax.rsqrt(var + LN_EPS) * g + b


def gdn_point_f(c, ba, alog_b, dtb_b):
    s = c * jax.nn.sigmoid(c)
    q, k, v = s[:, :512], s[:, 512:1024], s[:, 1024:]

    def l2(x):
        return jnp.concatenate([xh * lax.rsqrt(jnp.sum(xh * xh, -1, keepdims=True) + RMS_EPS) for xh in _heads(x)], axis=1)

    tb = c.shape[0]
    b_b = jnp.concatenate([jnp.broadcast_to(ba[:, 64 + h:65 + h], (tb, DK)) for h in range(H)], axis=1)
    a_b = jnp.concatenate([jnp.broadcast_to(ba[:, 68 + h:69 + h], (tb, DK)) for h in range(H)], axis=1)
    beta = jax.nn.sigmoid(b_b)
    g = -jnp.exp(alog_b) * jax.nn.softplus(a_b + dtb_b)
    return l2(q), l2(k), v, beta, g


def mla_pre_f(cq, ckv, krg, krs, cf, sf, qg, wq, kvg, wkv):
    cqn = cq * lax.rsqrt(jnp.sum(cq * cq, -1, keepdims=True) * (1.0 / QL) + RMS_EPS) * qg
    qa = bdot(cqn, wq)
    ckvn = ckv * lax.rsqrt(jnp.mean(ckv * ckv, -1, keepdims=True) + RMS_EPS) * kvg
    kv = bdot(ckvn, wkv)
    kro = krg * cf + krs * sf
    qs, ks, vs = [], [], []
    for h in range(H):
        qs += [qa[:, DK * h:DK * (h + 1)], qa[:, 512 + DK * h:512 + DK * (h + 1)] * cf + qa[:, 1024 + DK * h:1024 + DK * (h + 1)] * sf]
        ks += [kv[:, 256 * h:256 * h + DK], kro]
        vs += [kv[:, 256 * h + DK:256 * (h + 1)]]
    return jnp.concatenate(qs, axis=1) * ATT_C2, jnp.concatenate(ks, axis=1), jnp.concatenate(vs, axis=1)


def post1_f(o, z, omla, x, gg_b, wout, g1, b1):
    on = jnp.concatenate([oh * lax.rsqrt(jnp.mean(oh * oh, -1, keepdims=True) + RMS_EPS) for oh in _heads(o)], axis=1) * gg_b
    ogdn = on * (z * jax.nn.sigmoid(z))
    mix = bdot(jnp.concatenate([ogdn, omla], axis=1), wout)
    return (_layer_norm(ALPHA * x + mix, g1, b1),)


def post2_f(x1, down, p, g2, b2, wpg, wple):
    x2 = _layer_norm(ALPHA * x1 + down, g2, b2)
    return (x2 + jax.nn.sigmoid(bdot(x2, wpg)) * bdot(p, wple),)


HALO = 8
CW = 4


def _conv_from_scratch(xs, cw_ref, tb):
    c = xs[pl.ds(HALO - 3, tb), :] * cw_ref[0:1, :]
    for j in range(1, CW):
        c = c + xs[pl.ds(HALO - 3 + j, tb), :] * cw_ref[j:j + 1, :]
    return c


def gdn_pre_fwd(hbuf, conv_w8, alog_b, dtb_b):
    t = hbuf.shape[0]
    tb = min(ROW_TB, t)

    def body(x_ref, halo_ref, ba_ref, cw_ref, al_ref, dt_ref, q_ref, k_ref, v_ref, be_ref, g_ref, xs):
        i = pl.program_id(0)
        xs[pl.ds(0, HALO), :] = jnp.where(i == 0, 0.0, halo_ref[...])
        xs[pl.ds(HALO, tb), :] = x_ref[...]
        c = _conv_from_scratch(xs, cw_ref, tb)
        q, k, v, be, g = gdn_point_f(c, ba_ref[...], al_ref[...], dt_ref[...])
        q_ref[...], k_ref[...], v_ref[...], be_ref[...], g_ref[...] = q, k, v, be, g

    return pl.pallas_call(
        body, name="gdn_pre_fwd", grid=(t // tb,),
        in_specs=[_row_spec(tb, 1536, 0),
                  pl.BlockSpec((HALO, 1536), lambda i: (jnp.maximum(i * (tb // HALO) - 1, 0), 0)),
                  _row_spec(tb, LANE, OFF_KR // LANE),
                  _full_spec(conv_w8.shape), _full_spec(alog_b.shape), _full_spec(dtb_b.shape)],
        out_specs=[_row_spec(tb, 512, 0)] * 5,
        out_shape=[jax.ShapeDtypeStruct((t, 512), f32)] * 5,
        scratch_shapes=[pltpu.VMEM((tb + HALO, 1536), f32)],
        compiler_params=_cparams(("arbitrary",)),
    )(hbuf, hbuf, hbuf, conv_w8, alog_b, dtb_b)


def gdn_pre_bwd(hbuf, conv_w8, alog_b, dtb_b, dq, dk, dv, dbe, dg):
    t = hbuf.shape[0]
    tb = min(ROW_TB, t)
    n = t // tb

    def body(x_ref, halo_ref, ba_ref, cw_ref, al_ref, dt_ref, dq_ref, dk_ref, dv_ref, dbe_ref, dg_ref,
             dx_ref, dba_ref, dcw_ref, dal_ref, ddt_ref, xs, dcs):
        s = pl.program_id(0)
        i = n - 1 - s
        xs[pl.ds(0, HALO), :] = jnp.where(i == 0, 0.0, halo_ref[...])
        xs[pl.ds(HALO, tb), :] = x_ref[...]
        c = _conv_from_scratch(xs, cw_ref, tb)
        _, vf = jax.vjp(gdn_point_f, c, ba_ref[...], al_ref[...], dt_ref[...])
        dc, dba, dal, ddt = vf((dq_ref[...], dk_ref[...], dv_ref[...], dbe_ref[...], dg_ref[...]))

        @pl.when(s == 0)
        def _():
            dcs[pl.ds(tb, HALO), :] = jnp.zeros((HALO, 1536), f32)
            dcw_ref[...] = jnp.zeros_like(dcw_ref)
            dal_ref[...] = jnp.zeros_like(dal_ref)
            ddt_ref[...] = jnp.zeros_like(ddt_ref)

        @pl.when(s > 0)
        def _():
            dcs[pl.ds(tb, HALO), :] = dcs[pl.ds(0, HALO), :]

        dcs[pl.ds(0, tb), :] = dc
        dx = dcs[pl.ds(3, tb), :] * cw_ref[0:1, :]
        for j in range(1, CW):
            dx = dx + dcs[pl.ds(3 - j, tb), :] * cw_ref[j:j + 1, :]
        dx_ref[...] = dx.astype(dx_ref.dtype)
        dba_ref[...] = dba
        for j in range(CW):
            dcw_ref[j:j + 1, :] += jnp.sum(dc * xs[pl.ds(HALO - 3 + j, tb), :], axis=0, keepdims=True)
        dal_ref[...] += dal
        ddt_ref[...] += ddt

    rev = lambda cb: (lambda s: (n - 1 - s, cb))
    return pl.pallas_call(
        body, name="gdn_pre_bwd", grid=(n,),
        in_specs=[pl.BlockSpec((tb, 1536), rev(0)),
                  pl.BlockSpec((HALO, 1536), lambda s: (jnp.maximum((n - 1 - s) * (tb // HALO) - 1, 0), 0)),
                  pl.BlockSpec((tb, LANE), rev(OFF_KR // LANE)),
                  _full_spec(conv_w8.shape), _full_spec(alog_b.shape), _full_spec(dtb_b.shape)]
        + [pl.BlockSpec((tb, 512), rev(0))] * 5,
        out_specs=[pl.BlockSpec((tb, 1536), rev(0)), pl.BlockSpec((tb, LANE), rev(0)),
                   _full_spec(conv_w8.shape), _full_spec(alog_b.shape), _full_spec(dtb_b.shape)],
        out_shape=[jax.ShapeDtypeStruct((t, 1536), bf16), jax.ShapeDtypeStruct((t, LANE), f32),
                   jax.ShapeDtypeStruct(conv_w8.shape, f32), jax.ShapeDtypeStruct(alog_b.shape, f32),
                   jax.ShapeDtypeStruct(dtb_b.shape, f32)],
        scratch_shapes=[pltpu.VMEM((tb + HALO, 1536), f32), pltpu.VMEM((tb + HALO, 1536), f32)],
        compiler_params=_cparams(("arbitrary",)),
    )(hbuf, hbuf, hbuf, conv_w8, alog_b, dtb_b, dq, dk, dv, dbe, dg)


GDN_NB = 4


def _to_batch(x, rows=CHUNK):
    n = x.shape[0] // rows
    return jnp.concatenate([x[rows * ci:rows * (ci + 1), DK * h:DK * (h + 1)][None] for ci in range(n) for h in range(H)], axis=0)


def _from_batch(y):
    n = y.shape[0] // H
    return jnp.concatenate([jnp.concatenate([y[ci * H + h] for h in range(H)], axis=1) for ci in range(n)], axis=0)


def gdn_intra_b(q, k, v, beta, g):
    c = CHUNK
    n = q.shape[0] // c
    tri = jnp.where(lax.broadcasted_iota(jnp.int32, (c, c), 0) >= lax.broadcasted_iota(jnp.int32, (c, c), 1), 1.0, 0.0)
    gc = _to_batch(jnp.concatenate([hdot(tri, g[c * ci:c * (ci + 1)]) for ci in range(n)], axis=0))
    qb, kb, vb, bb = _to_batch(q) * DK ** -0.5, _to_batch(k), _to_batch(v), _to_batch(beta)
    nbat = qb.shape[0]
    row = lax.broadcasted_iota(jnp.int32, (1, c, DK), 1)
    col = lax.broadcasted_iota(jnp.int32, (1, c, DK), 2)
    incl, strict, eye = row >= col, row > col, row == col
    grow = hdot(jnp.ones((nbat, c, c), f32), jnp.where(eye, gc, 0.0))
    decay = jnp.where(incl, jnp.exp(jnp.where(incl, gc - grow, 0.0)), 0.0)
    kbeta = kb * bb
    kpad = jnp.concatenate([kb, jnp.zeros((nbat, DK - c, DK), f32)], axis=1)
    a = jnp.where(strict, bdot_nt(kbeta, kpad) * decay, 0.0)[:, :, :c]
    nn = -a
    bk = bdot(a, a)
    for step in range(5):
        nn = nn + bk + bdot(nn, bk)
        if step < 4:
            bk = bdot(bk, bk)
    eg = jnp.exp(gc)
    rhs_v, rhs_k = vb * bb, kbeta * eg
    g_last = gc[:, c - 1:c, :]
    u = rhs_v + bdot(nn, rhs_v)
    w = rhs_k + bdot(nn, rhs_k)
    attn = jnp.where(incl, bdot_nt(qb, kpad) * decay, 0.0)
    return u, w, qb * eg, kb * jnp.exp(g_last - gc), attn, jnp.exp(g_last)


def gdn_scan_b(u, w, qd, kd, attn, el, state):
    v_new = u - bdot(w, state)
    return bdot(qd, state) + bdot(attn[:, :, :CHUNK], v_new), state * el + bdot_tn(kd, v_new)


def gdn_scan_bwd_b(do, w, qd, kd, attn, el, dstate):
    dvn = bdot_tn(attn[:, :, :CHUNK], do) + bdot(kd, dstate)
    return bdot_tn(qd, do) + dstate * el - bdot_tn(w, dvn)


def gdn_chunks_f(q, k, v, beta, g, states):
    n = states.shape[0]
    o, s_new = gdn_scan_b(*gdn_intra_b(q, k, v, beta, g), states.reshape(n * H, DK, DK))
    return _from_batch(o), s_new.reshape(n, H * DK, DK)


def _chunk_rows(cidx):
    return pl.ds(cidx * CHUNK, CHUNK)


def gdn_intra(q, k, v, beta, g):
    t = q.shape[0]
    nb = min(GDN_NB, t // CHUNK)
    tb = nb * CHUNK

    def body(q_ref, k_ref, v_ref, be_ref, g_ref, u_ref, w_ref, qd_ref, kd_ref, at_ref, el_ref):
        u, w, qd, kd, at, el = gdn_intra_b(q_ref[...], k_ref[...], v_ref[...], be_ref[...], g_ref[...])
        u_ref[...], w_ref[...], qd_ref[...], kd_ref[...], at_ref[...] = (_from_batch(a) for a in (u, w, qd, kd, at))
        el_ref[...] = _from_batch(jnp.broadcast_to(el, (nb * H, 8, DK))).reshape(nb, 8, H * DK)

    return pl.pallas_call(
        body, name="gdn_intra", grid=(t // tb,),
        in_specs=[_row_spec(tb, 512, 0)] * 5,
        out_specs=[_row_spec(tb, 512, 0)] * 5 + [pl.BlockSpec((nb, 8, 512), lambda i: (i, 0, 0))],
        out_shape=[jax.ShapeDtypeStruct((t, 512), f32)] * 5 + [jax.ShapeDtypeStruct((t // CHUNK, 8, 512), f32)],
        compiler_params=_cparams(("parallel",)),
    )(q, k, v, beta, g)


def gdn_scan_fwd(u, w, qd, kd, attn, el):
    t = u.shape[0]
    nb = min(GDN_NB, t // CHUNK)
    tb = nb * CHUNK

    def body(u_ref, w_ref, qd_ref, kd_ref, at_ref, el_ref, o_ref, s_ref, state):
        @pl.when(pl.program_id(0) == 0)
        def _():
            state[...] = jnp.zeros_like(state)

        for cidx in range(nb):
            r = _chunk_rows(cidx)
            s_ref[cidx] = state[...]
            ins = [_to_batch(ref[r, :]) for ref in (u_ref, w_ref, qd_ref, kd_ref, at_ref)]
            el = _to_batch(el_ref[cidx], 8)[:, 0:1, :]
            o, s_new = gdn_scan_b(*ins, el, state[...].reshape(H, DK, DK))
            o_ref[r, :] = _from_batch(o)
            state[...] = s_new.reshape(H * DK, DK)

    return pl.pallas_call(
        body, name="gdn_scan_fwd", grid=(t // tb,),
        in_specs=[_row_spec(tb, 512, 0)] * 5 + [pl.BlockSpec((nb, 8, 512), lambda i: (i, 0, 0))],
        out_specs=[_row_spec(tb, 512, 0), pl.BlockSpec((nb, 512, DK), lambda i: (i, 0, 0))],
        out_shape=[jax.ShapeDtypeStruct((t, 512), f32), jax.ShapeDtypeStruct((t // CHUNK, 512, DK), f32)],
        scratch_shapes=[pltpu.VMEM((512, DK), f32)],
        compiler_params=_cparams(("arbitrary",)),
    )(u, w, qd, kd, attn, el)


def gdn_scan_bwd(do, w, qd, kd, attn, el):
    t = do.shape[0]
    nb = min(GDN_NB, t // CHUNK)
    tb = nb * CHUNK
    n = t // tb

    def body(do_ref, w_ref, qd_ref, kd_ref, at_ref, el_ref, ds_ref, dstate):
        @pl.when(pl.program_id(0) == 0)
        def _():
            dstate[...] = jnp.zeros_like(dstate)

        for cidx in reversed(range(nb)):
            r = _chunk_rows(cidx)
            ds_ref[cidx] = dstate[...]
            ins = [_to_batch(ref[r, :]) for ref in (do_ref, w_ref, qd_ref, kd_ref, at_ref)]
            el = _to_batch(el_ref[cidx], 8)[:, 0:1, :]
            dstate[...] = gdn_scan_bwd_b(*ins, el, dstate[...].reshape(H, DK, DK)).reshape(H * DK, DK)

    rev = pl.BlockSpec((tb, 512), lambda s: (n - 1 - s, 0))
    return pl.pallas_call(
        body, name="gdn_scan_bwd", grid=(n,),
        in_specs=[rev] * 5 + [pl.BlockSpec((nb, 8, 512), lambda s: (n - 1 - s, 0, 0))],
        out_specs=pl.BlockSpec((nb, 512, DK), lambda s: (n - 1 - s, 0, 0)),
        out_shape=jax.ShapeDtypeStruct((t // CHUNK, 512, DK), f32),
        scratch_shapes=[pltpu.VMEM((512, DK), f32)],
        compiler_params=_cparams(("arbitrary",)),
    )(do, w, qd, kd, attn, el)


def gdn_local_bwd(q, k, v, beta, g, states, dstates, do, name="gdn_local_bwd", comms=()):
    t = q.shape[0]
    nb = min(GDN_NB, t // CHUNK)
    tb = nb * CHUNK
    n = t // tb

    def body(*refs):
        ins, outs, _, comm_refs = _split_hosted(refs, 8, 5, 0, len(comms))
        q_ref, k_ref, v_ref, be_ref, g_ref, s_ref, ds_ref, do_ref = ins
        dq_ref, dk_ref, dv_ref, dbe_ref, dg_ref = outs
        _host_exchanges(comm_refs, comms, pl.program_id(0) == 0, pl.program_id(0) == n - 1)
        states_v = s_ref[...]
        _, vf = jax.vjp(lambda *a: gdn_chunks_f(*a, states_v), q_ref[...], k_ref[...], v_ref[...], be_ref[...], g_ref[...])
        dq_ref[...], dk_ref[...], dv_ref[...], dbe_ref[...], dg_ref[...] = vf((do_ref[...], ds_ref[...]))

    st = pl.BlockSpec((nb, 512, DK), lambda i: (i, 0, 0))
    in_specs = [_row_spec(tb, 512, 0)] * 5 + [st, st, _row_spec(tb, 512, 0)]
    out_specs = [_row_spec(tb, 512, 0)] * 5
    out_shape = [jax.ShapeDtypeStruct((t, 512), f32)] * 5
    scratch, args = [], [q, k, v, beta, g, states, dstates, do]
    _add_hosted(comms, in_specs, out_specs, out_shape, scratch, args)
    return pl.pallas_call(body, name=name, grid=(n,), in_specs=in_specs, out_specs=out_specs, out_shape=out_shape,
                          scratch_shapes=scratch, compiler_params=_cparams(("arbitrary",) if comms else ("parallel",)))(*args)


NEG = -1e30


LOG2E = 1.4426950408889634
ATT_C2 = ATT_SCALE * LOG2E
ATT_RB = 256


def _add_hosted(comms, in_specs, out_specs, out_shape, scratch, args):
    for src, scatter in comms:
        in_specs.append(_ANY)
        out_specs.append(_ANY)
        out_shape.append(_comm_out_shape(src, scatter))
        scratch += _comm_scratch()
        args.append(src)


def _split_hosted(refs, n_in, n_out, n_scratch, n_comm):
    ins, srcs = refs[:n_in], refs[n_in:n_in + n_comm]
    o0 = n_in + n_comm
    outs, dsts = refs[o0:o0 + n_out], refs[o0 + n_out:o0 + n_out + n_comm]
    rest = refs[o0 + n_out + n_comm:]
    sems = rest[n_scratch:]
    return ins, outs, rest[:n_scratch], [(srcs[c], dsts[c]) + tuple(sems[3 * c:3 * c + 3]) for c in range(n_comm)]


def _host_exchanges(comm_refs, comms, is_first, is_last):
    if not comms:
        return

    @pl.when(is_first)
    def _():
        for refs, (_, scatter) in zip(comm_refs, comms):
            for op in _exchange_copies(*refs, scatter):
                op.start()

    @pl.when(is_last)
    def _():
        for refs, (_, scatter) in zip(comm_refs, comms):
            for op in _exchange_copies(*refs, scatter):
                op.wait()


def attn_fwd(q, k, v, name="attn_fwd", comms=()):
    t = q.shape[0]
    tq = tk = min(1024, t)
    rb = min(ATT_RB, tq)
    nq, nk, nr = t // tq, t // tk, tq // rb
    last = lambda i: i

    def body(*refs):
        (q_ref, k_ref, v_ref), (o_ref, lse_ref), (m_s, acc), comm_refs = _split_hosted(refs, 3, 2, 2, len(comms))
        h, i, j = pl.program_id(0), pl.program_id(1), pl.program_id(2)
        _host_exchanges(comm_refs, comms, (h == 0) & (i == 0) & (j == 0), (h == H - 1) & (i == nq - 1) & (j == nk - 1))

        @pl.when(j == 0)
        def _():
            m_s[...] = jnp.full_like(m_s, NEG)
            acc[...] = jnp.zeros_like(acc)

        one_col = jnp.where(lax.broadcasted_iota(jnp.int32, (tk, DK), 1) == 0, 1.0, 0.0).astype(MX)
        vx = jnp.concatenate([v_ref[...], one_col], axis=1)

        def rows_of(r):
            return pl.ds(r * rb, rb)

        def soft(r, s):
            m_old = m_s[rows_of(r), :]
            m_new = jnp.maximum(m_old, jnp.max(s, axis=-1, keepdims=True))
            m_s[rows_of(r), :] = m_new
            return jnp.exp2(s - m_new).astype(MX), jnp.exp2(m_old - m_new)

        def pv(r, p, alpha, vals):
            acc[rows_of(r), :] = alpha * acc[rows_of(r), :] + _dot(p, vals, 1, 0)

        def whole_block():
            kb = k_ref[...]
            ss = [_dot(q_ref[rows_of(r), :], kb, 1, 1) for r in range(min(2, nr))]
            pend = None
            for r in range(nr):
                if r + 2 < nr:
                    ss.append(_dot(q_ref[rows_of(r + 2), :], kb, 1, 1))
                p, alpha = soft(r, ss[r])
                if pend is not None:
                    pv(*pend, vx)
                pend = (r, p, alpha)
            pv(*pend, vx)

        def diagonal_block():
            for r in range(nr):
                cols = (r + 1) * rb
                s = _dot(q_ref[rows_of(r), :], k_ref[pl.ds(0, cols), :], 1, 1)
                qrow = lax.broadcasted_iota(jnp.int32, (rb, cols), 0) + r * rb
                kcol = lax.broadcasted_iota(jnp.int32, (rb, cols), 1)
                s = jnp.where(jnp.right_shift(kcol, 6) <= jnp.right_shift(qrow, 6), s, NEG)
                p, alpha = soft(r, s)
                pv(r, p, alpha, vx[:cols])

        pl.when(j < i)(whole_block)
        pl.when(j == i)(diagonal_block)

        @pl.when(j == nk - 1)
        def _():
            a = acc[...]
            l = a[:, DK:DK + 1]
            o_ref[...] = a[:, :DK] / l
            lse_ref[0] = m_s[...] + jnp.log2(l)

    in_specs = [pl.BlockSpec((tq, 256), lambda h, i, j: (i, h)),
                pl.BlockSpec((tk, 256), lambda h, i, j: (jnp.minimum(j, last(i)), h)),
                pl.BlockSpec((tk, DK), lambda h, i, j: (jnp.minimum(j, last(i)), h))]
    out_specs = [pl.BlockSpec((tq, DK), lambda h, i, j: (i, h)), pl.BlockSpec((1, tq, 1), lambda h, i, j: (h, i, 0))]
    out_shape = [jax.ShapeDtypeStruct((t, 512), f32), jax.ShapeDtypeStruct((H, t, 1), f32)]
    scratch = [pltpu.VMEM((tq, 1), f32), pltpu.VMEM((tq, 2 * DK), f32)]
    args = [q, k, v]
    _add_hosted(comms, in_specs, out_specs, out_shape, scratch, args)
    sem = ("arbitrary",) * 3 if comms else ("parallel", "parallel", "arbitrary")
    return pl.pallas_call(body, name=name, grid=(H, nq, nk), in_specs=in_specs, out_specs=out_specs, out_shape=out_shape,
                          scratch_shapes=scratch, compiler_params=_cparams(sem))(*args)


def attn_bwd(q, k, v, o, lse, do, name="attn_bwd", comms=()):
    t = q.shape[0]
    tq = tk = min(1024, t)
    rb = min(ATT_RB, tq)
    nq, nk, nr = t // tq, t // tk, tq // rb
    first = lambda j: j

    def body(*refs):
        ins, outs, (dk_acc, dv_acc), comm_refs = _split_hosted(refs, 6, 3, 2, len(comms))
        q_ref, k_ref, v_ref, o_ref, lse_ref, do_ref = ins
        dq_ref, dk_ref, dv_ref = outs
        h, j, i = pl.program_id(0), pl.program_id(1), pl.program_id(2)
        _host_exchanges(comm_refs, comms, (h == 0) & (i == 0) & (j == 0), (h == H - 1) & (i == nq - 1) & (j == nk - 1))

        @pl.when((j == 0) & (i == 0))
        def _():
            dq_ref[...] = jnp.zeros_like(dq_ref)

        @pl.when(i == 0)
        def _():
            dk_acc[...] = jnp.zeros_like(dk_acc)
            dv_acc[...] = jnp.zeros_like(dv_acc)

        def rows_of(r):
            return pl.ds(r * rb, rb)

        def front(r, cols):
            qb, dob = q_ref[rows_of(r), :], do_ref[rows_of(r), :]
            return qb, dob, _dot(qb, k_ref[pl.ds(0, cols), :], 1, 1), _dot(dob.astype(MX), v_ref[pl.ds(0, cols), :], 1, 1)

        def middle(r, dob, s, dp):
            p = jnp.exp2(s - lse_ref[0, rows_of(r), :])
            delta = jnp.sum(dob * o_ref[rows_of(r), :], axis=-1, keepdims=True)
            return p.astype(MX), (p * (dp - delta)).astype(MX)

        def back(r, cols, qb, dob, pb, ds):
            dv_acc[pl.ds(0, cols), :] += _dot(pb, dob.astype(MX), 0, 0)
            dk_acc[pl.ds(0, cols), :] += _dot(ds, qb, 0, 0) * (1.0 / LOG2E)
            grow = pl.ds(pl.multiple_of(i * tq + r * rb, rb), rb)
            dq_ref[grow, :] += _dot(ds, k_ref[pl.ds(0, cols), :], 1, 0) * (1.0 / LOG2E)

        def whole_block():
            nxt = front(0, tk)
            for r in range(nr):
                qb, dob, s, dp = nxt
                if r + 1 < nr:
                    nxt = front(r + 1, tk)
                back(r, tk, qb, dob, *middle(r, dob, s, dp))

        def diagonal_block():
            for r in range(nr):
                cols = (r + 1) * rb
                qb, dob, s, dp = front(r, cols)
                qrow = lax.broadcasted_iota(jnp.int32, (rb, cols), 0) + r * rb
                kcol = lax.broadcasted_iota(jnp.int32, (rb, cols), 1)
                s = jnp.where(jnp.right_shift(kcol, 6) <= jnp.right_shift(qrow, 6), s, NEG)
                back(r, cols, qb, dob, *middle(r, dob, s, dp))

        pl.when(i > j)(whole_block)
        pl.when(i == j)(diagonal_block)

        @pl.when(i == nq - 1)
        def _():
            dk_ref[...] = dk_acc[...]
            dv_ref[...] = dv_acc[...]

    qi = lambda h, j, i: (jnp.maximum(i, first(j)), h)
    in_specs = [pl.BlockSpec((tq, 256), qi),
                pl.BlockSpec((tk, 256), lambda h, j, i: (j, h)),
                pl.BlockSpec((tk, DK), lambda h, j, i: (j, h)),
                pl.BlockSpec((tq, DK), qi),
                pl.BlockSpec((1, tq, 1), lambda h, j, i: (h, jnp.maximum(i, first(j)), 0)),
                pl.BlockSpec((tq, DK), qi)]
    out_specs = [pl.BlockSpec((t, 256), lambda h, j, i: (0, h)),
                 pl.BlockSpec((tk, 256), lambda h, j, i: (j, h)),
                 pl.BlockSpec((tk, DK), lambda h, j, i: (j, h))]
    out_shape = [jax.ShapeDtypeStruct((t, 1024), f32), jax.ShapeDtypeStruct((t, 1024), f32), jax.ShapeDtypeStruct((t, 512), f32)]
    scratch = [pltpu.VMEM((tk, 256), f32), pltpu.VMEM((tk, DK), f32)]
    args = [q, k, v, o, lse, do]
    _add_hosted(comms, in_specs, out_specs, out_shape, scratch, args)
    return pl.pallas_call(body, name=name, grid=(H, nk, nq), in_specs=in_specs, out_specs=out_specs, out_shape=out_shape,
                          scratch_shapes=scratch, compiler_params=_cparams(("arbitrary",) * 3))(*args)


def ffn_up(x1, wg, wu):
    t = x1.shape[0]
    tm, tn = min(512, t), _pick(DFF, 1536)

    def body(x_ref, wg_ref, wu_ref, g_ref, u_ref, a_ref):
        xb = x_ref[...].astype(MX)
        g = _dot(xb, wg_ref[...], 1, 0)
        u = _dot(xb, wu_ref[...], 1, 0)
        g_ref[...] = g.astype(g_ref.dtype)
        u_ref[...] = u.astype(u_ref.dtype)
        a_ref[...] = (g * jax.nn.sigmoid(g) * u).astype(a_ref.dtype)

    w_spec = pl.BlockSpec((D, tn), lambda i, j: (0, j))
    o_spec = pl.BlockSpec((tm, tn), lambda i, j: (i, j))
    return pl.pallas_call(
        body, name="ffn_up", grid=(t // tm, DFF // tn),
        in_specs=[pl.BlockSpec((tm, D), lambda i, j: (i, 0)), w_spec, w_spec],
        out_specs=[o_spec] * 3, out_shape=[jax.ShapeDtypeStruct((t, DFF), bf16)] * 3,
        compiler_params=_cparams(("parallel", "parallel")),
    )(x1, wg, wu)


def ffn_dact(ddown, wd, g, u):
    t = ddown.shape[0]
    tm, tn = min(512, t), _pick(DFF, 1536)

    def body(dd_ref, wd_ref, g_ref, u_ref, dg_ref, du_ref):
        dact = _dot(dd_ref[...].astype(MX), wd_ref[...], 1, 1)
        gv, uv = g_ref[...].astype(f32), u_ref[...].astype(f32)
        sig = jax.nn.sigmoid(gv)
        dg_ref[...] = (dact * uv * sig * (1.0 + gv * (1.0 - sig))).astype(dg_ref.dtype)
        du_ref[...] = (dact * gv * sig).astype(du_ref.dtype)

    o_spec = pl.BlockSpec((tm, tn), lambda i, j: (i, j))
    return pl.pallas_call(
        body, name="ffn_dact", grid=(t // tm, DFF // tn),
        in_specs=[pl.BlockSpec((tm, D), lambda i, j: (i, 0)), pl.BlockSpec((tn, D), lambda i, j: (j, 0)), o_spec, o_spec],
        out_specs=[o_spec] * 2, out_shape=[jax.ShapeDtypeStruct((t, DFF), bf16)] * 2,
        compiler_params=_cparams(("parallel", "parallel")),
    )(ddown, wd, g, u)


def loss_head(y, target):
    t = y.shape[0]
    tb = min(ROW_TB, t)
    n = t // tb

    def body(y_ref, t_ref, dy_ref, loss_ref, acc):
        i = pl.program_id(0)

        @pl.when(i == 0)
        def _():
            acc[...] = jnp.zeros_like(acc)

        e = y_ref[...] - t_ref[...]
        dy_ref[...] = e * (1.0 / D)
        acc[...] += jnp.sum(e * e, axis=0, keepdims=True)

        @pl.when(i == n - 1)
        def _():
            loss_ref[...] = jnp.sum(acc[...], axis=1, keepdims=True) * (0.5 / D)

    return pl.pallas_call(
        body, name="loss_head", grid=(n,),
        in_specs=[_row_spec(tb, D, 0)] * 2,
        out_specs=[_row_spec(tb, D, 0), _full_spec((1, 1))],
        out_shape=[jax.ShapeDtypeStruct((t, D), f32), jax.ShapeDtypeStruct((1, 1), f32)],
        scratch_shapes=[pltpu.VMEM((1, D), f32)],
        compiler_params=_cparams(("arbitrary",)),
    )(y, target)


def _me_and_peers():
    x, y, c = lax.axis_index("x"), lax.axis_index("y"), lax.axis_index("c")
    me = 4 * x + 2 * y + c
    peers = []
    for kk in range(1, N_DEV):
        px = 1 - x if kk & 4 else x
        py = 1 - y if kk & 2 else y
        pc = 1 - c if kk & 1 else c
        peers.append(((px, py, pc), 4 * px + 2 * py + pc))
    return me, peers


_ANY = pl.BlockSpec(memory_space=pl.ANY)


def _comm_scratch():
    return [pltpu.SemaphoreType.DMA((N_DEV - 1,)), pltpu.SemaphoreType.DMA((N_DEV - 1,)), pltpu.SemaphoreType.DMA]


def _exchange_copies(src_ref, out_ref, send_sems, recv_sems, local_sem, scatter):
    me, peers = _me_and_peers()
    pick = (lambda d: src_ref.at[d]) if scatter else (lambda d: src_ref)
    ops = [pltpu.make_async_copy(pick(me), out_ref.at[me], local_sem)]
    ops += [pltpu.make_async_remote_copy(src_ref=pick(pid), dst_ref=out_ref.at[me], send_sem=send_sems.at[kk],
                                         recv_sem=recv_sems.at[kk], device_id=dev, device_id_type=pl.DeviceIdType.MESH)
            for kk, (dev, pid) in enumerate(peers)]
    return ops


def _comm_out_shape(src, scatter):
    return jax.ShapeDtypeStruct(src.shape if scatter else (N_DEV,) + src.shape, src.dtype)


def exchange(name, src, scatter):
    def body(s_ref, out_ref, send_sems, recv_sems, local_sem):
        ops = _exchange_copies(s_ref, out_ref, send_sems, recv_sems, local_sem, scatter)
        for op in ops:
            op.start()
        for op in ops:
            op.wait()

    return pl.pallas_call(body, name=name, in_specs=[_ANY], out_specs=_ANY, out_shape=_comm_out_shape(src, scatter),
                          scratch_shapes=_comm_scratch())(src)


def sum_parts(parts):
    r = parts.shape[1]
    tb = _pick_rows(r)

    def body(p_ref, g_ref):
        g = p_ref[0]
        for s in range(1, N_DEV):
            g = g + p_ref[s]
        g_ref[...] = g

    return pl.pallas_call(
        body, name="sum_grad_parts", grid=(r // tb,),
        in_specs=[pl.BlockSpec((N_DEV, tb, D), lambda i: (0, i, 0))],
        out_specs=pl.BlockSpec((tb, D), lambda i: (i, 0)), out_shape=jax.ShapeDtypeStruct((r, D), f32),
        compiler_params=_cparams(("parallel",)),
    )(parts)


def adamw(w, m, v, g):
    r, c = w.shape
    tb = _pick_rows(r)

    def body(w_ref, m_ref, v_ref, g_ref, d_ref, nm_ref, nv_ref):
        g = g_ref[...]
        nm = ADAM_B1 * m_ref[...] + (1.0 - ADAM_B1) * g
        nv = ADAM_B2 * v_ref[...] + (1.0 - ADAM_B2) * jnp.square(g)
        m_hat = nm / (1.0 - ADAM_B1 ** ADAM_STEP)
        v_hat = nv / (1.0 - ADAM_B2 ** ADAM_STEP)
        d_ref[...] = -ADAM_LR * (m_hat / (jnp.sqrt(v_hat) + ADAM_EPS) + ADAM_WD * w_ref[...])
        nm_ref[...] = nm
        nv_ref[...] = nv

    spec = pl.BlockSpec((tb, c), lambda i: (i, 0))
    return pl.pallas_call(
        body, name="adamw", grid=(r // tb,), in_specs=[spec] * 4, out_specs=[spec] * 3,
        out_shape=[jax.ShapeDtypeStruct((r, c), f32)] * 3, compiler_params=_cparams(("parallel",)),
    )(w, m, v, g)


def _pick_rows(r):
    return max(tb for tb in range(8, ROW_TB + 1, 8) if r % tb == 0)


BIG = ("w_in", "w_uq", "w_ukv", "w_out", "w_gate_up", "w_down", "w_ple", "w_ple_gate")
COL_SHARDED = ("w_in", "w_uq", "w_ukv", "w_gate_up", "w_ple")
SMALL = ("conv_w", "a_log", "dt_bias", "gdn_norm_g", "q_norm_g", "kv_norm_g", "ln1_g", "ln1_b", "ln2_g", "ln2_b")
FULL_SHAPE = {"w_in": (D, IN_W), "w_uq": (QL, 768), "w_ukv": (KVL, 1024), "w_out": (D, D), "w_gate_up": (D, 2 * DFF),
              "w_down": (DFF, D), "w_ple": (PLE, D), "w_ple_gate": (D, D)}
ROW_ALIGN = 16


def _pad_rows(a, mult=ROW_ALIGN, axis=0):
    pad = (-a.shape[axis]) % mult
    widths = [(0, 0)] * a.ndim
    widths[axis] = (0, pad)
    return a if pad == 0 else jnp.pad(a, widths)


GATHER_EARLY = ("w_in", "w_uq", "w_ukv")
GATHER_LATE = ("w_out", "w_gate_up", "w_down", "w_ple", "w_ple_gate")
GRAD_EARLY = ("w_gate_up", "w_down", "w_ple", "w_ple_gate")
GRAD_LATE = ("w_in", "w_uq", "w_ukv", "w_out")


def _pack_layer(tree, l, dtype, names):
    return jnp.concatenate([_pad_rows(tree[n][l].reshape(-1, D).astype(dtype)) for n in names], axis=0)


def _big_rows(shards, names):
    rows, off = {}, 0
    for n in names:
        r = shards[n][1] * shards[n][2] // D
        rows[n] = (off, r)
        off += r + (-r) % ROW_ALIGN
    return rows, off


def _unpack_gathered(gathered, shards, names, n):
    off, r = _big_rows(shards, names)[0][n]
    _, sr, sc = shards[n]
    blk = gathered[:, off:off + r, :].reshape(N_DEV, sr, sc)
    if n in COL_SHARDED:
        return jnp.transpose(blk, (1, 0, 2)).reshape(sr, N_DEV * sc)
    return blk.reshape(N_DEV * sr, sc)


def _shard_grad(gfull, n, shards):
    _, sr, sc = shards[n]
    if n in COL_SHARDED:
        blk = jnp.transpose(gfull.reshape(sr, N_DEV, sc), (1, 0, 2))
    else:
        blk = gfull.reshape(N_DEV, sr, sc)
    return _pad_rows(blk.reshape(N_DEV, sr * sc // D, D), axis=1)


def _pack_small(tree, extra):
    flat = jnp.concatenate([tree[n].reshape(-1).astype(f32) for n in SMALL] + [extra.reshape(-1).astype(f32)])
    return jnp.pad(flat, (0, (-flat.shape[0]) % (ROW_ALIGN * D))).reshape(-1, D)


def _unpack_small(rows, like):
    flat = rows.reshape(-1)
    out, off = {}, 0
    for n in SMALL:
        sz = int(np.prod(like[n].shape))
        out[n] = flat[off:off + sz].reshape(like[n].shape)
        off += sz
    return out, flat[off]


def _zeros(r, c, dt):
    return jnp.zeros((r, c), dt)


def _prep_w_in(w):
    dt = w.dtype
    kr = w[:, 2696:2760]
    return jnp.concatenate([
        w[:, 0:2048],
        w[:, 2440:2696],
        kr, w[:, 2048:2056], _zeros(D, 56, dt),
        kr[:, 32:], kr[:, :32], _zeros(D, 64, dt),
        w[:, 2056:2440], _zeros(D, CQP - QL, dt)], axis=1)


def _unprep_w_in(g):
    krs = g[:, OFF_KRS:OFF_KRS + 64]
    kr = g[:, OFF_KR:OFF_KR + 64] + jnp.concatenate([krs[:, 32:], krs[:, :32]], axis=1)
    return jnp.concatenate([g[:, 0:2048], g[:, OFF_KR + 64:OFF_KR + 72], g[:, OFF_CQ:OFF_CQ + QL],
                            g[:, OFF_CKV:OFF_CKV + KVL], kr], axis=1)


def _prep_w_uq(w):
    dt = w.dtype
    z64 = _zeros(QL, 64, dt)
    nope, ra, rb = [], [], []
    for h in range(H):
        nope.append(w[:, 192 * h:192 * h + 128])
        x1, x2 = w[:, 192 * h + 128:192 * h + 160], w[:, 192 * h + 160:192 * h + 192]
        ra += [x1, x2, z64]
        rb += [x2, x1, z64]
    return jnp.concatenate([jnp.concatenate(nope + ra + rb, axis=1), _zeros(CQP - QL, 1536, dt)], axis=0)


def _unprep_w_uq(g):
    g = g[:QL]
    cols = []
    for h in range(H):
        a = g[:, 512 + 128 * h:512 + 128 * h + 64]
        b = g[:, 1024 + 128 * h:1024 + 128 * h + 64]
        cols += [g[:, 128 * h:128 * (h + 1)], a[:, :32] + b[:, 32:], a[:, 32:] + b[:, :32]]
    return jnp.concatenate(cols, axis=1)


def _rope_tables(positions):
    inv_freq = ROPE_THETA ** (-jnp.arange(0, ROPE, 2, dtype=f32) / ROPE)
    ang = positions.astype(f32)[:, None] * inv_freq
    c, s = jnp.cos(ang), jnp.sin(ang)
    z = jnp.zeros((positions.shape[0], 64), f32)
    return jnp.concatenate([c, c, z], axis=1), jnp.concatenate([-s, s, z], axis=1)


def _tile_heads(vec, n=H):
    return jnp.tile(vec.reshape(1, -1), (1, n))


def _bcast_heads(vec):
    return jnp.repeat(vec, DK).reshape(1, H * DK)


def kernel(x, p, positions, w_in, conv_w, a_log, dt_bias, gdn_norm_g, q_norm_g, w_uq, kv_norm_g, w_ukv, w_out, ln1_g, ln1_b, w_gate_up, w_down, ln2_g, ln2_b, w_ple, w_ple_gate, loss_target, m_w_in, m_conv_w, m_a_log, m_dt_bias, m_gdn_norm_g, m_q_norm_g, m_w_uq, m_kv_norm_g, m_w_ukv, m_w_out, m_ln1_g, m_ln1_b, m_w_gate_up, m_w_down, m_ln2_g, m_ln2_b, m_w_ple, m_w_ple_gate, v_w_in, v_conv_w, v_a_log, v_dt_bias, v_gdn_norm_g, v_q_norm_g, v_w_uq, v_kv_norm_g, v_w_ukv, v_w_out, v_ln1_g, v_ln1_b, v_w_gate_up, v_w_down, v_ln2_g, v_ln2_b, v_w_ple, v_w_ple_gate):
    W = dict(w_in=w_in, conv_w=conv_w, a_log=a_log, dt_bias=dt_bias, gdn_norm_g=gdn_norm_g, q_norm_g=q_norm_g, w_uq=w_uq,
             kv_norm_g=kv_norm_g, w_ukv=w_ukv, w_out=w_out, ln1_g=ln1_g, ln1_b=ln1_b, w_gate_up=w_gate_up, w_down=w_down,
             ln2_g=ln2_g, ln2_b=ln2_b, w_ple=w_ple, w_ple_gate=w_ple_gate)
    M = dict(w_in=m_w_in, conv_w=m_conv_w, a_log=m_a_log, dt_bias=m_dt_bias, gdn_norm_g=m_gdn_norm_g, q_norm_g=m_q_norm_g,
             w_uq=m_w_uq, kv_norm_g=m_kv_norm_g, w_ukv=m_w_ukv, w_out=m_w_out, ln1_g=m_ln1_g, ln1_b=m_ln1_b,
             w_gate_up=m_w_gate_up, w_down=m_w_down, ln2_g=m_ln2_g, ln2_b=m_ln2_b, w_ple=m_w_ple, w_ple_gate=m_w_ple_gate)
    V = dict(w_in=v_w_in, conv_w=v_conv_w, a_log=v_a_log, dt_bias=v_dt_bias, gdn_norm_g=v_gdn_norm_g, q_norm_g=v_q_norm_g,
             w_uq=v_w_uq, kv_norm_g=v_kv_norm_g, w_ukv=v_w_ukv, w_out=v_w_out, ln1_g=v_ln1_g, ln1_b=v_ln1_b,
             w_gate_up=v_w_gate_up, w_down=v_w_down, ln2_g=v_ln2_g, ln2_b=v_ln2_b, w_ple=v_w_ple, w_ple_gate=v_w_ple_gate)
    shards = {n: W[n].shape for n in BIG}
    t = x.shape[1]
    xin = x.reshape(t, D)
    target = loss_target.reshape(t, D)
    cf, sf = _rope_tables(positions.reshape(t))

    early = [exchange("all_gather_first", _pack_layer(W, 0, bf16, GATHER_EARLY), False)] + [None] * (DEPTH - 1)
    conv_g = exchange("all_gather_conv", _pad_rows(jnp.pad(conv_w.reshape(-1), (0, (-conv_w.size) % D)).reshape(-1, D), 8), False)
    csz = conv_w.shape[1] * conv_w.shape[2]
    conv_full = jnp.transpose(conv_g.reshape(N_DEV, -1)[:, :DEPTH * csz].reshape(N_DEV, DEPTH, CW, -1), (1, 2, 0, 3)).reshape(DEPTH, CW, 1536)

    acts = []
    h_cur = xin
    for l in range(DEPTH):
        full = functools.partial(_unpack_gathered, early[l], shards, GATHER_EARLY)
        win = _prep_w_in(full("w_in"))
        wq = _prep_w_uq(full("w_uq"))
        wkv = full("w_ukv")
        cw8 = jnp.concatenate([conv_full[l], jnp.zeros((8 - CW, 1536), f32)], axis=0)
        alog_b, dtb_b = _bcast_heads(a_log[l]), _bcast_heads(dt_bias[l])
        gg_b = _tile_heads(gdn_norm_g[l])
        qg = jnp.concatenate([q_norm_g[l], jnp.zeros((CQP - QL,), f32)]).reshape(1, CQP)
        kvg = kv_norm_g[l].reshape(1, KVL)
        g1, b1, g2, b2 = (a[l].reshape(1, D) for a in (ln1_g, ln1_b, ln2_g, ln2_b))
        p_l = p[l].reshape(t, PLE)

        hb = mm("in_proj", h_cur, win, mode="nn", out_dtype=f32)
        qn, kn, vs, beta, glog = gdn_pre_fwd(hb, cw8, alog_b, dtb_b)
        g_u, g_w, g_qd, g_kd, g_at, g_el = gdn_intra(qn, kn, vs, beta, glog)
        o_gdn, states = gdn_scan_fwd(g_u, g_w, g_qd, g_kd, g_at, g_el)
        mla_rows = [(hb, CQP, OFF_CQ // CQP), (hb, KVL, OFF_CKV // KVL), (hb, LANE, OFF_KR // LANE), (hb, LANE, OFF_KRS // LANE),
                    (cf, LANE, 0), (sf, LANE, 0)]
        mla_consts = [qg, wq, kvg, wkv]
        qm, km, vm = row_fwd("mla_pre", mla_pre_f, mla_rows, mla_consts, [(1024, bf16), (1024, bf16), (512, bf16)])
        comms = [(_pack_layer(W, l, bf16, GATHER_LATE), False)]
        if l + 1 < DEPTH:
            comms.append((_pack_layer(W, l + 1, bf16, GATHER_EARLY), False))
        res = attn_fwd(qm, km, vm, name=f"attn_fwd_gather{len(comms)}", comms=comms)
        o_mla, lse, late = res[:3]
        if l + 1 < DEPTH:
            early[l + 1] = res[3]
        full = functools.partial(_unpack_gathered, late, shards, GATHER_LATE)
        wout, wgu, wd = full("w_out"), full("w_gate_up"), full("w_down")
        wg, wu = wgu[:, :DFF], wgu[:, DFF:]
        wple, wpg = full("w_ple"), full("w_ple_gate")
        p1_rows = [(o_gdn, 512, 0), (hb, 512, OFF_Z // 512), (o_mla, 512, 0), (h_cur, D, 0)]
        p1_consts = [gg_b, wout, g1, b1]
        (x1,) = row_fwd("post1", post1_f, p1_rows, p1_consts, [(D, f32)])
        gate, up, act = ffn_up(x1, wg, wu)
        down = mm("ffn_down", act, wd, mode="nn", out_dtype=f32)
        p2_rows = [(x1, D, 0), (down, D, 0), (p_l, PLE, 0)]
        p2_consts = [g2, b2, wpg, wple]
        (y,) = row_fwd("post2", post2_f, p2_rows, p2_consts, [(D, f32)])
        acts.append(dict(x=h_cur, hb=hb, qn=qn, kn=kn, vs=vs, beta=beta, glog=glog, states=states, o_gdn=o_gdn, qm=qm, km=km,
                         scan=(g_w, g_qd, g_kd, g_at, g_el),
                         vm=vm, o_mla=o_mla, lse=lse, x1=x1, gate=gate, up=up, act=act, win=win, wd=wd, wg=wg, wu=wu,
                         cw8=cw8, alog_b=alog_b, dtb_b=dtb_b, mla_rows=mla_rows, mla_consts=mla_consts, p1_rows=p1_rows,
                         p1_consts=p1_consts, p2_rows=p2_rows, p2_consts=p2_consts))
        h_cur = y

    dy, loss_part = loss_head(h_cur, target)

    G = {n: [None] * DEPTH for n in list(BIG) + list(SMALL)}
    parts_early, parts_late = [None] * DEPTH, [None] * DEPTH
    pack_grads = lambda l, names: jnp.concatenate([_shard_grad(G[n][l], n, shards) for n in names], axis=1)
    pending = None
    for l in reversed(range(DEPTH)):
        a = acts[l]
        dx1, ddown, dg2, db2, dwpg, dwple = row_bwd("post2_bwd", post2_f, a["p2_rows"], a["p2_consts"], [dy], [0, 1], [0, 1, 2, 3], [f32, f32])
        dgate, dup = ffn_dact(ddown, a["wd"], a["gate"], a["up"])
        dwd = mm("dw_down", a["act"], ddown, mode="tn", out_dtype=f32)
        dx1 = mm("dx_gate", dgate, a["wg"], mode="nt", out_dtype=f32, add=dx1)
        dx1 = mm("dx_up", dup, a["wu"], mode="nt", out_dtype=f32, add=dx1)
        dwg = mm("dw_gate", a["x1"], dgate, mode="tn", out_dtype=f32)
        dwu = mm("dw_up", a["x1"], dup, mode="tn", out_dtype=f32)
        do_gdn, dz, do_mla, dxr, dgg, dwout, dg1, db1 = row_bwd("post1_bwd", post1_f, a["p1_rows"], a["p1_consts"], [dx1], [0, 1, 2, 3],
                                                              [0, 1, 2, 3], [f32, bf16, f32, f32])
        G["w_down"][l], G["w_gate_up"][l] = dwd, jnp.concatenate([dwg, dwu], axis=1)
        G["w_ple"][l], G["w_ple_gate"][l] = dwple, dwpg
        dqm, dkm, dvm, parts_early[l] = attn_bwd(a["qm"], a["km"], a["vm"], a["o_mla"], a["lse"], do_mla, name="attn_bwd_exchange",
                                                 comms=[(pack_grads(l, GRAD_EARLY), True)])
        dstates = gdn_scan_bwd(do_gdn, *a["scan"])
        if pending is not None:
            dqn, dkn, dvs, dbeta, dglog, parts_late[pending[0]] = gdn_local_bwd(
                a["qn"], a["kn"], a["vs"], a["beta"], a["glog"], a["states"], dstates, do_gdn, name="gdn_local_bwd_exchange",
                comms=[(pending[1], True)])
        else:
            dqn, dkn, dvs, dbeta, dglog = gdn_local_bwd(a["qn"], a["kn"], a["vs"], a["beta"], a["glog"], a["states"], dstates, do_gdn)
        dqkv, dba, dcw, dal, ddt = gdn_pre_bwd(a["hb"], a["cw8"], a["alog_b"], a["dtb_b"], dqn, dkn, dvs, dbeta, dglog)
        dcq, dckv, dkrg, dkrs, dqg, dwq, dkvg, dwkv = row_bwd("mla_pre_bwd", mla_pre_f, a["mla_rows"], a["mla_consts"], [dqm, dkm, dvm],
                                                              [0, 1, 2, 3], [0, 1, 2, 3], [bf16, bf16, bf16, bf16], row_add={2: dba})
        dh = jnp.concatenate([dqkv, dz, dckv, dkrg, dkrs, dcq], axis=1)
        dwin = mm("dw_in", a["x"], dh, mode="tn", out_dtype=f32)
        dy = mm("dx_in", dh, a["win"], mode="nt", out_dtype=f32, add=dxr)

        G["w_in"][l] = _unprep_w_in(dwin)
        G["w_uq"][l] = _unprep_w_uq(dwq)
        G["w_ukv"][l], G["w_out"][l] = dwkv, dwout
        G["conv_w"][l] = dcw[:CW]
        G["a_log"][l] = jnp.sum(dal.reshape(H, DK), axis=1)
        G["dt_bias"][l] = jnp.sum(ddt.reshape(H, DK), axis=1)
        G["gdn_norm_g"][l] = jnp.sum(dgg.reshape(H, DK), axis=0)
        G["q_norm_g"][l] = dqg[0, :QL]
        G["kv_norm_g"][l] = dkvg[0]
        G["ln1_g"][l], G["ln1_b"][l], G["ln2_g"][l], G["ln2_b"][l] = dg1[0], db1[0], dg2[0], db2[0]
        pending = (l, pack_grads(l, GRAD_LATE))
    grad_x = dy.reshape(x.shape)

    small_like = {n: W[n] for n in SMALL}
    conv_parts = jnp.stack(G["conv_w"]).reshape(DEPTH, CW, N_DEV, -1)
    smalls = []
    for d in range(N_DEV):
        tree = {n: jnp.stack(G[n]) for n in SMALL if n != "conv_w"}
        tree["conv_w"] = conv_parts[:, :, d, :]
        smalls.append(_pack_small(tree, loss_part))
    small_send = jnp.stack(smalls)
    n_late = _big_rows(shards, GRAD_LATE)[1]
    n_small = small_send.shape[1]
    tail = (-(n_late + n_small)) % (3 * ROW_ALIGN)
    parts_late[0] = exchange("exchange_last", jnp.concatenate([pending[1], small_send, jnp.zeros((N_DEV, tail, D), f32)], axis=1), True)

    g_out, d_out, m_out, v_out = {}, {}, {}, {}
    for names, parts in ((GRAD_EARLY, parts_early), (GRAD_LATE, parts_late)):
        g_rows = [sum_parts(parts[l]) for l in range(DEPTH)]
        for n in names:
            off, r = _big_rows(shards, names)[0][n]
            shape2 = (DEPTH * shards[n][1], shards[n][2])
            g_out[n] = jnp.stack([g_rows[l][off:off + r].reshape(shards[n][1:]) for l in range(DEPTH)])
            upd = adamw(W[n].reshape(shape2), M[n].reshape(shape2), V[n].reshape(shape2), g_out[n].reshape(shape2))
            d_out[n], m_out[n], v_out[n] = (u.reshape(shards[n]) for u in upd)
    zero = jnp.zeros((), f32)
    g_small = g_rows[0][n_late:n_late + n_small]
    upd = adamw(_pack_small(W, zero), _pack_small(M, zero), _pack_small(V, zero), g_small)
    for dst, rows_arr in zip((g_out, d_out, m_out, v_out), (g_small,) + tuple(upd)):
        dst.update(_unpack_small(rows_arr, small_like)[0])
    loss = _unpack_small(g_small, small_like)[1]
    order = ["w_in", "conv_w", "a_log", "dt_bias", "gdn_norm_g", "q_norm_g", "w_uq", "kv_norm_g", "w_ukv", "w_out", "ln1_g", "ln1_b",
             "w_gate_up", "w_down", "ln2_g", "ln2_b", "w_ple", "w_ple_gate"]
    return (loss, grad_x, *[g_out[n] for n in order], *[d_out[n] for n in order], *[m_out[n] for n in order],
            *[v_out[n] for n in order])
```

```python
import jax
import jax.numpy as jnp
import numpy as np
from jax import lax
from jax.experimental import pallas as pl
from jax.experimental.pallas import tpu as pltpu

f32 = jnp.float32
bf16 = jnp.bfloat16
MX = jnp.bfloat16

D = 1024
DEPTH = 4
CHUNK = 64
H = 4
DK = 128
PLE = 256
QL = 384
KVL = 256
ROPE = 64
DFF = 2816
IN_W = 2760
ROPE_THETA = 10000.0
ALPHA = (2.0 * DEPTH) ** 0.25
LN_EPS = 1e-5
RMS_EPS = 1e-6
ATT_SCALE = (128 + 64) ** -0.5
N_DEV = 8

ADAM_LR, ADAM_B1, ADAM_B2, ADAM_EPS, ADAM_WD, ADAM_STEP = 0.001, 0.9, 0.999, 1e-08, 0.01, 10

OFF_QKV, OFF_Z, OFF_CKV, OFF_KR, OFF_KRS, OFF_CQ, HP = 0, 1536, 2048, 2304, 2432, 2560, 3072
CQP = 512
LANE = 128
VMEM_LIMIT = 48 * 1024 * 1024
ROW_TB = 256


def _dot(a, b, ca, cb, prec=None):
    if a.ndim == 3:
        return lax.dot_general(a, b, (((ca + 1,), (cb + 1,)), ((0,), (0,))), precision=prec, preferred_element_type=f32)
    return lax.dot_general(a, b, (((ca,), (cb,)), ((), ())), precision=prec, preferred_element_type=f32)


@jax.custom_vjp
def bdot(a, w):
    return _dot(a.astype(MX), w.astype(MX), 1, 0)


def _bdot_fwd(a, w):
    return bdot(a, w), (a, w)


def _bdot_bwd(res, g):
    a, w = res
    gb = g.astype(MX)
    return _dot(gb, w.astype(MX), 1, 1).astype(a.dtype), _dot(a.astype(MX), gb, 0, 0).astype(w.dtype)


bdot.defvjp(_bdot_fwd, _bdot_bwd)


@jax.custom_vjp
def bdot_nt(a, b):
    return _dot(a.astype(MX), b.astype(MX), 1, 1)


def _bdot_nt_fwd(a, b):
    return bdot_nt(a, b), (a, b)


def _bdot_nt_bwd(res, g):
    a, b = res
    gb = g.astype(MX)
    return _dot(gb, b.astype(MX), 1, 0).astype(a.dtype), _dot(gb, a.astype(MX), 0, 0).astype(b.dtype)


bdot_nt.defvjp(_bdot_nt_fwd, _bdot_nt_bwd)


@jax.custom_vjp
def bdot_tn(a, b):
    return _dot(a.astype(MX), b.astype(MX), 0, 0)


def _bdot_tn_fwd(a, b):
    return bdot_tn(a, b), (a, b)


def _bdot_tn_bwd(res, g):
    a, b = res
    gb = g.astype(MX)
    return _dot(b.astype(MX), gb, 1, 1).astype(a.dtype), _dot(a.astype(MX), gb, 1, 0).astype(b.dtype)


bdot_tn.defvjp(_bdot_tn_fwd, _bdot_tn_bwd)


@jax.custom_vjp
def hdot(a, b):
    return _dot(a, b, 1, 0, lax.Precision.HIGHEST)


def _hdot_fwd(a, b):
    return hdot(a, b), (a, b)


def _hdot_bwd(res, g):
    a, b = res
    return _dot(g, b, 1, 1, lax.Precision.HIGHEST), _dot(a, g, 0, 0, lax.Precision.HIGHEST)


hdot.defvjp(_hdot_fwd, _hdot_bwd)


def _pick(n, cap):
    best = None
    for t in range(LANE, min(n, cap) + 1, LANE):
        if n % t == 0:
            best = t
    assert best is not None, (n, cap)
    return best


def _cparams(sem):
    return pltpu.CompilerParams(dimension_semantics=sem, vmem_limit_bytes=VMEM_LIMIT)


def mm(name, a, b, *, mode, out_dtype, add=None):
    if mode == "tn":
        kdim, m = a.shape
        n = b.shape[1]
        tm, tn, tk = _pick(m, 1536), _pick(n, 1536), min(512, kdim)
        nk = kdim // tk

        def body(a_ref, b_ref, o_ref, acc):
            k = pl.program_id(2)

            @pl.when(k == 0)
            def _():
                acc[...] = jnp.zeros_like(acc)

            acc[...] += _dot(a_ref[...].astype(MX), b_ref[...].astype(MX), 0, 0)

            @pl.when(k == nk - 1)
            def _():
                o_ref[...] = acc[...].astype(o_ref.dtype)

        return pl.pallas_call(
            body, name=name, grid=(m // tm, n // tn, nk),
            in_specs=[pl.BlockSpec((tk, tm), lambda i, j, k: (k, i)), pl.BlockSpec((tk, tn), lambda i, j, k: (k, j))],
            out_specs=pl.BlockSpec((tm, tn), lambda i, j, k: (i, j)),
            out_shape=jax.ShapeDtypeStruct((m, n), out_dtype),
            scratch_shapes=[pltpu.VMEM((tm, tn), f32)],
            compiler_params=_cparams(("parallel", "parallel", "arbitrary")),
        )(a, b)

    m, kdim = a.shape
    n = b.shape[1] if mode == "nn" else b.shape[0]
    tm, tn = min(512, m), _pick(n, 1536)
    has_add = add is not None

    def body(*refs):
        a_ref, b_ref = refs[0], refs[1]
        o_ref = refs[-1]
        r = _dot(a_ref[...].astype(MX), b_ref[...].astype(MX), 1, 0 if mode == "nn" else 1)
        if has_add:
            r = r + refs[2][...].astype(f32)
        o_ref[...] = r.astype(o_ref.dtype)

    b_spec = (pl.BlockSpec((kdim, tn), lambda i, j: (0, j)) if mode == "nn"
              else pl.BlockSpec((tn, kdim), lambda i, j: (j, 0)))
    in_specs = [pl.BlockSpec((tm, kdim), lambda i, j: (i, 0)), b_spec]
    args = [a, b]
    if has_add:
        in_specs.append(pl.BlockSpec((tm, tn), lambda i, j: (i, j)))
        args.append(add)
    return pl.pallas_call(
        body, name=name, grid=(m // tm, n // tn), in_specs=in_specs,
        out_specs=pl.BlockSpec((tm, tn), lambda i, j: (i, j)),
        out_shape=jax.ShapeDtypeStruct((m, n), out_dtype),
        compiler_params=_cparams(("parallel", "parallel")),
    )(*args)


def _row_spec(tb, w, cb):
    return pl.BlockSpec((tb, w), lambda i: (i, cb))


def _full_spec(shape):
    return pl.BlockSpec(shape, lambda i: (0,) * len(shape))


def row_fwd(name, f, rows, consts, outs):
    t = rows[0][0].shape[0]
    tb = min(ROW_TB, t)
    nr, nc = len(rows), len(consts)

    def body(*refs):
        vals = [r[...] for r in refs[:nr + nc]]
        res = f(*vals)
        for o_ref, val in zip(refs[nr + nc:], res):
            o_ref[...] = val.astype(o_ref.dtype)

    return pl.pallas_call(
        body, name=name, grid=(t // tb,),
        in_specs=[_row_spec(tb, w, cb) for _, w, cb in rows] + [_full_spec(c.shape) for c in consts],
        out_specs=[_row_spec(tb, w, 0) for w, _ in outs],
        out_shape=[jax.ShapeDtypeStruct((t, w), dt) for w, dt in outs],
        compiler_params=_cparams(("parallel",)),
    )(*[r[0] for r in rows], *consts)


def row_bwd(name, f, rows, consts, cots, row_diff, const_diff, drow_dtypes, row_add=None):
    t = rows[0][0].shape[0]
    tb = min(ROW_TB, t)
    nr, nc, nct = len(rows), len(consts), len(cots)
    row_add = row_add or {}
    add_keys = sorted(row_add)
    n_in = nr + nc + nct + len(add_keys)

    def body(*refs):
        i = pl.program_id(0)
        rv = [r[...] for r in refs[:nr]]
        cv = [refs[nr + k][...].astype(f32) if k in const_diff else refs[nr + k][...] for k in range(nc)]
        cot_refs = refs[nr + nc:nr + nc + nct]
        add_refs = refs[nr + nc + nct:n_in]
        drow_refs = refs[n_in:n_in + len(row_diff)]
        dconst_refs = refs[n_in + len(row_diff):]

        def g(*dv):
            r2, c2 = list(rv), list(cv)
            for p, k in enumerate(row_diff):
                r2[k] = dv[p]
            for p, k in enumerate(const_diff):
                c2[k] = dv[len(row_diff) + p]
            return tuple(f(*r2, *c2))

        prim = [rv[k].astype(f32) for k in row_diff] + [cv[k] for k in const_diff]
        outs, vf = jax.vjp(g, *prim)
        grads = vf(tuple(c[...].astype(o.dtype) for c, o in zip(cot_refs, outs)))
        for p, ref in enumerate(drow_refs):
            val = grads[p]
            if p in row_add:
                val = val + add_refs[add_keys.index(p)][...].astype(f32)
            ref[...] = val.astype(ref.dtype)

        @pl.when(i == 0)
        def _():
            for ref in dconst_refs:
                ref[...] = jnp.zeros_like(ref)

        for p, ref in enumerate(dconst_refs):
            ref[...] += grads[len(row_diff) + p]

    widths = [rows[k][1] for k in row_diff]
    return pl.pallas_call(
        body, name=name, grid=(t // tb,),
        in_specs=([_row_spec(tb, w, cb) for _, w, cb in rows] + [_full_spec(c.shape) for c in consts]
                  + [_row_spec(tb, c.shape[1], 0) for c in cots] + [_row_spec(tb, row_add[k].shape[1], 0) for k in add_keys]),
        out_specs=([_row_spec(tb, w, 0) for w in widths] + [_full_spec(consts[k].shape) for k in const_diff]),
        out_shape=([jax.ShapeDtypeStruct((t, w), dt) for w, dt in zip(widths, drow_dtypes)]
                   + [jax.ShapeDtypeStruct(consts[k].shape, f32) for k in const_diff]),
        compiler_params=_cparams(("arbitrary",)),
    )(*[r[0] for r in rows], *consts, *cots, *[row_add[k] for k in add_keys])


def _heads(x, w=DK):
    return [x[:, w * h:w * (h + 1)] for h in range(H)]


def _layer_norm(r, g, b):
    mu = jnp.mean(r, -1, keepdims=True)
    var = jnp.mean(jnp.square(r - mu), -1, keepdims=True)
    return (r - mu) * lax.rsqrt(var + LN_EPS) * g + b


def gdn_point_f(c, ba, alog_b, dtb_b):
    s = c * jax.nn.sigmoid(c)
    q, k, v = s[:, :512], s[:, 512:1024], s[:, 1024:]

    def l2(x):
        return jnp.concatenate([xh * lax.rsqrt(jnp.sum(xh * xh, -1, keepdims=True) + RMS_EPS) for xh in _heads(x)], axis=1)

    tb = c.shape[0]
    b_b = jnp.concatenate([jnp.broadcast_to(ba[:, 64 + h:65 + h], (tb, DK)) for h in range(H)], axis=1)
    a_b = jnp.concatenate([jnp.broadcast_to(ba[:, 68 + h:69 + h], (tb, DK)) for h in range(H)], axis=1)
    beta = jax.nn.sigmoid(b_b)
    g = -jnp.exp(alog_b) * jax.nn.softplus(a_b + dtb_b)
    return l2(q), l2(k), v, beta, g


def mla_pre_f(cq, ckv, krg, krs, cf, sf, qg, wq, kvg, wkv):
    cqn = cq * lax.rsqrt(jnp.sum(cq * cq, -1, keepdims=True) * (1.0 / QL) + RMS_EPS) * qg
    qa = bdot(cqn, wq)
    ckvn = ckv * lax.rsqrt(jnp.mean(ckv * ckv, -1, keepdims=True) + RMS_EPS) * kvg
    kv = bdot(ckvn, wkv)
    kro = krg * cf + krs * sf
    qs, ks, vs = [], [], []
    for h in range(H):
        qs += [qa[:, DK * h:DK * (h + 1)], qa[:, 512 + DK * h:512 + DK * (h + 1)] * cf + qa[:, 1024 + DK * h:1024 + DK * (h + 1)] * sf]
        ks += [kv[:, 256 * h:256 * h + DK], kro]
        vs += [kv[:, 256 * h + DK:256 * (h + 1)]]
    return jnp.concatenate(qs, axis=1) * ATT_C2, jnp.concatenate(ks, axis=1), jnp.concatenate(vs, axis=1)


def post1_f(o, z, omla, x, gg_b, wout, g1, b1):
    on = jnp.concatenate([oh * lax.rsqrt(jnp.mean(oh * oh, -1, keepdims=True) + RMS_EPS) for oh in _heads(o)], axis=1) * gg_b
    ogdn = on * (z * jax.nn.sigmoid(z))
    mix = bdot(jnp.concatenate([ogdn, omla], axis=1), wout)
    return (_layer_norm(ALPHA * x + mix, g1, b1),)


def post2_f(x1, down, p, g2, b2, wpg, wple):
    x2 = _layer_norm(ALPHA * x1 + down, g2, b2)
    return (x2 + jax.nn.sigmoid(bdot(x2, wpg)) * bdot(p, wple),)


HALO = 8
CW = 4


def _conv_from_scratch(xs, cw_ref, tb):
    c = xs[pl.ds(HALO - 3, tb), :] * cw_ref[0:1, :]
    for j in range(1, CW):
        c = c + xs[pl.ds(HALO - 3 + j, tb), :] * cw_ref[j:j + 1, :]
    return c


def gdn_pre_fwd(hbuf, conv_w8, alog_b, dtb_b):
    t = hbuf.shape[0]
    tb = min(ROW_TB, t)

    def body(x_ref, halo_ref, ba_ref, cw_ref, al_ref, dt_ref, q_ref, k_ref, v_ref, be_ref, g_ref, xs):
        i = pl.program_id(0)
        xs[pl.ds(0, HALO), :] = jnp.where(i == 0, 0.0, halo_ref[...])
        xs[pl.ds(HALO, tb), :] = x_ref[...]
        c = _conv_from_scratch(xs, cw_ref, tb)
        q, k, v, be, g = gdn_point_f(c, ba_ref[...], al_ref[...], dt_ref[...])
        q_ref[...], k_ref[...], v_ref[...], be_ref[...], g_ref[...] = q, k, v, be, g

    return pl.pallas_call(
        body, name="gdn_pre_fwd", grid=(t // tb,),
        in_specs=[_row_spec(tb, 1536, 0),
                  pl.BlockSpec((HALO, 1536), lambda i: (jnp.maximum(i * (tb // HALO) - 1, 0), 0)),
                  _row_spec(tb, LANE, OFF_KR // LANE),
                  _full_spec(conv_w8.shape), _full_spec(alog_b.shape), _full_spec(dtb_b.shape)],
        out_specs=[_row_spec(tb, 512, 0)] * 5,
        out_shape=[jax.ShapeDtypeStruct((t, 512), f32)] * 5,
        scratch_shapes=[pltpu.VMEM((tb + HALO, 1536), f32)],
        compiler_params=_cparams(("arbitrary",)),
    )(hbuf, hbuf, hbuf, conv_w8, alog_b, dtb_b)


def gdn_pre_bwd(hbuf, conv_w8, alog_b, dtb_b, dq, dk, dv, dbe, dg):
    t = hbuf.shape[0]
    tb = min(ROW_TB, t)
    n = t // tb

    def body(x_ref, halo_ref, ba_ref, cw_ref, al_ref, dt_ref, dq_ref, dk_ref, dv_ref, dbe_ref, dg_ref,
             dx_ref, dba_ref, dcw_ref, dal_ref, ddt_ref, xs, dcs):
        s = pl.program_id(0)
        i = n - 1 - s
        xs[pl.ds(0, HALO), :] = jnp.where(i == 0, 0.0, halo_ref[...])
        xs[pl.ds(HALO, tb), :] = x_ref[...]
        c = _conv_from_scratch(xs, cw_ref, tb)
        _, vf = jax.vjp(gdn_point_f, c, ba_ref[...], al_ref[...], dt_ref[...])
        dc, dba, dal, ddt = vf((dq_ref[...], dk_ref[...], dv_ref[...], dbe_ref[...], dg_ref[...]))

        @pl.when(s == 0)
        def _():
            dcs[pl.ds(tb, HALO), :] = jnp.zeros((HALO, 1536), f32)
            dcw_ref[...] = jnp.zeros_like(dcw_ref)
            dal_ref[...] = jnp.zeros_like(dal_ref)
            ddt_ref[...] = jnp.zeros_like(ddt_ref)

        @pl.when(s > 0)
        def _():
            dcs[pl.ds(tb, HALO), :] = dcs[pl.ds(0, HALO), :]

        dcs[pl.ds(0, tb), :] = dc
        dx = dcs[pl.ds(3, tb), :] * cw_ref[0:1, :]
        for j in range(1, CW):
            dx = dx + dcs[pl.ds(3 - j, tb), :] * cw_ref[j:j + 1, :]
        dx_ref[...] = dx.astype(dx_ref.dtype)
        dba_ref[...] = dba
        for j in range(CW):
            dcw_ref[j:j + 1, :] += jnp.sum(dc * xs[pl.ds(HALO - 3 + j, tb), :], axis=0, keepdims=True)
        dal_ref[...] += dal
        ddt_ref[...] += ddt

    rev = lambda cb: (lambda s: (n - 1 - s, cb))
    return pl.pallas_call(
        body, name="gdn_pre_bwd", grid=(n,),
        in_specs=[pl.BlockSpec((tb, 1536), rev(0)),
                  pl.BlockSpec((HALO, 1536), lambda s: (jnp.maximum((n - 1 - s) * (tb // HALO) - 1, 0), 0)),
                  pl.BlockSpec((tb, LANE), rev(OFF_KR // LANE)),
                  _full_spec(conv_w8.shape), _full_spec(alog_b.shape), _full_spec(dtb_b.shape)]
        + [pl.BlockSpec((tb, 512), rev(0))] * 5,
        out_specs=[pl.BlockSpec((tb, 1536), rev(0)), pl.BlockSpec((tb, LANE), rev(0)),
                   _full_spec(conv_w8.shape), _full_spec(alog_b.shape), _full_spec(dtb_b.shape)],
        out_shape=[jax.ShapeDtypeStruct((t, 1536), bf16), jax.ShapeDtypeStruct((t, LANE), f32),
                   jax.ShapeDtypeStruct(conv_w8.shape, f32), jax.ShapeDtypeStruct(alog_b.shape, f32),
                   jax.ShapeDtypeStruct(dtb_b.shape, f32)],
        scratch_shapes=[pltpu.VMEM((tb + HALO, 1536), f32), pltpu.VMEM((tb + HALO, 1536), f32)],
        compiler_params=_cparams(("arbitrary",)),
    )(hbuf, hbuf, hbuf, conv_w8, alog_b, dtb_b, dq, dk, dv, dbe, dg)


GDN_NB = 4


def _to_batch(x, rows=CHUNK):
    n = x.shape[0] // rows
    return jnp.concatenate([x[rows * ci:rows * (ci + 1), DK * h:DK * (h + 1)][None] for ci in range(n) for h in range(H)], axis=0)


def _from_batch(y):
    n = y.shape[0] // H
    return jnp.concatenate([jnp.concatenate([y[ci * H + h] for h in range(H)], axis=1) for ci in range(n)], axis=0)


def gdn_intra_b(q, k, v, beta, g):
    c = CHUNK
    n = q.shape[0] // c
    tri = jnp.where(lax.broadcasted_iota(jnp.int32, (c, c), 0) >= lax.broadcasted_iota(jnp.int32, (c, c), 1), 1.0, 0.0)
    gc = _to_batch(jnp.concatenate([hdot(tri, g[c * ci:c * (ci + 1)]) for ci in range(n)], axis=0))
    qb, kb, vb, bb = _to_batch(q) * DK ** -0.5, _to_batch(k), _to_batch(v), _to_batch(beta)
    nbat = qb.shape[0]
    row = lax.broadcasted_iota(jnp.int32, (1, c, DK), 1)
    col = lax.broadcasted_iota(jnp.int32, (1, c, DK), 2)
    incl, strict, eye = row >= col, row > col, row == col
    grow = hdot(jnp.ones((nbat, c, c), f32), jnp.where(eye, gc, 0.0))
    decay = jnp.where(incl, jnp.exp(jnp.where(incl, gc - grow, 0.0)), 0.0)
    kbeta = kb * bb
    kpad = jnp.concatenate([kb, jnp.zeros((nbat, DK - c, DK), f32)], axis=1)
    a = jnp.where(strict, bdot_nt(kbeta, kpad) * decay, 0.0)[:, :, :c]
    nn = -a
    bk = bdot(a, a)
    for step in range(5):
        nn = nn + bk + bdot(nn, bk)
        if step < 4:
            bk = bdot(bk, bk)
    eg = jnp.exp(gc)
    rhs_v, rhs_k = vb * bb, kbeta * eg
    g_last = gc[:, c - 1:c, :]
    u = rhs_v + bdot(nn, rhs_v)
    w = rhs_k + bdot(nn, rhs_k)
    attn = jnp.where(incl, bdot_nt(qb, kpad) * decay, 0.0)
    return u, w, qb * eg, kb * jnp.exp(g_last - gc), attn, jnp.exp(g_last)


def gdn_scan_b(u, w, qd, kd, attn, el, state):
    v_new = u - bdot(w, state)
    return bdot(qd, state) + bdot(attn[:, :, :CHUNK], v_new), state * el + bdot_tn(kd, v_new)


def gdn_scan_bwd_b(do, w, qd, kd, attn, el, dstate):
    dvn = bdot_tn(attn[:, :, :CHUNK], do) + bdot(kd, dstate)
    return bdot_tn(qd, do) + dstate * el - bdot_tn(w, dvn)


def gdn_chunks_f(q, k, v, beta, g, states):
    n = states.shape[0]
    o, s_new = gdn_scan_b(*gdn_intra_b(q, k, v, beta, g), states.reshape(n * H, DK, DK))
    return _from_batch(o), s_new.reshape(n, H * DK, DK)


def _chunk_rows(cidx):
    return pl.ds(cidx * CHUNK, CHUNK)


def gdn_intra(q, k, v, beta, g):
    t = q.shape[0]
    nb = min(GDN_NB, t // CHUNK)
    tb = nb * CHUNK

    def body(q_ref, k_ref, v_ref, be_ref, g_ref, u_ref, w_ref, qd_ref, kd_ref, at_ref, el_ref):
        u, w, qd, kd, at, el = gdn_intra_b(q_ref[...], k_ref[...], v_ref[...], be_ref[...], g_ref[...])
        u_ref[...], w_ref[...], qd_ref[...], kd_ref[...], at_ref[...] = (_from_batch(a) for a in (u, w, qd, kd, at))
        el_ref[...] = _from_batch(jnp.broadcast_to(el, (nb * H, 8, DK))).reshape(nb, 8, H * DK)

    return pl.pallas_call(
        body, name="gdn_intra", grid=(t // tb,),
        in_specs=[_row_spec(tb, 512, 0)] * 5,
        out_specs=[_row_spec(tb, 512, 0)] * 5 + [pl.BlockSpec((nb, 8, 512), lambda i: (i, 0, 0))],
        out_shape=[jax.ShapeDtypeStruct((t, 512), f32)] * 5 + [jax.ShapeDtypeStruct((t // CHUNK, 8, 512), f32)],
        compiler_params=_cparams(("parallel",)),
    )(q, k, v, beta, g)


def gdn_scan_fwd(u, w, qd, kd, attn, el):
    t = u.shape[0]
    nb = min(GDN_NB, t // CHUNK)
    tb = nb * CHUNK

    def body(u_ref, w_ref, qd_ref, kd_ref, at_ref, el_ref, o_ref, s_ref, state):
        @pl.when(pl.program_id(0) == 0)
        def _():
            state[...] = jnp.zeros_like(state)

        for cidx in range(nb):
            r = _chunk_rows(cidx)
            s_ref[cidx] = state[...]
            ins = [_to_batch(ref[r, :]) for ref in (u_ref, w_ref, qd_ref, kd_ref, at_ref)]
            el = _to_batch(el_ref[cidx], 8)[:, 0:1, :]
            o, s_new = gdn_scan_b(*ins, el, state[...].reshape(H, DK, DK))
            o_ref[r, :] = _from_batch(o)
            state[...] = s_new.reshape(H * DK, DK)

    return pl.pallas_call(
        body, name="gdn_scan_fwd", grid=(t // tb,),
        in_specs=[_row_spec(tb, 512, 0)] * 5 + [pl.BlockSpec((nb, 8, 512), lambda i: (i, 0, 0))],
        out_specs=[_row_spec(tb, 512, 0), pl.BlockSpec((nb, 512, DK), lambda i: (i, 0, 0))],
        out_shape=[jax.ShapeDtypeStruct((t, 512), f32), jax.ShapeDtypeStruct((t // CHUNK, 512, DK), f32)],
        scratch_shapes=[pltpu.VMEM((512, DK), f32)],
        compiler_params=_cparams(("arbitrary",)),
    )(u, w, qd, kd, attn, el)


def gdn_scan_bwd(do, w, qd, kd, attn, el):
    t = do.shape[0]
    nb = min(GDN_NB, t // CHUNK)
    tb = nb * CHUNK
    n = t // tb

    def body(do_ref, w_ref, qd_ref, kd_ref, at_ref, el_ref, ds_ref, dstate):
        @pl.when(pl.program_id(0) == 0)
        def _():
            dstate[...] = jnp.zeros_like(dstate)

        for cidx in reversed(range(nb)):
            r = _chunk_rows(cidx)
            ds_ref[cidx] = dstate[...]
            ins = [_to_batch(ref[r, :]) for ref in (do_ref, w_ref, qd_ref, kd_ref, at_ref)]
            el = _to_batch(el_ref[cidx], 8)[:, 0:1, :]
            dstate[...] = gdn_scan_bwd_b(*ins, el, dstate[...].reshape(H, DK, DK)).reshape(H * DK, DK)

    rev = pl.BlockSpec((tb, 512), lambda s: (n - 1 - s, 0))
    return pl.pallas_call(
        body, name="gdn_scan_bwd", grid=(n,),
        in_specs=[rev] * 5 + [pl.BlockSpec((nb, 8, 512), lambda s: (n - 1 - s, 0, 0))],
        out_specs=pl.BlockSpec((nb, 512, DK), lambda s: (n - 1 - s, 0, 0)),
        out_shape=jax.ShapeDtypeStruct((t // CHUNK, 512, DK), f32),
        scratch_shapes=[pltpu.VMEM((512, DK), f32)],
        compiler_params=_cparams(("arbitrary",)),
    )(do, w, qd, kd, attn, el)


def gdn_local_bwd(q, k, v, beta, g, states, dstates, do, name="gdn_local_bwd", comms=()):
    t = q.shape[0]
    nb = min(GDN_NB, t // CHUNK)
    tb = nb * CHUNK
    n = t // tb

    def body(*refs):
        ins, outs, _, comm_refs = _split_hosted(refs, 8, 5, 0, len(comms))
        q_ref, k_ref, v_ref, be_ref, g_ref, s_ref, ds_ref, do_ref = ins
        dq_ref, dk_ref, dv_ref, dbe_ref, dg_ref = outs
        _host_exchanges(comm_refs, comms, pl.program_id(0) == 0, pl.program_id(0) == n - 1)
        states_v = s_ref[...]
        _, vf = jax.vjp(lambda *a: gdn_chunks_f(*a, states_v), q_ref[...], k_ref[...], v_ref[...], be_ref[...], g_ref[...])
        dq_ref[...], dk_ref[...], dv_ref[...], dbe_ref[...], dg_ref[...] = vf((do_ref[...], ds_ref[...]))

    st = pl.BlockSpec((nb, 512, DK), lambda i: (i, 0, 0))
    in_specs = [_row_spec(tb, 512, 0)] * 5 + [st, st, _row_spec(tb, 512, 0)]
    out_specs = [_row_spec(tb, 512, 0)] * 5
    out_shape = [jax.ShapeDtypeStruct((t, 512), f32)] * 5
    scratch, args = [], [q, k, v, beta, g, states, dstates, do]
    _add_hosted(comms, in_specs, out_specs, out_shape, scratch, args)
    return pl.pallas_call(body, name=name, grid=(n,), in_specs=in_specs, out_specs=out_specs, out_shape=out_shape,
                          scratch_shapes=scratch, compiler_params=_cparams(("arbitrary",) if comms else ("parallel",)))(*args)


NEG = -1e30


LOG2E = 1.4426950408889634
ATT_C2 = ATT_SCALE * LOG2E
ATT_RB = 256


def _add_hosted(comms, in_specs, out_specs, out_shape, scratch, args):
    for src, scatter in comms:
        in_specs.append(_ANY)
        out_specs.append(_ANY)
        out_shape.append(_comm_out_shape(src, scatter))
        scratch += _comm_scratch()
        args.append(src)


def _split_hosted(refs, n_in, n_out, n_scratch, n_comm):
    ins, srcs = refs[:n_in], refs[n_in:n_in + n_comm]
    o0 = n_in + n_comm
    outs, dsts = refs[o0:o0 + n_out], refs[o0 + n_out:o0 + n_out + n_comm]
    rest = refs[o0 + n_out + n_comm:]
    sems = rest[n_scratch:]
    return ins, outs, rest[:n_scratch], [(srcs[c], dsts[c]) + tuple(sems[3 * c:3 * c + 3]) for c in range(n_comm)]


def _host_exchanges(comm_refs, comms, is_first, is_last):
    if not comms:
        return

    @pl.when(is_first)
    def _():
        for refs, (_, scatter) in zip(comm_refs, comms):
            for op in _exchange_copies(*refs, scatter):
                op.start()

    @pl.when(is_last)
    def _():
        for refs, (_, scatter) in zip(comm_refs, comms):
            for op in _exchange_copies(*refs, scatter):
                op.wait()


def attn_fwd(q, k, v, name="attn_fwd", comms=()):
    t = q.shape[0]
    tq = tk = min(1024, t)
    rb = min(ATT_RB, tq)
    nq, nk, nr = t // tq, t // tk, tq // rb
    last = lambda i: i

    def body(*refs):
        (q_ref, k_ref, v_ref), (o_ref, lse_ref), (m_s, acc), comm_refs = _split_hosted(refs, 3, 2, 2, len(comms))
        h, i, j = pl.program_id(0), pl.program_id(1), pl.program_id(2)
        _host_exchanges(comm_refs, comms, (h == 0) & (i == 0) & (j == 0), (h == H - 1) & (i == nq - 1) & (j == nk - 1))

        @pl.when(j == 0)
        def _():
            m_s[...] = jnp.full_like(m_s, NEG)
            acc[...] = jnp.zeros_like(acc)

        one_col = jnp.where(lax.broadcasted_iota(jnp.int32, (tk, DK), 1) == 0, 1.0, 0.0).astype(MX)
        vx = jnp.concatenate([v_ref[...], one_col], axis=1)

        def rows_of(r):
            return pl.ds(r * rb, rb)

        def soft(r, s):
            m_old = m_s[rows_of(r), :]
            m_new = jnp.maximum(m_old, jnp.max(s, axis=-1, keepdims=True))
            m_s[rows_of(r), :] = m_new
            return jnp.exp2(s - m_new).astype(MX), jnp.exp2(m_old - m_new)

        def pv(r, p, alpha, vals):
            acc[rows_of(r), :] = alpha * acc[rows_of(r), :] + _dot(p, vals, 1, 0)

        def whole_block():
            kb = k_ref[...]
            ss = [_dot(q_ref[rows_of(r), :], kb, 1, 1) for r in range(min(2, nr))]
            pend = None
            for r in range(nr):
                if r + 2 < nr:
                    ss.append(_dot(q_ref[rows_of(r + 2), :], kb, 1, 1))
                p, alpha = soft(r, ss[r])
                if pend is not None:
                    pv(*pend, vx)
                pend = (r, p, alpha)
            pv(*pend, vx)

        def diagonal_block():
            for r in range(nr):
                cols = (r + 1) * rb
                s = _dot(q_ref[rows_of(r), :], k_ref[pl.ds(0, cols), :], 1, 1)
                qrow = lax.broadcasted_iota(jnp.int32, (rb, cols), 0) + r * rb
                kcol = lax.broadcasted_iota(jnp.int32, (rb, cols), 1)
                s = jnp.where(jnp.right_shift(kcol, 6) <= jnp.right_shift(qrow, 6), s, NEG)
                p, alpha = soft(r, s)
                pv(r, p, alpha, vx[:cols])

        pl.when(j < i)(whole_block)
        pl.when(j == i)(diagonal_block)

        @pl.when(j == nk - 1)
        def _():
            a = acc[...]
            l = a[:, DK:DK + 1]
            o_ref[...] = a[:, :DK] / l
            lse_ref[0] = m_s[...] + jnp.log2(l)

    in_specs = [pl.BlockSpec((tq, 256), lambda h, i, j: (i, h)),
                pl.BlockSpec((tk, 256), lambda h, i, j: (jnp.minimum(j, last(i)), h)),
                pl.BlockSpec((tk, DK), lambda h, i, j: (jnp.minimum(j, last(i)), h))]
    out_specs = [pl.BlockSpec((tq, DK), lambda h, i, j: (i, h)), pl.BlockSpec((1, tq, 1), lambda h, i, j: (h, i, 0))]
    out_shape = [jax.ShapeDtypeStruct((t, 512), f32), jax.ShapeDtypeStruct((H, t, 1), f32)]
    scratch = [pltpu.VMEM((tq, 1), f32), pltpu.VMEM((tq, 2 * DK), f32)]
    args = [q, k, v]
    _add_hosted(comms, in_specs, out_specs, out_shape, scratch, args)
    sem = ("arbitrary",) * 3 if comms else ("parallel", "parallel", "arbitrary")
    return pl.pallas_call(body, name=name, grid=(H, nq, nk), in_specs=in_specs, out_specs=out_specs, out_shape=out_shape,
                          scratch_shapes=scratch, compiler_params=_cparams(sem))(*args)


def attn_bwd(q, k, v, o, lse, do, name="attn_bwd", comms=()):
    t = q.shape[0]
    tq = tk = min(1024, t)
    rb = min(ATT_RB, tq)
    nq, nk, nr = t // tq, t // tk, tq // rb
    first = lambda j: j

    def body(*refs):
        ins, outs, (dk_acc, dv_acc), comm_refs = _split_hosted(refs, 6, 3, 2, len(comms))
        q_ref, k_ref, v_ref, o_ref, lse_ref, do_ref = ins
        dq_ref, dk_ref, dv_ref = outs
        h, j, i = pl.program_id(0), pl.program_id(1), pl.program_id(2)
        _host_exchanges(comm_refs, comms, (h == 0) & (i == 0) & (j == 0), (h == H - 1) & (i == nq - 1) & (j == nk - 1))

        @pl.when((j == 0) & (i == 0))
        def _():
            dq_ref[...] = jnp.zeros_like(dq_ref)

        @pl.when(i == 0)
        def _():
            dk_acc[...] = jnp.zeros_like(dk_acc)
            dv_acc[...] = jnp.zeros_like(dv_acc)

        def rows_of(r):
            return pl.ds(r * rb, rb)

        def front(r, cols):
            qb, dob = q_ref[rows_of(r), :], do_ref[rows_of(r), :]
            return qb, dob, _dot(qb, k_ref[pl.ds(0, cols), :], 1, 1), _dot(dob.astype(MX), v_ref[pl.ds(0, cols), :], 1, 1)

        def middle(r, dob, s, dp):
            p = jnp.exp2(s - lse_ref[0, rows_of(r), :])
            delta = jnp.sum(dob * o_ref[rows_of(r), :], axis=-1, keepdims=True)
            return p.astype(MX), (p * (dp - delta)).astype(MX)

        def back(r, cols, qb, dob, pb, ds):
            dv_acc[pl.ds(0, cols), :] += _dot(pb, dob.astype(MX), 0, 0)
            dk_acc[pl.ds(0, cols), :] += _dot(ds, qb, 0, 0) * (1.0 / LOG2E)
            grow = pl.ds(pl.multiple_of(i * tq + r * rb, rb), rb)
            dq_ref[grow, :] += _dot(ds, k_ref[pl.ds(0, cols), :], 1, 0) * (1.0 / LOG2E)

        def whole_block():
            nxt = front(0, tk)
            for r in range(nr):
                qb, dob, s, dp = nxt
                if r + 1 < nr:
                    nxt = front(r + 1, tk)
                back(r, tk, qb, dob, *middle(r, dob, s, dp))

        def diagonal_block():
            for r in range(nr):
                cols = (r + 1) * rb
                qb, dob, s, dp = front(r, cols)
                qrow = lax.broadcasted_iota(jnp.int32, (rb, cols), 0) + r * rb
                kcol = lax.broadcasted_iota(jnp.int32, (rb, cols), 1)
                s = jnp.where(jnp.right_shift(kcol, 6) <= jnp.right_shift(qrow, 6), s, NEG)
                back(r, cols, qb, dob, *middle(r, dob, s, dp))

        pl.when(i > j)(whole_block)
        pl.when(i == j)(diagonal_block)

        @pl.when(i == nq - 1)
        def _():
            dk_ref[...] = dk_acc[...]
            dv_ref[...] = dv_acc[...]

    qi = lambda h, j, i: (jnp.maximum(i, first(j)), h)
    in_specs = [pl.BlockSpec((tq, 256), qi),
                pl.BlockSpec((tk, 256), lambda h, j, i: (j, h)),
                pl.BlockSpec((tk, DK), lambda h, j, i: (j, h)),
                pl.BlockSpec((tq, DK), qi),
                pl.BlockSpec((1, tq, 1), lambda h, j, i: (h, jnp.maximum(i, first(j)), 0)),
                pl.BlockSpec((tq, DK), qi)]
    out_specs = [pl.BlockSpec((t, 256), lambda h, j, i: (0, h)),
                 pl.BlockSpec((tk, 256), lambda h, j, i: (j, h)),
                 pl.BlockSpec((tk, DK), lambda h, j, i: (j, h))]
    out_shape = [jax.ShapeDtypeStruct((t, 1024), f32), jax.ShapeDtypeStruct((t, 1024), f32), jax.ShapeDtypeStruct((t, 512), f32)]
    scratch = [pltpu.VMEM((tk, 256), f32), pltpu.VMEM((tk, DK), f32)]
    args = [q, k, v, o, lse, do]
    _add_hosted(comms, in_specs, out_specs, out_shape, scratch, args)
    return pl.pallas_call(body, name=name, grid=(H, nk, nq), in_specs=in_specs, out_specs=out_specs, out_shape=out_shape,
                          scratch_shapes=scratch, compiler_params=_cparams(("arbitrary",) * 3))(*args)


def ffn_up(x1, wg, wu):
    t = x1.shape[0]
    tm, tn = min(512, t), _pick(DFF, 1536)

    def body(x_ref, wg_ref, wu_ref, g_ref, u_ref, a_ref):
        xb = x_ref[...].astype(MX)
        g = _dot(xb, wg_ref[...], 1, 0)
        u = _dot(xb, wu_ref[...], 1, 0)
        g_ref[...] = g.astype(g_ref.dtype)
        u_ref[...] = u.astype(u_ref.dtype)
        a_ref[...] = (g * jax.nn.sigmoid(g) * u).astype(a_ref.dtype)

    w_spec = pl.BlockSpec((D, tn), lambda i, j: (0, j))
    o_spec = pl.BlockSpec((tm, tn), lambda i, j: (i, j))
    return pl.pallas_call(
        body, name="ffn_up", grid=(t // tm, DFF // tn),
        in_specs=[pl.BlockSpec((tm, D), lambda i, j: (i, 0)), w_spec, w_spec],
        out_specs=[o_spec] * 3, out_shape=[jax.ShapeDtypeStruct((t, DFF), bf16)] * 3,
        compiler_params=_cparams(("parallel", "parallel")),
    )(x1, wg, wu)


def ffn_dact(ddown, wd, g, u):
    t = ddown.shape[0]
    tm, tn = min(512, t), _pick(DFF, 1536)

    def body(dd_ref, wd_ref, g_ref, u_ref, dg_ref, du_ref):
        dact = _dot(dd_ref[...].astype(MX), wd_ref[...], 1, 1)
        gv, uv = g_ref[...].astype(f32), u_ref[...].astype(f32)
        sig = jax.nn.sigmoid(gv)
        dg_ref[...] = (dact * uv * sig * (1.0 + gv * (1.0 - sig))).astype(dg_ref.dtype)
        du_ref[...] = (dact * gv * sig).astype(du_ref.dtype)

    o_spec = pl.BlockSpec((tm, tn), lambda i, j: (i, j))
    return pl.pallas_call(
        body, name="ffn_dact", grid=(t // tm, DFF // tn),
        in_specs=[pl.BlockSpec((tm, D), lambda i, j: (i, 0)), pl.BlockSpec((tn, D), lambda i, j: (j, 0)), o_spec, o_spec],
        out_specs=[o_spec] * 2, out_shape=[jax.ShapeDtypeStruct((t, DFF), bf16)] * 2,
        compiler_params=_cparams(("parallel", "parallel")),
    )(ddown, wd, g, u)


def loss_head(y, target):
    t = y.shape[0]
    tb = min(ROW_TB, t)
    n = t // tb

    def body(y_ref, t_ref, dy_ref, loss_ref, acc):
        i = pl.program_id(0)

        @pl.when(i == 0)
        def _():
            acc[...] = jnp.zeros_like(acc)

        e = y_ref[...] - t_ref[...]
        dy_ref[...] = e * (1.0 / D)
        acc[...] += jnp.sum(e * e, axis=0, keepdims=True)

        @pl.when(i == n - 1)
        def _():
            loss_ref[...] = jnp.sum(acc[...], axis=1, keepdims=True) * (0.5 / D)

    return pl.pallas_call(
        body, name="loss_head", grid=(n,),
        in_specs=[_row_spec(tb, D, 0)] * 2,
        out_specs=[_row_spec(tb, D, 0), _full_spec((1, 1))],
        out_shape=[jax.ShapeDtypeStruct((t, D), f32), jax.ShapeDtypeStruct((1, 1), f32)],
        scratch_shapes=[pltpu.VMEM((1, D), f32)],
        compiler_params=_cparams(("arbitrary",)),
    )(y, target)


def _me_and_peers():
    x, y, c = lax.axis_index("x"), lax.axis_index("y"), lax.axis_index("c")
    me = 4 * x + 2 * y + c
    peers = []
    for kk in range(1, N_DEV):
        px = 1 - x if kk & 4 else x
        py = 1 - y if kk & 2 else y
        pc = 1 - c if kk & 1 else c
        peers.append(((px, py, pc), 4 * px + 2 * py + pc))
    return me, peers


_ANY = pl.BlockSpec(memory_space=pl.ANY)


def _comm_scratch():
    return [pltpu.SemaphoreType.DMA((N_DEV - 1,)), pltpu.SemaphoreType.DMA((N_DEV - 1,)), pltpu.SemaphoreType.DMA]


def _exchange_copies(src_ref, out_ref, send_sems, recv_sems, local_sem, scatter):
    me, peers = _me_and_peers()
    pick = (lambda d: src_ref.at[d]) if scatter else (lambda d: src_ref)
    ops = [pltpu.make_async_copy(pick(me), out_ref.at[me], local_sem)]
    ops += [pltpu.make_async_remote_copy(src_ref=pick(pid), dst_ref=out_ref.at[me], send_sem=send_sems.at[kk],
                                         recv_sem=recv_sems.at[kk], device_id=dev, device_id_type=pl.DeviceIdType.MESH)
            for kk, (dev, pid) in enumerate(peers)]
    return ops


def _comm_out_shape(src, scatter):
    return jax.ShapeDtypeStruct(src.shape if scatter else (N_DEV,) + src.shape, src.dtype)


def exchange(name, comms):
    def body(*refs):
        _, _, _, comm_refs = _split_hosted(refs, 0, 0, 0, len(comms))
        ops = [op for refs_c, (_, scatter) in zip(comm_refs, comms) for op in _exchange_copies(*refs_c, scatter)]
        for op in ops:
            op.start()
        for op in ops:
            op.wait()

    in_specs, out_specs, out_shape, scratch, args = [], [], [], [], []
    _add_hosted(comms, in_specs, out_specs, out_shape, scratch, args)
    return pl.pallas_call(body, name=name, in_specs=in_specs, out_specs=out_specs, out_shape=out_shape, scratch_shapes=scratch)(*args)


def sum_parts(parts):
    _, r, c = parts.shape
    tb = _pick_rows(r)

    def body(p_ref, g_ref):
        g = p_ref[0]
        for s in range(1, N_DEV):
            g = g + p_ref[s]
        g_ref[...] = g

    return pl.pallas_call(
        body, name="sum_grad_parts", grid=(r // tb,),
        in_specs=[pl.BlockSpec((N_DEV, tb, c), lambda i: (0, i, 0))],
        out_specs=pl.BlockSpec((tb, c), lambda i: (i, 0)), out_shape=jax.ShapeDtypeStruct((r, c), f32),
        compiler_params=_cparams(("parallel",)),
    )(parts)


def adamw(w, m, v, g):
    r, c = w.shape
    tb = _pick_rows(r)

    def body(w_ref, m_ref, v_ref, g_ref, d_ref, nm_ref, nv_ref):
        g = g_ref[...]
        nm = ADAM_B1 * m_ref[...] + (1.0 - ADAM_B1) * g
        nv = ADAM_B2 * v_ref[...] + (1.0 - ADAM_B2) * jnp.square(g)
        m_hat = nm / (1.0 - ADAM_B1 ** ADAM_STEP)
        v_hat = nv / (1.0 - ADAM_B2 ** ADAM_STEP)
        d_ref[...] = -ADAM_LR * (m_hat / (jnp.sqrt(v_hat) + ADAM_EPS) + ADAM_WD * w_ref[...])
        nm_ref[...] = nm
        nv_ref[...] = nv

    spec = pl.BlockSpec((tb, c), lambda i: (i, 0))
    return pl.pallas_call(
        body, name="adamw", grid=(r // tb,), in_specs=[spec] * 4, out_specs=[spec] * 3,
        out_shape=[jax.ShapeDtypeStruct((r, c), f32)] * 3, compiler_params=_cparams(("parallel",)),
    )(w, m, v, g)


def _pick_rows(r):
    return max(tb for tb in range(8, ROW_TB + 1, 8) if r % tb == 0)


BIG = ("w_in", "w_uq", "w_ukv", "w_out", "w_gate_up", "w_down", "w_ple", "w_ple_gate")
COL_SHARDED = ("w_in", "w_uq", "w_ukv", "w_gate_up", "w_ple")
SMALL = ("conv_w", "a_log", "dt_bias", "gdn_norm_g", "q_norm_g", "kv_norm_g", "ln1_g", "ln1_b", "ln2_g", "ln2_b")
ROW_ALIGN = 16


def _pad_rows(a, mult=ROW_ALIGN, axis=0):
    pad = (-a.shape[axis]) % mult
    widths = [(0, 0)] * a.ndim
    widths[axis] = (0, pad)
    return a if pad == 0 else jnp.pad(a, widths)


GATHER_EARLY = ("w_in", "w_uq", "w_ukv")
GATHER_LATE = ("w_out", "w_gate_up", "w_down", "w_ple", "w_ple_gate")
GRAD_EARLY = ("w_gate_up", "w_down", "w_ple", "w_ple_gate")
GRAD_LATE = ("w_in", "w_uq", "w_ukv", "w_out")


def _gathers(tree, l, names):
    return [(tree[n][l].astype(bf16), False) for n in names]


def _full_weight(gathered, n, lo=0, hi=N_DEV):
    blk = gathered[lo:hi]
    k, sr, sc = blk.shape
    if n in COL_SHARDED:
        return jnp.transpose(blk, (1, 0, 2)).reshape(sr, k * sc)
    return blk.reshape(k * sr, sc)


def _shard_grad(gfull, n):
    if n in COL_SHARDED:
        sr = gfull.shape[0]
        k = gfull.shape[1] // (FULL_COLS[n] // N_DEV)
        return jnp.transpose(gfull.reshape(sr, k, gfull.shape[1] // k), (1, 0, 2))
    return gfull.reshape(N_DEV, gfull.shape[0] // N_DEV, gfull.shape[1])


FULL_COLS = {"w_in": IN_W, "w_uq": 768, "w_ukv": 1024, "w_gate_up": 2 * DFF, "w_ple": D}


def _pack_small(tree, extra):
    flat = jnp.concatenate([tree[n].reshape(-1).astype(f32) for n in SMALL] + [extra.reshape(-1).astype(f32)])
    return jnp.pad(flat, (0, (-flat.shape[0]) % (ROW_ALIGN * D))).reshape(-1, D)


def _unpack_small(rows, like):
    flat = rows.reshape(-1)
    out, off = {}, 0
    for n in SMALL:
        sz = int(np.prod(like[n].shape))
        out[n] = flat[off:off + sz].reshape(like[n].shape)
        off += sz
    return out, flat[off]


def _zeros(r, c, dt):
    return jnp.zeros((r, c), dt)


def _prep_w_in(w):
    dt = w.dtype
    kr = w[:, 2696:2760]
    return jnp.concatenate([
        w[:, 0:2048],
        w[:, 2440:2696],
        kr, w[:, 2048:2056], _zeros(D, 56, dt),
        kr[:, 32:], kr[:, :32], _zeros(D, 64, dt),
        w[:, 2056:2440], _zeros(D, CQP - QL, dt)], axis=1)


def _unprep_w_in(g):
    krs = g[:, OFF_KRS:OFF_KRS + 64]
    kr = g[:, OFF_KR:OFF_KR + 64] + jnp.concatenate([krs[:, 32:], krs[:, :32]], axis=1)
    return jnp.concatenate([g[:, 0:2048], g[:, OFF_KR + 64:OFF_KR + 72], g[:, OFF_CQ:OFF_CQ + QL],
                            g[:, OFF_CKV:OFF_CKV + KVL], kr], axis=1)


def _prep_w_uq(w):
    dt = w.dtype
    z64 = _zeros(QL, 64, dt)
    nope, ra, rb = [], [], []
    for h in range(H):
        nope.append(w[:, 192 * h:192 * h + 128])
        x1, x2 = w[:, 192 * h + 128:192 * h + 160], w[:, 192 * h + 160:192 * h + 192]
        ra += [x1, x2, z64]
        rb += [x2, x1, z64]
    return jnp.concatenate([jnp.concatenate(nope + ra + rb, axis=1), _zeros(CQP - QL, 1536, dt)], axis=0)


def _unprep_w_uq(g):
    g = g[:QL]
    cols = []
    for h in range(H):
        a = g[:, 512 + 128 * h:512 + 128 * h + 64]
        b = g[:, 1024 + 128 * h:1024 + 128 * h + 64]
        cols += [g[:, 128 * h:128 * (h + 1)], a[:, :32] + b[:, 32:], a[:, 32:] + b[:, :32]]
    return jnp.concatenate(cols, axis=1)


def _rope_tables(positions):
    inv_freq = ROPE_THETA ** (-jnp.arange(0, ROPE, 2, dtype=f32) / ROPE)
    ang = positions.astype(f32)[:, None] * inv_freq
    c, s = jnp.cos(ang), jnp.sin(ang)
    z = jnp.zeros((positions.shape[0], 64), f32)
    return jnp.concatenate([c, c, z], axis=1), jnp.concatenate([-s, s, z], axis=1)


def _tile_heads(vec, n=H):
    return jnp.tile(vec.reshape(1, -1), (1, n))


def _bcast_heads(vec):
    return jnp.repeat(vec, DK).reshape(1, H * DK)


def kernel(x, p, positions, w_in, conv_w, a_log, dt_bias, gdn_norm_g, q_norm_g, w_uq, kv_norm_g, w_ukv, w_out, ln1_g, ln1_b, w_gate_up, w_down, ln2_g, ln2_b, w_ple, w_ple_gate, loss_target, m_w_in, m_conv_w, m_a_log, m_dt_bias, m_gdn_norm_g, m_q_norm_g, m_w_uq, m_kv_norm_g, m_w_ukv, m_w_out, m_ln1_g, m_ln1_b, m_w_gate_up, m_w_down, m_ln2_g, m_ln2_b, m_w_ple, m_w_ple_gate, v_w_in, v_conv_w, v_a_log, v_dt_bias, v_gdn_norm_g, v_q_norm_g, v_w_uq, v_kv_norm_g, v_w_ukv, v_w_out, v_ln1_g, v_ln1_b, v_w_gate_up, v_w_down, v_ln2_g, v_ln2_b, v_w_ple, v_w_ple_gate):
    W = dict(w_in=w_in, conv_w=conv_w, a_log=a_log, dt_bias=dt_bias, gdn_norm_g=gdn_norm_g, q_norm_g=q_norm_g, w_uq=w_uq,
             kv_norm_g=kv_norm_g, w_ukv=w_ukv, w_out=w_out, ln1_g=ln1_g, ln1_b=ln1_b, w_gate_up=w_gate_up, w_down=w_down,
             ln2_g=ln2_g, ln2_b=ln2_b, w_ple=w_ple, w_ple_gate=w_ple_gate)
    M = dict(w_in=m_w_in, conv_w=m_conv_w, a_log=m_a_log, dt_bias=m_dt_bias, gdn_norm_g=m_gdn_norm_g, q_norm_g=m_q_norm_g,
             w_uq=m_w_uq, kv_norm_g=m_kv_norm_g, w_ukv=m_w_ukv, w_out=m_w_out, ln1_g=m_ln1_g, ln1_b=m_ln1_b,
             w_gate_up=m_w_gate_up, w_down=m_w_down, ln2_g=m_ln2_g, ln2_b=m_ln2_b, w_ple=m_w_ple, w_ple_gate=m_w_ple_gate)
    V = dict(w_in=v_w_in, conv_w=v_conv_w, a_log=v_a_log, dt_bias=v_dt_bias, gdn_norm_g=v_gdn_norm_g, q_norm_g=v_q_norm_g,
             w_uq=v_w_uq, kv_norm_g=v_kv_norm_g, w_ukv=v_w_ukv, w_out=v_w_out, ln1_g=v_ln1_g, ln1_b=v_ln1_b,
             w_gate_up=v_w_gate_up, w_down=v_w_down, ln2_g=v_ln2_g, ln2_b=v_ln2_b, w_ple=v_w_ple, w_ple_gate=v_w_ple_gate)
    shards = {n: W[n].shape for n in BIG}
    t = x.shape[1]
    xin = x.reshape(t, D)
    target = loss_target.reshape(t, D)
    cf, sf = _rope_tables(positions.reshape(t))

    conv_rows = _pad_rows(jnp.pad(conv_w.reshape(-1), (0, (-conv_w.size) % D)).reshape(-1, D), 8)
    first = exchange("all_gather_first", _gathers(W, 0, GATHER_EARLY) + [(conv_rows, False)])
    early = [dict(zip(GATHER_EARLY, first[:-1]))] + [None] * (DEPTH - 1)
    csz = conv_w.shape[1] * conv_w.shape[2]
    conv_full = jnp.transpose(first[-1].reshape(N_DEV, -1)[:, :DEPTH * csz].reshape(N_DEV, DEPTH, CW, -1), (1, 2, 0, 3)).reshape(DEPTH, CW, 1536)

    acts = []
    h_cur = xin
    for l in range(DEPTH):
        win = _prep_w_in(_full_weight(early[l]["w_in"], "w_in"))
        wq = _prep_w_uq(_full_weight(early[l]["w_uq"], "w_uq"))
        wkv = _full_weight(early[l]["w_ukv"], "w_ukv")
        cw8 = jnp.concatenate([conv_full[l], jnp.zeros((8 - CW, 1536), f32)], axis=0)
        alog_b, dtb_b = _bcast_heads(a_log[l]), _bcast_heads(dt_bias[l])
        gg_b = _tile_heads(gdn_norm_g[l])
        qg = jnp.concatenate([q_norm_g[l], jnp.zeros((CQP - QL,), f32)]).reshape(1, CQP)
        kvg = kv_norm_g[l].reshape(1, KVL)
        g1, b1, g2, b2 = (a[l].reshape(1, D) for a in (ln1_g, ln1_b, ln2_g, ln2_b))
        p_l = p[l].reshape(t, PLE)

        hb = mm("in_proj", h_cur, win, mode="nn", out_dtype=f32)
        qn, kn, vs, beta, glog = gdn_pre_fwd(hb, cw8, alog_b, dtb_b)
        g_u, g_w, g_qd, g_kd, g_at, g_el = gdn_intra(qn, kn, vs, beta, glog)
        o_gdn, states = gdn_scan_fwd(g_u, g_w, g_qd, g_kd, g_at, g_el)
        mla_rows = [(hb, CQP, OFF_CQ // CQP), (hb, KVL, OFF_CKV // KVL), (hb, LANE, OFF_KR // LANE), (hb, LANE, OFF_KRS // LANE),
                    (cf, LANE, 0), (sf, LANE, 0)]
        mla_consts = [qg, wq, kvg, wkv]
        qm, km, vm = row_fwd("mla_pre", mla_pre_f, mla_rows, mla_consts, [(1024, bf16), (1024, bf16), (512, bf16)])
        comms = _gathers(W, l, GATHER_LATE) + (_gathers(W, l + 1, GATHER_EARLY) if l + 1 < DEPTH else [])
        res = attn_fwd(qm, km, vm, name=f"attn_fwd_gather{len(comms)}", comms=comms)
        o_mla, lse = res[:2]
        late = dict(zip(GATHER_LATE, res[2:]))
        if l + 1 < DEPTH:
            early[l + 1] = dict(zip(GATHER_EARLY, res[2 + len(GATHER_LATE):]))
        wout, wd = _full_weight(late["w_out"], "w_out"), _full_weight(late["w_down"], "w_down")
        half = N_DEV // 2
        wg, wu = _full_weight(late["w_gate_up"], "w_gate_up", 0, half), _full_weight(late["w_gate_up"], "w_gate_up", half, N_DEV)
        wple, wpg = _full_weight(late["w_ple"], "w_ple"), _full_weight(late["w_ple_gate"], "w_ple_gate")
        p1_rows = [(o_gdn, 512, 0), (hb, 512, OFF_Z // 512), (o_mla, 512, 0), (h_cur, D, 0)]
        p1_consts = [gg_b, wout, g1, b1]
        (x1,) = row_fwd("post1", post1_f, p1_rows, p1_consts, [(D, f32)])
        gate, up, act = ffn_up(x1, wg, wu)
        down = mm("ffn_down", act, wd, mode="nn", out_dtype=f32)
        p2_rows = [(x1, D, 0), (down, D, 0), (p_l, PLE, 0)]
        p2_consts = [g2, b2, wpg, wple]
        (y,) = row_fwd("post2", post2_f, p2_rows, p2_consts, [(D, f32)])
        acts.append(dict(x=h_cur, hb=hb, qn=qn, kn=kn, vs=vs, beta=beta, glog=glog, states=states, o_gdn=o_gdn, qm=qm, km=km,
                         scan=(g_w, g_qd, g_kd, g_at, g_el),
                         vm=vm, o_mla=o_mla, lse=lse, x1=x1, gate=gate, up=up, act=act, win=win, wd=wd, wg=wg, wu=wu,
                         cw8=cw8, alog_b=alog_b, dtb_b=dtb_b, mla_rows=mla_rows, mla_consts=mla_consts, p1_rows=p1_rows,
                         p1_consts=p1_consts, p2_rows=p2_rows, p2_consts=p2_consts))
        h_cur = y

    dy, loss_part = loss_head(h_cur, target)

    G = {n: [None] * DEPTH for n in SMALL}
    S = {n: [None] * DEPTH for n in BIG}
    parts = {n: [None] * DEPTH for n in BIG}
    pending = None
    for l in reversed(range(DEPTH)):
        a = acts[l]
        dx1, ddown, dg2, db2, dwpg, dwple = row_bwd("post2_bwd", post2_f, a["p2_rows"], a["p2_consts"], [dy], [0, 1], [0, 1, 2, 3], [f32, f32])
        dgate, dup = ffn_dact(ddown, a["wd"], a["gate"], a["up"])
        dwd = mm("dw_down", a["act"], ddown, mode="tn", out_dtype=f32)
        dx1 = mm("dx_gate", dgate, a["wg"], mode="nt", out_dtype=f32, add=dx1)
        dx1 = mm("dx_up", dup, a["wu"], mode="nt", out_dtype=f32, add=dx1)
        dwg = mm("dw_gate", a["x1"], dgate, mode="tn", out_dtype=f32)
        dwu = mm("dw_up", a["x1"], dup, mode="tn", out_dtype=f32)
        do_gdn, dz, do_mla, dxr, dgg, dwout, dg1, db1 = row_bwd("post1_bwd", post1_f, a["p1_rows"], a["p1_consts"], [dx1], [0, 1, 2, 3],
                                                              [0, 1, 2, 3], [f32, bf16, f32, f32])
        S["w_down"][l], S["w_ple"][l], S["w_ple_gate"][l] = (_shard_grad(g, n) for g, n in ((dwd, "w_down"), (dwple, "w_ple"), (dwpg, "w_ple_gate")))
        S["w_gate_up"][l] = jnp.concatenate([_shard_grad(dwg, "w_gate_up"), _shard_grad(dwu, "w_gate_up")], axis=0)
        res = attn_bwd(a["qm"], a["km"], a["vm"], a["o_mla"], a["lse"], do_mla, name="attn_bwd_exchange",
                       comms=[(S[n][l], True) for n in GRAD_EARLY])
        dqm, dkm, dvm = res[:3]
        for n, got in zip(GRAD_EARLY, res[3:]):
            parts[n][l] = got
        dstates = gdn_scan_bwd(do_gdn, *a["scan"])
        comms = [(S[n][pending], True) for n in GRAD_LATE] if pending is not None else []
        res = gdn_local_bwd(a["qn"], a["kn"], a["vs"], a["beta"], a["glog"], a["states"], dstates, do_gdn,
                            name="gdn_local_bwd_exchange" if comms else "gdn_local_bwd", comms=comms)
        dqn, dkn, dvs, dbeta, dglog = res[:5]
        for n, got in zip(GRAD_LATE, res[5:]):
            parts[n][pending] = got
        dqkv, dba, dcw, dal, ddt = gdn_pre_bwd(a["hb"], a["cw8"], a["alog_b"], a["dtb_b"], dqn, dkn, dvs, dbeta, dglog)
        dcq, dckv, dkrg, dkrs, dqg, dwq, dkvg, dwkv = row_bwd("mla_pre_bwd", mla_pre_f, a["mla_rows"], a["mla_consts"], [dqm, dkm, dvm],
                                                              [0, 1, 2, 3], [0, 1, 2, 3], [bf16, bf16, bf16, bf16], row_add={2: dba})
        dh = jnp.concatenate([dqkv, dz, dckv, dkrg, dkrs, dcq], axis=1)
        dwin = mm("dw_in", a["x"], dh, mode="tn", out_dtype=f32)
        dy = mm("dx_in", dh, a["win"], mode="nt", out_dtype=f32, add=dxr)

        S["w_in"][l], S["w_uq"][l] = _shard_grad(_unprep_w_in(dwin), "w_in"), _shard_grad(_unprep_w_uq(dwq), "w_uq")
        S["w_ukv"][l], S["w_out"][l] = _shard_grad(dwkv, "w_ukv"), _shard_grad(dwout, "w_out")
        G["conv_w"][l] = dcw[:CW]
        G["a_log"][l] = jnp.sum(dal.reshape(H, DK), axis=1)
        G["dt_bias"][l] = jnp.sum(ddt.reshape(H, DK), axis=1)
        G["gdn_norm_g"][l] = jnp.sum(dgg.reshape(H, DK), axis=0)
        G["q_norm_g"][l] = dqg[0, :QL]
        G["kv_norm_g"][l] = dkvg[0]
        G["ln1_g"][l], G["ln1_b"][l], G["ln2_g"][l], G["ln2_b"][l] = dg1[0], db1[0], dg2[0], db2[0]
        pending = l
    grad_x = dy.reshape(x.shape)

    small_like = {n: W[n] for n in SMALL}
    conv_parts = jnp.stack(G["conv_w"]).reshape(DEPTH, CW, N_DEV, -1)
    smalls = []
    for d in range(N_DEV):
        tree = {n: jnp.stack(G[n]) for n in SMALL if n != "conv_w"}
        tree["conv_w"] = conv_parts[:, :, d, :]
        smalls.append(_pack_small(tree, loss_part))
    small_send = jnp.stack(smalls)
    res = exchange("exchange_last", [(S[n][pending], True) for n in GRAD_LATE] + [(small_send, True)])
    for n, got in zip(GRAD_LATE, res):
        parts[n][pending] = got

    g_out, d_out, m_out, v_out = {}, {}, {}, {}
    for n in BIG:
        shape2 = (DEPTH * shards[n][1], shards[n][2])
        g_out[n] = jnp.stack([sum_parts(parts[n][l]) for l in range(DEPTH)])
        upd = adamw(W[n].reshape(shape2), M[n].reshape(shape2), V[n].reshape(shape2), g_out[n].reshape(shape2))
        d_out[n], m_out[n], v_out[n] = (u.reshape(shards[n]) for u in upd)
    zero = jnp.zeros((), f32)
    g_small = sum_parts(res[-1])
    upd = adamw(_pack_small(W, zero), _pack_small(M, zero), _pack_small(V, zero), g_small)
    for dst, rows_arr in zip((g_out, d_out, m_out, v_out), (g_small,) + tuple(upd)):
        dst.update(_unpack_small(rows_arr, small_like)[0])
    loss = _unpack_small(g_small, small_like)[1]
    order = ["w_in", "conv_w", "a_log", "dt_bias", "gdn_norm_g", "q_norm_g", "w_uq", "kv_norm_g", "w_ukv", "w_out", "ln1_g", "ln1_b",
             "w_gate_up", "w_down", "ln2_g", "ln2_b", "w_ple", "w_ple_gate"]
    return (loss, grad_x, *[g_out[n] for n in order], *[d_out[n] for n in order], *[m_out[n] for n in order],
            *[v_out[n] for n in order])
```

```python
import jax
import jax.numpy as jnp
import numpy as np
from jax import lax
from jax.experimental import pallas as pl
from jax.experimental.pallas import tpu as pltpu

f32 = jnp.float32
bf16 = jnp.bfloat16
MX = jnp.bfloat16

D = 1024
DEPTH = 4
CHUNK = 64
H = 4
DK = 128
PLE = 256
QL = 384
KVL = 256
ROPE = 64
DFF = 2816
IN_W = 2760
ROPE_THETA = 10000.0
ALPHA = (2.0 * DEPTH) ** 0.25
LN_EPS = 1e-5
RMS_EPS = 1e-6
ATT_SCALE = (128 + 64) ** -0.5
N_DEV = 8

ADAM_LR, ADAM_B1, ADAM_B2, ADAM_EPS, ADAM_WD, ADAM_STEP = 0.001, 0.9, 0.999, 1e-08, 0.01, 10

OFF_QKV, OFF_Z, OFF_CKV, OFF_KR, OFF_KRS, OFF_CQ, HP = 0, 1536, 2048, 2304, 2432, 2560, 3072
CQP = 512
LANE = 128
VMEM_LIMIT = 48 * 1024 * 1024
ROW_TB = 256


def _dot(a, b, ca, cb, prec=None):
    if a.ndim == 3:
        return lax.dot_general(a, b, (((ca + 1,), (cb + 1,)), ((0,), (0,))), precision=prec, preferred_element_type=f32)
    return lax.dot_general(a, b, (((ca,), (cb,)), ((), ())), precision=prec, preferred_element_type=f32)


@jax.custom_vjp
def bdot(a, w):
    return _dot(a.astype(MX), w.astype(MX), 1, 0)


def _bdot_fwd(a, w):
    return bdot(a, w), (a, w)


def _bdot_bwd(res, g):
    a, w = res
    gb = g.astype(MX)
    return _dot(gb, w.astype(MX), 1, 1).astype(a.dtype), _dot(a.astype(MX), gb, 0, 0).astype(w.dtype)


bdot.defvjp(_bdot_fwd, _bdot_bwd)


@jax.custom_vjp
def bdot_nt(a, b):
    return _dot(a.astype(MX), b.astype(MX), 1, 1)


def _bdot_nt_fwd(a, b):
    return bdot_nt(a, b), (a, b)


def _bdot_nt_bwd(res, g):
    a, b = res
    gb = g.astype(MX)
    return _dot(gb, b.astype(MX), 1, 0).astype(a.dtype), _dot(gb, a.astype(MX), 0, 0).astype(b.dtype)


bdot_nt.defvjp(_bdot_nt_fwd, _bdot_nt_bwd)


@jax.custom_vjp
def bdot_tn(a, b):
    return _dot(a.astype(MX), b.astype(MX), 0, 0)


def _bdot_tn_fwd(a, b):
    return bdot_tn(a, b), (a, b)


def _bdot_tn_bwd(res, g):
    a, b = res
    gb = g.astype(MX)
    return _dot(b.astype(MX), gb, 1, 1).astype(a.dtype), _dot(a.astype(MX), gb, 1, 0).astype(b.dtype)


bdot_tn.defvjp(_bdot_tn_fwd, _bdot_tn_bwd)


@jax.custom_vjp
def hdot(a, b):
    return _dot(a, b, 1, 0, lax.Precision.HIGHEST)


def _hdot_fwd(a, b):
    return hdot(a, b), (a, b)


def _hdot_bwd(res, g):
    a, b = res
    return _dot(g, b, 1, 1, lax.Precision.HIGHEST), _dot(a, g, 0, 0, lax.Precision.HIGHEST)


hdot.defvjp(_hdot_fwd, _hdot_bwd)


def _pick(n, cap):
    best = None
    for t in range(LANE, min(n, cap) + 1, LANE):
        if n % t == 0:
            best = t
    assert best is not None, (n, cap)
    return best


def _cparams(sem):
    return pltpu.CompilerParams(dimension_semantics=sem, vmem_limit_bytes=VMEM_LIMIT)


def mm(name, a, b, *, mode, out_dtype, add=None):
    if mode == "tn":
        kdim, m = a.shape
        n = b.shape[1]
        tm, tn, tk = _pick(m, 1536), _pick(n, 1536), min(512, kdim)
        nk = kdim // tk

        def body(a_ref, b_ref, o_ref, acc):
            k = pl.program_id(2)

            @pl.when(k == 0)
            def _():
                acc[...] = jnp.zeros_like(acc)

            acc[...] += _dot(a_ref[...].astype(MX), b_ref[...].astype(MX), 0, 0)

            @pl.when(k == nk - 1)
            def _():
                o_ref[...] = acc[...].astype(o_ref.dtype)

        return pl.pallas_call(
            body, name=name, grid=(m // tm, n // tn, nk),
            in_specs=[pl.BlockSpec((tk, tm), lambda i, j, k: (k, i)), pl.BlockSpec((tk, tn), lambda i, j, k: (k, j))],
            out_specs=pl.BlockSpec((tm, tn), lambda i, j, k: (i, j)),
            out_shape=jax.ShapeDtypeStruct((m, n), out_dtype),
            scratch_shapes=[pltpu.VMEM((tm, tn), f32)],
            compiler_params=_cparams(("parallel", "parallel", "arbitrary")),
        )(a, b)

    m, kdim = a.shape
    n = b.shape[1] if mode == "nn" else b.shape[0]
    tm, tn = min(512, m), _pick(n, 1536)
    has_add = add is not None

    def body(*refs):
        a_ref, b_ref = refs[0], refs[1]
        o_ref = refs[-1]
        r = _dot(a_ref[...].astype(MX), b_ref[...].astype(MX), 1, 0 if mode == "nn" else 1)
        if has_add:
            r = r + refs[2][...].astype(f32)
        o_ref[...] = r.astype(o_ref.dtype)

    b_spec = (pl.BlockSpec((kdim, tn), lambda i, j: (0, j)) if mode == "nn"
              else pl.BlockSpec((tn, kdim), lambda i, j: (j, 0)))
    in_specs = [pl.BlockSpec((tm, kdim), lambda i, j: (i, 0)), b_spec]
    args = [a, b]
    if has_add:
        in_specs.append(pl.BlockSpec((tm, tn), lambda i, j: (i, j)))
        args.append(add)
    return pl.pallas_call(
        body, name=name, grid=(m // tm, n // tn), in_specs=in_specs,
        out_specs=pl.BlockSpec((tm, tn), lambda i, j: (i, j)),
        out_shape=jax.ShapeDtypeStruct((m, n), out_dtype),
        compiler_params=_cparams(("parallel", "parallel")),
    )(*args)


def _row_spec(tb, w, cb):
    return pl.BlockSpec((tb, w), lambda i: (i, cb))


def _full_spec(shape):
    return pl.BlockSpec(shape, lambda i: (0,) * len(shape))


def row_fwd(name, f, rows, consts, outs):
    t = rows[0][0].shape[0]
    tb = min(ROW_TB, t)
    nr, nc = len(rows), len(consts)

    def body(*refs):
        vals = [r[...] for r in refs[:nr + nc]]
        res = f(*vals)
        for o_ref, val in zip(refs[nr + nc:], res):
            o_ref[...] = val.astype(o_ref.dtype)

    return pl.pallas_call(
        body, name=name, grid=(t // tb,),
        in_specs=[_row_spec(tb, w, cb) for _, w, cb in rows] + [_full_spec(c.shape) for c in consts],
        out_specs=[_row_spec(tb, w, 0) for w, _ in outs],
        out_shape=[jax.ShapeDtypeStruct((t, w), dt) for w, dt in outs],
        compiler_params=_cparams(("parallel",)),
    )(*[r[0] for r in rows], *consts)


def row_bwd(name, f, rows, consts, cots, row_diff, const_diff, drow_dtypes, row_add=None):
    t = rows[0][0].shape[0]
    tb = min(ROW_TB, t)
    nr, nc, nct = len(rows), len(consts), len(cots)
    row_add = row_add or {}
    add_keys = sorted(row_add)
    n_in = nr + nc + nct + len(add_keys)

    def body(*refs):
        i = pl.program_id(0)
        rv = [r[...] for r in refs[:nr]]
        cv = [refs[nr + k][...].astype(f32) if k in const_diff else refs[nr + k][...] for k in range(nc)]
        cot_refs = refs[nr + nc:nr + nc + nct]
        add_refs = refs[nr + nc + nct:n_in]
        drow_refs = refs[n_in:n_in + len(row_diff)]
        dconst_refs = refs[n_in + len(row_diff):]

        def g(*dv):
            r2, c2 = list(rv), list(cv)
            for p, k in enumerate(row_diff):
                r2[k] = dv[p]
            for p, k in enumerate(const_diff):
                c2[k] = dv[len(row_diff) + p]
            return tuple(f(*r2, *c2))

        prim = [rv[k].astype(f32) for k in row_diff] + [cv[k] for k in const_diff]
        outs, vf = jax.vjp(g, *prim)
        grads = vf(tuple(c[...].astype(o.dtype) for c, o in zip(cot_refs, outs)))
        for p, ref in enumerate(drow_refs):
            val = grads[p]
            if p in row_add:
                val = val + add_refs[add_keys.index(p)][...].astype(f32)
            ref[...] = val.astype(ref.dtype)

        @pl.when(i == 0)
        def _():
            for ref in dconst_refs:
                ref[...] = jnp.zeros_like(ref)

        for p, ref in enumerate(dconst_refs):
            ref[...] += grads[len(row_diff) + p]

    widths = [rows[k][1] for k in row_diff]
    return pl.pallas_call(
        body, name=name, grid=(t // tb,),
        in_specs=([_row_spec(tb, w, cb) for _, w, cb in rows] + [_full_spec(c.shape) for c in consts]
                  + [_row_spec(tb, c.shape[1], 0) for c in cots] + [_row_spec(tb, row_add[k].shape[1], 0) for k in add_keys]),
        out_specs=([_row_spec(tb, w, 0) for w in widths] + [_full_spec(consts[k].shape) for k in const_diff]),
        out_shape=([jax.ShapeDtypeStruct((t, w), dt) for w, dt in zip(widths, drow_dtypes)]
                   + [jax.ShapeDtypeStruct(consts[k].shape, f32) for k in const_diff]),
        compiler_params=_cparams(("arbitrary",)),
    )(*[r[0] for r in rows], *consts, *cots, *[row_add[k] for k in add_keys])


def _heads(x, w=DK):
    return [x[:, w * h:w * (h + 1)] for h in range(H)]


def _layer_norm(r, g, b):
    mu = jnp.mean(r, -1, keepdims=True)
    var = jnp.mean(jnp.square(r - mu), -1, keepdims=True)
    return (r - mu) * lax.rsqrt(var + LN_EPS) * g + b


def gdn_point_f(c, ba, alog_b, dtb_b):
    s = c * jax.nn.sigmoid(c)
    q, k, v = s[:, :512], s[:, 512:1024], s[:, 1024:]

    def l2(x):
        return jnp.concatenate([xh * lax.rsqrt(jnp.sum(xh * xh, -1, keepdims=True) + RMS_EPS) for xh in _heads(x)], axis=1)

    tb = c.shape[0]
    b_b = jnp.concatenate([jnp.broadcast_to(ba[:, 64 + h:65 + h], (tb, DK)) for h in range(H)], axis=1)
    a_b = jnp.concatenate([jnp.broadcast_to(ba[:, 68 + h:69 + h], (tb, DK)) for h in range(H)], axis=1)
    beta = jax.nn.sigmoid(b_b)
    g = -jnp.exp(alog_b) * jax.nn.softplus(a_b + dtb_b)
    return l2(q), l2(k), v, beta, g


def mla_pre_f(cq, ckv, krg, krs, cf, sf, qg, wq, kvg, wkv):
    cqn = cq * lax.rsqrt(jnp.sum(cq * cq, -1, keepdims=True) * (1.0 / QL) + RMS_EPS) * qg
    qa = bdot(cqn, wq)
    ckvn = ckv * lax.rsqrt(jnp.mean(ckv * ckv, -1, keepdims=True) + RMS_EPS) * kvg
    kv = bdot(ckvn, wkv)
    kro = krg * cf + krs * sf
    qs, ks, vs = [], [], []
    for h in range(H):
        qs += [qa[:, DK * h:DK * (h + 1)], qa[:, 512 + DK * h:512 + DK * (h + 1)] * cf + qa[:, 1024 + DK * h:1024 + DK * (h + 1)] * sf]
        ks += [kv[:, 256 * h:256 * h + DK], kro]
        vs += [kv[:, 256 * h + DK:256 * (h + 1)]]
    return jnp.concatenate(qs, axis=1) * ATT_C2, jnp.concatenate(ks, axis=1), jnp.concatenate(vs, axis=1)


def post1_f(o, z, omla, x, gg_b, wout, g1, b1):
    on = jnp.concatenate([oh * lax.rsqrt(jnp.mean(oh * oh, -1, keepdims=True) + RMS_EPS) for oh in _heads(o)], axis=1) * gg_b
    ogdn = on * (z * jax.nn.sigmoid(z))
    mix = bdot(jnp.concatenate([ogdn, omla], axis=1), wout)
    return (_layer_norm(ALPHA * x + mix, g1, b1),)


def post2_f(x1, down, p, g2, b2, wpg, wple):
    x2 = _layer_norm(ALPHA * x1 + down, g2, b2)
    return (x2 + jax.nn.sigmoid(bdot(x2, wpg)) * bdot(p, wple),)


HALO = 8
CW = 4


def _conv_from_scratch(xs, cw_ref, tb):
    c = xs[pl.ds(HALO - 3, tb), :] * cw_ref[0:1, :]
    for j in range(1, CW):
        c = c + xs[pl.ds(HALO - 3 + j, tb), :] * cw_ref[j:j + 1, :]
    return c


def gdn_pre_fwd(hbuf, conv_w8, alog_b, dtb_b):
    t = hbuf.shape[0]
    tb = min(ROW_TB, t)

    def body(x_ref, halo_ref, ba_ref, cw_ref, al_ref, dt_ref, q_ref, k_ref, v_ref, be_ref, g_ref, xs):
        i = pl.program_id(0)
        xs[pl.ds(0, HALO), :] = jnp.where(i == 0, 0.0, halo_ref[...])
        xs[pl.ds(HALO, tb), :] = x_ref[...]
        c = _conv_from_scratch(xs, cw_ref, tb)
        q, k, v, be, g = gdn_point_f(c, ba_ref[...], al_ref[...], dt_ref[...])
        q_ref[...], k_ref[...], v_ref[...], be_ref[...], g_ref[...] = q, k, v, be, g

    return pl.pallas_call(
        body, name="gdn_pre_fwd", grid=(t // tb,),
        in_specs=[_row_spec(tb, 1536, 0),
                  pl.BlockSpec((HALO, 1536), lambda i: (jnp.maximum(i * (tb // HALO) - 1, 0), 0)),
                  _row_spec(tb, LANE, OFF_KR // LANE),
                  _full_spec(conv_w8.shape), _full_spec(alog_b.shape), _full_spec(dtb_b.shape)],
        out_specs=[_row_spec(tb, 512, 0)] * 5,
        out_shape=[jax.ShapeDtypeStruct((t, 512), f32)] * 5,
        scratch_shapes=[pltpu.VMEM((tb + HALO, 1536), f32)],
        compiler_params=_cparams(("arbitrary",)),
    )(hbuf, hbuf, hbuf, conv_w8, alog_b, dtb_b)


def gdn_pre_bwd(hbuf, conv_w8, alog_b, dtb_b, dq, dk, dv, dbe, dg):
    t = hbuf.shape[0]
    tb = min(ROW_TB, t)
    n = t // tb

    def body(x_ref, halo_ref, ba_ref, cw_ref, al_ref, dt_ref, dq_ref, dk_ref, dv_ref, dbe_ref, dg_ref,
             dx_ref, dba_ref, dcw_ref, dal_ref, ddt_ref, xs, dcs):
        s = pl.program_id(0)
        i = n - 1 - s
        xs[pl.ds(0, HALO), :] = jnp.where(i == 0, 0.0, halo_ref[...])
        xs[pl.ds(HALO, tb), :] = x_ref[...]
        c = _conv_from_scratch(xs, cw_ref, tb)
        _, vf = jax.vjp(gdn_point_f, c, ba_ref[...], al_ref[...], dt_ref[...])
        dc, dba, dal, ddt = vf((dq_ref[...], dk_ref[...], dv_ref[...], dbe_ref[...], dg_ref[...]))

        @pl.when(s == 0)
        def _():
            dcs[pl.ds(tb, HALO), :] = jnp.zeros((HALO, 1536), f32)
            dcw_ref[...] = jnp.zeros_like(dcw_ref)
            dal_ref[...] = jnp.zeros_like(dal_ref)
            ddt_ref[...] = jnp.zeros_like(ddt_ref)

        @pl.when(s > 0)
        def _():
            dcs[pl.ds(tb, HALO), :] = dcs[pl.ds(0, HALO), :]

        dcs[pl.ds(0, tb), :] = dc
        dx = dcs[pl.ds(3, tb), :] * cw_ref[0:1, :]
        for j in range(1, CW):
            dx = dx + dcs[pl.ds(3 - j, tb), :] * cw_ref[j:j + 1, :]
        dx_ref[...] = dx.astype(dx_ref.dtype)
        dba_ref[...] = dba
        for j in range(CW):
            dcw_ref[j:j + 1, :] += jnp.sum(dc * xs[pl.ds(HALO - 3 + j, tb), :], axis=0, keepdims=True)
        dal_ref[...] += dal
        ddt_ref[...] += ddt

    rev = lambda cb: (lambda s: (n - 1 - s, cb))
    return pl.pallas_call(
        body, name="gdn_pre_bwd", grid=(n,),
        in_specs=[pl.BlockSpec((tb, 1536), rev(0)),
                  pl.BlockSpec((HALO, 1536), lambda s: (jnp.maximum((n - 1 - s) * (tb // HALO) - 1, 0), 0)),
                  pl.BlockSpec((tb, LANE), rev(OFF_KR // LANE)),
                  _full_spec(conv_w8.shape), _full_spec(alog_b.shape), _full_spec(dtb_b.shape)]
        + [pl.BlockSpec((tb, 512), rev(0))] * 5,
        out_specs=[pl.BlockSpec((tb, 1536), rev(0)), pl.BlockSpec((tb, LANE), rev(0)),
                   _full_spec(conv_w8.shape), _full_spec(alog_b.shape), _full_spec(dtb_b.shape)],
        out_shape=[jax.ShapeDtypeStruct((t, 1536), bf16), jax.ShapeDtypeStruct((t, LANE), f32),
                   jax.ShapeDtypeStruct(conv_w8.shape, f32), jax.ShapeDtypeStruct(alog_b.shape, f32),
                   jax.ShapeDtypeStruct(dtb_b.shape, f32)],
        scratch_shapes=[pltpu.VMEM((tb + HALO, 1536), f32), pltpu.VMEM((tb + HALO, 1536), f32)],
        compiler_params=_cparams(("arbitrary",)),
    )(hbuf, hbuf, hbuf, conv_w8, alog_b, dtb_b, dq, dk, dv, dbe, dg)


GDN_NB = 4
GDN_SCAN_NB = 8


def _to_batch(x, rows=CHUNK):
    n = x.shape[0] // rows
    return jnp.concatenate([x[rows * ci:rows * (ci + 1), DK * h:DK * (h + 1)][None] for ci in range(n) for h in range(H)], axis=0)


def _from_batch(y):
    n = y.shape[0] // H
    return jnp.concatenate([jnp.concatenate([y[ci * H + h] for h in range(H)], axis=1) for ci in range(n)], axis=0)


def gdn_intra_b(q, k, v, beta, g):
    c = CHUNK
    n = q.shape[0] // c
    tri = jnp.where(lax.broadcasted_iota(jnp.int32, (c, c), 0) >= lax.broadcasted_iota(jnp.int32, (c, c), 1), 1.0, 0.0)
    gc = _to_batch(jnp.concatenate([hdot(tri, g[c * ci:c * (ci + 1)]) for ci in range(n)], axis=0))
    qb, kb, vb, bb = _to_batch(q) * DK ** -0.5, _to_batch(k), _to_batch(v), _to_batch(beta)
    nbat = qb.shape[0]
    row = lax.broadcasted_iota(jnp.int32, (1, c, DK), 1)
    col = lax.broadcasted_iota(jnp.int32, (1, c, DK), 2)
    incl, strict, eye = row >= col, row > col, row == col
    grow = hdot(jnp.ones((nbat, c, c), f32), jnp.where(eye, gc, 0.0))
    decay = jnp.where(incl, jnp.exp(jnp.where(incl, gc - grow, 0.0)), 0.0)
    kbeta = kb * bb
    kpad = jnp.concatenate([kb, jnp.zeros((nbat, DK - c, DK), f32)], axis=1)
    a = jnp.where(strict, bdot_nt(kbeta, kpad) * decay, 0.0)[:, :, :c]
    nn = -a
    bk = bdot(a, a)
    for step in range(5):
        nn = nn + bk + bdot(nn, bk)
        if step < 4:
            bk = bdot(bk, bk)
    eg = jnp.exp(gc)
    rhs_v, rhs_k = vb * bb, kbeta * eg
    g_last = gc[:, c - 1:c, :]
    u = rhs_v + bdot(nn, rhs_v)
    w = rhs_k + bdot(nn, rhs_k)
    attn = jnp.where(incl, bdot_nt(qb, kpad) * decay, 0.0)
    return u, w, qb * eg, kb * jnp.exp(g_last - gc), attn, jnp.exp(g_last)


def gdn_scan_b(u, w, qd, kd, attn, el, state):
    v_new = u - bdot(w, state)
    return bdot(qd, state) + bdot(attn[:, :, :CHUNK], v_new), state * el + bdot_tn(kd, v_new)


def gdn_scan_bwd_b(do, w, qd, kd, attn, el, dstate):
    dvn = bdot_tn(attn[:, :, :CHUNK], do) + bdot(kd, dstate)
    return bdot_tn(qd, do) + dstate * el - bdot_tn(w, dvn)


def gdn_chunks_f(q, k, v, beta, g, states):
    n = states.shape[0]
    o, s_new = gdn_scan_b(*gdn_intra_b(q, k, v, beta, g), states.reshape(n * H, DK, DK))
    return _from_batch(o), s_new.reshape(n, H * DK, DK)


def _chunk_rows(cidx):
    return pl.ds(cidx * CHUNK, CHUNK)


def gdn_intra(q, k, v, beta, g):
    t = q.shape[0]
    nb = min(GDN_NB, t // CHUNK)
    tb = nb * CHUNK

    def body(q_ref, k_ref, v_ref, be_ref, g_ref, u_ref, w_ref, qd_ref, kd_ref, at_ref, el_ref):
        u, w, qd, kd, at, el = gdn_intra_b(q_ref[...], k_ref[...], v_ref[...], be_ref[...], g_ref[...])
        u_ref[...], w_ref[...], qd_ref[...], kd_ref[...], at_ref[...] = (_from_batch(a) for a in (u, w, qd, kd, at))
        el_ref[...] = _from_batch(jnp.broadcast_to(el, (nb * H, 8, DK))).reshape(nb, 8, H * DK)

    return pl.pallas_call(
        body, name="gdn_intra", grid=(t // tb,),
        in_specs=[_row_spec(tb, 512, 0)] * 5,
        out_specs=[_row_spec(tb, 512, 0)] * 5 + [pl.BlockSpec((nb, 8, 512), lambda i: (i, 0, 0))],
        out_shape=[jax.ShapeDtypeStruct((t, 512), f32)] * 5 + [jax.ShapeDtypeStruct((t // CHUNK, 8, 512), f32)],
        compiler_params=_cparams(("parallel",)),
    )(q, k, v, beta, g)


def gdn_scan_fwd(u, w, qd, kd, attn, el):
    t = u.shape[0]
    nb = min(GDN_SCAN_NB, t // CHUNK)
    tb = nb * CHUNK

    def body(u_ref, w_ref, qd_ref, kd_ref, at_ref, el_ref, o_ref, s_ref, state):
        @pl.when(pl.program_id(0) == 0)
        def _():
            state[...] = jnp.zeros_like(state)

        for cidx in range(nb):
            r = _chunk_rows(cidx)
            s_ref[cidx] = state[...]
            ins = [_to_batch(ref[r, :]) for ref in (u_ref, w_ref, qd_ref, kd_ref, at_ref)]
            el = _to_batch(el_ref[cidx], 8)[:, 0:1, :]
            o, s_new = gdn_scan_b(*ins, el, state[...].reshape(H, DK, DK))
            o_ref[r, :] = _from_batch(o)
            state[...] = s_new.reshape(H * DK, DK)

    return pl.pallas_call(
        body, name="gdn_scan_fwd", grid=(t // tb,),
        in_specs=[_row_spec(tb, 512, 0)] * 5 + [pl.BlockSpec((nb, 8, 512), lambda i: (i, 0, 0))],
        out_specs=[_row_spec(tb, 512, 0), pl.BlockSpec((nb, 512, DK), lambda i: (i, 0, 0))],
        out_shape=[jax.ShapeDtypeStruct((t, 512), f32), jax.ShapeDtypeStruct((t // CHUNK, 512, DK), f32)],
        scratch_shapes=[pltpu.VMEM((512, DK), f32)],
        compiler_params=_cparams(("arbitrary",)),
    )(u, w, qd, kd, attn, el)


def gdn_scan_bwd(do, w, qd, kd, attn, el):
    t = do.shape[0]
    nb = min(GDN_SCAN_NB, t // CHUNK)
    tb = nb * CHUNK
    n = t // tb

    def body(do_ref, w_ref, qd_ref, kd_ref, at_ref, el_ref, ds_ref, dstate):
        @pl.when(pl.program_id(0) == 0)
        def _():
            dstate[...] = jnp.zeros_like(dstate)

        for cidx in reversed(range(nb)):
            r = _chunk_rows(cidx)
            ds_ref[cidx] = dstate[...]
            ins = [_to_batch(ref[r, :]) for ref in (do_ref, w_ref, qd_ref, kd_ref, at_ref)]
            el = _to_batch(el_ref[cidx], 8)[:, 0:1, :]
            dstate[...] = gdn_scan_bwd_b(*ins, el, dstate[...].reshape(H, DK, DK)).reshape(H * DK, DK)

    rev = pl.BlockSpec((tb, 512), lambda s: (n - 1 - s, 0))
    return pl.pallas_call(
        body, name="gdn_scan_bwd", grid=(n,),
        in_specs=[rev] * 5 + [pl.BlockSpec((nb, 8, 512), lambda s: (n - 1 - s, 0, 0))],
        out_specs=pl.BlockSpec((nb, 512, DK), lambda s: (n - 1 - s, 0, 0)),
        out_shape=jax.ShapeDtypeStruct((t // CHUNK, 512, DK), f32),
        scratch_shapes=[pltpu.VMEM((512, DK), f32)],
        compiler_params=_cparams(("arbitrary",)),
    )(do, w, qd, kd, attn, el)


def gdn_local_bwd(q, k, v, beta, g, states, dstates, do, name="gdn_local_bwd", comms=()):
    t = q.shape[0]
    nb = min(GDN_NB, t // CHUNK)
    tb = nb * CHUNK
    n = t // tb

    def body(*refs):
        ins, outs, _, comm_refs = _split_hosted(refs, 8, 5, 0, len(comms))
        q_ref, k_ref, v_ref, be_ref, g_ref, s_ref, ds_ref, do_ref = ins
        dq_ref, dk_ref, dv_ref, dbe_ref, dg_ref = outs
        _host_exchanges(comm_refs, comms, pl.program_id(0) == 0, pl.program_id(0) == n - 1)
        states_v = s_ref[...]
        _, vf = jax.vjp(lambda *a: gdn_chunks_f(*a, states_v), q_ref[...], k_ref[...], v_ref[...], be_ref[...], g_ref[...])
        dq_ref[...], dk_ref[...], dv_ref[...], dbe_ref[...], dg_ref[...] = vf((do_ref[...], ds_ref[...]))

    st = pl.BlockSpec((nb, 512, DK), lambda i: (i, 0, 0))
    in_specs = [_row_spec(tb, 512, 0)] * 5 + [st, st, _row_spec(tb, 512, 0)]
    out_specs = [_row_spec(tb, 512, 0)] * 5
    out_shape = [jax.ShapeDtypeStruct((t, 512), f32)] * 5
    scratch, args = [], [q, k, v, beta, g, states, dstates, do]
    _add_hosted(comms, in_specs, out_specs, out_shape, scratch, args)
    return pl.pallas_call(body, name=name, grid=(n,), in_specs=in_specs, out_specs=out_specs, out_shape=out_shape,
                          scratch_shapes=scratch, compiler_params=_cparams(("arbitrary",) if comms else ("parallel",)))(*args)


NEG = -1e30


LOG2E = 1.4426950408889634
ATT_C2 = ATT_SCALE * LOG2E
ATT_RB = 256


def _add_hosted(comms, in_specs, out_specs, out_shape, scratch, args):
    for src, scatter in comms:
        in_specs.append(_ANY)
        out_specs.append(_ANY)
        out_shape.append(_comm_out_shape(src, scatter))
        scratch += _comm_scratch()
        args.append(src)


def _split_hosted(refs, n_in, n_out, n_scratch, n_comm):
    ins, srcs = refs[:n_in], refs[n_in:n_in + n_comm]
    o0 = n_in + n_comm
    outs, dsts = refs[o0:o0 + n_out], refs[o0 + n_out:o0 + n_out + n_comm]
    rest = refs[o0 + n_out + n_comm:]
    sems = rest[n_scratch:]
    return ins, outs, rest[:n_scratch], [(srcs[c], dsts[c]) + tuple(sems[3 * c:3 * c + 3]) for c in range(n_comm)]


def _host_exchanges(comm_refs, comms, is_first, is_last):
    if not comms:
        return

    @pl.when(is_first)
    def _():
        for refs, (_, scatter) in zip(comm_refs, comms):
            for op in _exchange_copies(*refs, scatter):
                op.start()

    @pl.when(is_last)
    def _():
        for refs, (_, scatter) in zip(comm_refs, comms):
            for op in _exchange_copies(*refs, scatter):
                op.wait()


def attn_fwd(q, k, v, name="attn_fwd", comms=()):
    t = q.shape[0]
    tq = tk = min(1024, t)
    rb = min(ATT_RB, tq)
    nq, nk, nr = t // tq, t // tk, tq // rb
    last = lambda i: i

    def body(*refs):
        (q_ref, k_ref, v_ref), (o_ref, lse_ref), (m_s, acc), comm_refs = _split_hosted(refs, 3, 2, 2, len(comms))
        h, i, j = pl.program_id(0), pl.program_id(1), pl.program_id(2)
        _host_exchanges(comm_refs, comms, (h == 0) & (i == 0) & (j == 0), (h == H - 1) & (i == nq - 1) & (j == nk - 1))

        @pl.when(j == 0)
        def _():
            m_s[...] = jnp.full_like(m_s, NEG)
            acc[...] = jnp.zeros_like(acc)

        one_col = jnp.where(lax.broadcasted_iota(jnp.int32, (tk, DK), 1) == 0, 1.0, 0.0).astype(MX)
        vx = jnp.concatenate([v_ref[...], one_col], axis=1)

        def rows_of(r):
            return pl.ds(r * rb, rb)

        def soft(r, s):
            m_old = m_s[rows_of(r), :]
            m_new = jnp.maximum(m_old, jnp.max(s, axis=-1, keepdims=True))
            m_s[rows_of(r), :] = m_new
            return jnp.exp2(s - m_new).astype(MX), jnp.exp2(m_old - m_new)

        def pv(r, p, alpha, vals):
            acc[rows_of(r), :] = alpha * acc[rows_of(r), :] + _dot(p, vals, 1, 0)

        def whole_block():
            kb = k_ref[...]
            ss = [_dot(q_ref[rows_of(r), :], kb, 1, 1) for r in range(min(2, nr))]
            pend = None
            for r in range(nr):
                if r + 2 < nr:
                    ss.append(_dot(q_ref[rows_of(r + 2), :], kb, 1, 1))
                p, alpha = soft(r, ss[r])
                if pend is not None:
                    pv(*pend, vx)
                pend = (r, p, alpha)
            pv(*pend, vx)

        def diagonal_block():
            for r in range(nr):
                cols = (r + 1) * rb
                s = _dot(q_ref[rows_of(r), :], k_ref[pl.ds(0, cols), :], 1, 1)
                qrow = lax.broadcasted_iota(jnp.int32, (rb, cols), 0) + r * rb
                kcol = lax.broadcasted_iota(jnp.int32, (rb, cols), 1)
                s = jnp.where(jnp.right_shift(kcol, 6) <= jnp.right_shift(qrow, 6), s, NEG)
                p, alpha = soft(r, s)
                pv(r, p, alpha, vx[:cols])

        pl.when(j < i)(whole_block)
        pl.when(j == i)(diagonal_block)

        @pl.when(j == nk - 1)
        def _():
            a = acc[...]
            l = a[:, DK:DK + 1]
            o_ref[...] = a[:, :DK] / l
            lse_ref[0] = m_s[...] + jnp.log2(l)

    in_specs = [pl.BlockSpec((tq, 256), lambda h, i, j: (i, h)),
                pl.BlockSpec((tk, 256), lambda h, i, j: (jnp.minimum(j, last(i)), h)),
                pl.BlockSpec((tk, DK), lambda h, i, j: (jnp.minimum(j, last(i)), h))]
    out_specs = [pl.BlockSpec((tq, DK), lambda h, i, j: (i, h)), pl.BlockSpec((1, tq, 1), lambda h, i, j: (h, i, 0))]
    out_shape = [jax.ShapeDtypeStruct((t, 512), f32), jax.ShapeDtypeStruct((H, t, 1), f32)]
    scratch = [pltpu.VMEM((tq, 1), f32), pltpu.VMEM((tq, 2 * DK), f32)]
    args = [q, k, v]
    _add_hosted(comms, in_specs, out_specs, out_shape, scratch, args)
    sem = ("arbitrary",) * 3 if comms else ("parallel", "parallel", "arbitrary")
    return pl.pallas_call(body, name=name, grid=(H, nq, nk), in_specs=in_specs, out_specs=out_specs, out_shape=out_shape,
                          scratch_shapes=scratch, compiler_params=_cparams(sem))(*args)


def attn_bwd(q, k, v, o, lse, do, name="attn_bwd", comms=()):
    t = q.shape[0]
    tq = tk = min(1024, t)
    rb = min(ATT_RB, tq)
    nq, nk, nr = t // tq, t // tk, tq // rb
    first = lambda j: j

    def body(*refs):
        ins, outs, (dk_acc, dv_acc), comm_refs = _split_hosted(refs, 6, 3, 2, len(comms))
        q_ref, k_ref, v_ref, o_ref, lse_ref, do_ref = ins
        dq_ref, dk_ref, dv_ref = outs
        h, j, i = pl.program_id(0), pl.program_id(1), pl.program_id(2)
        _host_exchanges(comm_refs, comms, (h == 0) & (i == 0) & (j == 0), (h == H - 1) & (i == nq - 1) & (j == nk - 1))

        @pl.when((j == 0) & (i == 0))
        def _():
            dq_ref[...] = jnp.zeros_like(dq_ref)

        @pl.when(i == 0)
        def _():
            dk_acc[...] = jnp.zeros_like(dk_acc)
            dv_acc[...] = jnp.zeros_like(dv_acc)

        def rows_of(r):
            return pl.ds(r * rb, rb)

        def front(r, cols):
            qb, dob = q_ref[rows_of(r), :], do_ref[rows_of(r), :]
            return qb, dob, _dot(qb, k_ref[pl.ds(0, cols), :], 1, 1), _dot(dob.astype(MX), v_ref[pl.ds(0, cols), :], 1, 1)

        def middle(r, dob, s, dp):
            p = jnp.exp2(s - lse_ref[0, rows_of(r), :])
            delta = jnp.sum(dob * o_ref[rows_of(r), :], axis=-1, keepdims=True)
            return p.astype(MX), (p * (dp - delta)).astype(MX)

        def back(r, cols, qb, dob, pb, ds):
            dv_acc[pl.ds(0, cols), :] += _dot(pb, dob.astype(MX), 0, 0)
            dk_acc[pl.ds(0, cols), :] += _dot(ds, qb, 0, 0) * (1.0 / LOG2E)
            grow = pl.ds(pl.multiple_of(i * tq + r * rb, rb), rb)
            dq_ref[grow, :] += _dot(ds, k_ref[pl.ds(0, cols), :], 1, 0) * (1.0 / LOG2E)

        def whole_block():
            nxt = front(0, tk)
            for r in range(nr):
                qb, dob, s, dp = nxt
                if r + 1 < nr:
                    nxt = front(r + 1, tk)
                back(r, tk, qb, dob, *middle(r, dob, s, dp))

        def diagonal_block():
            for r in range(nr):
                cols = (r + 1) * rb
                qb, dob, s, dp = front(r, cols)
                qrow = lax.broadcasted_iota(jnp.int32, (rb, cols), 0) + r * rb
                kcol = lax.broadcasted_iota(jnp.int32, (rb, cols), 1)
                s = jnp.where(jnp.right_shift(kcol, 6) <= jnp.right_shift(qrow, 6), s, NEG)
                back(r, cols, qb, dob, *middle(r, dob, s, dp))

        pl.when(i > j)(whole_block)
        pl.when(i == j)(diagonal_block)

        @pl.when(i == nq - 1)
        def _():
            dk_ref[...] = dk_acc[...]
            dv_ref[...] = dv_acc[...]

    qi = lambda h, j, i: (jnp.maximum(i, first(j)), h)
    in_specs = [pl.BlockSpec((tq, 256), qi),
                pl.BlockSpec((tk, 256), lambda h, j, i: (j, h)),
                pl.BlockSpec((tk, DK), lambda h, j, i: (j, h)),
                pl.BlockSpec((tq, DK), qi),
                pl.BlockSpec((1, tq, 1), lambda h, j, i: (h, jnp.maximum(i, first(j)), 0)),
                pl.BlockSpec((tq, DK), qi)]
    out_specs = [pl.BlockSpec((t, 256), lambda h, j, i: (0, h)),
                 pl.BlockSpec((tk, 256), lambda h, j, i: (j, h)),
                 pl.BlockSpec((tk, DK), lambda h, j, i: (j, h))]
    out_shape = [jax.ShapeDtypeStruct((t, 1024), f32), jax.ShapeDtypeStruct((t, 1024), f32), jax.ShapeDtypeStruct((t, 512), f32)]
    scratch = [pltpu.VMEM((tk, 256), f32), pltpu.VMEM((tk, DK), f32)]
    args = [q, k, v, o, lse, do]
    _add_hosted(comms, in_specs, out_specs, out_shape, scratch, args)
    return pl.pallas_call(body, name=name, grid=(H, nk, nq), in_specs=in_specs, out_specs=out_specs, out_shape=out_shape,
                          scratch_shapes=scratch, compiler_params=_cparams(("arbitrary",) * 3))(*args)


def ffn_up(x1, wg, wu):
    t = x1.shape[0]
    tm, tn = min(512, t), _pick(DFF, 1536)

    def body(x_ref, wg_ref, wu_ref, g_ref, u_ref, a_ref):
        xb = x_ref[...].astype(MX)
        g = _dot(xb, wg_ref[...], 1, 0)
        u = _dot(xb, wu_ref[...], 1, 0)
        g_ref[...] = g.astype(g_ref.dtype)
        u_ref[...] = u.astype(u_ref.dtype)
        a_ref[...] = (g * jax.nn.sigmoid(g) * u).astype(a_ref.dtype)

    w_spec = pl.BlockSpec((D, tn), lambda i, j: (0, j))
    o_spec = pl.BlockSpec((tm, tn), lambda i, j: (i, j))
    return pl.pallas_call(
        body, name="ffn_up", grid=(t // tm, DFF // tn),
        in_specs=[pl.BlockSpec((tm, D), lambda i, j: (i, 0)), w_spec, w_spec],
        out_specs=[o_spec] * 3, out_shape=[jax.ShapeDtypeStruct((t, DFF), bf16)] * 3,
        compiler_params=_cparams(("parallel", "parallel")),
    )(x1, wg, wu)


def ffn_dact(ddown, wd, g, u):
    t = ddown.shape[0]
    tm, tn = min(512, t), _pick(DFF, 1536)

    def body(dd_ref, wd_ref, g_ref, u_ref, dg_ref, du_ref):
        dact = _dot(dd_ref[...].astype(MX), wd_ref[...], 1, 1)
        gv, uv = g_ref[...].astype(f32), u_ref[...].astype(f32)
        sig = jax.nn.sigmoid(gv)
        dg_ref[...] = (dact * uv * sig * (1.0 + gv * (1.0 - sig))).astype(dg_ref.dtype)
        du_ref[...] = (dact * gv * sig).astype(du_ref.dtype)

    o_spec = pl.BlockSpec((tm, tn), lambda i, j: (i, j))
    return pl.pallas_call(
        body, name="ffn_dact", grid=(t // tm, DFF // tn),
        in_specs=[pl.BlockSpec((tm, D), lambda i, j: (i, 0)), pl.BlockSpec((tn, D), lambda i, j: (j, 0)), o_spec, o_spec],
        out_specs=[o_spec] * 2, out_shape=[jax.ShapeDtypeStruct((t, DFF), bf16)] * 2,
        compiler_params=_cparams(("parallel", "parallel")),
    )(ddown, wd, g, u)


def loss_head(y, target):
    t = y.shape[0]
    tb = min(ROW_TB, t)
    n = t // tb

    def body(y_ref, t_ref, dy_ref, loss_ref, acc):
        i = pl.program_id(0)

        @pl.when(i == 0)
        def _():
            acc[...] = jnp.zeros_like(acc)

        e = y_ref[...] - t_ref[...]
        dy_ref[...] = e * (1.0 / D)
        acc[...] += jnp.sum(e * e, axis=0, keepdims=True)

        @pl.when(i == n - 1)
        def _():
            loss_ref[...] = jnp.sum(acc[...], axis=1, keepdims=True) * (0.5 / D)

    return pl.pallas_call(
        body, name="loss_head", grid=(n,),
        in_specs=[_row_spec(tb, D, 0)] * 2,
        out_specs=[_row_spec(tb, D, 0), _full_spec((1, 1))],
        out_shape=[jax.ShapeDtypeStruct((t, D), f32), jax.ShapeDtypeStruct((1, 1), f32)],
        scratch_shapes=[pltpu.VMEM((1, D), f32)],
        compiler_params=_cparams(("arbitrary",)),
    )(y, target)


def _me_and_peers():
    x, y, c = lax.axis_index("x"), lax.axis_index("y"), lax.axis_index("c")
    me = 4 * x + 2 * y + c
    peers = []
    for kk in range(1, N_DEV):
        px = 1 - x if kk & 4 else x
        py = 1 - y if kk & 2 else y
        pc = 1 - c if kk & 1 else c
        peers.append(((px, py, pc), 4 * px + 2 * py + pc))
    return me, peers


_ANY = pl.BlockSpec(memory_space=pl.ANY)


def _comm_scratch():
    return [pltpu.SemaphoreType.DMA((N_DEV - 1,)), pltpu.SemaphoreType.DMA((N_DEV - 1,)), pltpu.SemaphoreType.DMA]


def _exchange_copies(src_ref, out_ref, send_sems, recv_sems, local_sem, scatter):
    me, peers = _me_and_peers()
    pick = (lambda d: src_ref.at[d]) if scatter else (lambda d: src_ref)
    ops = [pltpu.make_async_copy(pick(me), out_ref.at[me], local_sem)]
    ops += [pltpu.make_async_remote_copy(src_ref=pick(pid), dst_ref=out_ref.at[me], send_sem=send_sems.at[kk],
                                         recv_sem=recv_sems.at[kk], device_id=dev, device_id_type=pl.DeviceIdType.MESH)
            for kk, (dev, pid) in enumerate(peers)]
    return ops


def _comm_out_shape(src, scatter):
    return jax.ShapeDtypeStruct(src.shape if scatter else (N_DEV,) + src.shape, src.dtype)


def exchange(name, comms):
    def body(*refs):
        _, _, _, comm_refs = _split_hosted(refs, 0, 0, 0, len(comms))
        ops = [op for refs_c, (_, scatter) in zip(comm_refs, comms) for op in _exchange_copies(*refs_c, scatter)]
        for op in ops:
            op.start()
        for op in ops:
            op.wait()

    in_specs, out_specs, out_shape, scratch, args = [], [], [], [], []
    _add_hosted(comms, in_specs, out_specs, out_shape, scratch, args)
    return pl.pallas_call(body, name=name, in_specs=in_specs, out_specs=out_specs, out_shape=out_shape, scratch_shapes=scratch)(*args)


def adamw(w, m, v, parts):
    nl, r, c = w.shape
    lanes = -(-c // LANE) * LANE
    tb = _pick_rows(r, max(8, ADAM_PART_BLOCK_BYTES // (N_DEV * lanes * 4)))
    n = r // tb

    def body(*refs):
        w_ref, m_ref, v_ref = refs[:3]
        p_refs = refs[3:3 + nl]
        g_ref, d_ref, nm_ref, nv_ref = refs[3 + nl:]
        layer = pl.program_id(0)
        for k in range(nl):
            @pl.when(layer == k)
            def _(p_ref=p_refs[k]):
                g = p_ref[0]
                for s in range(1, N_DEV):
                    g = g + p_ref[s]
                g_ref[0] = g

        g = g_ref[0]
        nm = ADAM_B1 * m_ref[0] + (1.0 - ADAM_B1) * g
        nv = ADAM_B2 * v_ref[0] + (1.0 - ADAM_B2) * jnp.square(g)
        m_hat = nm / (1.0 - ADAM_B1 ** ADAM_STEP)
        v_hat = nv / (1.0 - ADAM_B2 ** ADAM_STEP)
        d_ref[0] = -ADAM_LR * (m_hat / (jnp.sqrt(v_hat) + ADAM_EPS) + ADAM_WD * w_ref[0])
        nm_ref[0] = nm
        nv_ref[0] = nv

    def part_spec(k):
        return pl.BlockSpec((N_DEV, tb, c), lambda l, i: (0, jnp.where(l == k, i, jnp.where(l > k, n - 1, 0)), 0))

    spec = pl.BlockSpec((1, tb, c), lambda l, i: (l, i, 0))
    return pl.pallas_call(
        body, name="adamw", grid=(nl, n), in_specs=[spec] * 3 + [part_spec(k) for k in range(nl)], out_specs=[spec] * 4,
        out_shape=[jax.ShapeDtypeStruct((nl, r, c), f32)] * 4, compiler_params=_cparams(("arbitrary", "arbitrary")),
    )(w, m, v, *parts)


ADAM_PART_BLOCK_BYTES = 3 << 19


def _pick_rows(r, cap=ROW_TB):
    return max(tb for tb in range(8, min(cap, ROW_TB) + 1, 8) if r % tb == 0)


BIG = ("w_in", "w_uq", "w_ukv", "w_out", "w_gate_up", "w_down", "w_ple", "w_ple_gate")
COL_SHARDED = ("w_in", "w_uq", "w_ukv", "w_gate_up", "w_ple")
SMALL = ("conv_w", "a_log", "dt_bias", "gdn_norm_g", "q_norm_g", "kv_norm_g", "ln1_g", "ln1_b", "ln2_g", "ln2_b")
ROW_ALIGN = 16


def _pad_rows(a, mult=ROW_ALIGN, axis=0):
    pad = (-a.shape[axis]) % mult
    widths = [(0, 0)] * a.ndim
    widths[axis] = (0, pad)
    return a if pad == 0 else jnp.pad(a, widths)


GATHER_EARLY = ("w_in", "w_uq", "w_ukv")
GATHER_LATE = ("w_out", "w_gate_up", "w_down", "w_ple", "w_ple_gate")
GRAD_EARLY = ("w_gate_up", "w_down", "w_ple", "w_ple_gate", "w_out")
GRAD_LATE = ("w_in", "w_uq", "w_ukv")


def _gathers(tree, l, names):
    return [(tree[n][l].astype(bf16), False) for n in names]


def _full_weight(gathered, n, lo=0, hi=N_DEV):
    blk = gathered[lo:hi]
    k, sr, sc = blk.shape
    if n in COL_SHARDED:
        return jnp.transpose(blk, (1, 0, 2)).reshape(sr, k * sc)
    return blk.reshape(k * sr, sc)


def _shard_grad(gfull, n):
    if n in COL_SHARDED:
        sr = gfull.shape[0]
        k = gfull.shape[1] // (FULL_COLS[n] // N_DEV)
        return jnp.transpose(gfull.reshape(sr, k, gfull.shape[1] // k), (1, 0, 2))
    return gfull.reshape(N_DEV, gfull.shape[0] // N_DEV, gfull.shape[1])


FULL_COLS = {"w_in": IN_W, "w_uq": 768, "w_ukv": 1024, "w_gate_up": 2 * DFF, "w_ple": D}


def _pack_small(tree, extra):
    flat = jnp.concatenate([tree[n].reshape(-1).astype(f32) for n in SMALL] + [extra.reshape(-1).astype(f32)])
    return jnp.pad(flat, (0, (-flat.shape[0]) % (ROW_ALIGN * D))).reshape(-1, D)


def _unpack_small(rows, like):
    flat = rows.reshape(-1)
    out, off = {}, 0
    for n in SMALL:
        sz = int(np.prod(like[n].shape))
        out[n] = flat[off:off + sz].reshape(like[n].shape)
        off += sz
    return out, flat[off]


def _zeros(r, c, dt):
    return jnp.zeros((r, c), dt)


def _prep_w_in(w):
    dt = w.dtype
    kr = w[:, 2696:2760]
    return jnp.concatenate([
        w[:, 0:2048],
        w[:, 2440:2696],
        kr, w[:, 2048:2056], _zeros(D, 56, dt),
        kr[:, 32:], kr[:, :32], _zeros(D, 64, dt),
        w[:, 2056:2440], _zeros(D, CQP - QL, dt)], axis=1)


def _unprep_w_in(g):
    krs = g[:, OFF_KRS:OFF_KRS + 64]
    kr = g[:, OFF_KR:OFF_KR + 64] + jnp.concatenate([krs[:, 32:], krs[:, :32]], axis=1)
    return jnp.concatenate([g[:, 0:2048], g[:, OFF_KR + 64:OFF_KR + 72], g[:, OFF_CQ:OFF_CQ + QL],
                            g[:, OFF_CKV:OFF_CKV + KVL], kr], axis=1)


def _prep_w_uq(w):
    dt = w.dtype
    z64 = _zeros(QL, 64, dt)
    nope, ra, rb = [], [], []
    for h in range(H):
        nope.append(w[:, 192 * h:192 * h + 128])
        x1, x2 = w[:, 192 * h + 128:192 * h + 160], w[:, 192 * h + 160:192 * h + 192]
        ra += [x1, x2, z64]
        rb += [x2, x1, z64]
    return jnp.concatenate([jnp.concatenate(nope + ra + rb, axis=1), _zeros(CQP - QL, 1536, dt)], axis=0)


def _unprep_w_uq(g):
    g = g[:QL]
    cols = []
    for h in range(H):
        a = g[:, 512 + 128 * h:512 + 128 * h + 64]
        b = g[:, 1024 + 128 * h:1024 + 128 * h + 64]
        cols += [g[:, 128 * h:128 * (h + 1)], a[:, :32] + b[:, 32:], a[:, 32:] + b[:, :32]]
    return jnp.concatenate(cols, axis=1)


def _rope_tables(positions):
    inv_freq = ROPE_THETA ** (-jnp.arange(0, ROPE, 2, dtype=f32) / ROPE)
    ang = positions.astype(f32)[:, None] * inv_freq
    c, s = jnp.cos(ang), jnp.sin(ang)
    z = jnp.zeros((positions.shape[0], 64), f32)
    return jnp.concatenate([c, c, z], axis=1), jnp.concatenate([-s, s, z], axis=1)


def _tile_heads(vec, n=H):
    return jnp.tile(vec.reshape(1, -1), (1, n))


def _bcast_heads(vec):
    return jnp.repeat(vec, DK).reshape(1, H * DK)


def kernel(x, p, positions, w_in, conv_w, a_log, dt_bias, gdn_norm_g, q_norm_g, w_uq, kv_norm_g, w_ukv, w_out, ln1_g, ln1_b, w_gate_up, w_down, ln2_g, ln2_b, w_ple, w_ple_gate, loss_target, m_w_in, m_conv_w, m_a_log, m_dt_bias, m_gdn_norm_g, m_q_norm_g, m_w_uq, m_kv_norm_g, m_w_ukv, m_w_out, m_ln1_g, m_ln1_b, m_w_gate_up, m_w_down, m_ln2_g, m_ln2_b, m_w_ple, m_w_ple_gate, v_w_in, v_conv_w, v_a_log, v_dt_bias, v_gdn_norm_g, v_q_norm_g, v_w_uq, v_kv_norm_g, v_w_ukv, v_w_out, v_ln1_g, v_ln1_b, v_w_gate_up, v_w_down, v_ln2_g, v_ln2_b, v_w_ple, v_w_ple_gate):
    W = dict(w_in=w_in, conv_w=conv_w, a_log=a_log, dt_bias=dt_bias, gdn_norm_g=gdn_norm_g, q_norm_g=q_norm_g, w_uq=w_uq,
             kv_norm_g=kv_norm_g, w_ukv=w_ukv, w_out=w_out, ln1_g=ln1_g, ln1_b=ln1_b, w_gate_up=w_gate_up, w_down=w_down,
             ln2_g=ln2_g, ln2_b=ln2_b, w_ple=w_ple, w_ple_gate=w_ple_gate)
    M = dict(w_in=m_w_in, conv_w=m_conv_w, a_log=m_a_log, dt_bias=m_dt_bias, gdn_norm_g=m_gdn_norm_g, q_norm_g=m_q_norm_g,
             w_uq=m_w_uq, kv_norm_g=m_kv_norm_g, w_ukv=m_w_ukv, w_out=m_w_out, ln1_g=m_ln1_g, ln1_b=m_ln1_b,
             w_gate_up=m_w_gate_up, w_down=m_w_down, ln2_g=m_ln2_g, ln2_b=m_ln2_b, w_ple=m_w_ple, w_ple_gate=m_w_ple_gate)
    V = dict(w_in=v_w_in, conv_w=v_conv_w, a_log=v_a_log, dt_bias=v_dt_bias, gdn_norm_g=v_gdn_norm_g, q_norm_g=v_q_norm_g,
             w_uq=v_w_uq, kv_norm_g=v_kv_norm_g, w_ukv=v_w_ukv, w_out=v_w_out, ln1_g=v_ln1_g, ln1_b=v_ln1_b,
             w_gate_up=v_w_gate_up, w_down=v_w_down, ln2_g=v_ln2_g, ln2_b=v_ln2_b, w_ple=v_w_ple, w_ple_gate=v_w_ple_gate)
    shards = {n: W[n].shape for n in BIG}
    t = x.shape[1]
    xin = x.reshape(t, D)
    target = loss_target.reshape(t, D)
    cf, sf = _rope_tables(positions.reshape(t))

    conv_rows = _pad_rows(jnp.pad(conv_w.reshape(-1), (0, (-conv_w.size) % D)).reshape(-1, D), 8)
    first = exchange("all_gather_first", _gathers(W, 0, GATHER_EARLY) + [(conv_rows, False)])
    early = [dict(zip(GATHER_EARLY, first[:-1]))] + [None] * (DEPTH - 1)
    csz = conv_w.shape[1] * conv_w.shape[2]
    conv_full = jnp.transpose(first[-1].reshape(N_DEV, -1)[:, :DEPTH * csz].reshape(N_DEV, DEPTH, CW, -1), (1, 2, 0, 3)).reshape(DEPTH, CW, 1536)

    acts = []
    h_cur = xin
    for l in range(DEPTH):
        win = _prep_w_in(_full_weight(early[l]["w_in"], "w_in"))
        wq = _prep_w_uq(_full_weight(early[l]["w_uq"], "w_uq"))
        wkv = _full_weight(early[l]["w_ukv"], "w_ukv")
        cw8 = jnp.concatenate([conv_full[l], jnp.zeros((8 - CW, 1536), f32)], axis=0)
        alog_b, dtb_b = _bcast_heads(a_log[l]), _bcast_heads(dt_bias[l])
        gg_b = _tile_heads(gdn_norm_g[l])
        qg = jnp.concatenate([q_norm_g[l], jnp.zeros((CQP - QL,), f32)]).reshape(1, CQP)
        kvg = kv_norm_g[l].reshape(1, KVL)
        g1, b1, g2, b2 = (a[l].reshape(1, D) for a in (ln1_g, ln1_b, ln2_g, ln2_b))
        p_l = p[l].reshape(t, PLE)

        hb = mm("in_proj", h_cur, win, mode="nn", out_dtype=f32)
        qn, kn, vs, beta, glog = gdn_pre_fwd(hb, cw8, alog_b, dtb_b)
        g_u, g_w, g_qd, g_kd, g_at, g_el = gdn_intra(qn, kn, vs, beta, glog)
        o_gdn, states = gdn_scan_fwd(g_u, g_w, g_qd, g_kd, g_at, g_el)
        mla_rows = [(hb, CQP, OFF_CQ // CQP), (hb, KVL, OFF_CKV // KVL), (hb, LANE, OFF_KR // LANE), (hb, LANE, OFF_KRS // LANE),
                    (cf, LANE, 0), (sf, LANE, 0)]
        mla_consts = [qg, wq, kvg, wkv]
        qm, km, vm = row_fwd("mla_pre", mla_pre_f, mla_rows, mla_consts, [(1024, bf16), (1024, bf16), (512, bf16)])
        comms = _gathers(W, l, GATHER_LATE) + (_gathers(W, l + 1, GATHER_EARLY) if l + 1 < DEPTH else [])
        res = attn_fwd(qm, km, vm, name=f"attn_fwd_gather{len(comms)}", comms=comms)
        o_mla, lse = res[:2]
        late = dict(zip(GATHER_LATE, res[2:]))
        if l + 1 < DEPTH:
            early[l + 1] = dict(zip(GATHER_EARLY, res[2 + len(GATHER_LATE):]))
        wout, wd = _full_weight(late["w_out"], "w_out"), _full_weight(late["w_down"], "w_down")
        half = N_DEV // 2
        wg, wu = _full_weight(late["w_gate_up"], "w_gate_up", 0, half), _full_weight(late["w_gate_up"], "w_gate_up", half, N_DEV)
        wple, wpg = _full_weight(late["w_ple"], "w_ple"), _full_weight(late["w_ple_gate"], "w_ple_gate")
        p1_rows = [(o_gdn, 512, 0), (hb, 512, OFF_Z // 512), (o_mla, 512, 0), (h_cur, D, 0)]
        p1_consts = [gg_b, wout, g1, b1]
        (x1,) = row_fwd("post1", post1_f, p1_rows, p1_consts, [(D, f32)])
        gate, up, act = ffn_up(x1, wg, wu)
        down = mm("ffn_down", act, wd, mode="nn", out_dtype=f32)
        p2_rows = [(x1, D, 0), (down, D, 0), (p_l, PLE, 0)]
        p2_consts = [g2, b2, wpg, wple]
        (y,) = row_fwd("post2", post2_f, p2_rows, p2_consts, [(D, f32)])
        acts.append(dict(x=h_cur, hb=hb, qn=qn, kn=kn, vs=vs, beta=beta, glog=glog, states=states, o_gdn=o_gdn, qm=qm, km=km,
                         scan=(g_w, g_qd, g_kd, g_at, g_el),
                         vm=vm, o_mla=o_mla, lse=lse, x1=x1, gate=gate, up=up, act=act, win=win, wd=wd, wg=wg, wu=wu,
                         cw8=cw8, alog_b=alog_b, dtb_b=dtb_b, mla_rows=mla_rows, mla_consts=mla_consts, p1_rows=p1_rows,
                         p1_consts=p1_consts, p2_rows=p2_rows, p2_consts=p2_consts))
        h_cur = y

    dy, loss_part = loss_head(h_cur, target)

    G = {n: [None] * DEPTH for n in SMALL}
    S = {n: [None] * DEPTH for n in BIG}
    parts = {n: [None] * DEPTH for n in BIG}
    pending = None
    for l in reversed(range(DEPTH)):
        a = acts[l]
        dx1, ddown, dg2, db2, dwpg, dwple = row_bwd("post2_bwd", post2_f, a["p2_rows"], a["p2_consts"], [dy], [0, 1], [0, 1, 2, 3], [f32, f32])
        dgate, dup = ffn_dact(ddown, a["wd"], a["gate"], a["up"])
        dwd = mm("dw_down", a["act"], ddown, mode="tn", out_dtype=f32)
        dx1 = mm("dx_gate", dgate, a["wg"], mode="nt", out_dtype=f32, add=dx1)
        dx1 = mm("dx_up", dup, a["wu"], mode="nt", out_dtype=f32, add=dx1)
        dwg = mm("dw_gate", a["x1"], dgate, mode="tn", out_dtype=f32)
        dwu = mm("dw_up", a["x1"], dup, mode="tn", out_dtype=f32)
        do_gdn, dz, do_mla, dxr, dgg, dwout, dg1, db1 = row_bwd("post1_bwd", post1_f, a["p1_rows"], a["p1_consts"], [dx1], [0, 1, 2, 3],
                                                              [0, 1, 2, 3], [f32, bf16, f32, f32])
        for g, n in ((dwd, "w_down"), (dwple, "w_ple"), (dwpg, "w_ple_gate"), (dwout, "w_out")):
            S[n][l] = _shard_grad(g, n)
        S["w_gate_up"][l] = jnp.concatenate([_shard_grad(dwg, "w_gate_up"), _shard_grad(dwu, "w_gate_up")], axis=0)
        res = attn_bwd(a["qm"], a["km"], a["vm"], a["o_mla"], a["lse"], do_mla, name="attn_bwd_exchange",
                       comms=[(S[n][l], True) for n in GRAD_EARLY])
        dqm, dkm, dvm = res[:3]
        for n, got in zip(GRAD_EARLY, res[3:]):
            parts[n][l] = got
        dstates = gdn_scan_bwd(do_gdn, *a["scan"])
        comms = [(S[n][pending], True) for n in GRAD_LATE] if pending is not None else []
        res = gdn_local_bwd(a["qn"], a["kn"], a["vs"], a["beta"], a["glog"], a["states"], dstates, do_gdn,
                            name="gdn_local_bwd_exchange" if comms else "gdn_local_bwd", comms=comms)
        dqn, dkn, dvs, dbeta, dglog = res[:5]
        for n, got in zip(GRAD_LATE, res[5:]):
            parts[n][pending] = got
        dqkv, dba, dcw, dal, ddt = gdn_pre_bwd(a["hb"], a["cw8"], a["alog_b"], a["dtb_b"], dqn, dkn, dvs, dbeta, dglog)
        dcq, dckv, dkrg, dkrs, dqg, dwq, dkvg, dwkv = row_bwd("mla_pre_bwd", mla_pre_f, a["mla_rows"], a["mla_consts"], [dqm, dkm, dvm],
                                                              [0, 1, 2, 3], [0, 1, 2, 3], [bf16, bf16, bf16, bf16], row_add={2: dba})
        dh = jnp.concatenate([dqkv, dz, dckv, dkrg, dkrs, dcq], axis=1)
        dwin = mm("dw_in", a["x"], dh, mode="tn", out_dtype=f32)
        dy = mm("dx_in", dh, a["win"], mode="nt", out_dtype=f32, add=dxr)

        S["w_in"][l], S["w_uq"][l] = _shard_grad(_unprep_w_in(dwin), "w_in"), _shard_grad(_unprep_w_uq(dwq), "w_uq")
        S["w_ukv"][l] = _shard_grad(dwkv, "w_ukv")
        G["conv_w"][l] = dcw[:CW]
        G["a_log"][l] = jnp.sum(dal.reshape(H, DK), axis=1)
        G["dt_bias"][l] = jnp.sum(ddt.reshape(H, DK), axis=1)
        G["gdn_norm_g"][l] = jnp.sum(dgg.reshape(H, DK), axis=0)
        G["q_norm_g"][l] = dqg[0, :QL]
        G["kv_norm_g"][l] = dkvg[0]
        G["ln1_g"][l], G["ln1_b"][l], G["ln2_g"][l], G["ln2_b"][l] = dg1[0], db1[0], dg2[0], db2[0]
        pending = l
    grad_x = dy.reshape(x.shape)

    small_like = {n: W[n] for n in SMALL}
    conv_parts = jnp.stack(G["conv_w"]).reshape(DEPTH, CW, N_DEV, -1)
    smalls = []
    for d in range(N_DEV):
        tree = {n: jnp.stack(G[n]) for n in SMALL if n != "conv_w"}
        tree["conv_w"] = conv_parts[:, :, d, :]
        smalls.append(_pack_small(tree, loss_part))
    small_send = jnp.stack(smalls)
    res = exchange("exchange_last", [(S[n][pending], True) for n in GRAD_LATE] + [(small_send, True)])
    for n, got in zip(GRAD_LATE, res):
        parts[n][pending] = got

    g_out, d_out, m_out, v_out = {}, {}, {}, {}
    for n in BIG:
        g_out[n], d_out[n], m_out[n], v_out[n] = adamw(W[n], M[n], V[n], parts[n])
    zero = jnp.zeros((), f32)
    upd = adamw(_pack_small(W, zero)[None], _pack_small(M, zero)[None], _pack_small(V, zero)[None], [res[-1]])
    for dst, rows_arr in zip((g_out, d_out, m_out, v_out), upd):
        dst.update(_unpack_small(rows_arr[0], small_like)[0])
    loss = _unpack_small(upd[0][0], small_like)[1]
    order = ["w_in", "conv_w", "a_log", "dt_bias", "gdn_norm_g", "q_norm_g", "w_uq", "kv_norm_g", "w_ukv", "w_out", "ln1_g", "ln1_b",
             "w_gate_up", "w_down", "ln2_g", "ln2_b", "w_ple", "w_ple_gate"]
    return (loss, grad_x, *[g_out[n] for n in order], *[d_out[n] for n in order], *[m_out[n] for n in order],
            *[v_out[n] for n in order])
```

```python
import jax
import jax.numpy as jnp
import numpy as np
from jax import lax
from jax.experimental import pallas as pl
from jax.experimental.pallas import tpu as pltpu

f32 = jnp.float32
bf16 = jnp.bfloat16
MX = jnp.bfloat16

D = 1024
DEPTH = 4
CHUNK = 64
H = 4
DK = 128
PLE = 256
QL = 384
KVL = 256
ROPE = 64
DFF = 2816
IN_W = 2760
ROPE_THETA = 10000.0
ALPHA = (2.0 * DEPTH) ** 0.25
LN_EPS = 1e-5
RMS_EPS = 1e-6
ATT_SCALE = (128 + 64) ** -0.5
N_DEV = 8

ADAM_LR, ADAM_B1, ADAM_B2, ADAM_EPS, ADAM_WD, ADAM_STEP = 0.001, 0.9, 0.999, 1e-08, 0.01, 10

OFF_QKV, OFF_Z, OFF_CKV, OFF_KR, OFF_KRS, OFF_CQ, HP = 0, 1536, 2048, 2304, 2432, 2560, 3072
CQP = 512
LANE = 128
VMEM_LIMIT = 48 * 1024 * 1024
ROW_TB = 256


def _dot(a, b, ca, cb, prec=None):
    if a.ndim == 3:
        return lax.dot_general(a, b, (((ca + 1,), (cb + 1,)), ((0,), (0,))), precision=prec, preferred_element_type=f32)
    return lax.dot_general(a, b, (((ca,), (cb,)), ((), ())), precision=prec, preferred_element_type=f32)


@jax.custom_vjp
def bdot(a, w):
    return _dot(a.astype(MX), w.astype(MX), 1, 0)


def _bdot_fwd(a, w):
    return bdot(a, w), (a, w)


def _bdot_bwd(res, g):
    a, w = res
    gb = g.astype(MX)
    return _dot(gb, w.astype(MX), 1, 1).astype(a.dtype), _dot(a.astype(MX), gb, 0, 0).astype(w.dtype)


bdot.defvjp(_bdot_fwd, _bdot_bwd)


@jax.custom_vjp
def bdot_nt(a, b):
    return _dot(a.astype(MX), b.astype(MX), 1, 1)


def _bdot_nt_fwd(a, b):
    return bdot_nt(a, b), (a, b)


def _bdot_nt_bwd(res, g):
    a, b = res
    gb = g.astype(MX)
    return _dot(gb, b.astype(MX), 1, 0).astype(a.dtype), _dot(gb, a.astype(MX), 0, 0).astype(b.dtype)


bdot_nt.defvjp(_bdot_nt_fwd, _bdot_nt_bwd)


@jax.custom_vjp
def bdot_tn(a, b):
    return _dot(a.astype(MX), b.astype(MX), 0, 0)


def _bdot_tn_fwd(a, b):
    return bdot_tn(a, b), (a, b)


def _bdot_tn_bwd(res, g):
    a, b = res
    gb = g.astype(MX)
    return _dot(b.astype(MX), gb, 1, 1).astype(a.dtype), _dot(a.astype(MX), gb, 1, 0).astype(b.dtype)


bdot_tn.defvjp(_bdot_tn_fwd, _bdot_tn_bwd)


@jax.custom_vjp
def hdot(a, b):
    return _dot(a, b, 1, 0, lax.Precision.HIGHEST)


def _hdot_fwd(a, b):
    return hdot(a, b), (a, b)


def _hdot_bwd(res, g):
    a, b = res
    return _dot(g, b, 1, 1, lax.Precision.HIGHEST), _dot(a, g, 0, 0, lax.Precision.HIGHEST)


hdot.defvjp(_hdot_fwd, _hdot_bwd)


def _pick(n, cap):
    best = None
    for t in range(LANE, min(n, cap) + 1, LANE):
        if n % t == 0:
            best = t
    assert best is not None, (n, cap)
    return best


def _cparams(sem):
    return pltpu.CompilerParams(dimension_semantics=sem, vmem_limit_bytes=VMEM_LIMIT)


def mm(name, a, b, *, mode, out_dtype, add=None):
    if mode == "tn":
        kdim, m = a.shape
        n = b.shape[1]
        tm, tn, tk = _pick(m, 1536), _pick(n, 1536), min(512, kdim)
        nk = kdim // tk

        def body(a_ref, b_ref, o_ref, acc):
            k = pl.program_id(2)

            @pl.when(k == 0)
            def _():
                acc[...] = jnp.zeros_like(acc)

            acc[...] += _dot(a_ref[...].astype(MX), b_ref[...].astype(MX), 0, 0)

            @pl.when(k == nk - 1)
            def _():
                o_ref[...] = acc[...].astype(o_ref.dtype)

        return pl.pallas_call(
            body, name=name, grid=(m // tm, n // tn, nk),
            in_specs=[pl.BlockSpec((tk, tm), lambda i, j, k: (k, i)), pl.BlockSpec((tk, tn), lambda i, j, k: (k, j))],
            out_specs=pl.BlockSpec((tm, tn), lambda i, j, k: (i, j)),
            out_shape=jax.ShapeDtypeStruct((m, n), out_dtype),
            scratch_shapes=[pltpu.VMEM((tm, tn), f32)],
            compiler_params=_cparams(("parallel", "parallel", "arbitrary")),
        )(a, b)

    m, kdim = a.shape
    n = b.shape[1] if mode == "nn" else b.shape[0]
    tm, tn = min(512, m), _pick(n, 1536)
    has_add = add is not None

    def body(*refs):
        a_ref, b_ref = refs[0], refs[1]
        o_ref = refs[-1]
        r = _dot(a_ref[...].astype(MX), b_ref[...].astype(MX), 1, 0 if mode == "nn" else 1)
        if has_add:
            r = r + refs[2][...].astype(f32)
        o_ref[...] = r.astype(o_ref.dtype)

    b_spec = (pl.BlockSpec((kdim, tn), lambda i, j: (0, j)) if mode == "nn"
              else pl.BlockSpec((tn, kdim), lambda i, j: (j, 0)))
    in_specs = [pl.BlockSpec((tm, kdim), lambda i, j: (i, 0)), b_spec]
    args = [a, b]
    if has_add:
        in_specs.append(pl.BlockSpec((tm, tn), lambda i, j: (i, j)))
        args.append(add)
    return pl.pallas_call(
        body, name=name, grid=(m // tm, n // tn), in_specs=in_specs,
        out_specs=pl.BlockSpec((tm, tn), lambda i, j: (i, j)),
        out_shape=jax.ShapeDtypeStruct((m, n), out_dtype),
        compiler_params=_cparams(("parallel", "parallel")),
    )(*args)


def _row_spec(tb, w, cb):
    return pl.BlockSpec((tb, w), lambda i: (i, cb))


def _full_spec(shape):
    return pl.BlockSpec(shape, lambda i: (0,) * len(shape))


def row_fwd(name, f, rows, consts, outs):
    t = rows[0][0].shape[0]
    tb = min(ROW_TB, t)
    nr, nc = len(rows), len(consts)

    def body(*refs):
        vals = [r[...] for r in refs[:nr + nc]]
        res = f(*vals)
        for o_ref, val in zip(refs[nr + nc:], res):
            o_ref[...] = val.astype(o_ref.dtype)

    return pl.pallas_call(
        body, name=name, grid=(t // tb,),
        in_specs=[_row_spec(tb, w, cb) for _, w, cb in rows] + [_full_spec(c.shape) for c in consts],
        out_specs=[_row_spec(tb, w, 0) for w, _ in outs],
        out_shape=[jax.ShapeDtypeStruct((t, w), dt) for w, dt in outs],
        compiler_params=_cparams(("parallel",)),
    )(*[r[0] for r in rows], *consts)


def row_bwd(name, f, rows, consts, cots, row_diff, const_diff, drow_dtypes, row_add=None):
    t = rows[0][0].shape[0]
    tb = min(ROW_TB, t)
    nr, nc, nct = len(rows), len(consts), len(cots)
    row_add = row_add or {}
    add_keys = sorted(row_add)
    n_in = nr + nc + nct + len(add_keys)

    def body(*refs):
        i = pl.program_id(0)
        rv = [r[...] for r in refs[:nr]]
        cv = [refs[nr + k][...].astype(f32) if k in const_diff else refs[nr + k][...] for k in range(nc)]
        cot_refs = refs[nr + nc:nr + nc + nct]
        add_refs = refs[nr + nc + nct:n_in]
        drow_refs = refs[n_in:n_in + len(row_diff)]
        dconst_refs = refs[n_in + len(row_diff):]

        def g(*dv):
            r2, c2 = list(rv), list(cv)
            for p, k in enumerate(row_diff):
                r2[k] = dv[p]
            for p, k in enumerate(const_diff):
                c2[k] = dv[len(row_diff) + p]
            return tuple(f(*r2, *c2))

        prim = [rv[k].astype(f32) for k in row_diff] + [cv[k] for k in const_diff]
        outs, vf = jax.vjp(g, *prim)
        grads = vf(tuple(c[...].astype(o.dtype) for c, o in zip(cot_refs, outs)))
        for p, ref in enumerate(drow_refs):
            val = grads[p]
            if p in row_add:
                val = val + add_refs[add_keys.index(p)][...].astype(f32)
            ref[...] = val.astype(ref.dtype)

        @pl.when(i == 0)
        def _():
            for ref in dconst_refs:
                ref[...] = jnp.zeros_like(ref)

        for p, ref in enumerate(dconst_refs):
            ref[...] += grads[len(row_diff) + p]

    widths = [rows[k][1] for k in row_diff]
    return pl.pallas_call(
        body, name=name, grid=(t // tb,),
        in_specs=([_row_spec(tb, w, cb) for _, w, cb in rows] + [_full_spec(c.shape) for c in consts]
                  + [_row_spec(tb, c.shape[1], 0) for c in cots] + [_row_spec(tb, row_add[k].shape[1], 0) for k in add_keys]),
        out_specs=([_row_spec(tb, w, 0) for w in widths] + [_full_spec(consts[k].shape) for k in const_diff]),
        out_shape=([jax.ShapeDtypeStruct((t, w), dt) for w, dt in zip(widths, drow_dtypes)]
                   + [jax.ShapeDtypeStruct(consts[k].shape, f32) for k in const_diff]),
        compiler_params=_cparams(("arbitrary",)),
    )(*[r[0] for r in rows], *consts, *cots, *[row_add[k] for k in add_keys])


def _heads(x, w=DK):
    return [x[:, w * h:w * (h + 1)] for h in range(H)]


def _layer_norm(r, g, b):
    mu = jnp.mean(r, -1, keepdims=True)
    var = jnp.mean(jnp.square(r - mu), -1, keepdims=True)
    return (r - mu) * lax.rsqrt(var + LN_EPS) * g + b


def gdn_point_f(c, ba, alog_b, dtb_b):
    s = c * jax.nn.sigmoid(c)
    q, k, v = s[:, :512], s[:, 512:1024], s[:, 1024:]

    def l2(x):
        return jnp.concatenate([xh * lax.rsqrt(jnp.sum(xh * xh, -1, keepdims=True) + RMS_EPS) for xh in _heads(x)], axis=1)

    tb = c.shape[0]
    b_b = jnp.concatenate([jnp.broadcast_to(ba[:, 64 + h:65 + h], (tb, DK)) for h in range(H)], axis=1)
    a_b = jnp.concatenate([jnp.broadcast_to(ba[:, 68 + h:69 + h], (tb, DK)) for h in range(H)], axis=1)
    beta = jax.nn.sigmoid(b_b)
    g = -jnp.exp(alog_b) * jax.nn.softplus(a_b + dtb_b)
    return l2(q), l2(k), v, beta, g


def mla_pre_f(cq, ckv, krg, krs, cf, sf, qg, wq, kvg, wkv):
    cqn = cq * lax.rsqrt(jnp.sum(cq * cq, -1, keepdims=True) * (1.0 / QL) + RMS_EPS) * qg
    qa = bdot(cqn, wq)
    ckvn = ckv * lax.rsqrt(jnp.mean(ckv * ckv, -1, keepdims=True) + RMS_EPS) * kvg
    kv = bdot(ckvn, wkv)
    kro = krg * cf + krs * sf
    qs, ks, vs = [], [], []
    for h in range(H):
        qs += [qa[:, DK * h:DK * (h + 1)], qa[:, 512 + DK * h:512 + DK * (h + 1)] * cf + qa[:, 1024 + DK * h:1024 + DK * (h + 1)] * sf]
        ks += [kv[:, 256 * h:256 * h + DK], kro]
        vs += [kv[:, 256 * h + DK:256 * (h + 1)]]
    return jnp.concatenate(qs, axis=1) * ATT_C2, jnp.concatenate(ks, axis=1), jnp.concatenate(vs, axis=1)


def post1_f(o, z, omla, x, gg_b, wout, g1, b1):
    on = jnp.concatenate([oh * lax.rsqrt(jnp.mean(oh * oh, -1, keepdims=True) + RMS_EPS) for oh in _heads(o)], axis=1) * gg_b
    ogdn = on * (z * jax.nn.sigmoid(z))
    mix = bdot(jnp.concatenate([ogdn, omla], axis=1), wout)
    return (_layer_norm(ALPHA * x + mix, g1, b1),)


def post2_f(x1, down, p, g2, b2, wpg, wple):
    x2 = _layer_norm(ALPHA * x1 + down, g2, b2)
    return (x2 + jax.nn.sigmoid(bdot(x2, wpg)) * bdot(p, wple),)


HALO = 8
CW = 4


def _conv_from_scratch(xs, cw_ref, tb):
    c = xs[pl.ds(HALO - 3, tb), :] * cw_ref[0:1, :]
    for j in range(1, CW):
        c = c + xs[pl.ds(HALO - 3 + j, tb), :] * cw_ref[j:j + 1, :]
    return c


def gdn_pre_fwd(hbuf, conv_w8, alog_b, dtb_b):
    t = hbuf.shape[0]
    tb = min(ROW_TB, t)

    def body(x_ref, halo_ref, ba_ref, cw_ref, al_ref, dt_ref, q_ref, k_ref, v_ref, be_ref, g_ref, xs):
        i = pl.program_id(0)
        xs[pl.ds(0, HALO), :] = jnp.where(i == 0, 0.0, halo_ref[...])
        xs[pl.ds(HALO, tb), :] = x_ref[...]
        c = _conv_from_scratch(xs, cw_ref, tb)
        q, k, v, be, g = gdn_point_f(c, ba_ref[...], al_ref[...], dt_ref[...])
        q_ref[...], k_ref[...], v_ref[...], be_ref[...], g_ref[...] = q, k, v, be, g

    return pl.pallas_call(
        body, name="gdn_pre_fwd", grid=(t // tb,),
        in_specs=[_row_spec(tb, 1536, 0),
                  pl.BlockSpec((HALO, 1536), lambda i: (jnp.maximum(i * (tb // HALO) - 1, 0), 0)),
                  _row_spec(tb, LANE, OFF_KR // LANE),
                  _full_spec(conv_w8.shape), _full_spec(alog_b.shape), _full_spec(dtb_b.shape)],
        out_specs=[_row_spec(tb, 512, 0)] * 5,
        out_shape=[jax.ShapeDtypeStruct((t, 512), f32)] * 5,
        scratch_shapes=[pltpu.VMEM((tb + HALO, 1536), f32)],
        compiler_params=_cparams(("arbitrary",)),
    )(hbuf, hbuf, hbuf, conv_w8, alog_b, dtb_b)


def gdn_pre_bwd(hbuf, conv_w8, alog_b, dtb_b, dq, dk, dv, dbe, dg):
    t = hbuf.shape[0]
    tb = min(ROW_TB, t)
    n = t // tb

    def body(x_ref, halo_ref, ba_ref, cw_ref, al_ref, dt_ref, dq_ref, dk_ref, dv_ref, dbe_ref, dg_ref,
             dx_ref, dba_ref, dcw_ref, dal_ref, ddt_ref, xs, dcs):
        s = pl.program_id(0)
        i = n - 1 - s
        xs[pl.ds(0, HALO), :] = jnp.where(i == 0, 0.0, halo_ref[...])
        xs[pl.ds(HALO, tb), :] = x_ref[...]
        c = _conv_from_scratch(xs, cw_ref, tb)
        _, vf = jax.vjp(gdn_point_f, c, ba_ref[...], al_ref[...], dt_ref[...])
        dc, dba, dal, ddt = vf((dq_ref[...], dk_ref[...], dv_ref[...], dbe_ref[...], dg_ref[...]))

        @pl.when(s == 0)
        def _():
            dcs[pl.ds(tb, HALO), :] = jnp.zeros((HALO, 1536), f32)
            dcw_ref[...] = jnp.zeros_like(dcw_ref)
            dal_ref[...] = jnp.zeros_like(dal_ref)
            ddt_ref[...] = jnp.zeros_like(ddt_ref)

        @pl.when(s > 0)
        def _():
            dcs[pl.ds(tb, HALO), :] = dcs[pl.ds(0, HALO), :]

        dcs[pl.ds(0, tb), :] = dc
        dx = dcs[pl.ds(3, tb), :] * cw_ref[0:1, :]
        for j in range(1, CW):
            dx = dx + dcs[pl.ds(3 - j, tb), :] * cw_ref[j:j + 1, :]
        dx_ref[...] = dx.astype(dx_ref.dtype)
        dba_ref[...] = dba
        for j in range(CW):
            dcw_ref[j:j + 1, :] += jnp.sum(dc * xs[pl.ds(HALO - 3 + j, tb), :], axis=0, keepdims=True)
        dal_ref[...] += dal
        ddt_ref[...] += ddt

    rev = lambda cb: (lambda s: (n - 1 - s, cb))
    return pl.pallas_call(
        body, name="gdn_pre_bwd", grid=(n,),
        in_specs=[pl.BlockSpec((tb, 1536), rev(0)),
                  pl.BlockSpec((HALO, 1536), lambda s: (jnp.maximum((n - 1 - s) * (tb // HALO) - 1, 0), 0)),
                  pl.BlockSpec((tb, LANE), rev(OFF_KR // LANE)),
                  _full_spec(conv_w8.shape), _full_spec(alog_b.shape), _full_spec(dtb_b.shape)]
        + [pl.BlockSpec((tb, 512), rev(0))] * 5,
        out_specs=[pl.BlockSpec((tb, 1536), rev(0)), pl.BlockSpec((tb, LANE), rev(0)),
                   _full_spec(conv_w8.shape), _full_spec(alog_b.shape), _full_spec(dtb_b.shape)],
        out_shape=[jax.ShapeDtypeStruct((t, 1536), bf16), jax.ShapeDtypeStruct((t, LANE), f32),
                   jax.ShapeDtypeStruct(conv_w8.shape, f32), jax.ShapeDtypeStruct(alog_b.shape, f32),
                   jax.ShapeDtypeStruct(dtb_b.shape, f32)],
        scratch_shapes=[pltpu.VMEM((tb + HALO, 1536), f32), pltpu.VMEM((tb + HALO, 1536), f32)],
        compiler_params=_cparams(("arbitrary",)),
    )(hbuf, hbuf, hbuf, conv_w8, alog_b, dtb_b, dq, dk, dv, dbe, dg)


GDN_NB = 4
GDN_SCAN_NB = 8


def _to_batch(x, rows=CHUNK):
    n = x.shape[0] // rows
    return jnp.concatenate([x[rows * ci:rows * (ci + 1), DK * h:DK * (h + 1)][None] for ci in range(n) for h in range(H)], axis=0)


def _from_batch(y):
    n = y.shape[0] // H
    return jnp.concatenate([jnp.concatenate([y[ci * H + h] for h in range(H)], axis=1) for ci in range(n)], axis=0)


def gdn_intra_b(q, k, v, beta, g):
    c = CHUNK
    n = q.shape[0] // c
    tri = jnp.where(lax.broadcasted_iota(jnp.int32, (c, c), 0) >= lax.broadcasted_iota(jnp.int32, (c, c), 1), 1.0, 0.0)
    gc = _to_batch(jnp.concatenate([hdot(tri, g[c * ci:c * (ci + 1)]) for ci in range(n)], axis=0))
    qb, kb, vb, bb = _to_batch(q) * DK ** -0.5, _to_batch(k), _to_batch(v), _to_batch(beta)
    nbat = qb.shape[0]
    row = lax.broadcasted_iota(jnp.int32, (1, c, DK), 1)
    col = lax.broadcasted_iota(jnp.int32, (1, c, DK), 2)
    incl, strict, eye = row >= col, row > col, row == col
    grow = hdot(jnp.ones((nbat, c, c), f32), jnp.where(eye, gc, 0.0))
    decay = jnp.where(incl, jnp.exp(jnp.where(incl, gc - grow, 0.0)), 0.0)
    kbeta = kb * bb
    kpad = jnp.concatenate([kb, jnp.zeros((nbat, DK - c, DK), f32)], axis=1)
    a = jnp.where(strict, bdot_nt(kbeta, kpad) * decay, 0.0)[:, :, :c]
    nn = -a
    bk = bdot(a, a)
    for step in range(5):
        nn = nn + bk + bdot(nn, bk)
        if step < 4:
            bk = bdot(bk, bk)
    eg = jnp.exp(gc)
    rhs_v, rhs_k = vb * bb, kbeta * eg
    g_last = gc[:, c - 1:c, :]
    u = rhs_v + bdot(nn, rhs_v)
    w = rhs_k + bdot(nn, rhs_k)
    attn = jnp.where(incl, bdot_nt(qb, kpad) * decay, 0.0)
    return u, w, qb * eg, kb * jnp.exp(g_last - gc), attn, jnp.exp(g_last)


def gdn_scan_b(u, w, qd, kd, attn, el, state):
    v_new = u - bdot(w, state)
    return bdot(qd, state) + bdot(attn[:, :, :CHUNK], v_new), state * el + bdot_tn(kd, v_new)


def gdn_scan_bwd_b(do, w, qd, kd, attn, el, dstate):
    dvn = bdot_tn(attn[:, :, :CHUNK], do) + bdot(kd, dstate)
    return bdot_tn(qd, do) + dstate * el - bdot_tn(w, dvn)


def gdn_chunks_f(q, k, v, beta, g, states):
    n = states.shape[0]
    o, s_new = gdn_scan_b(*gdn_intra_b(q, k, v, beta, g), states.reshape(n * H, DK, DK))
    return _from_batch(o), s_new.reshape(n, H * DK, DK)


def _chunk_rows(cidx):
    return pl.ds(cidx * CHUNK, CHUNK)


def gdn_intra(q, k, v, beta, g):
    t = q.shape[0]
    nb = min(GDN_NB, t // CHUNK)
    tb = nb * CHUNK

    def body(q_ref, k_ref, v_ref, be_ref, g_ref, u_ref, w_ref, qd_ref, kd_ref, at_ref, el_ref):
        u, w, qd, kd, at, el = gdn_intra_b(q_ref[...], k_ref[...], v_ref[...], be_ref[...], g_ref[...])
        u_ref[...], w_ref[...], qd_ref[...], kd_ref[...], at_ref[...] = (_from_batch(a) for a in (u, w, qd, kd, at))
        el_ref[...] = _from_batch(jnp.broadcast_to(el, (nb * H, 8, DK))).reshape(nb, 8, H * DK)

    return pl.pallas_call(
        body, name="gdn_intra", grid=(t // tb,),
        in_specs=[_row_spec(tb, 512, 0)] * 5,
        out_specs=[_row_spec(tb, 512, 0)] * 5 + [pl.BlockSpec((nb, 8, 512), lambda i: (i, 0, 0))],
        out_shape=[jax.ShapeDtypeStruct((t, 512), f32)] * 5 + [jax.ShapeDtypeStruct((t // CHUNK, 8, 512), f32)],
        compiler_params=_cparams(("parallel",)),
    )(q, k, v, beta, g)


def gdn_scan_fwd(u, w, qd, kd, attn, el):
    t = u.shape[0]
    nb = min(GDN_SCAN_NB, t // CHUNK)
    tb = nb * CHUNK

    def body(u_ref, w_ref, qd_ref, kd_ref, at_ref, el_ref, o_ref, s_ref, state):
        @pl.when(pl.program_id(0) == 0)
        def _():
            state[...] = jnp.zeros_like(state)

        for cidx in range(nb):
            r = _chunk_rows(cidx)
            s_ref[cidx] = state[...]
            ins = [_to_batch(ref[r, :]) for ref in (u_ref, w_ref, qd_ref, kd_ref, at_ref)]
            el = _to_batch(el_ref[cidx], 8)[:, 0:1, :]
            o, s_new = gdn_scan_b(*ins, el, state[...].reshape(H, DK, DK))
            o_ref[r, :] = _from_batch(o)
            state[...] = s_new.reshape(H * DK, DK)

    return pl.pallas_call(
        body, name="gdn_scan_fwd", grid=(t // tb,),
        in_specs=[_row_spec(tb, 512, 0)] * 5 + [pl.BlockSpec((nb, 8, 512), lambda i: (i, 0, 0))],
        out_specs=[_row_spec(tb, 512, 0), pl.BlockSpec((nb, 512, DK), lambda i: (i, 0, 0))],
        out_shape=[jax.ShapeDtypeStruct((t, 512), f32), jax.ShapeDtypeStruct((t // CHUNK, 512, DK), f32)],
        scratch_shapes=[pltpu.VMEM((512, DK), f32)],
        compiler_params=_cparams(("arbitrary",)),
    )(u, w, qd, kd, attn, el)


def gdn_scan_bwd(do, w, qd, kd, attn, el):
    t = do.shape[0]
    nb = min(GDN_SCAN_NB, t // CHUNK)
    tb = nb * CHUNK
    n = t // tb

    def body(do_ref, w_ref, qd_ref, kd_ref, at_ref, el_ref, ds_ref, dstate):
        @pl.when(pl.program_id(0) == 0)
        def _():
            dstate[...] = jnp.zeros_like(dstate)

        for cidx in reversed(range(nb)):
            r = _chunk_rows(cidx)
            ds_ref[cidx] = dstate[...]
            ins = [_to_batch(ref[r, :]) for ref in (do_ref, w_ref, qd_ref, kd_ref, at_ref)]
            el = _to_batch(el_ref[cidx], 8)[:, 0:1, :]
            dstate[...] = gdn_scan_bwd_b(*ins, el, dstate[...].reshape(H, DK, DK)).reshape(H * DK, DK)

    rev = pl.BlockSpec((tb, 512), lambda s: (n - 1 - s, 0))
    return pl.pallas_call(
        body, name="gdn_scan_bwd", grid=(n,),
        in_specs=[rev] * 5 + [pl.BlockSpec((nb, 8, 512), lambda s: (n - 1 - s, 0, 0))],
        out_specs=pl.BlockSpec((nb, 512, DK), lambda s: (n - 1 - s, 0, 0)),
        out_shape=jax.ShapeDtypeStruct((t // CHUNK, 512, DK), f32),
        scratch_shapes=[pltpu.VMEM((512, DK), f32)],
        compiler_params=_cparams(("arbitrary",)),
    )(do, w, qd, kd, attn, el)


def gdn_local_bwd(q, k, v, beta, g, states, dstates, do, name="gdn_local_bwd", comms=()):
    t = q.shape[0]
    nb = min(GDN_NB, t // CHUNK)
    tb = nb * CHUNK
    n = t // tb

    def body(*refs):
        ins, outs, _, comm_refs = _split_hosted(refs, 8, 5, 0, len(comms))
        q_ref, k_ref, v_ref, be_ref, g_ref, s_ref, ds_ref, do_ref = ins
        dq_ref, dk_ref, dv_ref, dbe_ref, dg_ref = outs
        _host_exchanges(comm_refs, comms, pl.program_id(0) == 0, pl.program_id(0) == n - 1)
        states_v = s_ref[...]
        _, vf = jax.vjp(lambda *a: gdn_chunks_f(*a, states_v), q_ref[...], k_ref[...], v_ref[...], be_ref[...], g_ref[...])
        dq_ref[...], dk_ref[...], dv_ref[...], dbe_ref[...], dg_ref[...] = vf((do_ref[...], ds_ref[...]))

    st = pl.BlockSpec((nb, 512, DK), lambda i: (i, 0, 0))
    in_specs = [_row_spec(tb, 512, 0)] * 5 + [st, st, _row_spec(tb, 512, 0)]
    out_specs = [_row_spec(tb, 512, 0)] * 5
    out_shape = [jax.ShapeDtypeStruct((t, 512), f32)] * 5
    scratch, args = [], [q, k, v, beta, g, states, dstates, do]
    _add_hosted(comms, in_specs, out_specs, out_shape, scratch, args)
    return pl.pallas_call(body, name=name, grid=(n,), in_specs=in_specs, out_specs=out_specs, out_shape=out_shape,
                          scratch_shapes=scratch, compiler_params=_cparams(("arbitrary",) if comms else ("parallel",)))(*args)


NEG = -1e30


LOG2E = 1.4426950408889634
ATT_C2 = ATT_SCALE * LOG2E
ATT_RB = 256


def _add_hosted(comms, in_specs, out_specs, out_shape, scratch, args):
    for src, scatter in comms:
        in_specs.append(_ANY)
        out_specs.append(_ANY)
        out_shape.append(_comm_out_shape(src, scatter))
        scratch += _comm_scratch()
        args.append(src)


def _split_hosted(refs, n_in, n_out, n_scratch, n_comm):
    ins, srcs = refs[:n_in], refs[n_in:n_in + n_comm]
    o0 = n_in + n_comm
    outs, dsts = refs[o0:o0 + n_out], refs[o0 + n_out:o0 + n_out + n_comm]
    rest = refs[o0 + n_out + n_comm:]
    sems = rest[n_scratch:]
    return ins, outs, rest[:n_scratch], [(srcs[c], dsts[c]) + tuple(sems[3 * c:3 * c + 3]) for c in range(n_comm)]


def _host_exchanges(comm_refs, comms, is_first, is_last):
    if not comms:
        return

    @pl.when(is_first)
    def _():
        for refs, (_, scatter) in zip(comm_refs, comms):
            for op in _exchange_copies(*refs, scatter):
                op.start()

    @pl.when(is_last)
    def _():
        for refs, (_, scatter) in zip(comm_refs, comms):
            for op in _exchange_copies(*refs, scatter):
                op.wait()


def attn_fwd(q, k, v, name="attn_fwd", comms=()):
    t = q.shape[0]
    tq = tk = min(1024, t)
    rb = min(ATT_RB, tq)
    nq, nk, nr = t // tq, t // tk, tq // rb
    last = lambda i: i

    def body(*refs):
        (q_ref, k_ref, v_ref), (o_ref, lse_ref), (m_s, acc), comm_refs = _split_hosted(refs, 3, 2, 2, len(comms))
        h, i, j = pl.program_id(0), pl.program_id(1), pl.program_id(2)
        _host_exchanges(comm_refs, comms, (h == 0) & (i == 0) & (j == 0), (h == H - 1) & (i == nq - 1) & (j == nk - 1))

        @pl.when(j == 0)
        def _():
            m_s[...] = jnp.full_like(m_s, NEG)
            acc[...] = jnp.zeros_like(acc)

        one_col = jnp.where(lax.broadcasted_iota(jnp.int32, (tk, DK), 1) == 0, 1.0, 0.0).astype(MX)
        vx = jnp.concatenate([v_ref[...], one_col], axis=1)

        def rows_of(r):
            return pl.ds(r * rb, rb)

        def soft(r, s):
            m_old = m_s[rows_of(r), :]
            m_new = jnp.maximum(m_old, jnp.max(s, axis=-1, keepdims=True))
            m_s[rows_of(r), :] = m_new
            return jnp.exp2(s - m_new).astype(MX), jnp.exp2(m_old - m_new)

        def pv(r, p, alpha, vals):
            acc[rows_of(r), :] = alpha * acc[rows_of(r), :] + _dot(p, vals, 1, 0)

        def whole_block():
            kb = k_ref[...]
            ss = [_dot(q_ref[rows_of(r), :], kb, 1, 1) for r in range(min(2, nr))]
            pend = None
            for r in range(nr):
                if r + 2 < nr:
                    ss.append(_dot(q_ref[rows_of(r + 2), :], kb, 1, 1))
                p, alpha = soft(r, ss[r])
                if pend is not None:
                    pv(*pend, vx)
                pend = (r, p, alpha)
            pv(*pend, vx)

        def diagonal_block():
            for r in range(nr):
                cols = (r + 1) * rb
                s = _dot(q_ref[rows_of(r), :], k_ref[pl.ds(0, cols), :], 1, 1)
                qrow = lax.broadcasted_iota(jnp.int32, (rb, cols), 0) + r * rb
                kcol = lax.broadcasted_iota(jnp.int32, (rb, cols), 1)
                s = jnp.where(jnp.right_shift(kcol, 6) <= jnp.right_shift(qrow, 6), s, NEG)
                p, alpha = soft(r, s)
                pv(r, p, alpha, vx[:cols])

        pl.when(j < i)(whole_block)
        pl.when(j == i)(diagonal_block)

        @pl.when(j == nk - 1)
        def _():
            a = acc[...]
            l = a[:, DK:DK + 1]
            o_ref[...] = a[:, :DK] / l
            lse_ref[0] = m_s[...] + jnp.log2(l)

    in_specs = [pl.BlockSpec((tq, 256), lambda h, i, j: (i, h)),
                pl.BlockSpec((tk, 256), lambda h, i, j: (jnp.minimum(j, last(i)), h)),
                pl.BlockSpec((tk, DK), lambda h, i, j: (jnp.minimum(j, last(i)), h))]
    out_specs = [pl.BlockSpec((tq, DK), lambda h, i, j: (i, h)), pl.BlockSpec((1, tq, 1), lambda h, i, j: (h, i, 0))]
    out_shape = [jax.ShapeDtypeStruct((t, 512), f32), jax.ShapeDtypeStruct((H, t, 1), f32)]
    scratch = [pltpu.VMEM((tq, 1), f32), pltpu.VMEM((tq, 2 * DK), f32)]
    args = [q, k, v]
    _add_hosted(comms, in_specs, out_specs, out_shape, scratch, args)
    sem = ("arbitrary",) * 3 if comms else ("parallel", "parallel", "arbitrary")
    return pl.pallas_call(body, name=name, grid=(H, nq, nk), in_specs=in_specs, out_specs=out_specs, out_shape=out_shape,
                          scratch_shapes=scratch, compiler_params=_cparams(sem))(*args)


def attn_bwd(q, k, v, o, lse, do, name="attn_bwd", comms=()):
    t = q.shape[0]
    tk = min(1024, t)
    ratio = 2 if t % (2 * tk) == 0 else 1
    tq = ratio * tk
    rb = min(ATT_RB, tk)
    nq, nk, nr, kr = t // tq, t // tk, tq // rb, tk // rb
    first = lambda j: j // ratio

    def body(*refs):
        ins, outs, (dk_acc, dv_acc), comm_refs = _split_hosted(refs, 6, 3, 2, len(comms))
        q_ref, k_ref, v_ref, o_ref, lse_ref, do_ref = ins
        dq_ref, dk_ref, dv_ref = outs
        h, j, i = pl.program_id(0), pl.program_id(1), pl.program_id(2)
        _host_exchanges(comm_refs, comms, (h == 0) & (i == 0) & (j == 0), (h == H - 1) & (i == nq - 1) & (j == nk - 1))

        @pl.when((j == 0) & (i == 0))
        def _():
            dq_ref[...] = jnp.zeros_like(dq_ref)

        @pl.when(i == 0)
        def _():
            dk_acc[...] = jnp.zeros_like(dk_acc)
            dv_acc[...] = jnp.zeros_like(dv_acc)

        def rows_of(r):
            return pl.ds(r * rb, rb)

        def front(r, cols):
            qb, dob = q_ref[rows_of(r), :], do_ref[rows_of(r), :]
            return qb, dob, _dot(qb, k_ref[pl.ds(0, cols), :], 1, 1), _dot(dob.astype(MX), v_ref[pl.ds(0, cols), :], 1, 1)

        def middle(r, dob, s, dp):
            p = jnp.exp2(s - lse_ref[0, rows_of(r), :])
            delta = jnp.sum(dob * o_ref[rows_of(r), :], axis=-1, keepdims=True)
            return p.astype(MX), (p * (dp - delta)).astype(MX)

        def back(r, cols, qb, dob, pb, ds):
            dv_acc[pl.ds(0, cols), :] += _dot(pb, dob.astype(MX), 0, 0)
            dk_acc[pl.ds(0, cols), :] += _dot(ds, qb, 0, 0) * (1.0 / LOG2E)
            grow = pl.ds(pl.multiple_of(i * tq + r * rb, rb), rb)
            dq_ref[grow, :] += _dot(ds, k_ref[pl.ds(0, cols), :], 1, 0) * (1.0 / LOG2E)

        def whole_block():
            nxt = front(0, tk)
            for r in range(nr):
                qb, dob, s, dp = nxt
                if r + 1 < nr:
                    nxt = front(r + 1, tk)
                back(r, tk, qb, dob, *middle(r, dob, s, dp))

        def partial_block(part):
            def run():
                for r in range(nr):
                    rel = r - part * kr
                    if rel < 0:
                        continue
                    cols = min((rel + 1) * rb, tk)
                    qb, dob, s, dp = front(r, cols)
                    if rel < kr:
                        qrow = lax.broadcasted_iota(jnp.int32, (rb, cols), 0) + rel * rb
                        kcol = lax.broadcasted_iota(jnp.int32, (rb, cols), 1)
                        s = jnp.where(jnp.right_shift(kcol, 6) <= jnp.right_shift(qrow, 6), s, NEG)
                    back(r, cols, qb, dob, *middle(r, dob, s, dp))
            return run

        pl.when(j < i * ratio)(whole_block)
        for part in range(ratio):
            pl.when(j == i * ratio + part)(partial_block(part))

        @pl.when(i == nq - 1)
        def _():
            dk_ref[...] = dk_acc[...]
            dv_ref[...] = dv_acc[...]

    qi = lambda h, j, i: (jnp.maximum(i, first(j)), h)
    in_specs = [pl.BlockSpec((tq, 256), qi),
                pl.BlockSpec((tk, 256), lambda h, j, i: (j, h)),
                pl.BlockSpec((tk, DK), lambda h, j, i: (j, h)),
                pl.BlockSpec((tq, DK), qi),
                pl.BlockSpec((1, tq, 1), lambda h, j, i: (h, jnp.maximum(i, first(j)), 0)),
                pl.BlockSpec((tq, DK), qi)]
    out_specs = [pl.BlockSpec((t, 256), lambda h, j, i: (0, h)),
                 pl.BlockSpec((tk, 256), lambda h, j, i: (j, h)),
                 pl.BlockSpec((tk, DK), lambda h, j, i: (j, h))]
    out_shape = [jax.ShapeDtypeStruct((t, 1024), f32), jax.ShapeDtypeStruct((t, 1024), f32), jax.ShapeDtypeStruct((t, 512), f32)]
    scratch = [pltpu.VMEM((tk, 256), f32), pltpu.VMEM((tk, DK), f32)]
    args = [q, k, v, o, lse, do]
    _add_hosted(comms, in_specs, out_specs, out_shape, scratch, args)
    return pl.pallas_call(body, name=name, grid=(H, nk, nq), in_specs=in_specs, out_specs=out_specs, out_shape=out_shape,
                          scratch_shapes=scratch, compiler_params=_cparams(("arbitrary",) * 3))(*args)


def ffn_up(x1, wg, wu):
    t = x1.shape[0]
    tm, tn = min(512, t), _pick(DFF, 1536)

    def body(x_ref, wg_ref, wu_ref, g_ref, u_ref, a_ref):
        xb = x_ref[...].astype(MX)
        g = _dot(xb, wg_ref[...], 1, 0)
        u = _dot(xb, wu_ref[...], 1, 0)
        g_ref[...] = g.astype(g_ref.dtype)
        u_ref[...] = u.astype(u_ref.dtype)
        a_ref[...] = (g * jax.nn.sigmoid(g) * u).astype(a_ref.dtype)

    w_spec = pl.BlockSpec((D, tn), lambda i, j: (0, j))
    o_spec = pl.BlockSpec((tm, tn), lambda i, j: (i, j))
    return pl.pallas_call(
        body, name="ffn_up", grid=(t // tm, DFF // tn),
        in_specs=[pl.BlockSpec((tm, D), lambda i, j: (i, 0)), w_spec, w_spec],
        out_specs=[o_spec] * 3, out_shape=[jax.ShapeDtypeStruct((t, DFF), bf16)] * 3,
        compiler_params=_cparams(("parallel", "parallel")),
    )(x1, wg, wu)


def ffn_dact(ddown, wd, g, u):
    t = ddown.shape[0]
    tm, tn = min(512, t), _pick(DFF, 1536)

    def body(dd_ref, wd_ref, g_ref, u_ref, dg_ref, du_ref):
        dact = _dot(dd_ref[...].astype(MX), wd_ref[...], 1, 1)
        gv, uv = g_ref[...].astype(f32), u_ref[...].astype(f32)
        sig = jax.nn.sigmoid(gv)
        dg_ref[...] = (dact * uv * sig * (1.0 + gv * (1.0 - sig))).astype(dg_ref.dtype)
        du_ref[...] = (dact * gv * sig).astype(du_ref.dtype)

    o_spec = pl.BlockSpec((tm, tn), lambda i, j: (i, j))
    return pl.pallas_call(
        body, name="ffn_dact", grid=(t // tm, DFF // tn),
        in_specs=[pl.BlockSpec((tm, D), lambda i, j: (i, 0)), pl.BlockSpec((tn, D), lambda i, j: (j, 0)), o_spec, o_spec],
        out_specs=[o_spec] * 2, out_shape=[jax.ShapeDtypeStruct((t, DFF), bf16)] * 2,
        compiler_params=_cparams(("parallel", "parallel")),
    )(ddown, wd, g, u)


def loss_head(y, target):
    t = y.shape[0]
    tb = min(ROW_TB, t)
    n = t // tb

    def body(y_ref, t_ref, dy_ref, loss_ref, acc):
        i = pl.program_id(0)

        @pl.when(i == 0)
        def _():
            acc[...] = jnp.zeros_like(acc)

        e = y_ref[...] - t_ref[...]
        dy_ref[...] = e * (1.0 / D)
        acc[...] += jnp.sum(e * e, axis=0, keepdims=True)

        @pl.when(i == n - 1)
        def _():
            loss_ref[...] = jnp.sum(acc[...], axis=1, keepdims=True) * (0.5 / D)

    return pl.pallas_call(
        body, name="loss_head", grid=(n,),
        in_specs=[_row_spec(tb, D, 0)] * 2,
        out_specs=[_row_spec(tb, D, 0), _full_spec((1, 1))],
        out_shape=[jax.ShapeDtypeStruct((t, D), f32), jax.ShapeDtypeStruct((1, 1), f32)],
        scratch_shapes=[pltpu.VMEM((1, D), f32)],
        compiler_params=_cparams(("arbitrary",)),
    )(y, target)


def _me_and_peers():
    x, y, c = lax.axis_index("x"), lax.axis_index("y"), lax.axis_index("c")
    me = 4 * x + 2 * y + c
    peers = []
    for kk in range(1, N_DEV):
        px = 1 - x if kk & 4 else x
        py = 1 - y if kk & 2 else y
        pc = 1 - c if kk & 1 else c
        peers.append(((px, py, pc), 4 * px + 2 * py + pc))
    return me, peers


_ANY = pl.BlockSpec(memory_space=pl.ANY)


def _comm_scratch():
    return [pltpu.SemaphoreType.DMA((N_DEV - 1,)), pltpu.SemaphoreType.DMA((N_DEV - 1,)), pltpu.SemaphoreType.DMA]


def _exchange_copies(src_ref, out_ref, send_sems, recv_sems, local_sem, scatter):
    me, peers = _me_and_peers()
    pick = (lambda d: src_ref.at[d]) if scatter else (lambda d: src_ref)
    ops = [pltpu.make_async_copy(pick(me), out_ref.at[me], local_sem)]
    ops += [pltpu.make_async_remote_copy(src_ref=pick(pid), dst_ref=out_ref.at[me], send_sem=send_sems.at[kk],
                                         recv_sem=recv_sems.at[kk], device_id=dev, device_id_type=pl.DeviceIdType.MESH)
            for kk, (dev, pid) in enumerate(peers)]
    return ops


def _comm_out_shape(src, scatter):
    return jax.ShapeDtypeStruct(src.shape if scatter else (N_DEV,) + src.shape, src.dtype)


def exchange(name, comms):
    def body(*refs):
        _, _, _, comm_refs = _split_hosted(refs, 0, 0, 0, len(comms))
        ops = [op for refs_c, (_, scatter) in zip(comm_refs, comms) for op in _exchange_copies(*refs_c, scatter)]
        for op in ops:
            op.start()
        for op in ops:
            op.wait()

    in_specs, out_specs, out_shape, scratch, args = [], [], [], [], []
    _add_hosted(comms, in_specs, out_specs, out_shape, scratch, args)
    return pl.pallas_call(body, name=name, in_specs=in_specs, out_specs=out_specs, out_shape=out_shape, scratch_shapes=scratch)(*args)


def adamw(w, m, v, parts):
    nl, r, c = w.shape
    lanes = -(-c // LANE) * LANE
    tb = _pick_rows(r, max(8, ADAM_PART_BLOCK_BYTES // (N_DEV * lanes * 4)))
    n = r // tb

    def body(*refs):
        w_ref, m_ref, v_ref = refs[:3]
        p_refs = refs[3:3 + nl]
        g_ref, d_ref, nm_ref, nv_ref = refs[3 + nl:]
        layer = pl.program_id(0)
        for k in range(nl):
            @pl.when(layer == k)
            def _(p_ref=p_refs[k]):
                g = p_ref[0]
                for s in range(1, N_DEV):
                    g = g + p_ref[s]
                g_ref[0] = g

        g = g_ref[0]
        nm = ADAM_B1 * m_ref[0] + (1.0 - ADAM_B1) * g
        nv = ADAM_B2 * v_ref[0] + (1.0 - ADAM_B2) * jnp.square(g)
        m_hat = nm / (1.0 - ADAM_B1 ** ADAM_STEP)
        v_hat = nv / (1.0 - ADAM_B2 ** ADAM_STEP)
        d_ref[0] = -ADAM_LR * (m_hat / (jnp.sqrt(v_hat) + ADAM_EPS) + ADAM_WD * w_ref[0])
        nm_ref[0] = nm
        nv_ref[0] = nv

    def part_spec(k):
        return pl.BlockSpec((N_DEV, tb, c), lambda l, i: (0, jnp.where(l == k, i, jnp.where(l > k, n - 1, 0)), 0))

    spec = pl.BlockSpec((1, tb, c), lambda l, i: (l, i, 0))
    return pl.pallas_call(
        body, name="adamw", grid=(nl, n), in_specs=[spec] * 3 + [part_spec(k) for k in range(nl)], out_specs=[spec] * 4,
        out_shape=[jax.ShapeDtypeStruct((nl, r, c), f32)] * 4, compiler_params=_cparams(("arbitrary", "arbitrary")),
    )(w, m, v, *parts)


ADAM_PART_BLOCK_BYTES = 4 << 20


def _pick_rows(r, cap=ROW_TB):
    return max(tb for tb in range(8, min(cap, ROW_TB) + 1, 8) if r % tb == 0)


BIG = ("w_in", "w_uq", "w_ukv", "w_out", "w_gate_up", "w_down", "w_ple", "w_ple_gate")
COL_SHARDED = ("w_in", "w_uq", "w_ukv", "w_gate_up", "w_ple")
SMALL = ("conv_w", "a_log", "dt_bias", "gdn_norm_g", "q_norm_g", "kv_norm_g", "ln1_g", "ln1_b", "ln2_g", "ln2_b")
ROW_ALIGN = 16


def _pad_rows(a, mult=ROW_ALIGN, axis=0):
    pad = (-a.shape[axis]) % mult
    widths = [(0, 0)] * a.ndim
    widths[axis] = (0, pad)
    return a if pad == 0 else jnp.pad(a, widths)


GATHER_EARLY = ("w_in", "w_uq", "w_ukv")
GATHER_LATE = ("w_out", "w_gate_up", "w_down", "w_ple", "w_ple_gate")
GRAD_EARLY = ("w_gate_up", "w_down", "w_ple", "w_ple_gate", "w_out")
GRAD_LATE = ("w_in", "w_uq", "w_ukv")


def _gathers(tree, l, names):
    return [(tree[n][l].astype(bf16), False) for n in names]


def _full_weight(gathered, n, lo=0, hi=N_DEV):
    blk = gathered[lo:hi]
    k, sr, sc = blk.shape
    if n in COL_SHARDED:
        return jnp.transpose(blk, (1, 0, 2)).reshape(sr, k * sc)
    return blk.reshape(k * sr, sc)


def _shard_grad(gfull, n):
    if n in COL_SHARDED:
        sr = gfull.shape[0]
        k = gfull.shape[1] // (FULL_COLS[n] // N_DEV)
        return jnp.transpose(gfull.reshape(sr, k, gfull.shape[1] // k), (1, 0, 2))
    return gfull.reshape(N_DEV, gfull.shape[0] // N_DEV, gfull.shape[1])


FULL_COLS = {"w_in": IN_W, "w_uq": 768, "w_ukv": 1024, "w_gate_up": 2 * DFF, "w_ple": D}


def _pack_small(tree, extra):
    flat = jnp.concatenate([tree[n].reshape(-1).astype(f32) for n in SMALL] + [extra.reshape(-1).astype(f32)])
    return jnp.pad(flat, (0, (-flat.shape[0]) % (ROW_ALIGN * D))).reshape(-1, D)


def _unpack_small(rows, like):
    flat = rows.reshape(-1)
    out, off = {}, 0
    for n in SMALL:
        sz = int(np.prod(like[n].shape))
        out[n] = flat[off:off + sz].reshape(like[n].shape)
        off += sz
    return out, flat[off]


def _zeros(r, c, dt):
    return jnp.zeros((r, c), dt)


def _prep_w_in(w):
    dt = w.dtype
    kr = w[:, 2696:2760]
    return jnp.concatenate([
        w[:, 0:2048],
        w[:, 2440:2696],
        kr, w[:, 2048:2056], _zeros(D, 56, dt),
        kr[:, 32:], kr[:, :32], _zeros(D, 64, dt),
        w[:, 2056:2440], _zeros(D, CQP - QL, dt)], axis=1)


def _unprep_w_in(g):
    krs = g[:, OFF_KRS:OFF_KRS + 64]
    kr = g[:, OFF_KR:OFF_KR + 64] + jnp.concatenate([krs[:, 32:], krs[:, :32]], axis=1)
    return jnp.concatenate([g[:, 0:2048], g[:, OFF_KR + 64:OFF_KR + 72], g[:, OFF_CQ:OFF_CQ + QL],
                            g[:, OFF_CKV:OFF_CKV + KVL], kr], axis=1)


def _prep_w_uq(w):
    dt = w.dtype
    z64 = _zeros(QL, 64, dt)
    nope, ra, rb = [], [], []
    for h in range(H):
        nope.append(w[:, 192 * h:192 * h + 128])
        x1, x2 = w[:, 192 * h + 128:192 * h + 160], w[:, 192 * h + 160:192 * h + 192]
        ra += [x1, x2, z64]
        rb += [x2, x1, z64]
    return jnp.concatenate([jnp.concatenate(nope + ra + rb, axis=1), _zeros(CQP - QL, 1536, dt)], axis=0)


def _unprep_w_uq(g):
    g = g[:QL]
    cols = []
    for h in range(H):
        a = g[:, 512 + 128 * h:512 + 128 * h + 64]
        b = g[:, 1024 + 128 * h:1024 + 128 * h + 64]
        cols += [g[:, 128 * h:128 * (h + 1)], a[:, :32] + b[:, 32:], a[:, 32:] + b[:, :32]]
    return jnp.concatenate(cols, axis=1)


def _rope_tables(positions):
    inv_freq = ROPE_THETA ** (-jnp.arange(0, ROPE, 2, dtype=f32) / ROPE)
    ang = positions.astype(f32)[:, None] * inv_freq
    c, s = jnp.cos(ang), jnp.sin(ang)
    z = jnp.zeros((positions.shape[0], 64), f32)
    return jnp.concatenate([c, c, z], axis=1), jnp.concatenate([-s, s, z], axis=1)


def _tile_heads(vec, n=H):
    return jnp.tile(vec.reshape(1, -1), (1, n))


def _bcast_heads(vec):
    return jnp.repeat(vec, DK).reshape(1, H * DK)


def kernel(x, p, positions, w_in, conv_w, a_log, dt_bias, gdn_norm_g, q_norm_g, w_uq, kv_norm_g, w_ukv, w_out, ln1_g, ln1_b, w_gate_up, w_down, ln2_g, ln2_b, w_ple, w_ple_gate, loss_target, m_w_in, m_conv_w, m_a_log, m_dt_bias, m_gdn_norm_g, m_q_norm_g, m_w_uq, m_kv_norm_g, m_w_ukv, m_w_out, m_ln1_g, m_ln1_b, m_w_gate_up, m_w_down, m_ln2_g, m_ln2_b, m_w_ple, m_w_ple_gate, v_w_in, v_conv_w, v_a_log, v_dt_bias, v_gdn_norm_g, v_q_norm_g, v_w_uq, v_kv_norm_g, v_w_ukv, v_w_out, v_ln1_g, v_ln1_b, v_w_gate_up, v_w_down, v_ln2_g, v_ln2_b, v_w_ple, v_w_ple_gate):
    W = dict(w_in=w_in, conv_w=conv_w, a_log=a_log, dt_bias=dt_bias, gdn_norm_g=gdn_norm_g, q_norm_g=q_norm_g, w_uq=w_uq,
             kv_norm_g=kv_norm_g, w_ukv=w_ukv, w_out=w_out, ln1_g=ln1_g, ln1_b=ln1_b, w_gate_up=w_gate_up, w_down=w_down,
             ln2_g=ln2_g, ln2_b=ln2_b, w_ple=w_ple, w_ple_gate=w_ple_gate)
    M = dict(w_in=m_w_in, conv_w=m_conv_w, a_log=m_a_log, dt_bias=m_dt_bias, gdn_norm_g=m_gdn_norm_g, q_norm_g=m_q_norm_g,
             w_uq=m_w_uq, kv_norm_g=m_kv_norm_g, w_ukv=m_w_ukv, w_out=m_w_out, ln1_g=m_ln1_g, ln1_b=m_ln1_b,
             w_gate_up=m_w_gate_up, w_down=m_w_down, ln2_g=m_ln2_g, ln2_b=m_ln2_b, w_ple=m_w_ple, w_ple_gate=m_w_ple_gate)
    V = dict(w_in=v_w_in, conv_w=v_conv_w, a_log=v_a_log, dt_bias=v_dt_bias, gdn_norm_g=v_gdn_norm_g, q_norm_g=v_q_norm_g,
             w_uq=v_w_uq, kv_norm_g=v_kv_norm_g, w_ukv=v_w_ukv, w_out=v_w_out, ln1_g=v_ln1_g, ln1_b=v_ln1_b,
             w_gate_up=v_w_gate_up, w_down=v_w_down, ln2_g=v_ln2_g, ln2_b=v_ln2_b, w_ple=v_w_ple, w_ple_gate=v_w_ple_gate)
    shards = {n: W[n].shape for n in BIG}
    t = x.shape[1]
    xin = x.reshape(t, D)
    target = loss_target.reshape(t, D)
    cf, sf = _rope_tables(positions.reshape(t))

    conv_rows = _pad_rows(jnp.pad(conv_w.reshape(-1), (0, (-conv_w.size) % D)).reshape(-1, D), 8)
    first = exchange("all_gather_first", _gathers(W, 0, GATHER_EARLY) + [(conv_rows, False)])
    early = [dict(zip(GATHER_EARLY, first[:-1]))] + [None] * (DEPTH - 1)
    csz = conv_w.shape[1] * conv_w.shape[2]
    conv_full = jnp.transpose(first[-1].reshape(N_DEV, -1)[:, :DEPTH * csz].reshape(N_DEV, DEPTH, CW, -1), (1, 2, 0, 3)).reshape(DEPTH, CW, 1536)

    acts = []
    h_cur = xin
    for l in range(DEPTH):
        win = _prep_w_in(_full_weight(early[l]["w_in"], "w_in"))
        wq = _prep_w_uq(_full_weight(early[l]["w_uq"], "w_uq"))
        wkv = _full_weight(early[l]["w_ukv"], "w_ukv")
        cw8 = jnp.concatenate([conv_full[l], jnp.zeros((8 - CW, 1536), f32)], axis=0)
        alog_b, dtb_b = _bcast_heads(a_log[l]), _bcast_heads(dt_bias[l])
        gg_b = _tile_heads(gdn_norm_g[l])
        qg = jnp.concatenate([q_norm_g[l], jnp.zeros((CQP - QL,), f32)]).reshape(1, CQP)
        kvg = kv_norm_g[l].reshape(1, KVL)
        g1, b1, g2, b2 = (a[l].reshape(1, D) for a in (ln1_g, ln1_b, ln2_g, ln2_b))
        p_l = p[l].reshape(t, PLE)

        hb = mm("in_proj", h_cur, win, mode="nn", out_dtype=f32)
        qn, kn, vs, beta, glog = gdn_pre_fwd(hb, cw8, alog_b, dtb_b)
        g_u, g_w, g_qd, g_kd, g_at, g_el = gdn_intra(qn, kn, vs, beta, glog)
        o_gdn, states = gdn_scan_fwd(g_u, g_w, g_qd, g_kd, g_at, g_el)
        mla_rows = [(hb, CQP, OFF_CQ // CQP), (hb, KVL, OFF_CKV // KVL), (hb, LANE, OFF_KR // LANE), (hb, LANE, OFF_KRS // LANE),
                    (cf, LANE, 0), (sf, LANE, 0)]
        mla_consts = [qg, wq, kvg, wkv]
        qm, km, vm = row_fwd("mla_pre", mla_pre_f, mla_rows, mla_consts, [(1024, bf16), (1024, bf16), (512, bf16)])
        comms = _gathers(W, l, GATHER_LATE) + (_gathers(W, l + 1, GATHER_EARLY) if l + 1 < DEPTH else [])
        res = attn_fwd(qm, km, vm, name=f"attn_fwd_gather{len(comms)}", comms=comms)
        o_mla, lse = res[:2]
        late = dict(zip(GATHER_LATE, res[2:]))
        if l + 1 < DEPTH:
            early[l + 1] = dict(zip(GATHER_EARLY, res[2 + len(GATHER_LATE):]))
        wout, wd = _full_weight(late["w_out"], "w_out"), _full_weight(late["w_down"], "w_down")
        half = N_DEV // 2
        wg, wu = _full_weight(late["w_gate_up"], "w_gate_up", 0, half), _full_weight(late["w_gate_up"], "w_gate_up", half, N_DEV)
        wple, wpg = _full_weight(late["w_ple"], "w_ple"), _full_weight(late["w_ple_gate"], "w_ple_gate")
        p1_rows = [(o_gdn, 512, 0), (hb, 512, OFF_Z // 512), (o_mla, 512, 0), (h_cur, D, 0)]
        p1_consts = [gg_b, wout, g1, b1]
        (x1,) = row_fwd("post1", post1_f, p1_rows, p1_consts, [(D, f32)])
        gate, up, act = ffn_up(x1, wg, wu)
        down = mm("ffn_down", act, wd, mode="nn", out_dtype=f32)
        p2_rows = [(x1, D, 0), (down, D, 0), (p_l, PLE, 0)]
        p2_consts = [g2, b2, wpg, wple]
        (y,) = row_fwd("post2", post2_f, p2_rows, p2_consts, [(D, f32)])
        acts.append(dict(x=h_cur, hb=hb, qn=qn, kn=kn, vs=vs, beta=beta, glog=glog, states=states, o_gdn=o_gdn, qm=qm, km=km,
                         scan=(g_w, g_qd, g_kd, g_at, g_el),
                         vm=vm, o_mla=o_mla, lse=lse, x1=x1, gate=gate, up=up, act=act, win=win, wd=wd, wg=wg, wu=wu,
                         cw8=cw8, alog_b=alog_b, dtb_b=dtb_b, mla_rows=mla_rows, mla_consts=mla_consts, p1_rows=p1_rows,
                         p1_consts=p1_consts, p2_rows=p2_rows, p2_consts=p2_consts))
        h_cur = y

    dy, loss_part = loss_head(h_cur, target)

    G = {n: [None] * DEPTH for n in SMALL}
    S = {n: [None] * DEPTH for n in BIG}
    parts = {n: [None] * DEPTH for n in BIG}
    pending = None
    for l in reversed(range(DEPTH)):
        a = acts[l]
        dx1, ddown, dg2, db2, dwpg, dwple = row_bwd("post2_bwd", post2_f, a["p2_rows"], a["p2_consts"], [dy], [0, 1], [0, 1, 2, 3], [f32, f32])
        dgate, dup = ffn_dact(ddown, a["wd"], a["gate"], a["up"])
        dwd = mm("dw_down", a["act"], ddown, mode="tn", out_dtype=f32)
        dx1 = mm("dx_gate", dgate, a["wg"], mode="nt", out_dtype=f32, add=dx1)
        dx1 = mm("dx_up", dup, a["wu"], mode="nt", out_dtype=f32, add=dx1)
        dwg = mm("dw_gate", a["x1"], dgate, mode="tn", out_dtype=f32)
        dwu = mm("dw_up", a["x1"], dup, mode="tn", out_dtype=f32)
        do_gdn, dz, do_mla, dxr, dgg, dwout, dg1, db1 = row_bwd("post1_bwd", post1_f, a["p1_rows"], a["p1_consts"], [dx1], [0, 1, 2, 3],
                                                              [0, 1, 2, 3], [f32, bf16, f32, f32])
        for g, n in ((dwd, "w_down"), (dwple, "w_ple"), (dwpg, "w_ple_gate"), (dwout, "w_out")):
            S[n][l] = _shard_grad(g, n)
        S["w_gate_up"][l] = jnp.concatenate([_shard_grad(dwg, "w_gate_up"), _shard_grad(dwu, "w_gate_up")], axis=0)
        res = attn_bwd(a["qm"], a["km"], a["vm"], a["o_mla"], a["lse"], do_mla, name="attn_bwd_exchange",
                       comms=[(S[n][l], True) for n in GRAD_EARLY])
        dqm, dkm, dvm = res[:3]
        for n, got in zip(GRAD_EARLY, res[3:]):
            parts[n][l] = got
        dstates = gdn_scan_bwd(do_gdn, *a["scan"])
        comms = [(S[n][pending], True) for n in GRAD_LATE] if pending is not None else []
        res = gdn_local_bwd(a["qn"], a["kn"], a["vs"], a["beta"], a["glog"], a["states"], dstates, do_gdn,
                            name="gdn_local_bwd_exchange" if comms else "gdn_local_bwd", comms=comms)
        dqn, dkn, dvs, dbeta, dglog = res[:5]
        for n, got in zip(GRAD_LATE, res[5:]):
            parts[n][pending] = got
        dqkv, dba, dcw, dal, ddt = gdn_pre_bwd(a["hb"], a["cw8"], a["alog_b"], a["dtb_b"], dqn, dkn, dvs, dbeta, dglog)
        dcq, dckv, dkrg, dkrs, dqg, dwq, dkvg, dwkv = row_bwd("mla_pre_bwd", mla_pre_f, a["mla_rows"], a["mla_consts"], [dqm, dkm, dvm],
                                                              [0, 1, 2, 3], [0, 1, 2, 3], [bf16, bf16, bf16, bf16], row_add={2: dba})
        dh = jnp.concatenate([dqkv, dz, dckv, dkrg, dkrs, dcq], axis=1)
        dwin = mm("dw_in", a["x"], dh, mode="tn", out_dtype=f32)
        dy = mm("dx_in", dh, a["win"], mode="nt", out_dtype=f32, add=dxr)

        S["w_in"][l], S["w_uq"][l] = _shard_grad(_unprep_w_in(dwin), "w_in"), _shard_grad(_unprep_w_uq(dwq), "w_uq")
        S["w_ukv"][l] = _shard_grad(dwkv, "w_ukv")
        G["conv_w"][l] = dcw[:CW]
        G["a_log"][l] = jnp.sum(dal.reshape(H, DK), axis=1)
        G["dt_bias"][l] = jnp.sum(ddt.reshape(H, DK), axis=1)
        G["gdn_norm_g"][l] = jnp.sum(dgg.reshape(H, DK), axis=0)
        G["q_norm_g"][l] = dqg[0, :QL]
        G["kv_norm_g"][l] = dkvg[0]
        G["ln1_g"][l], G["ln1_b"][l], G["ln2_g"][l], G["ln2_b"][l] = dg1[0], db1[0], dg2[0], db2[0]
        pending = l
    grad_x = dy.reshape(x.shape)

    small_like = {n: W[n] for n in SMALL}
    conv_parts = jnp.stack(G["conv_w"]).reshape(DEPTH, CW, N_DEV, -1)
    smalls = []
    for d in range(N_DEV):
        tree = {n: jnp.stack(G[n]) for n in SMALL if n != "conv_w"}
        tree["conv_w"] = conv_parts[:, :, d, :]
        smalls.append(_pack_small(tree, loss_part))
    small_send = jnp.stack(smalls)
    res = exchange("exchange_last", [(S[n][pending], True) for n in GRAD_LATE] + [(small_send, True)])
    for n, got in zip(GRAD_LATE, res):
        parts[n][pending] = got

    g_out, d_out, m_out, v_out = {}, {}, {}, {}
    for n in BIG:
        g_out[n], d_out[n], m_out[n], v_out[n] = adamw(W[n], M[n], V[n], parts[n])
    zero = jnp.zeros((), f32)
    upd = adamw(_pack_small(W, zero)[None], _pack_small(M, zero)[None], _pack_small(V, zero)[None], [res[-1]])
    for dst, rows_arr in zip((g_out, d_out, m_out, v_out), upd):
        dst.update(_unpack_small(rows_arr[0], small_like)[0])
    loss = _unpack_small(upd[0][0], small_like)[1]
    order = ["w_in", "conv_w", "a_log", "dt_bias", "gdn_norm_g", "q_norm_g", "w_uq", "kv_norm_g", "w_ukv", "w_out", "ln1_g", "ln1_b",
             "w_gate_up", "w_down", "ln2_g", "ln2_b", "w_ple", "w_ple_gate"]
    return (loss, grad_x, *[g_out[n] for n in order], *[d_out[n] for n in order], *[m_out[n] for n in order],
            *[v_out[n] for n in order])
```

```python
import jax
import jax.numpy as jnp
import numpy as np
from jax import lax
from jax.experimental import pallas as pl
from jax.experimental.pallas import tpu as pltpu

f32 = jnp.float32
bf16 = jnp.bfloat16
MX = jnp.bfloat16

D = 1024
DEPTH = 4
CHUNK = 64
H = 4
DK = 128
PLE = 256
QL = 384
KVL = 256
ROPE = 64
DFF = 2816
IN_W = 2760
ROPE_THETA = 10000.0
ALPHA = (2.0 * DEPTH) ** 0.25
LN_EPS = 1e-5
RMS_EPS = 1e-6
ATT_SCALE = (128 + 64) ** -0.5
N_DEV = 8

ADAM_LR, ADAM_B1, ADAM_B2, ADAM_EPS, ADAM_WD, ADAM_STEP = 0.001, 0.9, 0.999, 1e-08, 0.01, 10

OFF_QKV, OFF_Z, OFF_CKV, OFF_KR, OFF_KRS, OFF_CQ, HP = 0, 1536, 2048, 2304, 2432, 2560, 3072
CQP = 512
LANE = 128
VMEM_LIMIT = 48 * 1024 * 1024
ROW_TB = 256


def _dot(a, b, ca, cb, prec=None):
    if a.ndim == 3:
        return lax.dot_general(a, b, (((ca + 1,), (cb + 1,)), ((0,), (0,))), precision=prec, preferred_element_type=f32)
    return lax.dot_general(a, b, (((ca,), (cb,)), ((), ())), precision=prec, preferred_element_type=f32)


@jax.custom_vjp
def bdot(a, w):
    return _dot(a.astype(MX), w.astype(MX), 1, 0)


def _bdot_fwd(a, w):
    return bdot(a, w), (a, w)


def _bdot_bwd(res, g):
    a, w = res
    gb = g.astype(MX)
    return _dot(gb, w.astype(MX), 1, 1).astype(a.dtype), _dot(a.astype(MX), gb, 0, 0).astype(w.dtype)


bdot.defvjp(_bdot_fwd, _bdot_bwd)


@jax.custom_vjp
def bdot_nt(a, b):
    return _dot(a.astype(MX), b.astype(MX), 1, 1)


def _bdot_nt_fwd(a, b):
    return bdot_nt(a, b), (a, b)


def _bdot_nt_bwd(res, g):
    a, b = res
    gb = g.astype(MX)
    return _dot(gb, b.astype(MX), 1, 0).astype(a.dtype), _dot(gb, a.astype(MX), 0, 0).astype(b.dtype)


bdot_nt.defvjp(_bdot_nt_fwd, _bdot_nt_bwd)


@jax.custom_vjp
def bdot_tn(a, b):
    return _dot(a.astype(MX), b.astype(MX), 0, 0)


def _bdot_tn_fwd(a, b):
    return bdot_tn(a, b), (a, b)


def _bdot_tn_bwd(res, g):
    a, b = res
    gb = g.astype(MX)
    return _dot(b.astype(MX), gb, 1, 1).astype(a.dtype), _dot(a.astype(MX), gb, 1, 0).astype(b.dtype)


bdot_tn.defvjp(_bdot_tn_fwd, _bdot_tn_bwd)


@jax.custom_vjp
def hdot(a, b):
    return _dot(a, b, 1, 0, lax.Precision.HIGHEST)


def _hdot_fwd(a, b):
    return hdot(a, b), (a, b)


def _hdot_bwd(res, g):
    a, b = res
    return _dot(g, b, 1, 1, lax.Precision.HIGHEST), _dot(a, g, 0, 0, lax.Precision.HIGHEST)


hdot.defvjp(_hdot_fwd, _hdot_bwd)


def _pick(n, cap):
    best = None
    for t in range(LANE, min(n, cap) + 1, LANE):
        if n % t == 0:
            best = t
    assert best is not None, (n, cap)
    return best


def _cparams(sem):
    return pltpu.CompilerParams(dimension_semantics=sem, vmem_limit_bytes=VMEM_LIMIT)


def mm(name, a, b, *, mode, out_dtype, add=None):
    if mode == "tn":
        kdim, m = a.shape
        n = b.shape[1]
        tm, tn, tk = _pick(m, 1536), _pick(n, 1536), min(512, kdim)
        nk = kdim // tk

        def body(a_ref, b_ref, o_ref, acc):
            k = pl.program_id(2)

            @pl.when(k == 0)
            def _():
                acc[...] = jnp.zeros_like(acc)

            acc[...] += _dot(a_ref[...].astype(MX), b_ref[...].astype(MX), 0, 0)

            @pl.when(k == nk - 1)
            def _():
                o_ref[...] = acc[...].astype(o_ref.dtype)

        return pl.pallas_call(
            body, name=name, grid=(m // tm, n // tn, nk),
            in_specs=[pl.BlockSpec((tk, tm), lambda i, j, k: (k, i)), pl.BlockSpec((tk, tn), lambda i, j, k: (k, j))],
            out_specs=pl.BlockSpec((tm, tn), lambda i, j, k: (i, j)),
            out_shape=jax.ShapeDtypeStruct((m, n), out_dtype),
            scratch_shapes=[pltpu.VMEM((tm, tn), f32)],
            compiler_params=_cparams(("parallel", "parallel", "arbitrary")),
        )(a, b)

    m, kdim = a.shape
    n = b.shape[1] if mode == "nn" else b.shape[0]
    tm, tn = min(512, m), _pick(n, 1536)
    has_add = add is not None

    def body(*refs):
        a_ref, b_ref = refs[0], refs[1]
        o_ref = refs[-1]
        r = _dot(a_ref[...].astype(MX), b_ref[...].astype(MX), 1, 0 if mode == "nn" else 1)
        if has_add:
            r = r + refs[2][...].astype(f32)
        o_ref[...] = r.astype(o_ref.dtype)

    b_spec = (pl.BlockSpec((kdim, tn), lambda i, j: (0, j)) if mode == "nn"
              else pl.BlockSpec((tn, kdim), lambda i, j: (j, 0)))
    in_specs = [pl.BlockSpec((tm, kdim), lambda i, j: (i, 0)), b_spec]
    args = [a, b]
    if has_add:
        in_specs.append(pl.BlockSpec((tm, tn), lambda i, j: (i, j)))
        args.append(add)
    return pl.pallas_call(
        body, name=name, grid=(m // tm, n // tn), in_specs=in_specs,
        out_specs=pl.BlockSpec((tm, tn), lambda i, j: (i, j)),
        out_shape=jax.ShapeDtypeStruct((m, n), out_dtype),
        compiler_params=_cparams(("parallel", "parallel")),
    )(*args)


def _row_spec(tb, w, cb):
    return pl.BlockSpec((tb, w), lambda i: (i, cb))


def _full_spec(shape):
    return pl.BlockSpec(shape, lambda i: (0,) * len(shape))


def row_fwd(name, f, rows, consts, outs):
    t = rows[0][0].shape[0]
    tb = min(ROW_TB, t)
    nr, nc = len(rows), len(consts)

    def body(*refs):
        vals = [r[...] for r in refs[:nr + nc]]
        res = f(*vals)
        for o_ref, val in zip(refs[nr + nc:], res):
            o_ref[...] = val.astype(o_ref.dtype)

    return pl.pallas_call(
        body, name=name, grid=(t // tb,),
        in_specs=[_row_spec(tb, w, cb) for _, w, cb in rows] + [_full_spec(c.shape) for c in consts],
        out_specs=[_row_spec(tb, w, 0) for w, _ in outs],
        out_shape=[jax.ShapeDtypeStruct((t, w), dt) for w, dt in outs],
        compiler_params=_cparams(("parallel",)),
    )(*[r[0] for r in rows], *consts)


def row_bwd(name, f, rows, consts, cots, row_diff, const_diff, drow_dtypes, row_add=None):
    t = rows[0][0].shape[0]
    tb = min(ROW_TB, t)
    nr, nc, nct = len(rows), len(consts), len(cots)
    row_add = row_add or {}
    add_keys = sorted(row_add)
    n_in = nr + nc + nct + len(add_keys)

    def body(*refs):
        i = pl.program_id(0)
        rv = [r[...] for r in refs[:nr]]
        cv = [refs[nr + k][...].astype(f32) if k in const_diff else refs[nr + k][...] for k in range(nc)]
        cot_refs = refs[nr + nc:nr + nc + nct]
        add_refs = refs[nr + nc + nct:n_in]
        drow_refs = refs[n_in:n_in + len(row_diff)]
        dconst_refs = refs[n_in + len(row_diff):]

        def g(*dv):
            r2, c2 = list(rv), list(cv)
            for p, k in enumerate(row_diff):
                r2[k] = dv[p]
            for p, k in enumerate(const_diff):
                c2[k] = dv[len(row_diff) + p]
            return tuple(f(*r2, *c2))

        prim = [rv[k].astype(f32) for k in row_diff] + [cv[k] for k in const_diff]
        outs, vf = jax.vjp(g, *prim)
        grads = vf(tuple(c[...].astype(o.dtype) for c, o in zip(cot_refs, outs)))
        for p, ref in enumerate(drow_refs):
            val = grads[p]
            if p in row_add:
                val = val + add_refs[add_keys.index(p)][...].astype(f32)
            ref[...] = val.astype(ref.dtype)

        @pl.when(i == 0)
        def _():
            for ref in dconst_refs:
                ref[...] = jnp.zeros_like(ref)

        for p, ref in enumerate(dconst_refs):
            ref[...] += grads[len(row_diff) + p]

    widths = [rows[k][1] for k in row_diff]
    return pl.pallas_call(
        body, name=name, grid=(t // tb,),
        in_specs=([_row_spec(tb, w, cb) for _, w, cb in rows] + [_full_spec(c.shape) for c in consts]
                  + [_row_spec(tb, c.shape[1], 0) for c in cots] + [_row_spec(tb, row_add[k].shape[1], 0) for k in add_keys]),
        out_specs=([_row_spec(tb, w, 0) for w in widths] + [_full_spec(consts[k].shape) for k in const_diff]),
        out_shape=([jax.ShapeDtypeStruct((t, w), dt) for w, dt in zip(widths, drow_dtypes)]
                   + [jax.ShapeDtypeStruct(consts[k].shape, f32) for k in const_diff]),
        compiler_params=_cparams(("arbitrary",)),
    )(*[r[0] for r in rows], *consts, *cots, *[row_add[k] for k in add_keys])


def _heads(x, w=DK):
    return [x[:, w * h:w * (h + 1)] for h in range(H)]


def _layer_norm(r, g, b):
    mu = jnp.mean(r, -1, keepdims=True)
    var = jnp.mean(jnp.square(r - mu), -1, keepdims=True)
    return (r - mu) * lax.rsqrt(var + LN_EPS) * g + b


def gdn_point_f(c, ba, alog_b, dtb_b):
    s = c * jax.nn.sigmoid(c)
    q, k, v = s[:, :512], s[:, 512:1024], s[:, 1024:]

    def l2(x):
        return jnp.concatenate([xh * lax.rsqrt(jnp.sum(xh * xh, -1, keepdims=True) + RMS_EPS) for xh in _heads(x)], axis=1)

    tb = c.shape[0]
    b_b = jnp.concatenate([jnp.broadcast_to(ba[:, 64 + h:65 + h], (tb, DK)) for h in range(H)], axis=1)
    a_b = jnp.concatenate([jnp.broadcast_to(ba[:, 68 + h:69 + h], (tb, DK)) for h in range(H)], axis=1)
    beta = jax.nn.sigmoid(b_b)
    g = -jnp.exp(alog_b) * jax.nn.softplus(a_b + dtb_b)
    return l2(q), l2(k), v, beta, g


def mla_pre_f(cq, ckv, krg, krs, cf, sf, qg, wq, kvg, wkv):
    cqn = cq * lax.rsqrt(jnp.sum(cq * cq, -1, keepdims=True) * (1.0 / QL) + RMS_EPS) * qg
    qa = bdot(cqn, wq)
    ckvn = ckv * lax.rsqrt(jnp.mean(ckv * ckv, -1, keepdims=True) + RMS_EPS) * kvg
    kv = bdot(ckvn, wkv)
    kro = krg * cf + krs * sf
    qs, ks, vs = [], [], []
    for h in range(H):
        qs += [qa[:, DK * h:DK * (h + 1)], qa[:, 512 + DK * h:512 + DK * (h + 1)] * cf + qa[:, 1024 + DK * h:1024 + DK * (h + 1)] * sf]
        ks += [kv[:, 256 * h:256 * h + DK], kro]
        vs += [kv[:, 256 * h + DK:256 * (h + 1)]]
    return jnp.concatenate(qs, axis=1) * ATT_C2, jnp.concatenate(ks, axis=1), jnp.concatenate(vs, axis=1)


def post1_f(o, z, omla, x, gg_b, wout, g1, b1):
    on = jnp.concatenate([oh * lax.rsqrt(jnp.mean(oh * oh, -1, keepdims=True) + RMS_EPS) for oh in _heads(o)], axis=1) * gg_b
    ogdn = on * (z * jax.nn.sigmoid(z))
    mix = bdot(jnp.concatenate([ogdn, omla], axis=1), wout)
    return (_layer_norm(ALPHA * x + mix, g1, b1),)


def post2_f(x1, down, p, g2, b2, wpg, wple):
    x2 = _layer_norm(ALPHA * x1 + down, g2, b2)
    return (x2 + jax.nn.sigmoid(bdot(x2, wpg)) * bdot(p, wple),)


HALO = 8
CW = 4


def _conv_from_scratch(xs, cw_ref, tb):
    c = xs[pl.ds(HALO - 3, tb), :] * cw_ref[0:1, :]
    for j in range(1, CW):
        c = c + xs[pl.ds(HALO - 3 + j, tb), :] * cw_ref[j:j + 1, :]
    return c


def gdn_pre_fwd(hbuf, conv_w8, alog_b, dtb_b):
    t = hbuf.shape[0]
    tb = min(ROW_TB, t)

    def body(x_ref, halo_ref, ba_ref, cw_ref, al_ref, dt_ref, q_ref, k_ref, v_ref, be_ref, g_ref, xs):
        i = pl.program_id(0)
        xs[pl.ds(0, HALO), :] = jnp.where(i == 0, 0.0, halo_ref[...])
        xs[pl.ds(HALO, tb), :] = x_ref[...]
        c = _conv_from_scratch(xs, cw_ref, tb)
        q, k, v, be, g = gdn_point_f(c, ba_ref[...], al_ref[...], dt_ref[...])
        q_ref[...], k_ref[...], v_ref[...], be_ref[...], g_ref[...] = q, k, v, be, g

    return pl.pallas_call(
        body, name="gdn_pre_fwd", grid=(t // tb,),
        in_specs=[_row_spec(tb, 1536, 0),
                  pl.BlockSpec((HALO, 1536), lambda i: (jnp.maximum(i * (tb // HALO) - 1, 0), 0)),
                  _row_spec(tb, LANE, OFF_KR // LANE),
                  _full_spec(conv_w8.shape), _full_spec(alog_b.shape), _full_spec(dtb_b.shape)],
        out_specs=[_row_spec(tb, 512, 0)] * 5,
        out_shape=[jax.ShapeDtypeStruct((t, 512), f32)] * 5,
        scratch_shapes=[pltpu.VMEM((tb + HALO, 1536), f32)],
        compiler_params=_cparams(("arbitrary",)),
    )(hbuf, hbuf, hbuf, conv_w8, alog_b, dtb_b)


def gdn_pre_bwd(hbuf, conv_w8, alog_b, dtb_b, dq, dk, dv, dbe, dg):
    t = hbuf.shape[0]
    tb = min(ROW_TB, t)
    n = t // tb

    def body(x_ref, halo_ref, ba_ref, cw_ref, al_ref, dt_ref, dq_ref, dk_ref, dv_ref, dbe_ref, dg_ref,
             dx_ref, dba_ref, dcw_ref, dal_ref, ddt_ref, xs, dcs):
        s = pl.program_id(0)
        i = n - 1 - s
        xs[pl.ds(0, HALO), :] = jnp.where(i == 0, 0.0, halo_ref[...])
        xs[pl.ds(HALO, tb), :] = x_ref[...]
        c = _conv_from_scratch(xs, cw_ref, tb)
        _, vf = jax.vjp(gdn_point_f, c, ba_ref[...], al_ref[...], dt_ref[...])
        dc, dba, dal, ddt = vf((dq_ref[...], dk_ref[...], dv_ref[...], dbe_ref[...], dg_ref[...]))

        @pl.when(s == 0)
        def _():
            dcs[pl.ds(tb, HALO), :] = jnp.zeros((HALO, 1536), f32)
            dcw_ref[...] = jnp.zeros_like(dcw_ref)
            dal_ref[...] = jnp.zeros_like(dal_ref)
            ddt_ref[...] = jnp.zeros_like(ddt_ref)

        @pl.when(s > 0)
        def _():
            dcs[pl.ds(tb, HALO), :] = dcs[pl.ds(0, HALO), :]

        dcs[pl.ds(0, tb), :] = dc
        dx = dcs[pl.ds(3, tb), :] * cw_ref[0:1, :]
        for j in range(1, CW):
            dx = dx + dcs[pl.ds(3 - j, tb), :] * cw_ref[j:j + 1, :]
        dx_ref[...] = dx.astype(dx_ref.dtype)
        dba_ref[...] = dba
        for j in range(CW):
            dcw_ref[j:j + 1, :] += jnp.sum(dc * xs[pl.ds(HALO - 3 + j, tb), :], axis=0, keepdims=True)
        dal_ref[...] += dal
        ddt_ref[...] += ddt

    rev = lambda cb: (lambda s: (n - 1 - s, cb))
    return pl.pallas_call(
        body, name="gdn_pre_bwd", grid=(n,),
        in_specs=[pl.BlockSpec((tb, 1536), rev(0)),
                  pl.BlockSpec((HALO, 1536), lambda s: (jnp.maximum((n - 1 - s) * (tb // HALO) - 1, 0), 0)),
                  pl.BlockSpec((tb, LANE), rev(OFF_KR // LANE)),
                  _full_spec(conv_w8.shape), _full_spec(alog_b.shape), _full_spec(dtb_b.shape)]
        + [pl.BlockSpec((tb, 512), rev(0))] * 5,
        out_specs=[pl.BlockSpec((tb, 1536), rev(0)), pl.BlockSpec((tb, LANE), rev(0)),
                   _full_spec(conv_w8.shape), _full_spec(alog_b.shape), _full_spec(dtb_b.shape)],
        out_shape=[jax.ShapeDtypeStruct((t, 1536), bf16), jax.ShapeDtypeStruct((t, LANE), f32),
                   jax.ShapeDtypeStruct(conv_w8.shape, f32), jax.ShapeDtypeStruct(alog_b.shape, f32),
                   jax.ShapeDtypeStruct(dtb_b.shape, f32)],
        scratch_shapes=[pltpu.VMEM((tb + HALO, 1536), f32), pltpu.VMEM((tb + HALO, 1536), f32)],
        compiler_params=_cparams(("arbitrary",)),
    )(hbuf, hbuf, hbuf, conv_w8, alog_b, dtb_b, dq, dk, dv, dbe, dg)


GDN_NB = 4
GDN_SCAN_NB = 8


def _to_batch(x, rows=CHUNK):
    n = x.shape[0] // rows
    return jnp.concatenate([x[rows * ci:rows * (ci + 1), DK * h:DK * (h + 1)][None] for ci in range(n) for h in range(H)], axis=0)


def _from_batch(y):
    n = y.shape[0] // H
    return jnp.concatenate([jnp.concatenate([y[ci * H + h] for h in range(H)], axis=1) for ci in range(n)], axis=0)


def gdn_intra_b(q, k, v, beta, g):
    c = CHUNK
    n = q.shape[0] // c
    tri = jnp.where(lax.broadcasted_iota(jnp.int32, (c, c), 0) >= lax.broadcasted_iota(jnp.int32, (c, c), 1), 1.0, 0.0)
    gc = _to_batch(jnp.concatenate([hdot(tri, g[c * ci:c * (ci + 1)]) for ci in range(n)], axis=0))
    qb, kb, vb, bb = _to_batch(q) * DK ** -0.5, _to_batch(k), _to_batch(v), _to_batch(beta)
    nbat = qb.shape[0]
    row = lax.broadcasted_iota(jnp.int32, (1, c, DK), 1)
    col = lax.broadcasted_iota(jnp.int32, (1, c, DK), 2)
    incl, strict, eye = row >= col, row > col, row == col
    grow = hdot(jnp.ones((nbat, c, c), f32), jnp.where(eye, gc, 0.0))
    decay = jnp.where(incl, jnp.exp(jnp.where(incl, gc - grow, 0.0)), 0.0)
    kbeta = kb * bb
    kpad = jnp.concatenate([kb, jnp.zeros((nbat, DK - c, DK), f32)], axis=1)
    a = jnp.where(strict, bdot_nt(kbeta, kpad) * decay, 0.0)[:, :, :c]
    nn = -a
    bk = bdot(a, a)
    for step in range(5):
        nn = nn + bk + bdot(nn, bk)
        if step < 4:
            bk = bdot(bk, bk)
    eg = jnp.exp(gc)
    rhs_v, rhs_k = vb * bb, kbeta * eg
    g_last = gc[:, c - 1:c, :]
    u = rhs_v + bdot(nn, rhs_v)
    w = rhs_k + bdot(nn, rhs_k)
    attn = jnp.where(incl, bdot_nt(qb, kpad) * decay, 0.0)
    return u, w, qb * eg, kb * jnp.exp(g_last - gc), attn, jnp.exp(g_last)


def gdn_scan_b(u, w, qd, kd, attn, el, state):
    v_new = u - bdot(w, state)
    return bdot(qd, state) + bdot(attn[:, :, :CHUNK], v_new), state * el + bdot_tn(kd, v_new)


def gdn_scan_bwd_b(do, w, qd, kd, attn, el, dstate):
    dvn = bdot_tn(attn[:, :, :CHUNK], do) + bdot(kd, dstate)
    return bdot_tn(qd, do) + dstate * el - bdot_tn(w, dvn)


def gdn_chunks_f(q, k, v, beta, g, states):
    n = states.shape[0]
    o, s_new = gdn_scan_b(*gdn_intra_b(q, k, v, beta, g), states.reshape(n * H, DK, DK))
    return _from_batch(o), s_new.reshape(n, H * DK, DK)


def _chunk_rows(cidx):
    return pl.ds(cidx * CHUNK, CHUNK)


def gdn_intra(q, k, v, beta, g):
    t = q.shape[0]
    nb = min(GDN_NB, t // CHUNK)
    tb = nb * CHUNK

    def body(q_ref, k_ref, v_ref, be_ref, g_ref, u_ref, w_ref, qd_ref, kd_ref, at_ref, el_ref):
        u, w, qd, kd, at, el = gdn_intra_b(q_ref[...], k_ref[...], v_ref[...], be_ref[...], g_ref[...])
        u_ref[...], w_ref[...], qd_ref[...], kd_ref[...], at_ref[...] = (_from_batch(a) for a in (u, w, qd, kd, at))
        el_ref[...] = _from_batch(jnp.broadcast_to(el, (nb * H, 8, DK))).reshape(nb, 8, H * DK)

    return pl.pallas_call(
        body, name="gdn_intra", grid=(t // tb,),
        in_specs=[_row_spec(tb, 512, 0)] * 5,
        out_specs=[_row_spec(tb, 512, 0)] * 5 + [pl.BlockSpec((nb, 8, 512), lambda i: (i, 0, 0))],
        out_shape=[jax.ShapeDtypeStruct((t, 512), f32)] * 5 + [jax.ShapeDtypeStruct((t // CHUNK, 8, 512), f32)],
        compiler_params=_cparams(("parallel",)),
    )(q, k, v, beta, g)


def gdn_scan_fwd(u, w, qd, kd, attn, el):
    t = u.shape[0]
    nb = min(GDN_SCAN_NB, t // CHUNK)
    tb = nb * CHUNK

    def body(u_ref, w_ref, qd_ref, kd_ref, at_ref, el_ref, o_ref, s_ref, state):
        @pl.when(pl.program_id(0) == 0)
        def _():
            state[...] = jnp.zeros_like(state)

        for cidx in range(nb):
            r = _chunk_rows(cidx)
            s_ref[cidx] = state[...]
            ins = [_to_batch(ref[r, :]) for ref in (u_ref, w_ref, qd_ref, kd_ref, at_ref)]
            el = _to_batch(el_ref[cidx], 8)[:, 0:1, :]
            o, s_new = gdn_scan_b(*ins, el, state[...].reshape(H, DK, DK))
            o_ref[r, :] = _from_batch(o)
            state[...] = s_new.reshape(H * DK, DK)

    return pl.pallas_call(
        body, name="gdn_scan_fwd", grid=(t // tb,),
        in_specs=[_row_spec(tb, 512, 0)] * 5 + [pl.BlockSpec((nb, 8, 512), lambda i: (i, 0, 0))],
        out_specs=[_row_spec(tb, 512, 0), pl.BlockSpec((nb, 512, DK), lambda i: (i, 0, 0))],
        out_shape=[jax.ShapeDtypeStruct((t, 512), f32), jax.ShapeDtypeStruct((t // CHUNK, 512, DK), f32)],
        scratch_shapes=[pltpu.VMEM((512, DK), f32)],
        compiler_params=_cparams(("arbitrary",)),
    )(u, w, qd, kd, attn, el)


def gdn_scan_bwd(do, w, qd, kd, attn, el):
    t = do.shape[0]
    nb = min(GDN_SCAN_NB, t // CHUNK)
    tb = nb * CHUNK
    n = t // tb

    def body(do_ref, w_ref, qd_ref, kd_ref, at_ref, el_ref, ds_ref, dstate):
        @pl.when(pl.program_id(0) == 0)
        def _():
            dstate[...] = jnp.zeros_like(dstate)

        for cidx in reversed(range(nb)):
            r = _chunk_rows(cidx)
            ds_ref[cidx] = dstate[...]
            ins = [_to_batch(ref[r, :]) for ref in (do_ref, w_ref, qd_ref, kd_ref, at_ref)]
            el = _to_batch(el_ref[cidx], 8)[:, 0:1, :]
            dstate[...] = gdn_scan_bwd_b(*ins, el, dstate[...].reshape(H, DK, DK)).reshape(H * DK, DK)

    rev = pl.BlockSpec((tb, 512), lambda s: (n - 1 - s, 0))
    return pl.pallas_call(
        body, name="gdn_scan_bwd", grid=(n,),
        in_specs=[rev] * 5 + [pl.BlockSpec((nb, 8, 512), lambda s: (n - 1 - s, 0, 0))],
        out_specs=pl.BlockSpec((nb, 512, DK), lambda s: (n - 1 - s, 0, 0)),
        out_shape=jax.ShapeDtypeStruct((t // CHUNK, 512, DK), f32),
        scratch_shapes=[pltpu.VMEM((512, DK), f32)],
        compiler_params=_cparams(("arbitrary",)),
    )(do, w, qd, kd, attn, el)


def gdn_local_bwd(q, k, v, beta, g, states, dstates, do, name="gdn_local_bwd", comms=()):
    t = q.shape[0]
    nb = min(GDN_NB, t // CHUNK)
    tb = nb * CHUNK
    n = t // tb

    def body(*refs):
        ins, outs, _, comm_refs = _split_hosted(refs, 8, 5, 0, len(comms))
        q_ref, k_ref, v_ref, be_ref, g_ref, s_ref, ds_ref, do_ref = ins
        dq_ref, dk_ref, dv_ref, dbe_ref, dg_ref = outs
        _host_exchanges(comm_refs, comms, pl.program_id(0) == 0, pl.program_id(0) == n - 1)
        states_v = s_ref[...]
        _, vf = jax.vjp(lambda *a: gdn_chunks_f(*a, states_v), q_ref[...], k_ref[...], v_ref[...], be_ref[...], g_ref[...])
        dq_ref[...], dk_ref[...], dv_ref[...], dbe_ref[...], dg_ref[...] = vf((do_ref[...], ds_ref[...]))

    st = pl.BlockSpec((nb, 512, DK), lambda i: (i, 0, 0))
    in_specs = [_row_spec(tb, 512, 0)] * 5 + [st, st, _row_spec(tb, 512, 0)]
    out_specs = [_row_spec(tb, 512, 0)] * 5
    out_shape = [jax.ShapeDtypeStruct((t, 512), f32)] * 5
    scratch, args = [], [q, k, v, beta, g, states, dstates, do]
    _add_hosted(comms, in_specs, out_specs, out_shape, scratch, args)
    return pl.pallas_call(body, name=name, grid=(n,), in_specs=in_specs, out_specs=out_specs, out_shape=out_shape,
                          scratch_shapes=scratch, compiler_params=_cparams(("arbitrary",) if comms else ("parallel",)))(*args)


NEG = -1e30


LOG2E = 1.4426950408889634
ATT_C2 = ATT_SCALE * LOG2E
ATT_RB = 256


def _add_hosted(comms, in_specs, out_specs, out_shape, scratch, args):
    for src, scatter in comms:
        in_specs.append(_ANY)
        out_specs.append(_ANY)
        out_shape.append(_comm_out_shape(src, scatter))
        scratch += _comm_scratch()
        args.append(src)


def _split_hosted(refs, n_in, n_out, n_scratch, n_comm):
    ins, srcs = refs[:n_in], refs[n_in:n_in + n_comm]
    o0 = n_in + n_comm
    outs, dsts = refs[o0:o0 + n_out], refs[o0 + n_out:o0 + n_out + n_comm]
    rest = refs[o0 + n_out + n_comm:]
    sems = rest[n_scratch:]
    return ins, outs, rest[:n_scratch], [(srcs[c], dsts[c]) + tuple(sems[3 * c:3 * c + 3]) for c in range(n_comm)]


def _host_exchanges(comm_refs, comms, is_first, is_last):
    if not comms:
        return

    @pl.when(is_first)
    def _():
        for refs, (_, scatter) in zip(comm_refs, comms):
            for op in _exchange_copies(*refs, scatter):
                op.start()

    @pl.when(is_last)
    def _():
        for refs, (_, scatter) in zip(comm_refs, comms):
            for op in _exchange_copies(*refs, scatter):
                op.wait()


def attn_fwd(q, k, v, name="attn_fwd", comms=()):
    t = q.shape[0]
    tq = tk = min(1024, t)
    rb = min(ATT_RB, tq)
    nq, nk, nr = t // tq, t // tk, tq // rb
    last = lambda i: i

    def body(*refs):
        (q_ref, k_ref, v_ref), (o_ref, lse_ref), (m_s, acc), comm_refs = _split_hosted(refs, 3, 2, 2, len(comms))
        h, i, j = pl.program_id(0), pl.program_id(1), pl.program_id(2)
        _host_exchanges(comm_refs, comms, (h == 0) & (i == 0) & (j == 0), (h == H - 1) & (i == nq - 1) & (j == nk - 1))

        @pl.when(j == 0)
        def _():
            m_s[...] = jnp.full_like(m_s, NEG)
            acc[...] = jnp.zeros_like(acc)

        one_col = jnp.where(lax.broadcasted_iota(jnp.int32, (tk, DK), 1) == 0, 1.0, 0.0).astype(MX)
        vx = jnp.concatenate([v_ref[...], one_col], axis=1)

        def rows_of(r):
            return pl.ds(r * rb, rb)

        def soft(r, s):
            m_old = m_s[rows_of(r), :]
            m_new = jnp.maximum(m_old, jnp.max(s, axis=-1, keepdims=True))
            m_s[rows_of(r), :] = m_new
            return jnp.exp2(s - m_new).astype(MX), jnp.exp2(m_old - m_new)

        def pv(r, p, alpha, vals):
            acc[rows_of(r), :] = alpha * acc[rows_of(r), :] + _dot(p, vals, 1, 0)

        def whole_block():
            kb = k_ref[...]
            ss = [_dot(q_ref[rows_of(r), :], kb, 1, 1) for r in range(min(2, nr))]
            pend = None
            for r in range(nr):
                if r + 2 < nr:
                    ss.append(_dot(q_ref[rows_of(r + 2), :], kb, 1, 1))
                p, alpha = soft(r, ss[r])
                if pend is not None:
                    pv(*pend, vx)
                pend = (r, p, alpha)
            pv(*pend, vx)

        def diagonal_block():
            for r in range(nr):
                cols = (r + 1) * rb
                s = _dot(q_ref[rows_of(r), :], k_ref[pl.ds(0, cols), :], 1, 1)
                qrow = lax.broadcasted_iota(jnp.int32, (rb, cols), 0) + r * rb
                kcol = lax.broadcasted_iota(jnp.int32, (rb, cols), 1)
                s = jnp.where(jnp.right_shift(kcol, 6) <= jnp.right_shift(qrow, 6), s, NEG)
                p, alpha = soft(r, s)
                pv(r, p, alpha, vx[:cols])

        pl.when(j < i)(whole_block)
        pl.when(j == i)(diagonal_block)

        @pl.when(j == nk - 1)
        def _():
            a = acc[...]
            l = a[:, DK:DK + 1]
            o_ref[...] = a[:, :DK] / l
            lse_ref[0] = m_s[...] + jnp.log2(l)

    in_specs = [pl.BlockSpec((tq, 256), lambda h, i, j: (i, h)),
                pl.BlockSpec((tk, 256), lambda h, i, j: (jnp.minimum(j, last(i)), h)),
                pl.BlockSpec((tk, DK), lambda h, i, j: (jnp.minimum(j, last(i)), h))]
    out_specs = [pl.BlockSpec((tq, DK), lambda h, i, j: (i, h)), pl.BlockSpec((1, tq, 1), lambda h, i, j: (h, i, 0))]
    out_shape = [jax.ShapeDtypeStruct((t, 512), f32), jax.ShapeDtypeStruct((H, t, 1), f32)]
    scratch = [pltpu.VMEM((tq, 1), f32), pltpu.VMEM((tq, 2 * DK), f32)]
    args = [q, k, v]
    _add_hosted(comms, in_specs, out_specs, out_shape, scratch, args)
    sem = ("arbitrary",) * 3 if comms else ("parallel", "parallel", "arbitrary")
    return pl.pallas_call(body, name=name, grid=(H, nq, nk), in_specs=in_specs, out_specs=out_specs, out_shape=out_shape,
                          scratch_shapes=scratch, compiler_params=_cparams(sem))(*args)


def attn_bwd(q, k, v, o, lse, do, name="attn_bwd", comms=()):
    t = q.shape[0]
    tk = min(1024, t)
    ratio = 2 if t % (2 * tk) == 0 else 1
    tq = ratio * tk
    rb = min(ATT_RB, tk)
    nq, nk, nr, kr = t // tq, t // tk, tq // rb, tk // rb
    first = lambda j: j // ratio

    def body(*refs):
        ins, outs, (dk_acc, dv_acc), comm_refs = _split_hosted(refs, 6, 3, 2, len(comms))
        q_ref, k_ref, v_ref, o_ref, lse_ref, do_ref = ins
        dq_ref, dk_ref, dv_ref = outs
        h, j, i = pl.program_id(0), pl.program_id(1), pl.program_id(2)
        _host_exchanges(comm_refs, comms, (h == 0) & (i == 0) & (j == 0), (h == H - 1) & (i == nq - 1) & (j == nk - 1))

        @pl.when((j == 0) & (i == 0))
        def _():
            dq_ref[...] = jnp.zeros_like(dq_ref)

        @pl.when(i == 0)
        def _():
            dk_acc[...] = jnp.zeros_like(dk_acc)
            dv_acc[...] = jnp.zeros_like(dv_acc)

        def rows_of(r):
            return pl.ds(r * rb, rb)

        def front(r, cols):
            qb, dob = q_ref[rows_of(r), :], do_ref[rows_of(r), :]
            return qb, dob, _dot(qb, k_ref[pl.ds(0, cols), :], 1, 1), _dot(dob.astype(MX), v_ref[pl.ds(0, cols), :], 1, 1)

        def middle(r, dob, s, dp):
            p = jnp.exp2(s - lse_ref[0, rows_of(r), :])
            delta = jnp.sum(dob * o_ref[rows_of(r), :], axis=-1, keepdims=True)
            return p.astype(MX), (p * (dp - delta)).astype(MX)

        def back(r, cols, qb, dob, pb, ds):
            dv_acc[pl.ds(0, cols), :] += _dot(pb, dob.astype(MX), 0, 0)
            dk_acc[pl.ds(0, cols), :] += _dot(ds, qb, 0, 0) * (1.0 / LOG2E)
            grow = pl.ds(pl.multiple_of(i * tq + r * rb, rb), rb)
            dq_ref[grow, :] += _dot(ds, k_ref[pl.ds(0, cols), :], 1, 0) * (1.0 / LOG2E)

        def whole_block():
            nxt = front(0, tk)
            for r in range(nr):
                qb, dob, s, dp = nxt
                if r + 1 < nr:
                    nxt = front(r + 1, tk)
                back(r, tk, qb, dob, *middle(r, dob, s, dp))

        def partial_block(part):
            def run():
                for r in range(nr):
                    rel = r - part * kr
                    if rel < 0:
                        continue
                    cols = min((rel + 1) * rb, tk)
                    qb, dob, s, dp = front(r, cols)
                    if rel < kr:
                        qrow = lax.broadcasted_iota(jnp.int32, (rb, cols), 0) + rel * rb
                        kcol = lax.broadcasted_iota(jnp.int32, (rb, cols), 1)
                        s = jnp.where(jnp.right_shift(kcol, 6) <= jnp.right_shift(qrow, 6), s, NEG)
                    back(r, cols, qb, dob, *middle(r, dob, s, dp))
            return run

        pl.when(j < i * ratio)(whole_block)
        for part in range(ratio):
            pl.when(j == i * ratio + part)(partial_block(part))

        @pl.when(i == nq - 1)
        def _():
            dk_ref[...] = dk_acc[...]
            dv_ref[...] = dv_acc[...]

    qi = lambda h, j, i: (jnp.maximum(i, first(j)), h)
    in_specs = [pl.BlockSpec((tq, 256), qi),
                pl.BlockSpec((tk, 256), lambda h, j, i: (j, h)),
                pl.BlockSpec((tk, DK), lambda h, j, i: (j, h)),
                pl.BlockSpec((tq, DK), qi),
                pl.BlockSpec((1, tq, 1), lambda h, j, i: (h, jnp.maximum(i, first(j)), 0)),
                pl.BlockSpec((tq, DK), qi)]
    out_specs = [pl.BlockSpec((t, 256), lambda h, j, i: (0, h)),
                 pl.BlockSpec((tk, 256), lambda h, j, i: (j, h)),
                 pl.BlockSpec((tk, DK), lambda h, j, i: (j, h))]
    out_shape = [jax.ShapeDtypeStruct((t, 1024), f32), jax.ShapeDtypeStruct((t, 1024), f32), jax.ShapeDtypeStruct((t, 512), f32)]
    scratch = [pltpu.VMEM((tk, 256), f32), pltpu.VMEM((tk, DK), f32)]
    args = [q, k, v, o, lse, do]
    _add_hosted(comms, in_specs, out_specs, out_shape, scratch, args)
    return pl.pallas_call(body, name=name, grid=(H, nk, nq), in_specs=in_specs, out_specs=out_specs, out_shape=out_shape,
                          scratch_shapes=scratch, compiler_params=_cparams(("arbitrary",) * 3))(*args)


def ffn_up(x1, wg, wu):
    t = x1.shape[0]
    tm, tn = min(512, t), _pick(DFF, 1536)

    def body(x_ref, wg_ref, wu_ref, g_ref, u_ref, a_ref):
        xb = x_ref[...].astype(MX)
        g = _dot(xb, wg_ref[...], 1, 0)
        u = _dot(xb, wu_ref[...], 1, 0)
        g_ref[...] = g.astype(g_ref.dtype)
        u_ref[...] = u.astype(u_ref.dtype)
        a_ref[...] = (g * jax.nn.sigmoid(g) * u).astype(a_ref.dtype)

    w_spec = pl.BlockSpec((D, tn), lambda i, j: (0, j))
    o_spec = pl.BlockSpec((tm, tn), lambda i, j: (i, j))
    return pl.pallas_call(
        body, name="ffn_up", grid=(t // tm, DFF // tn),
        in_specs=[pl.BlockSpec((tm, D), lambda i, j: (i, 0)), w_spec, w_spec],
        out_specs=[o_spec] * 3, out_shape=[jax.ShapeDtypeStruct((t, DFF), bf16)] * 3,
        compiler_params=_cparams(("parallel", "parallel")),
    )(x1, wg, wu)


def ffn_dact(ddown, wd, g, u):
    t = ddown.shape[0]
    tm, tn = min(512, t), _pick(DFF, 1536)

    def body(dd_ref, wd_ref, g_ref, u_ref, dg_ref, du_ref):
        dact = _dot(dd_ref[...].astype(MX), wd_ref[...], 1, 1)
        gv, uv = g_ref[...].astype(f32), u_ref[...].astype(f32)
        sig = jax.nn.sigmoid(gv)
        dg_ref[...] = (dact * uv * sig * (1.0 + gv * (1.0 - sig))).astype(dg_ref.dtype)
        du_ref[...] = (dact * gv * sig).astype(du_ref.dtype)

    o_spec = pl.BlockSpec((tm, tn), lambda i, j: (i, j))
    return pl.pallas_call(
        body, name="ffn_dact", grid=(t // tm, DFF // tn),
        in_specs=[pl.BlockSpec((tm, D), lambda i, j: (i, 0)), pl.BlockSpec((tn, D), lambda i, j: (j, 0)), o_spec, o_spec],
        out_specs=[o_spec] * 2, out_shape=[jax.ShapeDtypeStruct((t, DFF), bf16)] * 2,
        compiler_params=_cparams(("parallel", "parallel")),
    )(ddown, wd, g, u)


def loss_head(y, target):
    t = y.shape[0]
    tb = min(ROW_TB, t)
    n = t // tb

    def body(y_ref, t_ref, dy_ref, loss_ref, acc):
        i = pl.program_id(0)

        @pl.when(i == 0)
        def _():
            acc[...] = jnp.zeros_like(acc)

        e = y_ref[...] - t_ref[...]
        dy_ref[...] = e * (1.0 / D)
        acc[...] += jnp.sum(e * e, axis=0, keepdims=True)

        @pl.when(i == n - 1)
        def _():
            loss_ref[...] = jnp.sum(acc[...], axis=1, keepdims=True) * (0.5 / D)

    return pl.pallas_call(
        body, name="loss_head", grid=(n,),
        in_specs=[_row_spec(tb, D, 0)] * 2,
        out_specs=[_row_spec(tb, D, 0), _full_spec((1, 1))],
        out_shape=[jax.ShapeDtypeStruct((t, D), f32), jax.ShapeDtypeStruct((1, 1), f32)],
        scratch_shapes=[pltpu.VMEM((1, D), f32)],
        compiler_params=_cparams(("arbitrary",)),
    )(y, target)


def _me_and_peers():
    x, y, c = lax.axis_index("x"), lax.axis_index("y"), lax.axis_index("c")
    me = 4 * x + 2 * y + c
    peers = []
    for kk in range(1, N_DEV):
        px = 1 - x if kk & 4 else x
        py = 1 - y if kk & 2 else y
        pc = 1 - c if kk & 1 else c
        peers.append(((px, py, pc), 4 * px + 2 * py + pc))
    return me, peers


_ANY = pl.BlockSpec(memory_space=pl.ANY)


def _comm_scratch():
    return [pltpu.SemaphoreType.DMA((N_DEV - 1,)), pltpu.SemaphoreType.DMA((N_DEV - 1,)), pltpu.SemaphoreType.DMA]


def _exchange_copies(src_ref, out_ref, send_sems, recv_sems, local_sem, scatter):
    me, peers = _me_and_peers()
    pick = (lambda d: src_ref.at[d]) if scatter else (lambda d: src_ref)
    ops = [pltpu.make_async_copy(pick(me), out_ref.at[me], local_sem)]
    ops += [pltpu.make_async_remote_copy(src_ref=pick(pid), dst_ref=out_ref.at[me], send_sem=send_sems.at[kk],
                                         recv_sem=recv_sems.at[kk], device_id=dev, device_id_type=pl.DeviceIdType.MESH)
            for kk, (dev, pid) in enumerate(peers)]
    return ops


def _comm_out_shape(src, scatter):
    return jax.ShapeDtypeStruct(src.shape if scatter else (N_DEV,) + src.shape, src.dtype)


def exchange(name, comms):
    def body(*refs):
        _, _, _, comm_refs = _split_hosted(refs, 0, 0, 0, len(comms))
        ops = [op for refs_c, (_, scatter) in zip(comm_refs, comms) for op in _exchange_copies(*refs_c, scatter)]
        for op in ops:
            op.start()
        for op in ops:
            op.wait()

    in_specs, out_specs, out_shape, scratch, args = [], [], [], [], []
    _add_hosted(comms, in_specs, out_specs, out_shape, scratch, args)
    return pl.pallas_call(body, name=name, in_specs=in_specs, out_specs=out_specs, out_shape=out_shape, scratch_shapes=scratch)(*args)


def adamw(w, m, v, parts):
    nl, r, c = w.shape
    lanes = -(-c // LANE) * LANE
    tb = _pick_rows(r, max(8, ADAM_PART_BLOCK_BYTES // (N_DEV * lanes * 4)))
    n = r // tb

    def body(*refs):
        w_ref, m_ref, v_ref = refs[:3]
        p_refs = refs[3:3 + nl]
        g_ref, d_ref, nm_ref, nv_ref = refs[3 + nl:]
        layer = pl.program_id(0)
        for k in range(nl):
            @pl.when(layer == k)
            def _(p_ref=p_refs[k]):
                g = p_ref[0]
                for s in range(1, N_DEV):
                    g = g + p_ref[s]
                g_ref[0] = g

        g = g_ref[0]
        nm = ADAM_B1 * m_ref[0] + (1.0 - ADAM_B1) * g
        nv = ADAM_B2 * v_ref[0] + (1.0 - ADAM_B2) * jnp.square(g)
        m_hat = nm / (1.0 - ADAM_B1 ** ADAM_STEP)
        v_hat = nv / (1.0 - ADAM_B2 ** ADAM_STEP)
        d_ref[0] = -ADAM_LR * (m_hat / (jnp.sqrt(v_hat) + ADAM_EPS) + ADAM_WD * w_ref[0])
        nm_ref[0] = nm
        nv_ref[0] = nv

    def part_spec(k):
        return pl.BlockSpec((N_DEV, tb, c), lambda l, i: (0, jnp.where(l == k, i, jnp.where(l > k, n - 1, 0)), 0))

    spec = pl.BlockSpec((1, tb, c), lambda l, i: (l, i, 0))
    return pl.pallas_call(
        body, name="adamw", grid=(nl, n), in_specs=[spec] * 3 + [part_spec(k) for k in range(nl)], out_specs=[spec] * 4,
        out_shape=[jax.ShapeDtypeStruct((nl, r, c), f32)] * 4, compiler_params=_cparams(("arbitrary", "arbitrary")),
    )(w, m, v, *parts)


ADAM_PART_BLOCK_BYTES = 4 << 20


def _pick_rows(r, cap=ROW_TB):
    return max(tb for tb in range(8, min(cap, ROW_TB) + 1, 8) if r % tb == 0)


BIG = ("w_in", "w_uq", "w_ukv", "w_out", "w_gate_up", "w_down", "w_ple", "w_ple_gate")
COL_SHARDED = ("w_in", "w_uq", "w_ukv", "w_gate_up", "w_ple")
SMALL = ("conv_w", "a_log", "dt_bias", "gdn_norm_g", "q_norm_g", "kv_norm_g", "ln1_g", "ln1_b", "ln2_g", "ln2_b")
ROW_ALIGN = 16


def _pad_rows(a, mult=ROW_ALIGN, axis=0):
    pad = (-a.shape[axis]) % mult
    widths = [(0, 0)] * a.ndim
    widths[axis] = (0, pad)
    return a if pad == 0 else jnp.pad(a, widths)


GATHER_EARLY = ("w_in", "w_uq", "w_ukv")
GATHER_LATE = ("w_out", "w_gate_up", "w_down", "w_ple", "w_ple_gate")
GRAD_READY_EARLY = ("w_gate_up", "w_down", "w_ple", "w_ple_gate", "w_out")


def _grad_groups(l):
    early = GRAD_READY_EARLY if l == 0 else ("w_gate_up", "w_down")
    return early, tuple(n for n in BIG if n not in early)


def _gathers(tree, l, names):
    return [(tree[n][l].astype(bf16), False) for n in names]


def _full_weight(gathered, n, lo=0, hi=N_DEV):
    blk = gathered[lo:hi]
    k, sr, sc = blk.shape
    if n in COL_SHARDED:
        return jnp.transpose(blk, (1, 0, 2)).reshape(sr, k * sc)
    return blk.reshape(k * sr, sc)


def _shard_grad(gfull, n):
    if n in COL_SHARDED:
        sr = gfull.shape[0]
        k = gfull.shape[1] // (FULL_COLS[n] // N_DEV)
        return jnp.transpose(gfull.reshape(sr, k, gfull.shape[1] // k), (1, 0, 2))
    return gfull.reshape(N_DEV, gfull.shape[0] // N_DEV, gfull.shape[1])


FULL_COLS = {"w_in": IN_W, "w_uq": 768, "w_ukv": 1024, "w_gate_up": 2 * DFF, "w_ple": D}


def _pack_small(tree, extra):
    flat = jnp.concatenate([tree[n].reshape(-1).astype(f32) for n in SMALL] + [extra.reshape(-1).astype(f32)])
    return jnp.pad(flat, (0, (-flat.shape[0]) % (ROW_ALIGN * D))).reshape(-1, D)


def _unpack_small(rows, like):
    flat = rows.reshape(-1)
    out, off = {}, 0
    for n in SMALL:
        sz = int(np.prod(like[n].shape))
        out[n] = flat[off:off + sz].reshape(like[n].shape)
        off += sz
    return out, flat[off]


def _zeros(r, c, dt):
    return jnp.zeros((r, c), dt)


def _prep_w_in(w):
    dt = w.dtype
    kr = w[:, 2696:2760]
    return jnp.concatenate([
        w[:, 0:2048],
        w[:, 2440:2696],
        kr, w[:, 2048:2056], _zeros(D, 56, dt),
        kr[:, 32:], kr[:, :32], _zeros(D, 64, dt),
        w[:, 2056:2440], _zeros(D, CQP - QL, dt)], axis=1)


def _unprep_w_in(g):
    krs = g[:, OFF_KRS:OFF_KRS + 64]
    kr = g[:, OFF_KR:OFF_KR + 64] + jnp.concatenate([krs[:, 32:], krs[:, :32]], axis=1)
    return jnp.concatenate([g[:, 0:2048], g[:, OFF_KR + 64:OFF_KR + 72], g[:, OFF_CQ:OFF_CQ + QL],
                            g[:, OFF_CKV:OFF_CKV + KVL], kr], axis=1)


def _prep_w_uq(w):
    dt = w.dtype
    z64 = _zeros(QL, 64, dt)
    nope, ra, rb = [], [], []
    for h in range(H):
        nope.append(w[:, 192 * h:192 * h + 128])
        x1, x2 = w[:, 192 * h + 128:192 * h + 160], w[:, 192 * h + 160:192 * h + 192]
        ra += [x1, x2, z64]
        rb += [x2, x1, z64]
    return jnp.concatenate([jnp.concatenate(nope + ra + rb, axis=1), _zeros(CQP - QL, 1536, dt)], axis=0)


def _unprep_w_uq(g):
    g = g[:QL]
    cols = []
    for h in range(H):
        a = g[:, 512 + 128 * h:512 + 128 * h + 64]
        b = g[:, 1024 + 128 * h:1024 + 128 * h + 64]
        cols += [g[:, 128 * h:128 * (h + 1)], a[:, :32] + b[:, 32:], a[:, 32:] + b[:, :32]]
    return jnp.concatenate(cols, axis=1)


def _rope_tables(positions):
    inv_freq = ROPE_THETA ** (-jnp.arange(0, ROPE, 2, dtype=f32) / ROPE)
    ang = positions.astype(f32)[:, None] * inv_freq
    c, s = jnp.cos(ang), jnp.sin(ang)
    z = jnp.zeros((positions.shape[0], 64), f32)
    return jnp.concatenate([c, c, z], axis=1), jnp.concatenate([-s, s, z], axis=1)


def _tile_heads(vec, n=H):
    return jnp.tile(vec.reshape(1, -1), (1, n))


def _bcast_heads(vec):
    return jnp.repeat(vec, DK).reshape(1, H * DK)


def kernel(x, p, positions, w_in, conv_w, a_log, dt_bias, gdn_norm_g, q_norm_g, w_uq, kv_norm_g, w_ukv, w_out, ln1_g, ln1_b, w_gate_up, w_down, ln2_g, ln2_b, w_ple, w_ple_gate, loss_target, m_w_in, m_conv_w, m_a_log, m_dt_bias, m_gdn_norm_g, m_q_norm_g, m_w_uq, m_kv_norm_g, m_w_ukv, m_w_out, m_ln1_g, m_ln1_b, m_w_gate_up, m_w_down, m_ln2_g, m_ln2_b, m_w_ple, m_w_ple_gate, v_w_in, v_conv_w, v_a_log, v_dt_bias, v_gdn_norm_g, v_q_norm_g, v_w_uq, v_kv_norm_g, v_w_ukv, v_w_out, v_ln1_g, v_ln1_b, v_w_gate_up, v_w_down, v_ln2_g, v_ln2_b, v_w_ple, v_w_ple_gate):
    W = dict(w_in=w_in, conv_w=conv_w, a_log=a_log, dt_bias=dt_bias, gdn_norm_g=gdn_norm_g, q_norm_g=q_norm_g, w_uq=w_uq,
             kv_norm_g=kv_norm_g, w_ukv=w_ukv, w_out=w_out, ln1_g=ln1_g, ln1_b=ln1_b, w_gate_up=w_gate_up, w_down=w_down,
             ln2_g=ln2_g, ln2_b=ln2_b, w_ple=w_ple, w_ple_gate=w_ple_gate)
    M = dict(w_in=m_w_in, conv_w=m_conv_w, a_log=m_a_log, dt_bias=m_dt_bias, gdn_norm_g=m_gdn_norm_g, q_norm_g=m_q_norm_g,
             w_uq=m_w_uq, kv_norm_g=m_kv_norm_g, w_ukv=m_w_ukv, w_out=m_w_out, ln1_g=m_ln1_g, ln1_b=m_ln1_b,
             w_gate_up=m_w_gate_up, w_down=m_w_down, ln2_g=m_ln2_g, ln2_b=m_ln2_b, w_ple=m_w_ple, w_ple_gate=m_w_ple_gate)
    V = dict(w_in=v_w_in, conv_w=v_conv_w, a_log=v_a_log, dt_bias=v_dt_bias, gdn_norm_g=v_gdn_norm_g, q_norm_g=v_q_norm_g,
             w_uq=v_w_uq, kv_norm_g=v_kv_norm_g, w_ukv=v_w_ukv, w_out=v_w_out, ln1_g=v_ln1_g, ln1_b=v_ln1_b,
             w_gate_up=v_w_gate_up, w_down=v_w_down, ln2_g=v_ln2_g, ln2_b=v_ln2_b, w_ple=v_w_ple, w_ple_gate=v_w_ple_gate)
    shards = {n: W[n].shape for n in BIG}
    t = x.shape[1]
    xin = x.reshape(t, D)
    target = loss_target.reshape(t, D)
    cf, sf = _rope_tables(positions.reshape(t))

    conv_rows = _pad_rows(jnp.pad(conv_w.reshape(-1), (0, (-conv_w.size) % D)).reshape(-1, D), 8)
    first = exchange("all_gather_first", _gathers(W, 0, GATHER_EARLY) + [(conv_rows, False)])
    early = [dict(zip(GATHER_EARLY, first[:-1]))] + [None] * (DEPTH - 1)
    csz = conv_w.shape[1] * conv_w.shape[2]
    conv_full = jnp.transpose(first[-1].reshape(N_DEV, -1)[:, :DEPTH * csz].reshape(N_DEV, DEPTH, CW, -1), (1, 2, 0, 3)).reshape(DEPTH, CW, 1536)

    acts = []
    h_cur = xin
    for l in range(DEPTH):
        win = _prep_w_in(_full_weight(early[l]["w_in"], "w_in"))
        wq = _prep_w_uq(_full_weight(early[l]["w_uq"], "w_uq"))
        wkv = _full_weight(early[l]["w_ukv"], "w_ukv")
        cw8 = jnp.concatenate([conv_full[l], jnp.zeros((8 - CW, 1536), f32)], axis=0)
        alog_b, dtb_b = _bcast_heads(a_log[l]), _bcast_heads(dt_bias[l])
        gg_b = _tile_heads(gdn_norm_g[l])
        qg = jnp.concatenate([q_norm_g[l], jnp.zeros((CQP - QL,), f32)]).reshape(1, CQP)
        kvg = kv_norm_g[l].reshape(1, KVL)
        g1, b1, g2, b2 = (a[l].reshape(1, D) for a in (ln1_g, ln1_b, ln2_g, ln2_b))
        p_l = p[l].reshape(t, PLE)

        hb = mm("in_proj", h_cur, win, mode="nn", out_dtype=f32)
        qn, kn, vs, beta, glog = gdn_pre_fwd(hb, cw8, alog_b, dtb_b)
        g_u, g_w, g_qd, g_kd, g_at, g_el = gdn_intra(qn, kn, vs, beta, glog)
        o_gdn, states = gdn_scan_fwd(g_u, g_w, g_qd, g_kd, g_at, g_el)
        mla_rows = [(hb, CQP, OFF_CQ // CQP), (hb, KVL, OFF_CKV // KVL), (hb, LANE, OFF_KR // LANE), (hb, LANE, OFF_KRS // LANE),
                    (cf, LANE, 0), (sf, LANE, 0)]
        mla_consts = [qg, wq, kvg, wkv]
        qm, km, vm = row_fwd("mla_pre", mla_pre_f, mla_rows, mla_consts, [(1024, bf16), (1024, bf16), (512, bf16)])
        comms = _gathers(W, l, GATHER_LATE) + (_gathers(W, l + 1, GATHER_EARLY) if l + 1 < DEPTH else [])
        res = attn_fwd(qm, km, vm, name=f"attn_fwd_gather{len(comms)}", comms=comms)
        o_mla, lse = res[:2]
        late = dict(zip(GATHER_LATE, res[2:]))
        if l + 1 < DEPTH:
            early[l + 1] = dict(zip(GATHER_EARLY, res[2 + len(GATHER_LATE):]))
        wout, wd = _full_weight(late["w_out"], "w_out"), _full_weight(late["w_down"], "w_down")
        half = N_DEV // 2
        wg, wu = _full_weight(late["w_gate_up"], "w_gate_up", 0, half), _full_weight(late["w_gate_up"], "w_gate_up", half, N_DEV)
        wple, wpg = _full_weight(late["w_ple"], "w_ple"), _full_weight(late["w_ple_gate"], "w_ple_gate")
        p1_rows = [(o_gdn, 512, 0), (hb, 512, OFF_Z // 512), (o_mla, 512, 0), (h_cur, D, 0)]
        p1_consts = [gg_b, wout, g1, b1]
        (x1,) = row_fwd("post1", post1_f, p1_rows, p1_consts, [(D, f32)])
        gate, up, act = ffn_up(x1, wg, wu)
        down = mm("ffn_down", act, wd, mode="nn", out_dtype=f32)
        p2_rows = [(x1, D, 0), (down, D, 0), (p_l, PLE, 0)]
        p2_consts = [g2, b2, wpg, wple]
        (y,) = row_fwd("post2", post2_f, p2_rows, p2_consts, [(D, f32)])
        acts.append(dict(x=h_cur, hb=hb, qn=qn, kn=kn, vs=vs, beta=beta, glog=glog, states=states, o_gdn=o_gdn, qm=qm, km=km,
                         scan=(g_w, g_qd, g_kd, g_at, g_el),
                         vm=vm, o_mla=o_mla, lse=lse, x1=x1, gate=gate, up=up, act=act, win=win, wd=wd, wg=wg, wu=wu,
                         cw8=cw8, alog_b=alog_b, dtb_b=dtb_b, mla_rows=mla_rows, mla_consts=mla_consts, p1_rows=p1_rows,
                         p1_consts=p1_consts, p2_rows=p2_rows, p2_consts=p2_consts))
        h_cur = y

    dy, loss_part = loss_head(h_cur, target)

    G = {n: [None] * DEPTH for n in SMALL}
    S = {n: [None] * DEPTH for n in BIG}
    parts = {n: [None] * DEPTH for n in BIG}
    pending = None
    for l in reversed(range(DEPTH)):
        a = acts[l]
        dx1, ddown, dg2, db2, dwpg, dwple = row_bwd("post2_bwd", post2_f, a["p2_rows"], a["p2_consts"], [dy], [0, 1], [0, 1, 2, 3], [f32, f32])
        dgate, dup = ffn_dact(ddown, a["wd"], a["gate"], a["up"])
        dwd = mm("dw_down", a["act"], ddown, mode="tn", out_dtype=f32)
        dx1 = mm("dx_gate", dgate, a["wg"], mode="nt", out_dtype=f32, add=dx1)
        dx1 = mm("dx_up", dup, a["wu"], mode="nt", out_dtype=f32, add=dx1)
        dwg = mm("dw_gate", a["x1"], dgate, mode="tn", out_dtype=f32)
        dwu = mm("dw_up", a["x1"], dup, mode="tn", out_dtype=f32)
        do_gdn, dz, do_mla, dxr, dgg, dwout, dg1, db1 = row_bwd("post1_bwd", post1_f, a["p1_rows"], a["p1_consts"], [dx1], [0, 1, 2, 3],
                                                              [0, 1, 2, 3], [f32, bf16, f32, f32])
        for g, n in ((dwd, "w_down"), (dwple, "w_ple"), (dwpg, "w_ple_gate"), (dwout, "w_out")):
            S[n][l] = _shard_grad(g, n)
        S["w_gate_up"][l] = jnp.concatenate([_shard_grad(dwg, "w_gate_up"), _shard_grad(dwu, "w_gate_up")], axis=0)
        early = _grad_groups(l)[0]
        res = attn_bwd(a["qm"], a["km"], a["vm"], a["o_mla"], a["lse"], do_mla, name=f"attn_bwd_exchange{len(early)}",
                       comms=[(S[n][l], True) for n in early])
        dqm, dkm, dvm = res[:3]
        for n, got in zip(early, res[3:]):
            parts[n][l] = got
        dstates = gdn_scan_bwd(do_gdn, *a["scan"])
        late = _grad_groups(pending)[1] if pending is not None else ()
        res = gdn_local_bwd(a["qn"], a["kn"], a["vs"], a["beta"], a["glog"], a["states"], dstates, do_gdn,
                            name="gdn_local_bwd_exchange" if late else "gdn_local_bwd", comms=[(S[n][pending], True) for n in late])
        dqn, dkn, dvs, dbeta, dglog = res[:5]
        for n, got in zip(late, res[5:]):
            parts[n][pending] = got
        dqkv, dba, dcw, dal, ddt = gdn_pre_bwd(a["hb"], a["cw8"], a["alog_b"], a["dtb_b"], dqn, dkn, dvs, dbeta, dglog)
        dcq, dckv, dkrg, dkrs, dqg, dwq, dkvg, dwkv = row_bwd("mla_pre_bwd", mla_pre_f, a["mla_rows"], a["mla_consts"], [dqm, dkm, dvm],
                                                              [0, 1, 2, 3], [0, 1, 2, 3], [bf16, bf16, bf16, bf16], row_add={2: dba})
        dh = jnp.concatenate([dqkv, dz, dckv, dkrg, dkrs, dcq], axis=1)
        dwin = mm("dw_in", a["x"], dh, mode="tn", out_dtype=f32)
        dy = mm("dx_in", dh, a["win"], mode="nt", out_dtype=f32, add=dxr)

        S["w_in"][l], S["w_uq"][l] = _shard_grad(_unprep_w_in(dwin), "w_in"), _shard_grad(_unprep_w_uq(dwq), "w_uq")
        S["w_ukv"][l] = _shard_grad(dwkv, "w_ukv")
        G["conv_w"][l] = dcw[:CW]
        G["a_log"][l] = jnp.sum(dal.reshape(H, DK), axis=1)
        G["dt_bias"][l] = jnp.sum(ddt.reshape(H, DK), axis=1)
        G["gdn_norm_g"][l] = jnp.sum(dgg.reshape(H, DK), axis=0)
        G["q_norm_g"][l] = dqg[0, :QL]
        G["kv_norm_g"][l] = dkvg[0]
        G["ln1_g"][l], G["ln1_b"][l], G["ln2_g"][l], G["ln2_b"][l] = dg1[0], db1[0], dg2[0], db2[0]
        pending = l
    grad_x = dy.reshape(x.shape)

    small_like = {n: W[n] for n in SMALL}
    conv_parts = jnp.stack(G["conv_w"]).reshape(DEPTH, CW, N_DEV, -1)
    smalls = []
    for d in range(N_DEV):
        tree = {n: jnp.stack(G[n]) for n in SMALL if n != "conv_w"}
        tree["conv_w"] = conv_parts[:, :, d, :]
        smalls.append(_pack_small(tree, loss_part))
    small_send = jnp.stack(smalls)
    late = _grad_groups(pending)[1]
    res = exchange("exchange_last", [(S[n][pending], True) for n in late] + [(small_send, True)])
    for n, got in zip(late, res):
        parts[n][pending] = got

    g_out, d_out, m_out, v_out = {}, {}, {}, {}
    for n in BIG:
        g_out[n], d_out[n], m_out[n], v_out[n] = adamw(W[n], M[n], V[n], parts[n])
    zero = jnp.zeros((), f32)
    upd = adamw(_pack_small(W, zero)[None], _pack_small(M, zero)[None], _pack_small(V, zero)[None], [res[-1]])
    for dst, rows_arr in zip((g_out, d_out, m_out, v_out), upd):
        dst.update(_unpack_small(rows_arr[0], small_like)[0])
    loss = _unpack_small(upd[0][0], small_like)[1]
    order = ["w_in", "conv_w", "a_log", "dt_bias", "gdn_norm_g", "q_norm_g", "w_uq", "kv_norm_g", "w_ukv", "w_out", "ln1_g", "ln1_b",
             "w_gate_up", "w_down", "ln2_g", "ln2_b", "w_ple", "w_ple_gate"]
    return (loss, grad_x, *[g_out[n] for n in order], *[d_out[n] for n in order], *[m_out[n] for n in order],
            *[v_out[n] for n in order])
```

```python
import jax
import jax.numpy as jnp
import numpy as np
from jax import lax
from jax.experimental import pallas as pl
from jax.experimental.pallas import tpu as pltpu

f32 = jnp.float32
bf16 = jnp.bfloat16
MX = jnp.bfloat16

D = 1024
DEPTH = 4
CHUNK = 64
H = 4
DK = 128
PLE = 256
QL = 384
KVL = 256
ROPE = 64
DFF = 2816
IN_W = 2760
ROPE_THETA = 10000.0
ALPHA = (2.0 * DEPTH) ** 0.25
LN_EPS = 1e-5
RMS_EPS = 1e-6
ATT_SCALE = (128 + 64) ** -0.5
N_DEV = 8

ADAM_LR, ADAM_B1, ADAM_B2, ADAM_EPS, ADAM_WD, ADAM_STEP = 0.001, 0.9, 0.999, 1e-08, 0.01, 10

OFF_QKV, OFF_Z, OFF_CKV, OFF_KR, OFF_KRS, OFF_CQ, HP = 0, 1536, 2048, 2304, 2432, 2560, 3072
CQP = 512
LANE = 128
VMEM_LIMIT = 48 * 1024 * 1024
ROW_TB = 256


def _dot(a, b, ca, cb, prec=None):
    if a.ndim == 3:
        return lax.dot_general(a, b, (((ca + 1,), (cb + 1,)), ((0,), (0,))), precision=prec, preferred_element_type=f32)
    return lax.dot_general(a, b, (((ca,), (cb,)), ((), ())), precision=prec, preferred_element_type=f32)


@jax.custom_vjp
def bdot(a, w):
    return _dot(a.astype(MX), w.astype(MX), 1, 0)


def _bdot_fwd(a, w):
    return bdot(a, w), (a, w)


def _bdot_bwd(res, g):
    a, w = res
    gb = g.astype(MX)
    return _dot(gb, w.astype(MX), 1, 1).astype(a.dtype), _dot(a.astype(MX), gb, 0, 0).astype(w.dtype)


bdot.defvjp(_bdot_fwd, _bdot_bwd)


@jax.custom_vjp
def bdot_nt(a, b):
    return _dot(a.astype(MX), b.astype(MX), 1, 1)


def _bdot_nt_fwd(a, b):
    return bdot_nt(a, b), (a, b)


def _bdot_nt_bwd(res, g):
    a, b = res
    gb = g.astype(MX)
    return _dot(gb, b.astype(MX), 1, 0).astype(a.dtype), _dot(gb, a.astype(MX), 0, 0).astype(b.dtype)


bdot_nt.defvjp(_bdot_nt_fwd, _bdot_nt_bwd)


@jax.custom_vjp
def bdot_tn(a, b):
    return _dot(a.astype(MX), b.astype(MX), 0, 0)


def _bdot_tn_fwd(a, b):
    return bdot_tn(a, b), (a, b)


def _bdot_tn_bwd(res, g):
    a, b = res
    gb = g.astype(MX)
    return _dot(b.astype(MX), gb, 1, 1).astype(a.dtype), _dot(a.astype(MX), gb, 1, 0).astype(b.dtype)


bdot_tn.defvjp(_bdot_tn_fwd, _bdot_tn_bwd)


@jax.custom_vjp
def hdot(a, b):
    return _dot(a, b, 1, 0, lax.Precision.HIGHEST)


def _hdot_fwd(a, b):
    return hdot(a, b), (a, b)


def _hdot_bwd(res, g):
    a, b = res
    return _dot(g, b, 1, 1, lax.Precision.HIGHEST), _dot(a, g, 0, 0, lax.Precision.HIGHEST)


hdot.defvjp(_hdot_fwd, _hdot_bwd)


def _pick(n, cap):
    best = None
    for t in range(LANE, min(n, cap) + 1, LANE):
        if n % t == 0:
            best = t
    assert best is not None, (n, cap)
    return best


def _cparams(sem):
    return pltpu.CompilerParams(dimension_semantics=sem, vmem_limit_bytes=VMEM_LIMIT)


def mm(name, a, b, *, mode, out_dtype, add=None):
    if mode == "tn":
        kdim, m = a.shape
        n = b.shape[1]
        tm, tn, tk = _pick(m, 1536), _pick(n, 1536), min(512, kdim)
        nk = kdim // tk

        def body(a_ref, b_ref, o_ref, acc):
            k = pl.program_id(2)

            @pl.when(k == 0)
            def _():
                acc[...] = jnp.zeros_like(acc)

            acc[...] += _dot(a_ref[...].astype(MX), b_ref[...].astype(MX), 0, 0)

            @pl.when(k == nk - 1)
            def _():
                o_ref[...] = acc[...].astype(o_ref.dtype)

        return pl.pallas_call(
            body, name=name, grid=(m // tm, n // tn, nk),
            in_specs=[pl.BlockSpec((tk, tm), lambda i, j, k: (k, i)), pl.BlockSpec((tk, tn), lambda i, j, k: (k, j))],
            out_specs=pl.BlockSpec((tm, tn), lambda i, j, k: (i, j)),
            out_shape=jax.ShapeDtypeStruct((m, n), out_dtype),
            scratch_shapes=[pltpu.VMEM((tm, tn), f32)],
            compiler_params=_cparams(("parallel", "parallel", "arbitrary")),
        )(a, b)

    m, kdim = a.shape
    n = b.shape[1] if mode == "nn" else b.shape[0]
    tm, tn = min(512, m), _pick(n, 1536)
    has_add = add is not None

    def body(*refs):
        a_ref, b_ref = refs[0], refs[1]
        o_ref = refs[-1]
        r = _dot(a_ref[...].astype(MX), b_ref[...].astype(MX), 1, 0 if mode == "nn" else 1)
        if has_add:
            r = r + refs[2][...].astype(f32)
        o_ref[...] = r.astype(o_ref.dtype)

    b_spec = (pl.BlockSpec((kdim, tn), lambda i, j: (0, j)) if mode == "nn"
              else pl.BlockSpec((tn, kdim), lambda i, j: (j, 0)))
    in_specs = [pl.BlockSpec((tm, kdim), lambda i, j: (i, 0)), b_spec]
    args = [a, b]
    if has_add:
        in_specs.append(pl.BlockSpec((tm, tn), lambda i, j: (i, j)))
        args.append(add)
    return pl.pallas_call(
        body, name=name, grid=(m // tm, n // tn), in_specs=in_specs,
        out_specs=pl.BlockSpec((tm, tn), lambda i, j: (i, j)),
        out_shape=jax.ShapeDtypeStruct((m, n), out_dtype),
        compiler_params=_cparams(("parallel", "parallel")),
    )(*args)


def _row_spec(tb, w, cb):
    return pl.BlockSpec((tb, w), lambda i: (i, cb))


def _full_spec(shape):
    return pl.BlockSpec(shape, lambda i: (0,) * len(shape))


def row_fwd(name, f, rows, consts, outs):
    t = rows[0][0].shape[0]
    tb = min(ROW_TB, t)
    nr, nc = len(rows), len(consts)

    def body(*refs):
        vals = [r[...] for r in refs[:nr + nc]]
        res = f(*vals)
        for o_ref, val in zip(refs[nr + nc:], res):
            o_ref[...] = val.astype(o_ref.dtype)

    return pl.pallas_call(
        body, name=name, grid=(t // tb,),
        in_specs=[_row_spec(tb, w, cb) for _, w, cb in rows] + [_full_spec(c.shape) for c in consts],
        out_specs=[_row_spec(tb, w, 0) for w, _ in outs],
        out_shape=[jax.ShapeDtypeStruct((t, w), dt) for w, dt in outs],
        compiler_params=_cparams(("parallel",)),
    )(*[r[0] for r in rows], *consts)


def row_bwd(name, f, rows, consts, cots, row_diff, const_diff, drow_dtypes, row_add=None):
    t = rows[0][0].shape[0]
    tb = min(ROW_TB, t)
    nr, nc, nct = len(rows), len(consts), len(cots)
    row_add = row_add or {}
    add_keys = sorted(row_add)
    n_in = nr + nc + nct + len(add_keys)

    def body(*refs):
        i = pl.program_id(0)
        rv = [r[...] for r in refs[:nr]]
        cv = [refs[nr + k][...].astype(f32) if k in const_diff else refs[nr + k][...] for k in range(nc)]
        cot_refs = refs[nr + nc:nr + nc + nct]
        add_refs = refs[nr + nc + nct:n_in]
        drow_refs = refs[n_in:n_in + len(row_diff)]
        dconst_refs = refs[n_in + len(row_diff):]

        def g(*dv):
            r2, c2 = list(rv), list(cv)
            for p, k in enumerate(row_diff):
                r2[k] = dv[p]
            for p, k in enumerate(const_diff):
                c2[k] = dv[len(row_diff) + p]
            return tuple(f(*r2, *c2))

        prim = [rv[k].astype(f32) for k in row_diff] + [cv[k] for k in const_diff]
        outs, vf = jax.vjp(g, *prim)
        grads = vf(tuple(c[...].astype(o.dtype) for c, o in zip(cot_refs, outs)))
        for p, ref in enumerate(drow_refs):
            val = grads[p]
            if p in row_add:
                val = val + add_refs[add_keys.index(p)][...].astype(f32)
            ref[...] = val.astype(ref.dtype)

        @pl.when(i == 0)
        def _():
            for ref in dconst_refs:
                ref[...] = jnp.zeros_like(ref)

        for p, ref in enumerate(dconst_refs):
            ref[...] += grads[len(row_diff) + p]

    widths = [rows[k][1] for k in row_diff]
    return pl.pallas_call(
        body, name=name, grid=(t // tb,),
        in_specs=([_row_spec(tb, w, cb) for _, w, cb in rows] + [_full_spec(c.shape) for c in consts]
                  + [_row_spec(tb, c.shape[1], 0) for c in cots] + [_row_spec(tb, row_add[k].shape[1], 0) for k in add_keys]),
        out_specs=([_row_spec(tb, w, 0) for w in widths] + [_full_spec(consts[k].shape) for k in const_diff]),
        out_shape=([jax.ShapeDtypeStruct((t, w), dt) for w, dt in zip(widths, drow_dtypes)]
                   + [jax.ShapeDtypeStruct(consts[k].shape, f32) for k in const_diff]),
        compiler_params=_cparams(("arbitrary",)),
    )(*[r[0] for r in rows], *consts, *cots, *[row_add[k] for k in add_keys])


def _heads(x, w=DK):
    return [x[:, w * h:w * (h + 1)] for h in range(H)]


def _layer_norm(r, g, b):
    mu = jnp.mean(r, -1, keepdims=True)
    var = jnp.mean(jnp.square(r - mu), -1, keepdims=True)
    return (r - mu) * lax.rsqrt(var + LN_EPS) * g + b


def gdn_point_f(c, ba, alog_b, dtb_b):
    s = c * jax.nn.sigmoid(c)
    q, k, v = s[:, :512], s[:, 512:1024], s[:, 1024:]

    def l2(x):
        return jnp.concatenate([xh * lax.rsqrt(jnp.sum(xh * xh, -1, keepdims=True) + RMS_EPS) for xh in _heads(x)], axis=1)

    tb = c.shape[0]
    b_b = jnp.concatenate([jnp.broadcast_to(ba[:, 64 + h:65 + h], (tb, DK)) for h in range(H)], axis=1)
    a_b = jnp.concatenate([jnp.broadcast_to(ba[:, 68 + h:69 + h], (tb, DK)) for h in range(H)], axis=1)
    beta = jax.nn.sigmoid(b_b)
    g = -jnp.exp(alog_b) * jax.nn.softplus(a_b + dtb_b)
    return l2(q), l2(k), v, beta, g


def mla_pre_f(cq, ckv, krg, krs, cf, sf, qg, wq, kvg, wkv):
    cqn = cq * lax.rsqrt(jnp.sum(cq * cq, -1, keepdims=True) * (1.0 / QL) + RMS_EPS) * qg
    qa = bdot(cqn, wq)
    ckvn = ckv * lax.rsqrt(jnp.mean(ckv * ckv, -1, keepdims=True) + RMS_EPS) * kvg
    kv = bdot(ckvn, wkv)
    kro = krg * cf + krs * sf
    qs, ks, vs = [], [], []
    for h in range(H):
        qs += [qa[:, DK * h:DK * (h + 1)], qa[:, 512 + DK * h:512 + DK * (h + 1)] * cf + qa[:, 1024 + DK * h:1024 + DK * (h + 1)] * sf]
        ks += [kv[:, 256 * h:256 * h + DK], kro]
        vs += [kv[:, 256 * h + DK:256 * (h + 1)]]
    return jnp.concatenate(qs, axis=1) * ATT_C2, jnp.concatenate(ks, axis=1), jnp.concatenate(vs, axis=1)


def post1_f(o, z, omla, x, gg_b, wout, g1, b1):
    on = jnp.concatenate([oh * lax.rsqrt(jnp.mean(oh * oh, -1, keepdims=True) + RMS_EPS) for oh in _heads(o)], axis=1) * gg_b
    ogdn = on * (z * jax.nn.sigmoid(z))
    mix = bdot(jnp.concatenate([ogdn, omla], axis=1), wout)
    return (_layer_norm(ALPHA * x + mix, g1, b1),)


def post2_f(x1, down, p, g2, b2, wpg, wple):
    x2 = _layer_norm(ALPHA * x1 + down, g2, b2)
    return (x2 + jax.nn.sigmoid(bdot(x2, wpg)) * bdot(p, wple),)


HALO = 8
CW = 4


def _conv_from_scratch(xs, cw_ref, tb):
    c = xs[pl.ds(HALO - 3, tb), :] * cw_ref[0:1, :]
    for j in range(1, CW):
        c = c + xs[pl.ds(HALO - 3 + j, tb), :] * cw_ref[j:j + 1, :]
    return c


def gdn_pre_fwd(hbuf, conv_w8, alog_b, dtb_b):
    t = hbuf.shape[0]
    tb = min(ROW_TB, t)

    def body(x_ref, halo_ref, ba_ref, cw_ref, al_ref, dt_ref, q_ref, k_ref, v_ref, be_ref, g_ref, xs):
        i = pl.program_id(0)
        xs[pl.ds(0, HALO), :] = jnp.where(i == 0, 0.0, halo_ref[...])
        xs[pl.ds(HALO, tb), :] = x_ref[...]
        c = _conv_from_scratch(xs, cw_ref, tb)
        q, k, v, be, g = gdn_point_f(c, ba_ref[...], al_ref[...], dt_ref[...])
        q_ref[...], k_ref[...], v_ref[...], be_ref[...], g_ref[...] = q, k, v, be, g

    return pl.pallas_call(
        body, name="gdn_pre_fwd", grid=(t // tb,),
        in_specs=[_row_spec(tb, 1536, 0),
                  pl.BlockSpec((HALO, 1536), lambda i: (jnp.maximum(i * (tb // HALO) - 1, 0), 0)),
                  _row_spec(tb, LANE, OFF_KR // LANE),
                  _full_spec(conv_w8.shape), _full_spec(alog_b.shape), _full_spec(dtb_b.shape)],
        out_specs=[_row_spec(tb, 512, 0)] * 5,
        out_shape=[jax.ShapeDtypeStruct((t, 512), f32)] * 5,
        scratch_shapes=[pltpu.VMEM((tb + HALO, 1536), f32)],
        compiler_params=_cparams(("arbitrary",)),
    )(hbuf, hbuf, hbuf, conv_w8, alog_b, dtb_b)


def gdn_pre_bwd(hbuf, conv_w8, alog_b, dtb_b, dq, dk, dv, dbe, dg):
    t = hbuf.shape[0]
    tb = min(ROW_TB, t)
    n = t // tb

    def body(x_ref, halo_ref, ba_ref, cw_ref, al_ref, dt_ref, dq_ref, dk_ref, dv_ref, dbe_ref, dg_ref,
             dx_ref, dba_ref, dcw_ref, dal_ref, ddt_ref, xs, dcs):
        s = pl.program_id(0)
        i = n - 1 - s
        xs[pl.ds(0, HALO), :] = jnp.where(i == 0, 0.0, halo_ref[...])
        xs[pl.ds(HALO, tb), :] = x_ref[...]
        c = _conv_from_scratch(xs, cw_ref, tb)
        _, vf = jax.vjp(gdn_point_f, c, ba_ref[...], al_ref[...], dt_ref[...])
        dc, dba, dal, ddt = vf((dq_ref[...], dk_ref[...], dv_ref[...], dbe_ref[...], dg_ref[...]))

        @pl.when(s == 0)
        def _():
            dcs[pl.ds(tb, HALO), :] = jnp.zeros((HALO, 1536), f32)
            dcw_ref[...] = jnp.zeros_like(dcw_ref)
            dal_ref[...] = jnp.zeros_like(dal_ref)
            ddt_ref[...] = jnp.zeros_like(ddt_ref)

        @pl.when(s > 0)
        def _():
            dcs[pl.ds(tb, HALO), :] = dcs[pl.ds(0, HALO), :]

        dcs[pl.ds(0, tb), :] = dc
        dx = dcs[pl.ds(3, tb), :] * cw_ref[0:1, :]
        for j in range(1, CW):
            dx = dx + dcs[pl.ds(3 - j, tb), :] * cw_ref[j:j + 1, :]
        dx_ref[...] = dx.astype(dx_ref.dtype)
        dba_ref[...] = dba
        for j in range(CW):
            dcw_ref[j:j + 1, :] += jnp.sum(dc * xs[pl.ds(HALO - 3 + j, tb), :], axis=0, keepdims=True)
        dal_ref[...] += dal
        ddt_ref[...] += ddt

    rev = lambda cb: (lambda s: (n - 1 - s, cb))
    return pl.pallas_call(
        body, name="gdn_pre_bwd", grid=(n,),
        in_specs=[pl.BlockSpec((tb, 1536), rev(0)),
                  pl.BlockSpec((HALO, 1536), lambda s: (jnp.maximum((n - 1 - s) * (tb // HALO) - 1, 0), 0)),
                  pl.BlockSpec((tb, LANE), rev(OFF_KR // LANE)),
                  _full_spec(conv_w8.shape), _full_spec(alog_b.shape), _full_spec(dtb_b.shape)]
        + [pl.BlockSpec((tb, 512), rev(0))] * 5,
        out_specs=[pl.BlockSpec((tb, 1536), rev(0)), pl.BlockSpec((tb, LANE), rev(0)),
                   _full_spec(conv_w8.shape), _full_spec(alog_b.shape), _full_spec(dtb_b.shape)],
        out_shape=[jax.ShapeDtypeStruct((t, 1536), bf16), jax.ShapeDtypeStruct((t, LANE), f32),
                   jax.ShapeDtypeStruct(conv_w8.shape, f32), jax.ShapeDtypeStruct(alog_b.shape, f32),
                   jax.ShapeDtypeStruct(dtb_b.shape, f32)],
        scratch_shapes=[pltpu.VMEM((tb + HALO, 1536), f32), pltpu.VMEM((tb + HALO, 1536), f32)],
        compiler_params=_cparams(("arbitrary",)),
    )(hbuf, hbuf, hbuf, conv_w8, alog_b, dtb_b, dq, dk, dv, dbe, dg)


GDN_NB = 4
GDN_SCAN_NB = 8


def _to_batch(x, rows=CHUNK):
    n = x.shape[0] // rows
    return jnp.concatenate([x[rows * ci:rows * (ci + 1), DK * h:DK * (h + 1)][None] for ci in range(n) for h in range(H)], axis=0)


def _from_batch(y):
    n = y.shape[0] // H
    return jnp.concatenate([jnp.concatenate([y[ci * H + h] for h in range(H)], axis=1) for ci in range(n)], axis=0)


def gdn_intra_b(q, k, v, beta, g):
    c = CHUNK
    n = q.shape[0] // c
    tri = jnp.where(lax.broadcasted_iota(jnp.int32, (c, c), 0) >= lax.broadcasted_iota(jnp.int32, (c, c), 1), 1.0, 0.0)
    gc = _to_batch(jnp.concatenate([hdot(tri, g[c * ci:c * (ci + 1)]) for ci in range(n)], axis=0))
    qb, kb, vb, bb = _to_batch(q) * DK ** -0.5, _to_batch(k), _to_batch(v), _to_batch(beta)
    nbat = qb.shape[0]
    row = lax.broadcasted_iota(jnp.int32, (1, c, DK), 1)
    col = lax.broadcasted_iota(jnp.int32, (1, c, DK), 2)
    incl, strict, eye = row >= col, row > col, row == col
    grow = hdot(jnp.ones((nbat, c, c), f32), jnp.where(eye, gc, 0.0))
    decay = jnp.where(incl, jnp.exp(jnp.where(incl, gc - grow, 0.0)), 0.0)
    kbeta = kb * bb
    kpad = jnp.concatenate([kb, jnp.zeros((nbat, DK - c, DK), f32)], axis=1)
    a = jnp.where(strict, bdot_nt(kbeta, kpad) * decay, 0.0)[:, :, :c]
    nn = -a
    bk = bdot(a, a)
    for step in range(5):
        nn = nn + bk + bdot(nn, bk)
        if step < 4:
            bk = bdot(bk, bk)
    eg = jnp.exp(gc)
    rhs_v, rhs_k = vb * bb, kbeta * eg
    g_last = gc[:, c - 1:c, :]
    u = rhs_v + bdot(nn, rhs_v)
    w = rhs_k + bdot(nn, rhs_k)
    attn = jnp.where(incl, bdot_nt(qb, kpad) * decay, 0.0)
    return u, w, qb * eg, kb * jnp.exp(g_last - gc), attn, jnp.exp(g_last)


def gdn_scan_b(u, w, qd, kd, attn, el, state):
    v_new = u - bdot(w, state)
    return bdot(qd, state) + bdot(attn[:, :, :CHUNK], v_new), state * el + bdot_tn(kd, v_new)


def gdn_scan_bwd_b(do, w, qd, kd, attn, el, dstate):
    dvn = bdot_tn(attn[:, :, :CHUNK], do) + bdot(kd, dstate)
    return bdot_tn(qd, do) + dstate * el - bdot_tn(w, dvn)


def gdn_chunks_f(q, k, v, beta, g, states):
    n = states.shape[0]
    o, s_new = gdn_scan_b(*gdn_intra_b(q, k, v, beta, g), states.reshape(n * H, DK, DK))
    return _from_batch(o), s_new.reshape(n, H * DK, DK)


def _chunk_rows(cidx):
    return pl.ds(cidx * CHUNK, CHUNK)


def gdn_intra(q, k, v, beta, g):
    t = q.shape[0]
    nb = min(GDN_NB, t // CHUNK)
    tb = nb * CHUNK

    def body(q_ref, k_ref, v_ref, be_ref, g_ref, u_ref, w_ref, qd_ref, kd_ref, at_ref, el_ref):
        u, w, qd, kd, at, el = gdn_intra_b(q_ref[...], k_ref[...], v_ref[...], be_ref[...], g_ref[...])
        u_ref[...], w_ref[...], qd_ref[...], kd_ref[...], at_ref[...] = (_from_batch(a) for a in (u, w, qd, kd, at))
        el_ref[...] = _from_batch(jnp.broadcast_to(el, (nb * H, 8, DK))).reshape(nb, 8, H * DK)

    return pl.pallas_call(
        body, name="gdn_intra", grid=(t // tb,),
        in_specs=[_row_spec(tb, 512, 0)] * 5,
        out_specs=[_row_spec(tb, 512, 0)] * 5 + [pl.BlockSpec((nb, 8, 512), lambda i: (i, 0, 0))],
        out_shape=[jax.ShapeDtypeStruct((t, 512), f32)] * 5 + [jax.ShapeDtypeStruct((t // CHUNK, 8, 512), f32)],
        compiler_params=_cparams(("parallel",)),
    )(q, k, v, beta, g)


def gdn_scan_fwd(u, w, qd, kd, attn, el):
    t = u.shape[0]
    nb = min(GDN_SCAN_NB, t // CHUNK)
    tb = nb * CHUNK

    def body(u_ref, w_ref, qd_ref, kd_ref, at_ref, el_ref, o_ref, s_ref, state):
        @pl.when(pl.program_id(0) == 0)
        def _():
            state[...] = jnp.zeros_like(state)

        for cidx in range(nb):
            r = _chunk_rows(cidx)
            s_ref[cidx] = state[...]
            ins = [_to_batch(ref[r, :]) for ref in (u_ref, w_ref, qd_ref, kd_ref, at_ref)]
            el = _to_batch(el_ref[cidx], 8)[:, 0:1, :]
            o, s_new = gdn_scan_b(*ins, el, state[...].reshape(H, DK, DK))
            o_ref[r, :] = _from_batch(o)
            state[...] = s_new.reshape(H * DK, DK)

    return pl.pallas_call(
        body, name="gdn_scan_fwd", grid=(t // tb,),
        in_specs=[_row_spec(tb, 512, 0)] * 5 + [pl.BlockSpec((nb, 8, 512), lambda i: (i, 0, 0))],
        out_specs=[_row_spec(tb, 512, 0), pl.BlockSpec((nb, 512, DK), lambda i: (i, 0, 0))],
        out_shape=[jax.ShapeDtypeStruct((t, 512), f32), jax.ShapeDtypeStruct((t // CHUNK, 512, DK), f32)],
        scratch_shapes=[pltpu.VMEM((512, DK), f32)],
        compiler_params=_cparams(("arbitrary",)),
    )(u, w, qd, kd, attn, el)


def gdn_scan_bwd(do, w, qd, kd, attn, el):
    t = do.shape[0]
    nb = min(GDN_SCAN_NB, t // CHUNK)
    tb = nb * CHUNK
    n = t // tb

    def body(do_ref, w_ref, qd_ref, kd_ref, at_ref, el_ref, ds_ref, dstate):
        @pl.when(pl.program_id(0) == 0)
        def _():
            dstate[...] = jnp.zeros_like(dstate)

        for cidx in reversed(range(nb)):
            r = _chunk_rows(cidx)
            ds_ref[cidx] = dstate[...]
            ins = [_to_batch(ref[r, :]) for ref in (do_ref, w_ref, qd_ref, kd_ref, at_ref)]
            el = _to_batch(el_ref[cidx], 8)[:, 0:1, :]
            dstate[...] = gdn_scan_bwd_b(*ins, el, dstate[...].reshape(H, DK, DK)).reshape(H * DK, DK)

    rev = pl.BlockSpec((tb, 512), lambda s: (n - 1 - s, 0))
    return pl.pallas_call(
        body, name="gdn_scan_bwd", grid=(n,),
        in_specs=[rev] * 5 + [pl.BlockSpec((nb, 8, 512), lambda s: (n - 1 - s, 0, 0))],
        out_specs=pl.BlockSpec((nb, 512, DK), lambda s: (n - 1 - s, 0, 0)),
        out_shape=jax.ShapeDtypeStruct((t // CHUNK, 512, DK), f32),
        scratch_shapes=[pltpu.VMEM((512, DK), f32)],
        compiler_params=_cparams(("arbitrary",)),
    )(do, w, qd, kd, attn, el)


def gdn_local_bwd(q, k, v, beta, g, states, dstates, do, name="gdn_local_bwd", comms=()):
    t = q.shape[0]
    nb = min(GDN_NB, t // CHUNK)
    tb = nb * CHUNK
    n = t // tb

    def body(*refs):
        ins, outs, _, comm_refs = _split_hosted(refs, 8, 5, 0, len(comms))
        q_ref, k_ref, v_ref, be_ref, g_ref, s_ref, ds_ref, do_ref = ins
        dq_ref, dk_ref, dv_ref, dbe_ref, dg_ref = outs
        _host_exchanges(comm_refs, comms, pl.program_id(0) == 0, pl.program_id(0) == n - 1)
        states_v = s_ref[...]
        _, vf = jax.vjp(lambda *a: gdn_chunks_f(*a, states_v), q_ref[...], k_ref[...], v_ref[...], be_ref[...], g_ref[...])
        dq_ref[...], dk_ref[...], dv_ref[...], dbe_ref[...], dg_ref[...] = vf((do_ref[...], ds_ref[...]))

    st = pl.BlockSpec((nb, 512, DK), lambda i: (i, 0, 0))
    in_specs = [_row_spec(tb, 512, 0)] * 5 + [st, st, _row_spec(tb, 512, 0)]
    out_specs = [_row_spec(tb, 512, 0)] * 5
    out_shape = [jax.ShapeDtypeStruct((t, 512), f32)] * 5
    scratch, args = [], [q, k, v, beta, g, states, dstates, do]
    _add_hosted(comms, in_specs, out_specs, out_shape, scratch, args)
    return pl.pallas_call(body, name=name, grid=(n,), in_specs=in_specs, out_specs=out_specs, out_shape=out_shape,
                          scratch_shapes=scratch, compiler_params=_cparams(("arbitrary",) if comms else ("parallel",)))(*args)


NEG = -1e30


LOG2E = 1.4426950408889634
ATT_C2 = ATT_SCALE * LOG2E
ATT_RB = 256


def _add_hosted(comms, in_specs, out_specs, out_shape, scratch, args):
    for src, scatter in comms:
        in_specs.append(_ANY)
        out_specs.append(_ANY)
        out_shape.append(_comm_out_shape(src, scatter))
        scratch += _comm_scratch()
        args.append(src)


def _split_hosted(refs, n_in, n_out, n_scratch, n_comm):
    ins, srcs = refs[:n_in], refs[n_in:n_in + n_comm]
    o0 = n_in + n_comm
    outs, dsts = refs[o0:o0 + n_out], refs[o0 + n_out:o0 + n_out + n_comm]
    rest = refs[o0 + n_out + n_comm:]
    sems = rest[n_scratch:]
    return ins, outs, rest[:n_scratch], [(srcs[c], dsts[c]) + tuple(sems[3 * c:3 * c + 3]) for c in range(n_comm)]


def _host_exchanges(comm_refs, comms, is_first, is_last):
    if not comms:
        return

    @pl.when(is_first)
    def _():
        for refs, (_, scatter) in zip(comm_refs, comms):
            for op in _exchange_copies(*refs, scatter):
                op.start()

    @pl.when(is_last)
    def _():
        for refs, (_, scatter) in zip(comm_refs, comms):
            for op in _exchange_copies(*refs, scatter):
                op.wait()


def attn_fwd(q, k, v, name="attn_fwd", comms=()):
    t = q.shape[0]
    tk = min(1024, t)
    ratio = 2 if t % (2 * tk) == 0 else 1
    tq = ratio * tk
    rb = min(ATT_RB, tk)
    nq, nk, nr, kr = t // tq, t // tk, tq // rb, tk // rb
    last = lambda i: i * ratio + ratio - 1

    def body(*refs):
        (q_ref, k_ref, v_ref), (o_ref, lse_ref), (m_s, acc), comm_refs = _split_hosted(refs, 3, 2, 2, len(comms))
        h, i, j = pl.program_id(0), pl.program_id(1), pl.program_id(2)
        _host_exchanges(comm_refs, comms, (h == 0) & (i == 0) & (j == 0), (h == H - 1) & (i == nq - 1) & (j == nk - 1))

        @pl.when(j == 0)
        def _():
            m_s[...] = jnp.full_like(m_s, NEG)
            acc[...] = jnp.zeros_like(acc)

        one_col = jnp.where(lax.broadcasted_iota(jnp.int32, (tk, DK), 1) == 0, 1.0, 0.0).astype(MX)
        vx = jnp.concatenate([v_ref[...], one_col], axis=1)

        def rows_of(r):
            return pl.ds(r * rb, rb)

        def soft(r, s):
            m_old = m_s[rows_of(r), :]
            m_new = jnp.maximum(m_old, jnp.max(s, axis=-1, keepdims=True))
            m_s[rows_of(r), :] = m_new
            return jnp.exp2(s - m_new).astype(MX), jnp.exp2(m_old - m_new)

        def pv(r, p, alpha, vals):
            acc[rows_of(r), :] = alpha * acc[rows_of(r), :] + _dot(p, vals, 1, 0)

        def whole_block():
            kb = k_ref[...]
            ss = [_dot(q_ref[rows_of(r), :], kb, 1, 1) for r in range(min(2, nr))]
            pend = None
            for r in range(nr):
                if r + 2 < nr:
                    ss.append(_dot(q_ref[rows_of(r + 2), :], kb, 1, 1))
                p, alpha = soft(r, ss[r])
                if pend is not None:
                    pv(*pend, vx)
                pend = (r, p, alpha)
            pv(*pend, vx)

        def partial_block(part):
            def run():
                for r in range(nr):
                    rel = r - part * kr
                    if rel < 0:
                        continue
                    cols = min((rel + 1) * rb, tk)
                    s = _dot(q_ref[rows_of(r), :], k_ref[pl.ds(0, cols), :], 1, 1)
                    if rel < kr:
                        qrow = lax.broadcasted_iota(jnp.int32, (rb, cols), 0) + rel * rb
                        kcol = lax.broadcasted_iota(jnp.int32, (rb, cols), 1)
                        s = jnp.where(jnp.right_shift(kcol, 6) <= jnp.right_shift(qrow, 6), s, NEG)
                    p, alpha = soft(r, s)
                    pv(r, p, alpha, vx[:cols])
            return run

        pl.when(j < i * ratio)(whole_block)
        for part in range(ratio):
            pl.when(j == i * ratio + part)(partial_block(part))

        @pl.when(j == nk - 1)
        def _():
            a = acc[...]
            l = a[:, DK:DK + 1]
            o_ref[...] = a[:, :DK] / l
            lse_ref[0] = m_s[...] + jnp.log2(l)

    in_specs = [pl.BlockSpec((tq, 256), lambda h, i, j: (i, h)),
                pl.BlockSpec((tk, 256), lambda h, i, j: (jnp.minimum(j, last(i)), h)),
                pl.BlockSpec((tk, DK), lambda h, i, j: (jnp.minimum(j, last(i)), h))]
    out_specs = [pl.BlockSpec((tq, DK), lambda h, i, j: (i, h)), pl.BlockSpec((1, tq, 1), lambda h, i, j: (h, i, 0))]
    out_shape = [jax.ShapeDtypeStruct((t, 512), f32), jax.ShapeDtypeStruct((H, t, 1), f32)]
    scratch = [pltpu.VMEM((tq, 1), f32), pltpu.VMEM((tq, 2 * DK), f32)]
    args = [q, k, v]
    _add_hosted(comms, in_specs, out_specs, out_shape, scratch, args)
    sem = ("arbitrary",) * 3 if comms else ("parallel", "parallel", "arbitrary")
    return pl.pallas_call(body, name=name, grid=(H, nq, nk), in_specs=in_specs, out_specs=out_specs, out_shape=out_shape,
                          scratch_shapes=scratch, compiler_params=_cparams(sem))(*args)


def attn_bwd(q, k, v, o, lse, do, name="attn_bwd", comms=()):
    t = q.shape[0]
    tk = min(1024, t)
    ratio = 2 if t % (2 * tk) == 0 else 1
    tq = ratio * tk
    rb = min(ATT_RB, tk)
    nq, nk, nr, kr = t // tq, t // tk, tq // rb, tk // rb
    first = lambda j: j // ratio

    def body(*refs):
        ins, outs, (dk_acc, dv_acc), comm_refs = _split_hosted(refs, 6, 3, 2, len(comms))
        q_ref, k_ref, v_ref, o_ref, lse_ref, do_ref = ins
        dq_ref, dk_ref, dv_ref = outs
        h, j, i = pl.program_id(0), pl.program_id(1), pl.program_id(2)
        _host_exchanges(comm_refs, comms, (h == 0) & (i == 0) & (j == 0), (h == H - 1) & (i == nq - 1) & (j == nk - 1))

        @pl.when((j == 0) & (i == 0))
        def _():
            dq_ref[...] = jnp.zeros_like(dq_ref)

        @pl.when(i == 0)
        def _():
            dk_acc[...] = jnp.zeros_like(dk_acc)
            dv_acc[...] = jnp.zeros_like(dv_acc)

        def rows_of(r):
            return pl.ds(r * rb, rb)

        def front(r, cols):
            qb, dob = q_ref[rows_of(r), :], do_ref[rows_of(r), :]
            return qb, dob, _dot(qb, k_ref[pl.ds(0, cols), :], 1, 1), _dot(dob.astype(MX), v_ref[pl.ds(0, cols), :], 1, 1)

        def middle(r, dob, s, dp):
            p = jnp.exp2(s - lse_ref[0, rows_of(r), :])
            delta = jnp.sum(dob * o_ref[rows_of(r), :], axis=-1, keepdims=True)
            return p.astype(MX), (p * (dp - delta)).astype(MX)

        def back(r, cols, qb, dob, pb, ds):
            dv_acc[pl.ds(0, cols), :] += _dot(pb, dob.astype(MX), 0, 0)
            dk_acc[pl.ds(0, cols), :] += _dot(ds, qb, 0, 0) * (1.0 / LOG2E)
            grow = pl.ds(pl.multiple_of(i * tq + r * rb, rb), rb)
            dq_ref[grow, :] += _dot(ds, k_ref[pl.ds(0, cols), :], 1, 0) * (1.0 / LOG2E)

        def whole_block():
            nxt = front(0, tk)
            for r in range(nr):
                qb, dob, s, dp = nxt
                if r + 1 < nr:
                    nxt = front(r + 1, tk)
                back(r, tk, qb, dob, *middle(r, dob, s, dp))

        def partial_block(part):
            def run():
                for r in range(nr):
                    rel = r - part * kr
                    if rel < 0:
                        continue
                    cols = min((rel + 1) * rb, tk)
                    qb, dob, s, dp = front(r, cols)
                    if rel < kr:
                        qrow = lax.broadcasted_iota(jnp.int32, (rb, cols), 0) + rel * rb
                        kcol = lax.broadcasted_iota(jnp.int32, (rb, cols), 1)
                        s = jnp.where(jnp.right_shift(kcol, 6) <= jnp.right_shift(qrow, 6), s, NEG)
                    back(r, cols, qb, dob, *middle(r, dob, s, dp))
            return run

        pl.when(j < i * ratio)(whole_block)
        for part in range(ratio):
            pl.when(j == i * ratio + part)(partial_block(part))

        @pl.when(i == nq - 1)
        def _():
            dk_ref[...] = dk_acc[...]
            dv_ref[...] = dv_acc[...]

    qi = lambda h, j, i: (jnp.maximum(i, first(j)), h)
    in_specs = [pl.BlockSpec((tq, 256), qi),
                pl.BlockSpec((tk, 256), lambda h, j, i: (j, h)),
                pl.BlockSpec((tk, DK), lambda h, j, i: (j, h)),
                pl.BlockSpec((tq, DK), qi),
                pl.BlockSpec((1, tq, 1), lambda h, j, i: (h, jnp.maximum(i, first(j)), 0)),
                pl.BlockSpec((tq, DK), qi)]
    out_specs = [pl.BlockSpec((t, 256), lambda h, j, i: (0, h)),
                 pl.BlockSpec((tk, 256), lambda h, j, i: (j, h)),
                 pl.BlockSpec((tk, DK), lambda h, j, i: (j, h))]
    out_shape = [jax.ShapeDtypeStruct((t, 1024), f32), jax.ShapeDtypeStruct((t, 1024), f32), jax.ShapeDtypeStruct((t, 512), f32)]
    scratch = [pltpu.VMEM((tk, 256), f32), pltpu.VMEM((tk, DK), f32)]
    args = [q, k, v, o, lse, do]
    _add_hosted(comms, in_specs, out_specs, out_shape, scratch, args)
    return pl.pallas_call(body, name=name, grid=(H, nk, nq), in_specs=in_specs, out_specs=out_specs, out_shape=out_shape,
                          scratch_shapes=scratch, compiler_params=_cparams(("arbitrary",) * 3))(*args)


def ffn_up(x1, wg, wu):
    t = x1.shape[0]
    tm, tn = min(512, t), _pick(DFF, 1536)

    def body(x_ref, wg_ref, wu_ref, g_ref, u_ref, a_ref):
        xb = x_ref[...].astype(MX)
        g = _dot(xb, wg_ref[...], 1, 0)
        u = _dot(xb, wu_ref[...], 1, 0)
        g_ref[...] = g.astype(g_ref.dtype)
        u_ref[...] = u.astype(u_ref.dtype)
        a_ref[...] = (g * jax.nn.sigmoid(g) * u).astype(a_ref.dtype)

    w_spec = pl.BlockSpec((D, tn), lambda i, j: (0, j))
    o_spec = pl.BlockSpec((tm, tn), lambda i, j: (i, j))
    return pl.pallas_call(
        body, name="ffn_up", grid=(t // tm, DFF // tn),
        in_specs=[pl.BlockSpec((tm, D), lambda i, j: (i, 0)), w_spec, w_spec],
        out_specs=[o_spec] * 3, out_shape=[jax.ShapeDtypeStruct((t, DFF), bf16)] * 3,
        compiler_params=_cparams(("parallel", "parallel")),
    )(x1, wg, wu)


def ffn_dact(ddown, wd, g, u):
    t = ddown.shape[0]
    tm, tn = min(512, t), _pick(DFF, 1536)

    def body(dd_ref, wd_ref, g_ref, u_ref, dg_ref, du_ref):
        dact = _dot(dd_ref[...].astype(MX), wd_ref[...], 1, 1)
        gv, uv = g_ref[...].astype(f32), u_ref[...].astype(f32)
        sig = jax.nn.sigmoid(gv)
        dg_ref[...] = (dact * uv * sig * (1.0 + gv * (1.0 - sig))).astype(dg_ref.dtype)
        du_ref[...] = (dact * gv * sig).astype(du_ref.dtype)

    o_spec = pl.BlockSpec((tm, tn), lambda i, j: (i, j))
    return pl.pallas_call(
        body, name="ffn_dact", grid=(t // tm, DFF // tn),
        in_specs=[pl.BlockSpec((tm, D), lambda i, j: (i, 0)), pl.BlockSpec((tn, D), lambda i, j: (j, 0)), o_spec, o_spec],
        out_specs=[o_spec] * 2, out_shape=[jax.ShapeDtypeStruct((t, DFF), bf16)] * 2,
        compiler_params=_cparams(("parallel", "parallel")),
    )(ddown, wd, g, u)


def loss_head(y, target):
    t = y.shape[0]
    tb = min(ROW_TB, t)
    n = t // tb

    def body(y_ref, t_ref, dy_ref, loss_ref, acc):
        i = pl.program_id(0)

        @pl.when(i == 0)
        def _():
            acc[...] = jnp.zeros_like(acc)

        e = y_ref[...] - t_ref[...]
        dy_ref[...] = e * (1.0 / D)
        acc[...] += jnp.sum(e * e, axis=0, keepdims=True)

        @pl.when(i == n - 1)
        def _():
            loss_ref[...] = jnp.sum(acc[...], axis=1, keepdims=True) * (0.5 / D)

    return pl.pallas_call(
        body, name="loss_head", grid=(n,),
        in_specs=[_row_spec(tb, D, 0)] * 2,
        out_specs=[_row_spec(tb, D, 0), _full_spec((1, 1))],
        out_shape=[jax.ShapeDtypeStruct((t, D), f32), jax.ShapeDtypeStruct((1, 1), f32)],
        scratch_shapes=[pltpu.VMEM((1, D), f32)],
        compiler_params=_cparams(("arbitrary",)),
    )(y, target)


def _me_and_peers():
    x, y, c = lax.axis_index("x"), lax.axis_index("y"), lax.axis_index("c")
    me = 4 * x + 2 * y + c
    peers = []
    for kk in range(1, N_DEV):
        px = 1 - x if kk & 4 else x
        py = 1 - y if kk & 2 else y
        pc = 1 - c if kk & 1 else c
        peers.append(((px, py, pc), 4 * px + 2 * py + pc))
    return me, peers


_ANY = pl.BlockSpec(memory_space=pl.ANY)


def _comm_scratch():
    return [pltpu.SemaphoreType.DMA((N_DEV - 1,)), pltpu.SemaphoreType.DMA((N_DEV - 1,)), pltpu.SemaphoreType.DMA]


def _exchange_copies(src_ref, out_ref, send_sems, recv_sems, local_sem, scatter):
    me, peers = _me_and_peers()
    pick = (lambda d: src_ref.at[d]) if scatter else (lambda d: src_ref)
    ops = [pltpu.make_async_copy(pick(me), out_ref.at[me], local_sem)]
    ops += [pltpu.make_async_remote_copy(src_ref=pick(pid), dst_ref=out_ref.at[me], send_sem=send_sems.at[kk],
                                         recv_sem=recv_sems.at[kk], device_id=dev, device_id_type=pl.DeviceIdType.MESH)
            for kk, (dev, pid) in enumerate(peers)]
    return ops


def _comm_out_shape(src, scatter):
    return jax.ShapeDtypeStruct(src.shape if scatter else (N_DEV,) + src.shape, src.dtype)


def exchange(name, comms):
    def body(*refs):
        _, _, _, comm_refs = _split_hosted(refs, 0, 0, 0, len(comms))
        ops = [op for refs_c, (_, scatter) in zip(comm_refs, comms) for op in _exchange_copies(*refs_c, scatter)]
        for op in ops:
            op.start()
        for op in ops:
            op.wait()

    in_specs, out_specs, out_shape, scratch, args = [], [], [], [], []
    _add_hosted(comms, in_specs, out_specs, out_shape, scratch, args)
    return pl.pallas_call(body, name=name, in_specs=in_specs, out_specs=out_specs, out_shape=out_shape, scratch_shapes=scratch)(*args)


def adamw(w, m, v, parts):
    nl, r, c = w.shape
    lanes = -(-c // LANE) * LANE
    tb = _pick_rows(r, max(8, ADAM_PART_BLOCK_BYTES // (N_DEV * lanes * 4)))
    n = r // tb

    def body(*refs):
        w_ref, m_ref, v_ref = refs[:3]
        p_refs = refs[3:3 + nl]
        g_ref, d_ref, nm_ref, nv_ref = refs[3 + nl:]
        layer = pl.program_id(0)
        for k in range(nl):
            @pl.when(layer == k)
            def _(p_ref=p_refs[k]):
                g = p_ref[0]
                for s in range(1, N_DEV):
                    g = g + p_ref[s]
                g_ref[0] = g

        g = g_ref[0]
        nm = ADAM_B1 * m_ref[0] + (1.0 - ADAM_B1) * g
        nv = ADAM_B2 * v_ref[0] + (1.0 - ADAM_B2) * jnp.square(g)
        m_hat = nm / (1.0 - ADAM_B1 ** ADAM_STEP)
        v_hat = nv / (1.0 - ADAM_B2 ** ADAM_STEP)
        d_ref[0] = -ADAM_LR * (m_hat / (jnp.sqrt(v_hat) + ADAM_EPS) + ADAM_WD * w_ref[0])
        nm_ref[0] = nm
        nv_ref[0] = nv

    def part_spec(k):
        return pl.BlockSpec((N_DEV, tb, c), lambda l, i: (0, jnp.where(l == k, i, jnp.where(l > k, n - 1, 0)), 0))

    spec = pl.BlockSpec((1, tb, c), lambda l, i: (l, i, 0))
    return pl.pallas_call(
        body, name="adamw", grid=(nl, n), in_specs=[spec] * 3 + [part_spec(k) for k in range(nl)], out_specs=[spec] * 4,
        out_shape=[jax.ShapeDtypeStruct((nl, r, c), f32)] * 4, compiler_params=_cparams(("arbitrary", "arbitrary")),
    )(w, m, v, *parts)


ADAM_PART_BLOCK_BYTES = 4 << 20


def _pick_rows(r, cap=ROW_TB):
    return max(tb for tb in range(8, min(cap, ROW_TB) + 1, 8) if r % tb == 0)


BIG = ("w_in", "w_uq", "w_ukv", "w_out", "w_gate_up", "w_down", "w_ple", "w_ple_gate")
COL_SHARDED = ("w_in", "w_uq", "w_ukv", "w_gate_up", "w_ple")
SMALL = ("conv_w", "a_log", "dt_bias", "gdn_norm_g", "q_norm_g", "kv_norm_g", "ln1_g", "ln1_b", "ln2_g", "ln2_b")
ROW_ALIGN = 16


def _pad_rows(a, mult=ROW_ALIGN, axis=0):
    pad = (-a.shape[axis]) % mult
    widths = [(0, 0)] * a.ndim
    widths[axis] = (0, pad)
    return a if pad == 0 else jnp.pad(a, widths)


GATHER_EARLY = ("w_in", "w_uq", "w_ukv")
GATHER_LATE = ("w_out", "w_gate_up", "w_down", "w_ple", "w_ple_gate")
GRAD_READY_EARLY = ("w_gate_up", "w_down", "w_ple", "w_ple_gate", "w_out")


def _grad_groups(l):
    early = GRAD_READY_EARLY if l == 0 else ("w_gate_up", "w_down")
    return early, tuple(n for n in BIG if n not in early)


def _gathers(tree, l, names):
    return [(tree[n][l].astype(bf16), False) for n in names]


def _full_weight(gathered, n, lo=0, hi=N_DEV):
    blk = gathered[lo:hi]
    k, sr, sc = blk.shape
    if n in COL_SHARDED:
        return jnp.transpose(blk, (1, 0, 2)).reshape(sr, k * sc)
    return blk.reshape(k * sr, sc)


def _shard_grad(gfull, n):
    if n in COL_SHARDED:
        sr = gfull.shape[0]
        k = gfull.shape[1] // (FULL_COLS[n] // N_DEV)
        return jnp.transpose(gfull.reshape(sr, k, gfull.shape[1] // k), (1, 0, 2))
    return gfull.reshape(N_DEV, gfull.shape[0] // N_DEV, gfull.shape[1])


FULL_COLS = {"w_in": IN_W, "w_uq": 768, "w_ukv": 1024, "w_gate_up": 2 * DFF, "w_ple": D}


def _pack_small(tree, extra):
    flat = jnp.concatenate([tree[n].reshape(-1).astype(f32) for n in SMALL] + [extra.reshape(-1).astype(f32)])
    return jnp.pad(flat, (0, (-flat.shape[0]) % (ROW_ALIGN * D))).reshape(-1, D)


def _unpack_small(rows, like):
    flat = rows.reshape(-1)
    out, off = {}, 0
    for n in SMALL:
        sz = int(np.prod(like[n].shape))
        out[n] = flat[off:off + sz].reshape(like[n].shape)
        off += sz
    return out, flat[off]


def _zeros(r, c, dt):
    return jnp.zeros((r, c), dt)


def _prep_w_in(w):
    dt = w.dtype
    kr = w[:, 2696:2760]
    return jnp.concatenate([
        w[:, 0:2048],
        w[:, 2440:2696],
        kr, w[:, 2048:2056], _zeros(D, 56, dt),
        kr[:, 32:], kr[:, :32], _zeros(D, 64, dt),
        w[:, 2056:2440], _zeros(D, CQP - QL, dt)], axis=1)


def _unprep_w_in(g):
    krs = g[:, OFF_KRS:OFF_KRS + 64]
    kr = g[:, OFF_KR:OFF_KR + 64] + jnp.concatenate([krs[:, 32:], krs[:, :32]], axis=1)
    return jnp.concatenate([g[:, 0:2048], g[:, OFF_KR + 64:OFF_KR + 72], g[:, OFF_CQ:OFF_CQ + QL],
                            g[:, OFF_CKV:OFF_CKV + KVL], kr], axis=1)


def _prep_w_uq(w):
    dt = w.dtype
    z64 = _zeros(QL, 64, dt)
    nope, ra, rb = [], [], []
    for h in range(H):
        nope.append(w[:, 192 * h:192 * h + 128])
        x1, x2 = w[:, 192 * h + 128:192 * h + 160], w[:, 192 * h + 160:192 * h + 192]
        ra += [x1, x2, z64]
        rb += [x2, x1, z64]
    return jnp.concatenate([jnp.concatenate(nope + ra + rb, axis=1), _zeros(CQP - QL, 1536, dt)], axis=0)


def _unprep_w_uq(g):
    g = g[:QL]
    cols = []
    for h in range(H):
        a = g[:, 512 + 128 * h:512 + 128 * h + 64]
        b = g[:, 1024 + 128 * h:1024 + 128 * h + 64]
        cols += [g[:, 128 * h:128 * (h + 1)], a[:, :32] + b[:, 32:], a[:, 32:] + b[:, :32]]
    return jnp.concatenate(cols, axis=1)


def _rope_tables(positions):
    inv_freq = ROPE_THETA ** (-jnp.arange(0, ROPE, 2, dtype=f32) / ROPE)
    ang = positions.astype(f32)[:, None] * inv_freq
    c, s = jnp.cos(ang), jnp.sin(ang)
    z = jnp.zeros((positions.shape[0], 64), f32)
    return jnp.concatenate([c, c, z], axis=1), jnp.concatenate([-s, s, z], axis=1)


def _tile_heads(vec, n=H):
    return jnp.tile(vec.reshape(1, -1), (1, n))


def _bcast_heads(vec):
    return jnp.repeat(vec, DK).reshape(1, H * DK)


def kernel(x, p, positions, w_in, conv_w, a_log, dt_bias, gdn_norm_g, q_norm_g, w_uq, kv_norm_g, w_ukv, w_out, ln1_g, ln1_b, w_gate_up, w_down, ln2_g, ln2_b, w_ple, w_ple_gate, loss_target, m_w_in, m_conv_w, m_a_log, m_dt_bias, m_gdn_norm_g, m_q_norm_g, m_w_uq, m_kv_norm_g, m_w_ukv, m_w_out, m_ln1_g, m_ln1_b, m_w_gate_up, m_w_down, m_ln2_g, m_ln2_b, m_w_ple, m_w_ple_gate, v_w_in, v_conv_w, v_a_log, v_dt_bias, v_gdn_norm_g, v_q_norm_g, v_w_uq, v_kv_norm_g, v_w_ukv, v_w_out, v_ln1_g, v_ln1_b, v_w_gate_up, v_w_down, v_ln2_g, v_ln2_b, v_w_ple, v_w_ple_gate):
    W = dict(w_in=w_in, conv_w=conv_w, a_log=a_log, dt_bias=dt_bias, gdn_norm_g=gdn_norm_g, q_norm_g=q_norm_g, w_uq=w_uq,
             kv_norm_g=kv_norm_g, w_ukv=w_ukv, w_out=w_out, ln1_g=ln1_g, ln1_b=ln1_b, w_gate_up=w_gate_up, w_down=w_down,
             ln2_g=ln2_g, ln2_b=ln2_b, w_ple=w_ple, w_ple_gate=w_ple_gate)
    M = dict(w_in=m_w_in, conv_w=m_conv_w, a_log=m_a_log, dt_bias=m_dt_bias, gdn_norm_g=m_gdn_norm_g, q_norm_g=m_q_norm_g,
             w_uq=m_w_uq, kv_norm_g=m_kv_norm_g, w_ukv=m_w_ukv, w_out=m_w_out, ln1_g=m_ln1_g, ln1_b=m_ln1_b,
             w_gate_up=m_w_gate_up, w_down=m_w_down, ln2_g=m_ln2_g, ln2_b=m_ln2_b, w_ple=m_w_ple, w_ple_gate=m_w_ple_gate)
    V = dict(w_in=v_w_in, conv_w=v_conv_w, a_log=v_a_log, dt_bias=v_dt_bias, gdn_norm_g=v_gdn_norm_g, q_norm_g=v_q_norm_g,
             w_uq=v_w_uq, kv_norm_g=v_kv_norm_g, w_ukv=v_w_ukv, w_out=v_w_out, ln1_g=v_ln1_g, ln1_b=v_ln1_b,
             w_gate_up=v_w_gate_up, w_down=v_w_down, ln2_g=v_ln2_g, ln2_b=v_ln2_b, w_ple=v_w_ple, w_ple_gate=v_w_ple_gate)
    shards = {n: W[n].shape for n in BIG}
    t = x.shape[1]
    xin = x.reshape(t, D)
    target = loss_target.reshape(t, D)
    cf, sf = _rope_tables(positions.reshape(t))

    conv_rows = _pad_rows(jnp.pad(conv_w.reshape(-1), (0, (-conv_w.size) % D)).reshape(-1, D), 8)
    first = exchange("all_gather_first", _gathers(W, 0, GATHER_EARLY) + [(conv_rows, False)])
    early = [dict(zip(GATHER_EARLY, first[:-1]))] + [None] * (DEPTH - 1)
    csz = conv_w.shape[1] * conv_w.shape[2]
    conv_full = jnp.transpose(first[-1].reshape(N_DEV, -1)[:, :DEPTH * csz].reshape(N_DEV, DEPTH, CW, -1), (1, 2, 0, 3)).reshape(DEPTH, CW, 1536)

    acts = []
    h_cur = xin
    for l in range(DEPTH):
        win = _prep_w_in(_full_weight(early[l]["w_in"], "w_in"))
        wq = _prep_w_uq(_full_weight(early[l]["w_uq"], "w_uq"))
        wkv = _full_weight(early[l]["w_ukv"], "w_ukv")
        cw8 = jnp.concatenate([conv_full[l], jnp.zeros((8 - CW, 1536), f32)], axis=0)
        alog_b, dtb_b = _bcast_heads(a_log[l]), _bcast_heads(dt_bias[l])
        gg_b = _tile_heads(gdn_norm_g[l])
        qg = jnp.concatenate([q_norm_g[l], jnp.zeros((CQP - QL,), f32)]).reshape(1, CQP)
        kvg = kv_norm_g[l].reshape(1, KVL)
        g1, b1, g2, b2 = (a[l].reshape(1, D) for a in (ln1_g, ln1_b, ln2_g, ln2_b))
        p_l = p[l].reshape(t, PLE)

        hb = mm("in_proj", h_cur, win, mode="nn", out_dtype=f32)
        qn, kn, vs, beta, glog = gdn_pre_fwd(hb, cw8, alog_b, dtb_b)
        g_u, g_w, g_qd, g_kd, g_at, g_el = gdn_intra(qn, kn, vs, beta, glog)
        o_gdn, states = gdn_scan_fwd(g_u, g_w, g_qd, g_kd, g_at, g_el)
        mla_rows = [(hb, CQP, OFF_CQ // CQP), (hb, KVL, OFF_CKV // KVL), (hb, LANE, OFF_KR // LANE), (hb, LANE, OFF_KRS // LANE),
                    (cf, LANE, 0), (sf, LANE, 0)]
        mla_consts = [qg, wq, kvg, wkv]
        qm, km, vm = row_fwd("mla_pre", mla_pre_f, mla_rows, mla_consts, [(1024, bf16), (1024, bf16), (512, bf16)])
        comms = _gathers(W, l, GATHER_LATE) + (_gathers(W, l + 1, GATHER_EARLY) if l + 1 < DEPTH else [])
        res = attn_fwd(qm, km, vm, name=f"attn_fwd_gather{len(comms)}", comms=comms)
        o_mla, lse = res[:2]
        late = dict(zip(GATHER_LATE, res[2:]))
        if l + 1 < DEPTH:
            early[l + 1] = dict(zip(GATHER_EARLY, res[2 + len(GATHER_LATE):]))
        wout, wd = _full_weight(late["w_out"], "w_out"), _full_weight(late["w_down"], "w_down")
        half = N_DEV // 2
        wg, wu = _full_weight(late["w_gate_up"], "w_gate_up", 0, half), _full_weight(late["w_gate_up"], "w_gate_up", half, N_DEV)
        wple, wpg = _full_weight(late["w_ple"], "w_ple"), _full_weight(late["w_ple_gate"], "w_ple_gate")
        p1_rows = [(o_gdn, 512, 0), (hb, 512, OFF_Z // 512), (o_mla, 512, 0), (h_cur, D, 0)]
        p1_consts = [gg_b, wout, g1, b1]
        (x1,) = row_fwd("post1", post1_f, p1_rows, p1_consts, [(D, f32)])
        gate, up, act = ffn_up(x1, wg, wu)
        down = mm("ffn_down", act, wd, mode="nn", out_dtype=f32)
        p2_rows = [(x1, D, 0), (down, D, 0), (p_l, PLE, 0)]
        p2_consts = [g2, b2, wpg, wple]
        (y,) = row_fwd("post2", post2_f, p2_rows, p2_consts, [(D, f32)])
        acts.append(dict(x=h_cur, hb=hb, qn=qn, kn=kn, vs=vs, beta=beta, glog=glog, states=states, o_gdn=o_gdn, qm=qm, km=km,
                         scan=(g_w, g_qd, g_kd, g_at, g_el),
                         vm=vm, o_mla=o_mla, lse=lse, x1=x1, gate=gate, up=up, act=act, win=win, wd=wd, wg=wg, wu=wu,
                         cw8=cw8, alog_b=alog_b, dtb_b=dtb_b, mla_rows=mla_rows, mla_consts=mla_consts, p1_rows=p1_rows,
                         p1_consts=p1_consts, p2_rows=p2_rows, p2_consts=p2_consts))
        h_cur = y

    dy, loss_part = loss_head(h_cur, target)

    G = {n: [None] * DEPTH for n in SMALL}
    S = {n: [None] * DEPTH for n in BIG}
    parts = {n: [None] * DEPTH for n in BIG}
    pending = None
    for l in reversed(range(DEPTH)):
        a = acts[l]
        dx1, ddown, dg2, db2, dwpg, dwple = row_bwd("post2_bwd", post2_f, a["p2_rows"], a["p2_consts"], [dy], [0, 1], [0, 1, 2, 3], [f32, f32])
        dgate, dup = ffn_dact(ddown, a["wd"], a["gate"], a["up"])
        dwd = mm("dw_down", a["act"], ddown, mode="tn", out_dtype=f32)
        dx1 = mm("dx_gate", dgate, a["wg"], mode="nt", out_dtype=f32, add=dx1)
        dx1 = mm("dx_up", dup, a["wu"], mode="nt", out_dtype=f32, add=dx1)
        dwg = mm("dw_gate", a["x1"], dgate, mode="tn", out_dtype=f32)
        dwu = mm("dw_up", a["x1"], dup, mode="tn", out_dtype=f32)
        do_gdn, dz, do_mla, dxr, dgg, dwout, dg1, db1 = row_bwd("post1_bwd", post1_f, a["p1_rows"], a["p1_consts"], [dx1], [0, 1, 2, 3],
                                                              [0, 1, 2, 3], [f32, bf16, f32, f32])
        for g, n in ((dwd, "w_down"), (dwple, "w_ple"), (dwpg, "w_ple_gate"), (dwout, "w_out")):
            S[n][l] = _shard_grad(g, n)
        S["w_gate_up"][l] = jnp.concatenate([_shard_grad(dwg, "w_gate_up"), _shard_grad(dwu, "w_gate_up")], axis=0)
        early = _grad_groups(l)[0]
        res = attn_bwd(a["qm"], a["km"], a["vm"], a["o_mla"], a["lse"], do_mla, name=f"attn_bwd_exchange{len(early)}",
                       comms=[(S[n][l], True) for n in early])
        dqm, dkm, dvm = res[:3]
        for n, got in zip(early, res[3:]):
            parts[n][l] = got
        dstates = gdn_scan_bwd(do_gdn, *a["scan"])
        late = _grad_groups(pending)[1] if pending is not None else ()
        res = gdn_local_bwd(a["qn"], a["kn"], a["vs"], a["beta"], a["glog"], a["states"], dstates, do_gdn,
                            name="gdn_local_bwd_exchange" if late else "gdn_local_bwd", comms=[(S[n][pending], True) for n in late])
        dqn, dkn, dvs, dbeta, dglog = res[:5]
        for n, got in zip(late, res[5:]):
            parts[n][pending] = got
        dqkv, dba, dcw, dal, ddt = gdn_pre_bwd(a["hb"], a["cw8"], a["alog_b"], a["dtb_b"], dqn, dkn, dvs, dbeta, dglog)
        dcq, dckv, dkrg, dkrs, dqg, dwq, dkvg, dwkv = row_bwd("mla_pre_bwd", mla_pre_f, a["mla_rows"], a["mla_consts"], [dqm, dkm, dvm],
                                                              [0, 1, 2, 3], [0, 1, 2, 3], [bf16, bf16, bf16, bf16], row_add={2: dba})
        dh = jnp.concatenate([dqkv, dz, dckv, dkrg, dkrs, dcq], axis=1)
        dwin = mm("dw_in", a["x"], dh, mode="tn", out_dtype=f32)
        dy = mm("dx_in", dh, a["win"], mode="nt", out_dtype=f32, add=dxr)

        S["w_in"][l], S["w_uq"][l] = _shard_grad(_unprep_w_in(dwin), "w_in"), _shard_grad(_unprep_w_uq(dwq), "w_uq")
        S["w_ukv"][l] = _shard_grad(dwkv, "w_ukv")
        G["conv_w"][l] = dcw[:CW]
        G["a_log"][l] = jnp.sum(dal.reshape(H, DK), axis=1)
        G["dt_bias"][l] = jnp.sum(ddt.reshape(H, DK), axis=1)
        G["gdn_norm_g"][l] = jnp.sum(dgg.reshape(H, DK), axis=0)
        G["q_norm_g"][l] = dqg[0, :QL]
        G["kv_norm_g"][l] = dkvg[0]
        G["ln1_g"][l], G["ln1_b"][l], G["ln2_g"][l], G["ln2_b"][l] = dg1[0], db1[0], dg2[0], db2[0]
        pending = l
    grad_x = dy.reshape(x.shape)

    small_like = {n: W[n] for n in SMALL}
    conv_parts = jnp.stack(G["conv_w"]).reshape(DEPTH, CW, N_DEV, -1)
    smalls = []
    for d in range(N_DEV):
        tree = {n: jnp.stack(G[n]) for n in SMALL if n != "conv_w"}
        tree["conv_w"] = conv_parts[:, :, d, :]
        smalls.append(_pack_small(tree, loss_part))
    small_send = jnp.stack(smalls)
    late = _grad_groups(pending)[1]
    res = exchange("exchange_last", [(S[n][pending], True) for n in late] + [(small_send, True)])
    for n, got in zip(late, res):
        parts[n][pending] = got

    g_out, d_out, m_out, v_out = {}, {}, {}, {}
    for n in BIG:
        g_out[n], d_out[n], m_out[n], v_out[n] = adamw(W[n], M[n], V[n], parts[n])
    zero = jnp.zeros((), f32)
    upd = adamw(_pack_small(W, zero)[None], _pack_small(M, zero)[None], _pack_small(V, zero)[None], [res[-1]])
    for dst, rows_arr in zip((g_out, d_out, m_out, v_out), upd):
        dst.update(_unpack_small(rows_arr[0], small_like)[0])
    loss = _unpack_small(upd[0][0], small_like)[1]
    order = ["w_in", "conv_w", "a_log", "dt_bias", "gdn_norm_g", "q_norm_g", "w_uq", "kv_norm_g", "w_ukv", "w_out", "ln1_g", "ln1_b",
             "w_gate_up", "w_down", "ln2_g", "ln2_b", "w_ple", "w_ple_gate"]
    return (loss, grad_x, *[g_out[n] for n in order], *[d_out[n] for n in order], *[m_out[n] for n in order],
            *[v_out[n] for n in order])
```
